```python
import jax
import jax.numpy as jnp
from jax import lax
import numpy as np

D_MODEL = 2048
BATCH = 4
SEQ = 4096
DEPTH = 2

GRID_W = 64
CTX_LEN = 256
N_BRANCHES = 3
N_MOD = 6
EPS = 1e-6

CONV_A_DIM = 1024
CONV_A_WIDTH = 31

MLA_HEADS = 8
Q_LORA = 512
KV_LORA = 512
NOPE_DIM = 128
ROPE_DIM = 64
V_DIM = 128
QK_DIM = NOPE_DIM + ROPE_DIM
AXIS_DIM = ROPE_DIM // 2
ROPE_THETA = 10000.0
ATTN_SCALE = QK_DIM ** -0.5
Q_BLOCK = 128

SHORT_DIM = 1024
SHORT_WIDTH = 3

N_EXPERTS = 32
N_GROUPS = 8
EXPERTS_PER_GROUP = N_EXPERTS // N_GROUPS
TOP_K = 2
EXPERT_FF = 768
MOE_BLOCK = 256

GATE_COLS = N_BRANCHES * D_MODEL
A_COLS = 2 * CONV_A_DIM
B_COLS = Q_LORA + KV_LORA + ROPE_DIM
C_COLS = 3 * SHORT_DIM
OFF_A = GATE_COLS
OFF_B = OFF_A + A_COLS
OFF_C = OFF_B + B_COLS
IN_COLS = OFF_C + C_COLS

kernel_name = 'hybrid_mla_conformer_shortconv_grouped_moe_dit'


def rms_norm(x, g):
    xf = x.astype(jnp.float32)
    y = xf * lax.rsqrt(jnp.mean(xf * xf, axis=-1, keepdims=True) + EPS)
    return (y * g.astype(jnp.float32)).astype(x.dtype)


def layer_norm(x, g, b):
    xf = x.astype(jnp.float32)
    mu = jnp.mean(xf, axis=-1, keepdims=True)
    var = jnp.mean(jnp.square(xf - mu), axis=-1, keepdims=True)
    y = (xf - mu) * lax.rsqrt(var + EPS) * g.astype(jnp.float32) + b.astype(jnp.float32)
    return y.astype(x.dtype)


def modulate(xn, shift, scale):
    return xn * (1 + scale) + shift


def depthwise_conv(x, w):
    k = w.shape[0]
    return lax.conv_general_dilated(
        x, w[:, None, :].astype(x.dtype), window_strides=(1,),
        padding=[(k // 2, k // 2)], dimension_numbers=('NWC', 'WIO', 'NWC'),
        feature_group_count=x.shape[-1])


def axial_rope_angles(row, col):
    inv = ROPE_THETA ** (-jnp.arange(0, AXIS_DIM, 2, dtype=jnp.float32) / AXIS_DIM)
    ang_r = row.astype(jnp.float32)[:, None] * inv
    ang_c = col.astype(jnp.float32)[:, None] * inv
    return ang_r[:, None, :], ang_c[:, None, :]


def rope_half(x, ang):
    x1, x2 = jnp.split(x, 2, axis=-1)
    cos = jnp.cos(ang).astype(x.dtype)
    sin = jnp.sin(ang).astype(x.dtype)
    return jnp.concatenate([x1 * cos - x2 * sin, x2 * cos + x1 * sin], axis=-1)


def rope_2d(x, ang):
    xr, xc = jnp.split(x, 2, axis=-1)
    return jnp.concatenate([rope_half(xr, ang[0]), rope_half(xc, ang[1])], axis=-1)


def mla_q(ub, g_cq, w_uq, g_q, ang):
    bsz, n, _ = ub.shape
    c_q = ub[..., :Q_LORA]
    q = (rms_norm(c_q, g_cq) @ w_uq).reshape(bsz, n, MLA_HEADS, QK_DIM)
    q = rms_norm(q, g_q)
    if ang is not None:
        q = jnp.concatenate([q[..., :NOPE_DIM], rope_2d(q[..., NOPE_DIM:], ang)], axis=-1)
    return q


def mla_kv(ub, g_ckv, w_ukv, g_k, ang):
    bsz, n, _ = ub.shape
    c_kv = ub[..., Q_LORA:Q_LORA + KV_LORA]
    k_pe = ub[..., Q_LORA + KV_LORA:]
    kv = (rms_norm(c_kv, g_ckv) @ w_ukv).reshape(bsz, n, MLA_HEADS, NOPE_DIM + V_DIM)
    k_nope, v = kv[..., :NOPE_DIM], kv[..., NOPE_DIM:]
    k_pe = jnp.broadcast_to(k_pe[:, :, None, :], (bsz, n, MLA_HEADS, ROPE_DIM))
    k = rms_norm(jnp.concatenate([k_nope, k_pe], axis=-1), g_k)
    if ang is not None:
        k = jnp.concatenate([k[..., :NOPE_DIM], rope_2d(k[..., NOPE_DIM:], ang)], axis=-1)
    return k, v


def attend(q, k, v):
    s = jnp.einsum('bqhd,bkhd->bhqk', q, k).astype(jnp.float32) * ATTN_SCALE
    p = jax.nn.softmax(s, axis=-1).astype(v.dtype)
    return jnp.einsum('bhqk,bkhd->bqhd', p, v)


def attend_blocked(q, k, v):
    bsz, n, h, dk = q.shape
    nb = n // Q_BLOCK
    qb = q.reshape(bsz, nb, Q_BLOCK, h, dk).transpose(1, 0, 2, 3, 4)
    ob = lax.map(lambda qi: attend(qi, k, v), qb)
    return ob.transpose(1, 0, 2, 3, 4).reshape(bsz, n, h, V_DIM)


def conformer_conv(ua, conv_w, ln_g, ln_b, w_out):
    val, gate = jnp.split(ua, 2, axis=-1)
    h = val * jax.nn.sigmoid(gate)
    h = depthwise_conv(h, conv_w)
    h = jax.nn.silu(layer_norm(h, ln_g, ln_b))
    return h @ w_out


def short_conv(uc, conv_w, w_out):
    b_gate, c_gate, h = jnp.split(uc, 3, axis=-1)
    return (b_gate * depthwise_conv(c_gate * h, conv_w)) @ w_out


def branch_merge(u_gate, ua, uc, y_b, conv_a, ln_g, ln_b, w_a_out, conv_c, w_c_out, w_o):
    y_a = conformer_conv(ua, conv_a, ln_g, ln_b, w_a_out)
    y_c = short_conv(uc, conv_c, w_c_out)
    g_a, g_b, g_c = jnp.split(jax.nn.sigmoid(u_gate), N_BRANCHES, axis=-1)
    return (g_a * y_a + g_b * y_b + g_c * y_c) @ w_o


def moe(xt, w_router, router_bias, w_e_in, w_e_out):
    n, d = xt.shape
    scores = jax.nn.sigmoid((xt @ w_router).astype(jnp.float32))
    sel = (scores + router_bias.astype(jnp.float32)).reshape(n, N_GROUPS, EXPERTS_PER_GROUP)
    group_score = jnp.sum(lax.top_k(sel, 2)[0], axis=-1)
    grp = jnp.argmax(group_score, axis=-1)
    sel_in = sel[jnp.arange(n), grp]
    _, local = lax.top_k(sel_in, TOP_K)
    expert = grp[:, None] * EXPERTS_PER_GROUP + local
    w = jnp.take_along_axis(scores, expert, axis=-1)
    w = w / jnp.sum(w, axis=-1, keepdims=True)
    flat_e = expert.reshape(-1).astype(jnp.int32)
    a = n * TOP_K
    order = jnp.argsort(flat_e)
    sorted_e = flat_e[order]
    counts = jnp.bincount(flat_e, length=N_EXPERTS)
    padded = (counts + MOE_BLOCK - 1) // MOE_BLOCK * MOE_BLOCK
    start = jnp.cumsum(counts) - counts
    pend = jnp.cumsum(padded)
    pstart = pend - padded
    dest = pstart[sorted_e] + jnp.arange(a) - start[sorted_e]
    n_blocks = -(-a // MOE_BLOCK) + N_EXPERTS
    rows = n_blocks * MOE_BLOCK
    row_tok = jnp.zeros((rows,), jnp.int32).at[dest].set((order // TOP_K).astype(jnp.int32))
    xs = xt[row_tok].reshape(n_blocks, MOE_BLOCK, d)
    block_start = jnp.arange(n_blocks) * MOE_BLOCK
    block_e = jnp.minimum(jnp.searchsorted(pend, block_start, side='right'), N_EXPERTS - 1)

    def run(args):
        xb, e = args
        gu = xb @ w_e_in[e]
        g, u = jnp.split(gu, 2, axis=-1)
        return (jax.nn.silu(g) * u) @ w_e_out[e]

    ys = lax.map(run, (xs, block_e)).reshape(rows, d)
    y_assign = jnp.zeros((a, d), ys.dtype).at[order].set(ys[dest]).reshape(n, TOP_K, d)
    return jnp.einsum('nk,nkd->nd', w.astype(ys.dtype), y_assign)


def setup_inputs(seed: int = 0) -> dict:
    key = jax.random.key(seed)
    ks = jax.random.split(key, 27)

    def nrm(k, shape, scale):
        return jax.random.normal(k, shape, jnp.float32) * scale

    L = DEPTH
    return {
        'x': nrm(ks[0], (BATCH, SEQ, D_MODEL), 1.0),
        'c': nrm(ks[1], (BATCH, D_MODEL), 1.0),
        'ctx': nrm(ks[2], (BATCH, CTX_LEN, D_MODEL), 1.0),
        'c_ctx': nrm(ks[3], (D_MODEL,), 1.0),
        'w_mod': nrm(ks[4], (L, D_MODEL, N_MOD * D_MODEL), 0.5 * D_MODEL ** -0.5),
        'b_mod': nrm(ks[5], (L, N_MOD * D_MODEL), 0.02),
        'g_norm1': 1.0 + nrm(ks[6], (L, D_MODEL), 0.1),
        'g_norm2': 1.0 + nrm(ks[7], (L, D_MODEL), 0.1),
        'w_in': nrm(ks[8], (L, D_MODEL, IN_COLS), D_MODEL ** -0.5),
        'conv_a': nrm(ks[9], (L, CONV_A_WIDTH, CONV_A_DIM), CONV_A_WIDTH ** -0.5),
        'ln_a_g': 1.0 + nrm(ks[10], (L, CONV_A_DIM), 0.1),
        'ln_a_b': nrm(ks[11], (L, CONV_A_DIM), 0.02),
        'w_a_out': nrm(ks[12], (L, CONV_A_DIM, D_MODEL), CONV_A_DIM ** -0.5),
        'g_cq': 1.0 + nrm(ks[13], (L, Q_LORA), 0.1),
        'g_ckv': 1.0 + nrm(ks[14], (L, KV_LORA), 0.1),
        'w_uq': nrm(ks[15], (L, Q_LORA, MLA_HEADS * QK_DIM), Q_LORA ** -0.5),
        'w_ukv': nrm(ks[16], (L, KV_LORA, MLA_HEADS * (NOPE_DIM + V_DIM)), KV_LORA ** -0.5),
        'g_q': 1.0 + nrm(ks[17], (L, QK_DIM), 0.1),
        'g_k': 1.0 + nrm(ks[18], (L, QK_DIM), 0.1),
        'w_b_out': nrm(ks[19], (L, MLA_HEADS * V_DIM, D_MODEL), (MLA_HEADS * V_DIM) ** -0.5),
        'conv_c': nrm(ks[20], (L, SHORT_WIDTH, SHORT_DIM), SHORT_WIDTH ** -0.5),
        'w_c_out': nrm(ks[21], (L, SHORT_DIM, D_MODEL), SHORT_DIM ** -0.5),
        'w_o': nrm(ks[22], (L, D_MODEL, D_MODEL), D_MODEL ** -0.5),
        'w_router': nrm(ks[23], (D_MODEL, N_EXPERTS), D_MODEL ** -0.5),
        'router_bias': nrm(ks[24], (N_EXPERTS,), 0.01),
        'w_e_in': nrm(ks[25], (L, N_EXPERTS, D_MODEL, 2 * EXPERT_FF), D_MODEL ** -0.5),
        'w_e_out': nrm(ks[26], (L, N_EXPERTS, EXPERT_FF, D_MODEL), EXPERT_FF ** -0.5),
    }


def reference(x, c, ctx, c_ctx, w_mod, b_mod, g_norm1, g_norm2, w_in, conv_a, ln_a_g, ln_a_b,
              w_a_out, g_cq, g_ckv, w_uq, w_ukv, g_q, g_k, w_b_out, conv_c, w_c_out, w_o,
              w_router, router_bias, w_e_in, w_e_out):
    b, s, d = x.shape
    rows_n = s // GRID_W
    row = jnp.repeat(jnp.arange(rows_n), GRID_W)
    col = jnp.tile(jnp.arange(GRID_W), rows_n)
    ang = axial_rope_angles(row, col)
    silu_c = jax.nn.silu(c)[:, None, :]
    silu_cc = jax.nn.silu(c_ctx)[None, None, :]
    xc = ctx
    n_c = ctx.shape[1]
    splits = [OFF_A, OFF_B, OFF_C]
    for l in range(DEPTH):
        last = l == DEPTH - 1
        mod_x = jnp.split(silu_c @ w_mod[l] + b_mod[l], N_MOD, axis=-1)
        mod_c = jnp.split(silu_cc @ w_mod[l] + b_mod[l], N_MOD, axis=-1)
        mix_p = (conv_a[l], ln_a_g[l], ln_a_b[l], w_a_out[l], conv_c[l], w_c_out[l], w_o[l])

        h_x = modulate(rms_norm(x, g_norm1[l]), mod_x[0], mod_x[1])
        h_c = modulate(rms_norm(xc, g_norm1[l]), mod_c[0], mod_c[1])
        gate_x, ua_x, ub_x, uc_x = jnp.split(h_x @ w_in[l], splits, axis=-1)
        if last:
            ub_c = h_c @ w_in[l][:, OFF_B:OFF_C]
        else:
            gate_c, ua_c, ub_c, uc_c = jnp.split(h_c @ w_in[l], splits, axis=-1)
        k_c, v_c = mla_kv(ub_c, g_ckv[l], w_ukv[l], g_k[l], None)
        k_x, v_x = mla_kv(ub_x, g_ckv[l], w_ukv[l], g_k[l], ang)
        q_x = mla_q(ub_x, g_cq[l], w_uq[l], g_q[l], ang)
        o_x = attend_blocked(q_x, jnp.concatenate([k_c, k_x], axis=1),
                             jnp.concatenate([v_c, v_x], axis=1))
        yb_x = o_x.reshape(b, s, MLA_HEADS * V_DIM) @ w_b_out[l]
        x_new = x + mod_x[2] * branch_merge(gate_x, ua_x, uc_x, yb_x, *mix_p)
        if not last:
            q_c = mla_q(ub_c, g_cq[l], w_uq[l], g_q[l], None)
            yb_c = attend(q_c, k_c, v_c).reshape(b, n_c, MLA_HEADS * V_DIM) @ w_b_out[l]
            xc = xc + mod_c[2] * branch_merge(gate_c, ua_c, uc_c, yb_c, *mix_p)
        x = x_new

        h2_x = modulate(rms_norm(x, g_norm2[l]), mod_x[3], mod_x[4])
        if last:
            y_x = moe(h2_x.reshape(-1, d), w_router, router_bias, w_e_in[l], w_e_out[l])
            x = x + mod_x[5] * y_x.reshape(b, s, d)
        else:
            h2_c = modulate(rms_norm(xc, g_norm2[l]), mod_c[3], mod_c[4])
            tokens = jnp.concatenate([h2_c, h2_x], axis=1).reshape(-1, d)
            y = moe(tokens, w_router, router_bias, w_e_in[l], w_e_out[l]).reshape(b, n_c + s, d)
            xc = xc + mod_c[5] * y[:, :n_c]
            x = x + mod_x[5] * y[:, n_c:]
    return x
```

```python
import functools
import math

import jax
import jax.numpy as jnp
from jax import lax
from jax.experimental import pallas as pl
from jax.experimental.pallas import tpu as pltpu

F32 = jnp.float32
BF16 = jnp.bfloat16

GRID_W = 64
MLA_HEADS = 8
NOPE_DIM = 128
ROPE_DIM = 64
V_DIM = 128
QK_DIM = NOPE_DIM + ROPE_DIM
AXIS_DIM = ROPE_DIM // 2
ROPE_THETA = 10000.0
N_GROUPS = 8
TOP_K = 2
N_MOD = 6
EPS = 1e-6

LANES = 128
HEAD_SLOT = 2 * LANES
HALO_ROWS = 16
MOD_ROWS = 8

TILES = dict(
    row=512,
    col=1024,
    mod_col=1024,
    mla=256,
    tq=256,
    conv=256,
    conv_rows=64,
    router=512,
    moe_block=256,
    vmem=56 * 1024 * 1024,
)


def _cparams(sem):
    return pltpu.CompilerParams(dimension_semantics=sem, vmem_limit_bytes=TILES["vmem"])


def _sigmoid(x):
    return 1.0 / (1.0 + jnp.exp(-x))


def _mod_kernel(c_ref, w_ref, b_ref, o_ref):
    c = c_ref[...]
    s = (c * _sigmoid(c)).astype(BF16)
    o_ref[...] = jnp.dot(s, w_ref[...].astype(BF16), preferred_element_type=F32) + b_ref[...]


def modulation(cvec, w_mod, b_mod):
    n_layers, d, n = w_mod.shape
    tn = min(TILES["mod_col"], n)
    return pl.pallas_call(
        _mod_kernel,
        out_shape=jax.ShapeDtypeStruct((n_layers, MOD_ROWS, n), F32),
        grid=(n_layers, n // tn),
        in_specs=[
            pl.BlockSpec((MOD_ROWS, d), lambda l, j: (0, 0)),
            pl.BlockSpec((None, d, tn), lambda l, j: (l, 0, j)),
            pl.BlockSpec((None, 1, tn), lambda l, j: (l, 0, j)),
        ],
        out_specs=pl.BlockSpec((None, MOD_ROWS, tn), lambda l, j: (l, 0, j)),
        compiler_params=_cparams(("parallel", "parallel")),
        name="modulation",
    )(cvec, w_mod, b_mod.reshape(n_layers, 1, n))


def _norm_mod(x, g, shift, scale):
    ms = jnp.mean(x * x, axis=-1, keepdims=True)
    return x * lax.rsqrt(ms + EPS) * g * (1.0 + scale) + shift


def _normproj_kernel(x_ref, g_ref, shift_ref, scale_ref, w_ref, o_ref, h_ref):
    @pl.when(pl.program_id(1) == 0)
    def _():
        h_ref[...] = _norm_mod(x_ref[...], g_ref[...], shift_ref[...], scale_ref[...]).astype(BF16)

    o_ref[...] = jnp.dot(h_ref[...], w_ref[...], preferred_element_type=F32).astype(o_ref.dtype)


def _mod_row_fn(n_x_tiles, tiles_per_seq, batch, layer):
    def f(m):
        return layer * MOD_ROWS + jnp.where(m < n_x_tiles, m // tiles_per_seq, batch)
    return f


def norm_proj(xall, g, mod3, w, *, layer, shift_idx, n_rows, geom):
    batch, seq, _ = geom
    d = xall.shape[1]
    n = w.shape[1]
    tm = min(TILES["row"], seq)
    tn = min(TILES["col"], n)
    if n % tn:
        tn = n
    mrow = _mod_row_fn(batch * seq // tm, seq // tm, batch, layer)
    return pl.pallas_call(
        _normproj_kernel,
        out_shape=jax.ShapeDtypeStruct((n_rows, n), BF16),
        grid=(n_rows // tm, n // tn),
        in_specs=[
            pl.BlockSpec((tm, d), lambda m, j: (m, 0)),
            pl.BlockSpec((1, d), lambda m, j: (0, 0)),
            pl.BlockSpec((None, 1, d), lambda m, j: (mrow(m), 0, shift_idx)),
            pl.BlockSpec((None, 1, d), lambda m, j: (mrow(m), 0, shift_idx + 1)),
            pl.BlockSpec((d, tn), lambda m, j: (0, j)),
        ],
        out_specs=pl.BlockSpec((tm, tn), lambda m, j: (m, j)),
        scratch_shapes=[pltpu.VMEM((tm, d), BF16)],
        compiler_params=_cparams(("parallel", "arbitrary")),
        name="norm_proj",
    )(xall, g.reshape(1, d), mod3, mod3, w)


def _mla_kernel(ub_ref, gcq_ref, gckv_ref, wq_ref, wkv_ref, gq_ref, gk_ref, cos_ref, sin_ref,
                q_ref, kt_ref, v_ref, *, q_lora, kv_lora, heads):
    hn = heads * LANES

    def rms(xv, g):
        return (xv * lax.rsqrt(jnp.mean(xv * xv, axis=-1, keepdims=True) + EPS) * g).astype(BF16)

    cqn = rms(ub_ref[:, 0:q_lora].astype(F32), gcq_ref[...])
    ckvn = rms(ub_ref[:, q_lora:q_lora + kv_lora].astype(F32), gckv_ref[...])
    off = q_lora + kv_lora
    kpe = ub_ref[:, off:off + LANES].astype(F32)
    kpes = ub_ref[:, off + LANES:off + 2 * LANES].astype(F32)

    qall = jnp.dot(cqn, wq_ref[...], preferred_element_type=F32)
    kvall = jnp.dot(ckvn, wkv_ref[...], preferred_element_type=F32)

    cos = cos_ref[...]
    sin = sin_ref[...]
    gq_n, gq_r, gq_s = gq_ref[0:1, :], gq_ref[1:2, :], gq_ref[2:3, :]
    gk_n, gk_r, gk_s = gk_ref[0:1, :], gk_ref[1:2, :], gk_ref[2:3, :]
    q_cos, q_sin = gq_r * cos, gq_s * sin
    k_rope = kpe * (gk_r * cos) + kpes * (gk_s * sin)
    ssq_pe = jnp.sum(kpe * kpe, axis=-1, keepdims=True)
    attn_scale = QK_DIM ** -0.5

    for h in range(heads):
        lo = h * LANES
        qn = qall[:, lo:lo + LANES]
        qr = qall[:, hn + lo:hn + lo + LANES]
        qs = qall[:, 2 * hn + lo:2 * hn + lo + LANES]
        ssq = jnp.sum(qn * qn, axis=-1, keepdims=True) + jnp.sum(qr * qr, axis=-1, keepdims=True)
        inv = lax.rsqrt(ssq * (1.0 / QK_DIM) + EPS) * attn_scale
        q_ref[:, h * HEAD_SLOT:h * HEAD_SLOT + LANES] = (qn * gq_n * inv).astype(BF16)
        q_ref[:, h * HEAD_SLOT + LANES:(h + 1) * HEAD_SLOT] = ((qr * q_cos + qs * q_sin) * inv).astype(BF16)

        kn = kvall[:, lo:lo + LANES]
        invk = lax.rsqrt((jnp.sum(kn * kn, axis=-1, keepdims=True) + ssq_pe) * (1.0 / QK_DIM) + EPS)
        kt_ref[h * HEAD_SLOT:h * HEAD_SLOT + LANES, :] = (kn * gk_n * invk).T.astype(BF16)
        kt_ref[h * HEAD_SLOT + LANES:(h + 1) * HEAD_SLOT, :] = (k_rope * invk).T.astype(BF16)

    v_ref[...] = kvall[:, hn:].astype(BF16)


def mla_project(ub, g_cq, g_ckv, wq, wkv, gq3, gk3, cos_tab, sin_tab, *, geom):
    batch, seq, ctx = geom
    t_all = ub.shape[0]
    q_lora, kv_lora = g_cq.shape[0], g_ckv.shape[0]
    heads = MLA_HEADS
    tm = min(TILES["mla"], seq, ctx)
    n_x = batch * seq // tm
    per_seq = seq // tm

    def pos(m):
        return jnp.where(m < n_x, 1 + m % per_seq, 0)

    kern = functools.partial(_mla_kernel, q_lora=q_lora, kv_lora=kv_lora, heads=heads)
    return pl.pallas_call(
        kern,
        out_shape=(
            jax.ShapeDtypeStruct((t_all, heads * HEAD_SLOT), BF16),
            jax.ShapeDtypeStruct((heads * HEAD_SLOT, t_all), BF16),
            jax.ShapeDtypeStruct((t_all, heads * V_DIM), BF16),
        ),
        grid=(t_all // tm,),
        in_specs=[
            pl.BlockSpec((tm, ub.shape[1]), lambda m: (m, 0)),
            pl.BlockSpec((1, q_lora), lambda m: (0, 0)),
            pl.BlockSpec((1, kv_lora), lambda m: (0, 0)),
            pl.BlockSpec(wq.shape, lambda m: (0, 0)),
            pl.BlockSpec(wkv.shape, lambda m: (0, 0)),
            pl.BlockSpec((8, LANES), lambda m: (0, 0)),
            pl.BlockSpec((8, LANES), lambda m: (0, 0)),
            pl.BlockSpec((tm, LANES), lambda m: (pos(m), 0)),
            pl.BlockSpec((tm, LANES), lambda m: (pos(m), 0)),
        ],
        out_specs=(
            pl.BlockSpec((tm, heads * HEAD_SLOT), lambda m: (m, 0)),
            pl.BlockSpec((heads * HEAD_SLOT, tm), lambda m: (0, m)),
            pl.BlockSpec((tm, heads * V_DIM), lambda m: (m, 0)),
        ),
        compiler_params=_cparams(("parallel",)),
        name="mla_project",
    )(ub, g_cq.reshape(1, -1), g_ckv.reshape(1, -1), wq, wkv, gq3, gk3, cos_tab, sin_tab)


def _attn_kernel(q_ref, kx_ref, kc_ref, vx_ref, vc_ref, o_ref, *, n_qx, with_ctx_queries):
    q = q_ref[...]
    sc = jnp.dot(q, kc_ref[...], preferred_element_type=F32)

    def latent_queries():
        sx = jnp.dot(q, kx_ref[...], preferred_element_type=F32)
        m = jnp.maximum(jnp.max(sx, axis=-1, keepdims=True), jnp.max(sc, axis=-1, keepdims=True))
        px = jnp.exp(sx - m)
        pc = jnp.exp(sc - m)
        denom = jnp.sum(px, axis=-1, keepdims=True) + jnp.sum(pc, axis=-1, keepdims=True)
        o = jnp.dot(px.astype(BF16), vx_ref[...], preferred_element_type=F32)
        o = o + jnp.dot(pc.astype(BF16), vc_ref[...], preferred_element_type=F32)
        o_ref[...] = (o / denom).astype(o_ref.dtype)

    def context_queries():
        m = jnp.max(sc, axis=-1, keepdims=True)
        pc = jnp.exp(sc - m)
        denom = jnp.sum(pc, axis=-1, keepdims=True)
        o = jnp.dot(pc.astype(BF16), vc_ref[...], preferred_element_type=F32)
        o_ref[...] = (o / denom).astype(o_ref.dtype)

    if with_ctx_queries:
        i = pl.program_id(2)
        pl.when(i < n_qx)(latent_queries)
        pl.when(i >= n_qx)(context_queries)
    else:
        latent_queries()


def attention(q, kt, v, *, geom, with_ctx_queries):
    batch, seq, ctx = geom
    heads = MLA_HEADS
    t_all = q.shape[0]
    tq = min(TILES["tq"], ctx)
    n_qx = seq // tq
    n_qc = ctx // tq if with_ctx_queries else 0
    x_q_tiles = batch * seq // tq
    ctx_blk0 = batch * seq // ctx

    def qrow(b, h, i):
        return jnp.where(i < n_qx, b * n_qx + i, x_q_tiles + b * (ctx // tq) + (i - n_qx))

    kern = functools.partial(_attn_kernel, n_qx=n_qx, with_ctx_queries=with_ctx_queries)
    return pl.pallas_call(
        kern,
        out_shape=jax.ShapeDtypeStruct((t_all, heads * V_DIM), BF16),
        grid=(batch, heads, n_qx + n_qc),
        in_specs=[
            pl.BlockSpec((tq, HEAD_SLOT), lambda b, h, i: (qrow(b, h, i), h)),
            pl.BlockSpec((HEAD_SLOT, seq), lambda b, h, i: (h, b)),
            pl.BlockSpec((HEAD_SLOT, ctx), lambda b, h, i: (h, ctx_blk0 + b)),
            pl.BlockSpec((seq, V_DIM), lambda b, h, i: (b, h)),
            pl.BlockSpec((ctx, V_DIM), lambda b, h, i: (ctx_blk0 + b, h)),
        ],
        out_specs=pl.BlockSpec((tq, V_DIM), lambda b, h, i: (qrow(b, h, i), h)),
        compiler_params=_cparams(("parallel", "parallel", "arbitrary")),
        name="attention",
    )(q, kt, kt, v, v)


def _conv_kernel(*refs, width, conformer, tp, n_x_tiles, x_per_seq, c_per_seq):
    if conformer:
        (ap, bp, am, bm, an, bn, w_ref, lng_ref, lnb_ref, o_ref, buf, ybuf) = refs
        gate_ref = None
    else:
        (gate_ref, ap, bp, am, bm, an, bn, w_ref, o_ref, buf, ybuf) = refs
    nchunk = buf.shape[0]
    i = pl.program_id(0)
    in_x = i < n_x_tiles
    pos = jnp.where(in_x, i % x_per_seq, (i - n_x_tiles) % c_per_seq)
    per_seq = jnp.where(in_x, x_per_seq, c_per_seq)
    keep_prev = (pos > 0).astype(F32)
    keep_next = (pos < per_seq - 1).astype(F32)

    def pre(a, b):
        a = a.astype(F32)
        b = b.astype(F32)
        return a * _sigmoid(b) if conformer else a * b

    for c in range(nchunk):
        sl = slice(c * LANES, (c + 1) * LANES)
        buf[c, 0:HALO_ROWS, :] = pre(ap[:, sl], bp[:, sl]) * keep_prev
        buf[c, HALO_ROWS:HALO_ROWS + tp, :] = pre(am[:, sl], bm[:, sl])
        buf[c, HALO_ROWS + tp:2 * HALO_ROWS + tp, :] = pre(an[:, sl], bn[:, sl]) * keep_next

    rc = min(TILES["conv_rows"], tp)
    base = HALO_ROWS - width // 2

    def chunk(c, carry):
        wts = w_ref[c]
        for r0 in range(0, tp, rc):
            acc = jnp.zeros((rc, LANES), F32)
            for k in range(width):
                acc = acc + wts[k:k + 1, :] * buf[c, pl.ds(base + r0 + k, rc), :]
            ybuf[c, pl.ds(r0, rc), :] = acc
        return carry

    lax.fori_loop(0, nchunk, chunk, 0)

    if conformer:
        n_ch = nchunk * LANES
        tot = jnp.zeros((tp, 1), F32)
        for c in range(nchunk):
            tot = tot + jnp.sum(ybuf[c], axis=-1, keepdims=True)
        mu = tot * (1.0 / n_ch)
        tot2 = jnp.zeros((tp, 1), F32)
        for c in range(nchunk):
            dlt = ybuf[c] - mu
            tot2 = tot2 + jnp.sum(dlt * dlt, axis=-1, keepdims=True)
        inv = lax.rsqrt(tot2 * (1.0 / n_ch) + EPS)
        for c in range(nchunk):
            sl = slice(c * LANES, (c + 1) * LANES)
            y = (ybuf[c] - mu) * inv * lng_ref[:, sl] + lnb_ref[:, sl]
            o_ref[:, sl] = (y * _sigmoid(y)).astype(o_ref.dtype)
    else:
        for c in range(nchunk):
            sl = slice(c * LANES, (c + 1) * LANES)
            o_ref[:, sl] = (gate_ref[:, sl].astype(F32) * ybuf[c]).astype(o_ref.dtype)


def depthwise_branch(u, conv_w, *, col_a, col_b, col_gate, ln, n_rows, geom):
    batch, seq, ctx = geom
    width, ch = conv_w.shape
    conformer = ln is not None
    tp = min(TILES["conv"], seq, ctx)
    nchunk = ch // LANES
    hb = tp // HALO_ROWS
    last_halo = u.shape[0] // HALO_ROWS - 1
    wpad = -(-width // 8) * 8
    w3 = jnp.pad(conv_w.astype(F32), ((0, wpad - width), (0, 0))).reshape(wpad, nchunk, LANES).transpose(1, 0, 2)

    def prev_map(col):
        return lambda i: (jnp.maximum(i * hb - 1, 0), col)

    def main_map(col):
        return lambda i: (i, col)

    def next_map(col):
        return lambda i: (jnp.minimum((i + 1) * hb, last_halo), col)

    in_specs, args = [], []
    if not conformer:
        in_specs.append(pl.BlockSpec((tp, ch), main_map(col_gate)))
        args.append(u)
    for rows, mk in ((HALO_ROWS, prev_map), (tp, main_map), (HALO_ROWS, next_map)):
        for col in (col_a, col_b):
            in_specs.append(pl.BlockSpec((rows, ch), mk(col)))
            args.append(u)
    in_specs.append(pl.BlockSpec((nchunk, wpad, LANES), lambda i: (0, 0, 0)))
    args.append(w3)
    if conformer:
        for p in ln:
            in_specs.append(pl.BlockSpec((1, ch), lambda i: (0, 0)))
            args.append(p.reshape(1, ch).astype(F32))

    kern = functools.partial(
        _conv_kernel, width=width, conformer=conformer, tp=tp, n_x_tiles=batch * seq // tp,
        x_per_seq=seq // tp, c_per_seq=ctx // tp)
    return pl.pallas_call(
        kern,
        out_shape=jax.ShapeDtypeStruct((n_rows, ch), BF16),
        grid=(n_rows // tp,),
        in_specs=in_specs,
        out_specs=pl.BlockSpec((tp, ch), lambda i: (i, 0)),
        scratch_shapes=[pltpu.VMEM((nchunk, tp + 2 * HALO_ROWS, LANES), F32),
                        pltpu.VMEM((nchunk, tp, LANES), F32)],
        compiler_params=_cparams(("parallel",)),
        name="conformer_conv" if conformer else "short_conv",
    )(*args)


def _merge_kernel(ha_ref, ob_ref, hc_ref, wa_ref, wb_ref, wc_ref, ga_ref, gb_ref, gc_ref, o_ref):
    ya = jnp.dot(ha_ref[...], wa_ref[...], preferred_element_type=F32)
    yb = jnp.dot(ob_ref[...], wb_ref[...], preferred_element_type=F32)
    yc = jnp.dot(hc_ref[...], wc_ref[...], preferred_element_type=F32)
    m = (_sigmoid(ga_ref[...].astype(F32)) * ya + _sigmoid(gb_ref[...].astype(F32)) * yb
         + _sigmoid(gc_ref[...].astype(F32)) * yc)
    o_ref[...] = m.astype(o_ref.dtype)


def branch_merge(ha, ob, hc, wa, wb, wc, u, *, n_rows, geom):
    batch, seq, _ = geom
    d = wa.shape[1]
    tm = min(TILES["row"], seq)
    tn = min(TILES["col"], d)
    ncol = d // tn
    row_spec = lambda a: pl.BlockSpec((tm, a.shape[1]), lambda m, j: (m, 0))
    w_spec = lambda a: pl.BlockSpec((a.shape[0], tn), lambda m, j: (0, j))
    gate_spec = lambda k: pl.BlockSpec((tm, tn), lambda m, j: (m, k * ncol + j))
    return pl.pallas_call(
        _merge_kernel,
        out_shape=jax.ShapeDtypeStruct((n_rows, d), BF16),
        grid=(n_rows // tm, ncol),
        in_specs=[row_spec(ha), row_spec(ob), row_spec(hc), w_spec(wa), w_spec(wb), w_spec(wc),
                  gate_spec(0), gate_spec(1), gate_spec(2)],
        out_specs=pl.BlockSpec((tm, tn), lambda m, j: (m, j)),
        compiler_params=_cparams(("parallel", "arbitrary")),
        name="branch_merge",
    )(ha, ob, hc, wa, wb, wc, u, u, u)


def _outproj_kernel(m_ref, w_ref, x_ref, gate_ref, o_ref):
    y = jnp.dot(m_ref[...], w_ref[...], preferred_element_type=F32)
    o_ref[...] = x_ref[...] + gate_ref[...] * y


def out_proj_residual(mrg, w_o, xall, mod3, *, layer, gate_idx, n_rows, geom):
    batch, seq, _ = geom
    d = w_o.shape[1]
    tm = min(TILES["row"], seq)
    tn = min(TILES["col"], d)
    ncol = d // tn
    mrow = _mod_row_fn(batch * seq // tm, seq // tm, batch, layer)
    return pl.pallas_call(
        _outproj_kernel,
        out_shape=jax.ShapeDtypeStruct((n_rows, d), F32),
        grid=(n_rows // tm, ncol),
        in_specs=[
            pl.BlockSpec((tm, mrg.shape[1]), lambda m, j: (m, 0)),
            pl.BlockSpec((w_o.shape[0], tn), lambda m, j: (0, j)),
            pl.BlockSpec((tm, tn), lambda m, j: (m, j)),
            pl.BlockSpec((None, 1, tn), lambda m, j: (mrow(m), 0, gate_idx * ncol + j)),
        ],
        out_specs=pl.BlockSpec((tm, tn), lambda m, j: (m, j)),
        compiler_params=_cparams(("parallel", "arbitrary")),
        name="out_proj_residual",
    )(mrg, w_o, xall, mod3)


def _router_kernel(x_ref, g_ref, shift_ref, scale_ref, wr_ref, bias_ref, h_ref, ids_ref, wts_ref,
                   *, per_group):
    h = _norm_mod(x_ref[...], g_ref[...], shift_ref[...], scale_ref[...])
    h_ref[...] = h.astype(h_ref.dtype)
    h_hi = h.astype(BF16)
    h_lo = (h - h_hi.astype(F32)).astype(BF16)
    w = wr_ref[...]
    w_hi = w.astype(BF16)
    w_lo = (w - w_hi.astype(F32)).astype(BF16)
    logits = (jnp.dot(h_hi, w_hi, preferred_element_type=F32)
              + jnp.dot(h_hi, w_lo, preferred_element_type=F32)
              + jnp.dot(h_lo, w_hi, preferred_element_type=F32))
    lane_i = lax.broadcasted_iota(jnp.int32, (h.shape[0], LANES), 1)
    lane = lane_i.astype(F32)
    valid = lane_i < N_GROUPS
    neg = jnp.float32(-jnp.inf)
    score, sel = [], []
    for j in range(per_group):
        s = _sigmoid(logits[:, j * LANES:(j + 1) * LANES])
        score.append(s)
        sel.append(jnp.where(valid, s + bias_ref[j:j + 1, :], neg))
    assert per_group == 4
    hi1, lo1 = jnp.maximum(sel[0], sel[1]), jnp.minimum(sel[0], sel[1])
    hi2, lo2 = jnp.maximum(sel[2], sel[3]), jnp.minimum(sel[2], sel[3])
    top1 = jnp.maximum(hi1, hi2)
    top2 = jnp.maximum(jnp.minimum(hi1, hi2), jnp.maximum(lo1, lo2))
    gscore = jnp.where(valid, top1 + top2, neg)
    gmax = jnp.max(gscore, axis=-1, keepdims=True)
    grp = jnp.min(jnp.where(gscore == gmax, lane, float(LANES)), axis=-1, keepdims=True)
    pick = lane == grp
    zero = jnp.float32(0.0)
    sel_g = [jnp.sum(jnp.where(pick, s, zero), axis=-1, keepdims=True) for s in sel]
    sc_g = [jnp.sum(jnp.where(pick, s, zero), axis=-1, keepdims=True) for s in score]
    best, best_i, best_s = sel_g[0], jnp.zeros_like(grp), sc_g[0]
    for j in range(1, per_group):
        better = sel_g[j] > best
        best = jnp.where(better, sel_g[j], best)
        best_i = jnp.where(better, float(j), best_i)
        best_s = jnp.where(better, sc_g[j], best_s)
    second, second_i, second_s = jnp.full_like(best, neg), jnp.zeros_like(grp), jnp.zeros_like(best)
    for j in range(per_group):
        better = (sel_g[j] > second) & (best_i != float(j))
        second = jnp.where(better, sel_g[j], second)
        second_i = jnp.where(better, float(j), second_i)
        second_s = jnp.where(better, sc_g[j], second_s)
    tot = best_s + second_s
    e0 = grp * per_group + best_i
    e1 = grp * per_group + second_i
    ids_ref[...] = jnp.where(lane_i == 0, e0, jnp.where(lane_i == 1, e1, zero)).astype(jnp.int32)
    wts_ref[...] = jnp.where(lane_i == 0, best_s / tot, jnp.where(lane_i == 1, second_s / tot, zero))


def router(xall, g, mod3, wr_pad, bias_pad, *, layer, shift_idx, n_rows, geom, per_group):
    batch, seq, _ = geom
    d = xall.shape[1]
    tm = min(TILES["router"], seq)
    mrow = _mod_row_fn(batch * seq // tm, seq // tm, batch, layer)
    kern = functools.partial(_router_kernel, per_group=per_group)
    return pl.pallas_call(
        kern,
        out_shape=(jax.ShapeDtypeStruct((n_rows, d), BF16),
                   jax.ShapeDtypeStruct((n_rows, LANES), jnp.int32),
                   jax.ShapeDtypeStruct((n_rows, LANES), F32)),
        grid=(n_rows // tm,),
        in_specs=[
            pl.BlockSpec((tm, d), lambda m: (m, 0)),
            pl.BlockSpec((1, d), lambda m: (0, 0)),
            pl.BlockSpec((None, 1, d), lambda m: (mrow(m), 0, shift_idx)),
            pl.BlockSpec((None, 1, d), lambda m: (mrow(m), 0, shift_idx + 1)),
            pl.BlockSpec(wr_pad.shape, lambda m: (0, 0)),
            pl.BlockSpec(bias_pad.shape, lambda m: (0, 0)),
        ],
        out_specs=(pl.BlockSpec((tm, d), lambda m: (m, 0)),
                   pl.BlockSpec((tm, LANES), lambda m: (m, 0)),
                   pl.BlockSpec((tm, LANES), lambda m: (m, 0))),
        compiler_params=_cparams(("parallel",)),
        name="router",
    )(xall, g.reshape(1, d), mod3, mod3, wr_pad, bias_pad)


def _expert_kernel(be_ref, x_ref, win_ref, wout_ref, o_ref, *, ff):
    gu = jnp.dot(x_ref[...], win_ref[...], preferred_element_type=F32)
    g = gu[:, :ff]
    act = (g * _sigmoid(g) * gu[:, ff:]).astype(BF16)
    o_ref[...] = jnp.dot(act, wout_ref[...], preferred_element_type=F32).astype(o_ref.dtype)


def expert_ffn(xs, block_e, w_in, w_out):
    rows, d = xs.shape
    blk = TILES["moe_block"]
    ff = w_out.shape[1]
    kern = functools.partial(_expert_kernel, ff=ff)
    grid_spec = pltpu.PrefetchScalarGridSpec(
        num_scalar_prefetch=1,
        grid=(rows // blk,),
        in_specs=[
            pl.BlockSpec((blk, d), lambda i, be: (i, 0)),
            pl.BlockSpec((None, d, 2 * ff), lambda i, be: (be[i], 0, 0)),
            pl.BlockSpec((None, ff, d), lambda i, be: (be[i], 0, 0)),
        ],
        out_specs=pl.BlockSpec((blk, d), lambda i, be: (i, 0)),
    )
    return pl.pallas_call(
        kern,
        out_shape=jax.ShapeDtypeStruct((rows, d), BF16),
        grid_spec=grid_spec,
        compiler_params=_cparams(("arbitrary",)),
        name="expert_ffn",
    )(block_e, xs, w_in, w_out)


def _combine_kernel(x_ref, y0_ref, y1_ref, w_ref, gate_ref, o_ref):
    w = w_ref[...]
    y = w[:, 0:1] * y0_ref[...].astype(F32) + w[:, 1:2] * y1_ref[...].astype(F32)
    o_ref[...] = x_ref[...] + gate_ref[...] * y


def moe_combine(xall, y0, y1, wts, mod3, *, layer, gate_idx, n_rows, geom):
    batch, seq, _ = geom
    d = xall.shape[1]
    tm = min(TILES["row"], seq)
    mrow = _mod_row_fn(batch * seq // tm, seq // tm, batch, layer)
    return pl.pallas_call(
        _combine_kernel,
        out_shape=jax.ShapeDtypeStruct((n_rows, d), F32),
        grid=(n_rows // tm,),
        in_specs=[
            pl.BlockSpec((tm, d), lambda m: (m, 0)),
            pl.BlockSpec((tm, d), lambda m: (m, 0)),
            pl.BlockSpec((tm, d), lambda m: (m, 0)),
            pl.BlockSpec((tm, LANES), lambda m: (m, 0)),
            pl.BlockSpec((None, 1, d), lambda m: (mrow(m), 0, gate_idx)),
        ],
        out_specs=pl.BlockSpec((tm, d), lambda m: (m, 0)),
        compiler_params=_cparams(("parallel",)),
        name="moe_combine",
    )(xall, y0, y1, wts, mod3)


def moe_dispatch(ids, n_experts):
    blk = TILES["moe_block"]
    n = ids.shape[0]
    a = n * TOP_K
    flat_e = ids.reshape(-1)
    onehot = (flat_e[:, None] == jnp.arange(n_experts, dtype=jnp.int32)[None, :]).astype(jnp.int32)
    rank = jnp.sum((jnp.cumsum(onehot, axis=0) - onehot) * onehot, axis=1)
    counts = jnp.sum(onehot, axis=0)
    padded = (counts + blk - 1) // blk * blk
    pend = jnp.cumsum(padded)
    pstart = pend - padded
    dest = pstart[flat_e] + rank
    n_blocks = -(-a // blk) + n_experts
    rows = n_blocks * blk
    row_tok = jnp.zeros((rows,), jnp.int32).at[dest].set(jnp.arange(a, dtype=jnp.int32) // TOP_K)
    block_start = jnp.arange(n_blocks, dtype=jnp.int32) * blk
    block_e = jnp.minimum(jnp.searchsorted(pend, block_start, side="right"), n_experts - 1).astype(jnp.int32)
    return row_tok, dest.reshape(n, TOP_K), block_e


def _rope_tables(seq, tm):
    inv = ROPE_THETA ** (-jnp.arange(0, AXIS_DIM, 2, dtype=F32) / AXIS_DIM)
    t = jnp.arange(seq)
    ang_r = (t // GRID_W).astype(F32)[:, None] * inv
    ang_c = (t % GRID_W).astype(F32)[:, None] * inv
    cos = jnp.concatenate([jnp.cos(ang_r), jnp.cos(ang_r), jnp.cos(ang_c), jnp.cos(ang_c)], axis=-1)
    sin = jnp.concatenate([-jnp.sin(ang_r), jnp.sin(ang_r), -jnp.sin(ang_c), jnp.sin(ang_c)], axis=-1)
    pad = LANES - ROPE_DIM
    cos = jnp.pad(cos, ((0, 0), (0, pad)))
    sin = jnp.pad(sin, ((0, 0), (0, pad)))
    ident_c = jnp.pad(jnp.ones((tm, ROPE_DIM), F32), ((0, 0), (0, pad)))
    return (jnp.concatenate([ident_c, cos], axis=0),
            jnp.concatenate([jnp.zeros((tm, LANES), F32), sin], axis=0))


def _partner():
    j = jnp.arange(ROPE_DIM)
    half = AXIS_DIM // 2
    return jnp.where((j % AXIS_DIM) < half, j + half, j - half)


def _pad_lanes(a):
    return jnp.pad(a, [(0, 0)] * (a.ndim - 1) + [(0, LANES - a.shape[-1])])


def _norm_gain_rows(g):
    rope = g[NOPE_DIM:]
    rows = jnp.stack([g[:NOPE_DIM], _pad_lanes(rope), _pad_lanes(rope[_partner()])])
    return jnp.pad(rows, ((0, 8 - rows.shape[0]), (0, 0))).astype(F32)


def _layer_weights(l, w_in, w_uq, w_ukv, q_lora, kv_lora):
    d = w_in.shape[1]
    heads = MLA_HEADS
    wl = w_in[l]
    gate_cols = 3 * d
    conv_a2 = (wl.shape[1] - gate_cols - (q_lora + kv_lora + ROPE_DIM)) // 5 * 2
    off_b = gate_cols + conv_a2
    off_c = off_b + q_lora + kv_lora + ROPE_DIM
    w_main = jnp.concatenate([wl[:, :off_b], wl[:, off_c:]], axis=1).astype(BF16)
    kpe = wl[:, off_b + q_lora + kv_lora:off_c]
    w_b = jnp.concatenate([wl[:, off_b:off_b + q_lora + kv_lora], _pad_lanes(kpe),
                           _pad_lanes(kpe[:, _partner()])], axis=1).astype(BF16)
    wq = w_uq[l].reshape(q_lora, heads, QK_DIM)
    wq_n = wq[:, :, :NOPE_DIM].reshape(q_lora, heads * LANES)
    wq_r = _pad_lanes(wq[:, :, NOPE_DIM:]).reshape(q_lora, heads * LANES)
    wq_s = _pad_lanes(wq[:, :, NOPE_DIM:][:, :, _partner()]).reshape(q_lora, heads * LANES)
    wq_all = jnp.concatenate([wq_n, wq_r, wq_s], axis=1).astype(BF16)
    wkv = w_ukv[l].reshape(kv_lora, heads, NOPE_DIM + V_DIM)
    wkv_all = jnp.concatenate([wkv[:, :, :NOPE_DIM].reshape(kv_lora, heads * NOPE_DIM),
                               wkv[:, :, NOPE_DIM:].reshape(kv_lora, heads * V_DIM)], axis=1).astype(BF16)
    return w_main, w_b, wq_all, wkv_all


def _router_layout(w_router, router_bias):
    d, n_experts = w_router.shape
    per_group = n_experts // N_GROUPS
    wr = w_router.reshape(d, N_GROUPS, per_group).transpose(0, 2, 1)
    wr_pad = _pad_lanes(wr).reshape(d, per_group * LANES).astype(F32)
    bias = _pad_lanes(router_bias.reshape(N_GROUPS, per_group).T).astype(F32)
    bias = jnp.pad(bias, ((0, 8 - per_group), (0, 0)))
    return wr_pad, bias, per_group


def kernel(x, c, ctx, c_ctx, w_mod, b_mod, g_norm1, g_norm2, w_in, conv_a, ln_a_g, ln_a_b, w_a_out, g_cq, g_ckv, w_uq, w_ukv, g_q, g_k, w_b_out, conv_c, w_c_out, w_o, w_router, router_bias, w_e_in, w_e_out):
    batch, seq, d = x.shape
    n_ctx = ctx.shape[1]
    depth = w_mod.shape[0]
    geom = (batch, seq, n_ctx)
    t_x = batch * seq
    t_all = t_x + batch * n_ctx
    q_lora, kv_lora = g_cq.shape[1], g_ckv.shape[1]
    ch_a, ch_c = conv_a.shape[2], conv_c.shape[2]
    n_experts = w_router.shape[1]
    assert batch + 1 <= MOD_ROWS and t_x % n_ctx == 0 and ch_a == ch_c and (3 * d) % ch_a == 0

    xall = jnp.concatenate([x.reshape(t_x, d), ctx.reshape(batch * n_ctx, d)], axis=0)
    cvec = jnp.concatenate([c, c_ctx[None, :], jnp.zeros((MOD_ROWS - batch - 1, d), F32)], axis=0)
    mod3 = modulation(cvec, w_mod, b_mod).reshape(depth * MOD_ROWS, 1, N_MOD * d)

    cos_tab, sin_tab = _rope_tables(seq, min(TILES["mla"], seq, n_ctx))
    wr_pad, bias_pad, per_group = _router_layout(w_router, router_bias)
    col0 = 3 * d // ch_a

    for l in range(depth):
        last = l == depth - 1
        rows_main = t_x if last else t_all
        w_main, w_b, wq_all, wkv_all = _layer_weights(l, w_in, w_uq, w_ukv, q_lora, kv_lora)

        u = norm_proj(xall, g_norm1[l], mod3, w_main, layer=l, shift_idx=0, n_rows=rows_main, geom=geom)
        ub = norm_proj(xall, g_norm1[l], mod3, w_b, layer=l, shift_idx=0, n_rows=t_all, geom=geom)
        q, kt, v = mla_project(ub, g_cq[l], g_ckv[l], wq_all, wkv_all, _norm_gain_rows(g_q[l]),
                               _norm_gain_rows(g_k[l]), cos_tab, sin_tab, geom=geom)
        ob = attention(q, kt, v, geom=geom, with_ctx_queries=not last)
        ha = depthwise_branch(u, conv_a[l], col_a=col0, col_b=col0 + 1, col_gate=None,
                              ln=(ln_a_g[l], ln_a_b[l]), n_rows=rows_main, geom=geom)
        hc = depthwise_branch(u, conv_c[l], col_a=col0 + 3, col_b=col0 + 4, col_gate=col0 + 2,
                              ln=None, n_rows=rows_main, geom=geom)
        mrg = branch_merge(ha, ob, hc, w_a_out[l].astype(BF16), w_b_out[l].astype(BF16),
                           w_c_out[l].astype(BF16), u, n_rows=rows_main, geom=geom)
        x1 = out_proj_residual(mrg, w_o[l].astype(BF16), xall, mod3, layer=l, gate_idx=2,
                               n_rows=rows_main, geom=geom)

        h2, ids, wts = router(x1, g_norm2[l], mod3, wr_pad, bias_pad, layer=l, shift_idx=3,
                              n_rows=rows_main, geom=geom, per_group=per_group)
        row_tok, dest, block_e = moe_dispatch(ids[:, :TOP_K], n_experts)
        ys = expert_ffn(jnp.take(h2, row_tok, axis=0), block_e, w_e_in[l].astype(BF16),
                        w_e_out[l].astype(BF16))
        y0 = jnp.take(ys, dest[:, 0], axis=0)
        y1 = jnp.take(ys, dest[:, 1], axis=0)
        xall = moe_combine(x1, y0, y1, wts, mod3, layer=l, gate_idx=5, n_rows=rows_main, geom=geom)

    return xall[:t_x].reshape(batch, seq, d)
```

```python
import functools
import math

import jax
import jax.numpy as jnp
from jax import lax
from jax.experimental import pallas as pl
from jax.experimental.pallas import tpu as pltpu

F32 = jnp.float32
BF16 = jnp.bfloat16

GRID_W = 64
MLA_HEADS = 8
NOPE_DIM = 128
ROPE_DIM = 64
V_DIM = 128
QK_DIM = NOPE_DIM + ROPE_DIM
AXIS_DIM = ROPE_DIM // 2
ROPE_THETA = 10000.0
N_GROUPS = 8
TOP_K = 2
N_MOD = 6
EPS = 1e-6

LANES = 128
HEAD_SLOT = 2 * LANES
HALO_ROWS = 16
MOD_ROWS = 8

TILES = dict(
    row=512,
    col=1024,
    mod_col=1024,
    mla=256,
    tq=256,
    conv=256,
    conv_rows=64,
    router=512,
    moe_block=256,
    vmem=56 * 1024 * 1024,
    vmem_expert=60 * 1024 * 1024,
)


def _cparams(sem):
    return pltpu.CompilerParams(dimension_semantics=sem, vmem_limit_bytes=TILES["vmem"])


def _sigmoid(x):
    return 1.0 / (1.0 + jnp.exp(-x))


def _mod_kernel(c_ref, w_ref, b_ref, o_ref):
    c = c_ref[...]
    s = (c * _sigmoid(c)).astype(BF16)
    o_ref[...] = jnp.dot(s, w_ref[...].astype(BF16), preferred_element_type=F32) + b_ref[...]


def modulation(cvec, w_mod, b_mod):
    n_layers, d, n = w_mod.shape
    tn = min(TILES["mod_col"], n)
    return pl.pallas_call(
        _mod_kernel,
        out_shape=jax.ShapeDtypeStruct((n_layers, MOD_ROWS, n), F32),
        grid=(n_layers, n // tn),
        in_specs=[
            pl.BlockSpec((MOD_ROWS, d), lambda l, j: (0, 0)),
            pl.BlockSpec((None, d, tn), lambda l, j: (l, 0, j)),
            pl.BlockSpec((None, 1, tn), lambda l, j: (l, 0, j)),
        ],
        out_specs=pl.BlockSpec((None, MOD_ROWS, tn), lambda l, j: (l, 0, j)),
        compiler_params=_cparams(("parallel", "parallel")),
        name="modulation",
    )(cvec, w_mod, b_mod.reshape(n_layers, 1, n))


def _norm_mod(x, g, shift, scale):
    ms = jnp.mean(x * x, axis=-1, keepdims=True)
    return x * lax.rsqrt(ms + EPS) * g * (1.0 + scale) + shift


def _normproj_kernel(x_ref, g_ref, shift_ref, scale_ref, w_ref, o_ref, h_ref):
    @pl.when(pl.program_id(1) == 0)
    def _():
        h_ref[...] = _norm_mod(x_ref[...], g_ref[...], shift_ref[...], scale_ref[...]).astype(BF16)

    o_ref[...] = jnp.dot(h_ref[...], w_ref[...], preferred_element_type=F32).astype(o_ref.dtype)


def _mod_row_fn(n_x_tiles, tiles_per_seq, batch, layer):
    def f(m):
        return layer * MOD_ROWS + jnp.where(m < n_x_tiles, m // tiles_per_seq, batch)
    return f


def norm_proj(xall, g, mod3, w, *, layer, shift_idx, n_rows, geom):
    batch, seq, _ = geom
    d = xall.shape[1]
    n = w.shape[1]
    tm = min(TILES["row"], seq)
    tn = min(TILES["col"], n)
    if n % tn:
        tn = n
    mrow = _mod_row_fn(batch * seq // tm, seq // tm, batch, layer)
    return pl.pallas_call(
        _normproj_kernel,
        out_shape=jax.ShapeDtypeStruct((n_rows, n), BF16),
        grid=(n_rows // tm, n // tn),
        in_specs=[
            pl.BlockSpec((tm, d), lambda m, j: (m, 0)),
            pl.BlockSpec((1, d), lambda m, j: (0, 0)),
            pl.BlockSpec((None, 1, d), lambda m, j: (mrow(m), 0, shift_idx)),
            pl.BlockSpec((None, 1, d), lambda m, j: (mrow(m), 0, shift_idx + 1)),
            pl.BlockSpec((d, tn), lambda m, j: (0, j)),
        ],
        out_specs=pl.BlockSpec((tm, tn), lambda m, j: (m, j)),
        scratch_shapes=[pltpu.VMEM((tm, d), BF16)],
        compiler_params=_cparams(("parallel", "arbitrary")),
        name="norm_proj",
    )(xall, g.reshape(1, d), mod3, mod3, w)


def _mla_kernel(ub_ref, gcq_ref, gckv_ref, wq_ref, wkv_ref, gq_ref, gk_ref, cos_ref, sin_ref,
                q_ref, kt_ref, v_ref, *, q_lora, kv_lora, heads):
    hn = heads * LANES

    def rms(xv, g):
        return (xv * lax.rsqrt(jnp.mean(xv * xv, axis=-1, keepdims=True) + EPS) * g).astype(BF16)

    cqn = rms(ub_ref[:, 0:q_lora].astype(F32), gcq_ref[...])
    ckvn = rms(ub_ref[:, q_lora:q_lora + kv_lora].astype(F32), gckv_ref[...])
    off = q_lora + kv_lora
    kpe = ub_ref[:, off:off + LANES].astype(F32)
    kpes = ub_ref[:, off + LANES:off + 2 * LANES].astype(F32)

    qall = jnp.dot(cqn, wq_ref[...], preferred_element_type=F32)
    kvall = jnp.dot(ckvn, wkv_ref[...], preferred_element_type=F32)

    cos = cos_ref[...]
    sin = sin_ref[...]
    gq_n, gq_r, gq_s = gq_ref[0:1, :], gq_ref[1:2, :], gq_ref[2:3, :]
    gk_n, gk_r, gk_s = gk_ref[0:1, :], gk_ref[1:2, :], gk_ref[2:3, :]
    q_cos, q_sin = gq_r * cos, gq_s * sin
    k_rope = kpe * (gk_r * cos) + kpes * (gk_s * sin)
    ssq_pe = jnp.sum(kpe * kpe, axis=-1, keepdims=True)
    attn_scale = QK_DIM ** -0.5

    for h in range(heads):
        lo = h * LANES
        qn = qall[:, lo:lo + LANES]
        qr = qall[:, hn + lo:hn + lo + LANES]
        qs = qall[:, 2 * hn + lo:2 * hn + lo + LANES]
        ssq = jnp.sum(qn * qn, axis=-1, keepdims=True) + jnp.sum(qr * qr, axis=-1, keepdims=True)
        inv = lax.rsqrt(ssq * (1.0 / QK_DIM) + EPS) * attn_scale
        q_ref[:, h * HEAD_SLOT:h * HEAD_SLOT + LANES] = (qn * gq_n * inv).astype(BF16)
        q_ref[:, h * HEAD_SLOT + LANES:(h + 1) * HEAD_SLOT] = ((qr * q_cos + qs * q_sin) * inv).astype(BF16)

        kn = kvall[:, lo:lo + LANES]
        invk = lax.rsqrt((jnp.sum(kn * kn, axis=-1, keepdims=True) + ssq_pe) * (1.0 / QK_DIM) + EPS)
        kt_ref[h * HEAD_SLOT:h * HEAD_SLOT + LANES, :] = (kn * gk_n * invk).T.astype(BF16)
        kt_ref[h * HEAD_SLOT + LANES:(h + 1) * HEAD_SLOT, :] = (k_rope * invk).T.astype(BF16)

    v_ref[...] = kvall[:, hn:].astype(BF16)


def mla_project(ub, g_cq, g_ckv, wq, wkv, gq3, gk3, cos_tab, sin_tab, *, geom):
    batch, seq, ctx = geom
    t_all = ub.shape[0]
    q_lora, kv_lora = g_cq.shape[0], g_ckv.shape[0]
    heads = MLA_HEADS
    tm = min(TILES["mla"], seq, ctx)
    n_x = batch * seq // tm
    per_seq = seq // tm

    def pos(m):
        return jnp.where(m < n_x, 1 + m % per_seq, 0)

    kern = functools.partial(_mla_kernel, q_lora=q_lora, kv_lora=kv_lora, heads=heads)
    return pl.pallas_call(
        kern,
        out_shape=(
            jax.ShapeDtypeStruct((t_all, heads * HEAD_SLOT), BF16),
            jax.ShapeDtypeStruct((heads * HEAD_SLOT, t_all), BF16),
            jax.ShapeDtypeStruct((t_all, heads * V_DIM), BF16),
        ),
        grid=(t_all // tm,),
        in_specs=[
            pl.BlockSpec((tm, ub.shape[1]), lambda m: (m, 0)),
            pl.BlockSpec((1, q_lora), lambda m: (0, 0)),
            pl.BlockSpec((1, kv_lora), lambda m: (0, 0)),
            pl.BlockSpec(wq.shape, lambda m: (0, 0)),
            pl.BlockSpec(wkv.shape, lambda m: (0, 0)),
            pl.BlockSpec((8, LANES), lambda m: (0, 0)),
            pl.BlockSpec((8, LANES), lambda m: (0, 0)),
            pl.BlockSpec((tm, LANES), lambda m: (pos(m), 0)),
            pl.BlockSpec((tm, LANES), lambda m: (pos(m), 0)),
        ],
        out_specs=(
            pl.BlockSpec((tm, heads * HEAD_SLOT), lambda m: (m, 0)),
            pl.BlockSpec((heads * HEAD_SLOT, tm), lambda m: (0, m)),
            pl.BlockSpec((tm, heads * V_DIM), lambda m: (m, 0)),
        ),
        compiler_params=_cparams(("parallel",)),
        name="mla_project",
    )(ub, g_cq.reshape(1, -1), g_ckv.reshape(1, -1), wq, wkv, gq3, gk3, cos_tab, sin_tab)


def _attn_kernel(q_ref, kx_ref, kc_ref, vx_ref, vc_ref, o_ref, *, n_qx, with_ctx_queries):
    q = q_ref[...]
    sc = jnp.dot(q, kc_ref[...], preferred_element_type=F32)

    def latent_queries():
        sx = jnp.dot(q, kx_ref[...], preferred_element_type=F32)
        m = jnp.maximum(jnp.max(sx, axis=-1, keepdims=True), jnp.max(sc, axis=-1, keepdims=True))
        px = jnp.exp(sx - m)
        pc = jnp.exp(sc - m)
        denom = jnp.sum(px, axis=-1, keepdims=True) + jnp.sum(pc, axis=-1, keepdims=True)
        o = jnp.dot(px.astype(BF16), vx_ref[...], preferred_element_type=F32)
        o = o + jnp.dot(pc.astype(BF16), vc_ref[...], preferred_element_type=F32)
        o_ref[...] = (o / denom).astype(o_ref.dtype)

    def context_queries():
        m = jnp.max(sc, axis=-1, keepdims=True)
        pc = jnp.exp(sc - m)
        denom = jnp.sum(pc, axis=-1, keepdims=True)
        o = jnp.dot(pc.astype(BF16), vc_ref[...], preferred_element_type=F32)
        o_ref[...] = (o / denom).astype(o_ref.dtype)

    if with_ctx_queries:
        i = pl.program_id(2)
        pl.when(i < n_qx)(latent_queries)
        pl.when(i >= n_qx)(context_queries)
    else:
        latent_queries()


def attention(q, kt, v, *, geom, with_ctx_queries):
    batch, seq, ctx = geom
    heads = MLA_HEADS
    t_all = q.shape[0]
    tq = min(TILES["tq"], ctx)
    n_qx = seq // tq
    n_qc = ctx // tq if with_ctx_queries else 0
    x_q_tiles = batch * seq // tq
    ctx_blk0 = batch * seq // ctx

    def qrow(b, h, i):
        return jnp.where(i < n_qx, b * n_qx + i, x_q_tiles + b * (ctx // tq) + (i - n_qx))

    kern = functools.partial(_attn_kernel, n_qx=n_qx, with_ctx_queries=with_ctx_queries)
    return pl.pallas_call(
        kern,
        out_shape=jax.ShapeDtypeStruct((t_all, heads * V_DIM), BF16),
        grid=(batch, heads, n_qx + n_qc),
        in_specs=[
            pl.BlockSpec((tq, HEAD_SLOT), lambda b, h, i: (qrow(b, h, i), h)),
            pl.BlockSpec((HEAD_SLOT, seq), lambda b, h, i: (h, b)),
            pl.BlockSpec((HEAD_SLOT, ctx), lambda b, h, i: (h, ctx_blk0 + b)),
            pl.BlockSpec((seq, V_DIM), lambda b, h, i: (b, h)),
            pl.BlockSpec((ctx, V_DIM), lambda b, h, i: (ctx_blk0 + b, h)),
        ],
        out_specs=pl.BlockSpec((tq, V_DIM), lambda b, h, i: (qrow(b, h, i), h)),
        compiler_params=_cparams(("parallel", "parallel", "arbitrary")),
        name="attention",
    )(q, kt, kt, v, v)


def _conv_kernel(*refs, width, conformer, tp, n_x_tiles, x_per_seq, c_per_seq):
    if conformer:
        (ap, bp, am, bm, an, bn, w_ref, lng_ref, lnb_ref, o_ref, buf, ybuf) = refs
        gate_ref = None
    else:
        (gate_ref, ap, bp, am, bm, an, bn, w_ref, o_ref, buf, ybuf) = refs
    nchunk = buf.shape[0]
    i = pl.program_id(0)
    in_x = i < n_x_tiles
    pos = jnp.where(in_x, i % x_per_seq, (i - n_x_tiles) % c_per_seq)
    per_seq = jnp.where(in_x, x_per_seq, c_per_seq)
    keep_prev = (pos > 0).astype(F32)
    keep_next = (pos < per_seq - 1).astype(F32)

    def pre(a, b):
        a = a.astype(F32)
        b = b.astype(F32)
        return a * _sigmoid(b) if conformer else a * b

    for c in range(nchunk):
        sl = slice(c * LANES, (c + 1) * LANES)
        buf[c, 0:HALO_ROWS, :] = pre(ap[:, sl], bp[:, sl]) * keep_prev
        buf[c, HALO_ROWS:HALO_ROWS + tp, :] = pre(am[:, sl], bm[:, sl])
        buf[c, HALO_ROWS + tp:2 * HALO_ROWS + tp, :] = pre(an[:, sl], bn[:, sl]) * keep_next

    rc = min(TILES["conv_rows"], tp)
    base = HALO_ROWS - width // 2

    def chunk(c, carry):
        wts = w_ref[c]
        for r0 in range(0, tp, rc):
            acc = jnp.zeros((rc, LANES), F32)
            for k in range(width):
                acc = acc + wts[k:k + 1, :] * buf[c, pl.ds(base + r0 + k, rc), :]
            ybuf[c, pl.ds(r0, rc), :] = acc
        return carry

    lax.fori_loop(0, nchunk, chunk, 0)

    if conformer:
        n_ch = nchunk * LANES
        tot = jnp.zeros((tp, 1), F32)
        for c in range(nchunk):
            tot = tot + jnp.sum(ybuf[c], axis=-1, keepdims=True)
        mu = tot * (1.0 / n_ch)
        tot2 = jnp.zeros((tp, 1), F32)
        for c in range(nchunk):
            dlt = ybuf[c] - mu
            tot2 = tot2 + jnp.sum(dlt * dlt, axis=-1, keepdims=True)
        inv = lax.rsqrt(tot2 * (1.0 / n_ch) + EPS)
        for c in range(nchunk):
            sl = slice(c * LANES, (c + 1) * LANES)
            y = (ybuf[c] - mu) * inv * lng_ref[:, sl] + lnb_ref[:, sl]
            o_ref[:, sl] = (y * _sigmoid(y)).astype(o_ref.dtype)
    else:
        for c in range(nchunk):
            sl = slice(c * LANES, (c + 1) * LANES)
            o_ref[:, sl] = (gate_ref[:, sl].astype(F32) * ybuf[c]).astype(o_ref.dtype)


def depthwise_branch(u, conv_w, *, col_a, col_b, col_gate, ln, n_rows, geom):
    batch, seq, ctx = geom
    width, ch = conv_w.shape
    conformer = ln is not None
    tp = min(TILES["conv"], seq, ctx)
    nchunk = ch // LANES
    hb = tp // HALO_ROWS
    last_halo = u.shape[0] // HALO_ROWS - 1
    wpad = -(-width // 8) * 8
    w3 = jnp.pad(conv_w.astype(F32), ((0, wpad - width), (0, 0))).reshape(wpad, nchunk, LANES).transpose(1, 0, 2)

    def prev_map(col):
        return lambda i: (jnp.maximum(i * hb - 1, 0), col)

    def main_map(col):
        return lambda i: (i, col)

    def next_map(col):
        return lambda i: (jnp.minimum((i + 1) * hb, last_halo), col)

    in_specs, args = [], []
    if not conformer:
        in_specs.append(pl.BlockSpec((tp, ch), main_map(col_gate)))
        args.append(u)
    for rows, mk in ((HALO_ROWS, prev_map), (tp, main_map), (HALO_ROWS, next_map)):
        for col in (col_a, col_b):
            in_specs.append(pl.BlockSpec((rows, ch), mk(col)))
            args.append(u)
    in_specs.append(pl.BlockSpec((nchunk, wpad, LANES), lambda i: (0, 0, 0)))
    args.append(w3)
    if conformer:
        for p in ln:
            in_specs.append(pl.BlockSpec((1, ch), lambda i: (0, 0)))
            args.append(p.reshape(1, ch).astype(F32))

    kern = functools.partial(
        _conv_kernel, width=width, conformer=conformer, tp=tp, n_x_tiles=batch * seq // tp,
        x_per_seq=seq // tp, c_per_seq=ctx // tp)
    return pl.pallas_call(
        kern,
        out_shape=jax.ShapeDtypeStruct((n_rows, ch), BF16),
        grid=(n_rows // tp,),
        in_specs=in_specs,
        out_specs=pl.BlockSpec((tp, ch), lambda i: (i, 0)),
        scratch_shapes=[pltpu.VMEM((nchunk, tp + 2 * HALO_ROWS, LANES), F32),
                        pltpu.VMEM((nchunk, tp, LANES), F32)],
        compiler_params=_cparams(("parallel",)),
        name="conformer_conv" if conformer else "short_conv",
    )(*args)


def _merge_kernel(ha_ref, ob_ref, hc_ref, wa_ref, wb_ref, wc_ref, ga_ref, gb_ref, gc_ref, o_ref):
    ya = jnp.dot(ha_ref[...], wa_ref[...], preferred_element_type=F32)
    yb = jnp.dot(ob_ref[...], wb_ref[...], preferred_element_type=F32)
    yc = jnp.dot(hc_ref[...], wc_ref[...], preferred_element_type=F32)
    m = (_sigmoid(ga_ref[...].astype(F32)) * ya + _sigmoid(gb_ref[...].astype(F32)) * yb
         + _sigmoid(gc_ref[...].astype(F32)) * yc)
    o_ref[...] = m.astype(o_ref.dtype)


def branch_merge(ha, ob, hc, wa, wb, wc, u, *, n_rows, geom):
    batch, seq, _ = geom
    d = wa.shape[1]
    tm = min(TILES["row"], seq)
    tn = min(TILES["col"], d)
    ncol = d // tn
    row_spec = lambda a: pl.BlockSpec((tm, a.shape[1]), lambda m, j: (m, 0))
    w_spec = lambda a: pl.BlockSpec((a.shape[0], tn), lambda m, j: (0, j))
    gate_spec = lambda k: pl.BlockSpec((tm, tn), lambda m, j: (m, k * ncol + j))
    return pl.pallas_call(
        _merge_kernel,
        out_shape=jax.ShapeDtypeStruct((n_rows, d), BF16),
        grid=(n_rows // tm, ncol),
        in_specs=[row_spec(ha), row_spec(ob), row_spec(hc), w_spec(wa), w_spec(wb), w_spec(wc),
                  gate_spec(0), gate_spec(1), gate_spec(2)],
        out_specs=pl.BlockSpec((tm, tn), lambda m, j: (m, j)),
        compiler_params=_cparams(("parallel", "arbitrary")),
        name="branch_merge",
    )(ha, ob, hc, wa, wb, wc, u, u, u)


def _outproj_kernel(m_ref, w_ref, x_ref, gate_ref, o_ref):
    y = jnp.dot(m_ref[...], w_ref[...], preferred_element_type=F32)
    o_ref[...] = x_ref[...] + gate_ref[...] * y


def out_proj_residual(mrg, w_o, xall, mod3, *, layer, gate_idx, n_rows, geom):
    batch, seq, _ = geom
    d = w_o.shape[1]
    tm = min(TILES["row"], seq)
    tn = min(TILES["col"], d)
    ncol = d // tn
    mrow = _mod_row_fn(batch * seq // tm, seq // tm, batch, layer)
    return pl.pallas_call(
        _outproj_kernel,
        out_shape=jax.ShapeDtypeStruct((n_rows, d), F32),
        grid=(n_rows // tm, ncol),
        in_specs=[
            pl.BlockSpec((tm, mrg.shape[1]), lambda m, j: (m, 0)),
            pl.BlockSpec((w_o.shape[0], tn), lambda m, j: (0, j)),
            pl.BlockSpec((tm, tn), lambda m, j: (m, j)),
            pl.BlockSpec((None, 1, tn), lambda m, j: (mrow(m), 0, gate_idx * ncol + j)),
        ],
        out_specs=pl.BlockSpec((tm, tn), lambda m, j: (m, j)),
        compiler_params=_cparams(("parallel", "arbitrary")),
        name="out_proj_residual",
    )(mrg, w_o, xall, mod3)


def _router_kernel(x_ref, g_ref, shift_ref, scale_ref, wr_ref, bias_ref, h_ref, ids_ref, wts_ref, cnt_ref,
                   *, per_group):
    h = _norm_mod(x_ref[...], g_ref[...], shift_ref[...], scale_ref[...])
    h_ref[...] = h.astype(h_ref.dtype)
    h_hi = h.astype(BF16)
    h_lo = (h - h_hi.astype(F32)).astype(BF16)
    w = wr_ref[...]
    w_hi = w.astype(BF16)
    w_lo = (w - w_hi.astype(F32)).astype(BF16)
    logits = (jnp.dot(h_hi, w_hi, preferred_element_type=F32)
              + jnp.dot(h_hi, w_lo, preferred_element_type=F32)
              + jnp.dot(h_lo, w_hi, preferred_element_type=F32))
    lane_i = lax.broadcasted_iota(jnp.int32, (h.shape[0], LANES), 1)
    lane = lane_i.astype(F32)
    valid = lane_i < N_GROUPS
    neg = jnp.float32(-jnp.inf)
    score, sel = [], []
    for j in range(per_group):
        s = _sigmoid(logits[:, j * LANES:(j + 1) * LANES])
        score.append(s)
        sel.append(jnp.where(valid, s + bias_ref[j:j + 1, :], neg))
    assert per_group == 4
    hi1, lo1 = jnp.maximum(sel[0], sel[1]), jnp.minimum(sel[0], sel[1])
    hi2, lo2 = jnp.maximum(sel[2], sel[3]), jnp.minimum(sel[2], sel[3])
    top1 = jnp.maximum(hi1, hi2)
    top2 = jnp.maximum(jnp.minimum(hi1, hi2), jnp.maximum(lo1, lo2))
    gscore = jnp.where(valid, top1 + top2, neg)
    gmax = jnp.max(gscore, axis=-1, keepdims=True)
    grp = jnp.min(jnp.where(gscore == gmax, lane, float(LANES)), axis=-1, keepdims=True)
    pick = lane == grp
    zero = jnp.float32(0.0)
    sel_g = [jnp.sum(jnp.where(pick, s, zero), axis=-1, keepdims=True) for s in sel]
    sc_g = [jnp.sum(jnp.where(pick, s, zero), axis=-1, keepdims=True) for s in score]
    best, best_i, best_s = sel_g[0], jnp.zeros_like(grp), sc_g[0]
    for j in range(1, per_group):
        better = sel_g[j] > best
        best = jnp.where(better, sel_g[j], best)
        best_i = jnp.where(better, float(j), best_i)
        best_s = jnp.where(better, sc_g[j], best_s)
    second, second_i, second_s = jnp.full_like(best, neg), jnp.zeros_like(grp), jnp.zeros_like(best)
    for j in range(per_group):
        better = (sel_g[j] > second) & (best_i != float(j))
        second = jnp.where(better, sel_g[j], second)
        second_i = jnp.where(better, float(j), second_i)
        second_s = jnp.where(better, sc_g[j], second_s)
    tot = best_s + second_s
    e0 = grp * per_group + best_i
    e1 = grp * per_group + second_i
    @pl.when(pl.program_id(0) == 0)
    def _():
        cnt_ref[...] = jnp.zeros_like(cnt_ref)

    oh0 = lane == e0
    oh1 = lane == e1
    both = jnp.where(oh0 | oh1, jnp.float32(1.0), zero)
    tm = h.shape[0]
    earlier = (lax.broadcasted_iota(jnp.int32, (tm, tm), 1)
               < lax.broadcasted_iota(jnp.int32, (tm, tm), 0)).astype(BF16)
    before = jnp.dot(earlier, both.astype(BF16), preferred_element_type=F32) + cnt_ref[0:1, :]
    r0 = jnp.sum(jnp.where(oh0, before, zero), axis=-1, keepdims=True)
    r1 = jnp.sum(jnp.where(oh1, before, zero), axis=-1, keepdims=True)
    cnt_ref[...] = cnt_ref[...] + jnp.sum(both, axis=0, keepdims=True)
    ids = jnp.where(lane_i == 0, e0, jnp.where(lane_i == 1, e1, jnp.where(lane_i == 2, r0,
                    jnp.where(lane_i == 3, r1, zero))))
    ids_ref[...] = ids.astype(jnp.int32)
    wts_ref[...] = jnp.where(lane_i == 0, best_s / tot, jnp.where(lane_i == 1, second_s / tot, zero))


def router(xall, g, mod3, wr_pad, bias_pad, *, layer, shift_idx, n_rows, geom, per_group):
    batch, seq, _ = geom
    d = xall.shape[1]
    tm = min(TILES["router"], seq)
    mrow = _mod_row_fn(batch * seq // tm, seq // tm, batch, layer)
    kern = functools.partial(_router_kernel, per_group=per_group)
    return pl.pallas_call(
        kern,
        out_shape=(jax.ShapeDtypeStruct((n_rows, d), BF16),
                   jax.ShapeDtypeStruct((n_rows, LANES), jnp.int32),
                   jax.ShapeDtypeStruct((n_rows, LANES), F32),
                   jax.ShapeDtypeStruct((8, LANES), F32)),
        grid=(n_rows // tm,),
        in_specs=[
            pl.BlockSpec((tm, d), lambda m: (m, 0)),
            pl.BlockSpec((1, d), lambda m: (0, 0)),
            pl.BlockSpec((None, 1, d), lambda m: (mrow(m), 0, shift_idx)),
            pl.BlockSpec((None, 1, d), lambda m: (mrow(m), 0, shift_idx + 1)),
            pl.BlockSpec(wr_pad.shape, lambda m: (0, 0)),
            pl.BlockSpec(bias_pad.shape, lambda m: (0, 0)),
        ],
        out_specs=(pl.BlockSpec((tm, d), lambda m: (m, 0)),
                   pl.BlockSpec((tm, LANES), lambda m: (m, 0)),
                   pl.BlockSpec((tm, LANES), lambda m: (m, 0)),
                   pl.BlockSpec((8, LANES), lambda m: (0, 0))),
        compiler_params=_cparams(("arbitrary",)),
        name="router",
    )(xall, g.reshape(1, d), mod3, mod3, wr_pad, bias_pad)


def _expert_kernel(be_ref, x_ref, win_ref, wout_ref, o_ref, win_bf, wout_bf, *, ff):
    i = pl.program_id(0)

    @pl.when((i == 0) | (be_ref[i] != be_ref[jnp.maximum(i - 1, 0)]))
    def _():
        win_bf[...] = win_ref[...].astype(BF16)
        wout_bf[...] = wout_ref[...].astype(BF16)

    gu = jnp.dot(x_ref[...], win_bf[...], preferred_element_type=F32)
    g = gu[:, :ff]
    act = (g * _sigmoid(g) * gu[:, ff:]).astype(BF16)
    o_ref[...] = jnp.dot(act, wout_bf[...], preferred_element_type=F32).astype(o_ref.dtype)


def expert_ffn(xs, block_e, w_in, w_out, *, layer):
    rows, d = xs.shape
    blk = TILES["moe_block"]
    ff = w_out.shape[2]
    kern = functools.partial(_expert_kernel, ff=ff)
    grid_spec = pltpu.PrefetchScalarGridSpec(
        num_scalar_prefetch=1,
        grid=(rows // blk,),
        in_specs=[
            pl.BlockSpec((blk, d), lambda i, be: (i, 0)),
            pl.BlockSpec((None, None, d, 2 * ff), lambda i, be: (layer, be[i], 0, 0)),
            pl.BlockSpec((None, None, ff, d), lambda i, be: (layer, be[i], 0, 0)),
        ],
        out_specs=pl.BlockSpec((blk, d), lambda i, be: (i, 0)),
        scratch_shapes=[pltpu.VMEM((d, 2 * ff), BF16), pltpu.VMEM((ff, d), BF16)],
    )
    return pl.pallas_call(
        kern,
        out_shape=jax.ShapeDtypeStruct((rows, d), BF16),
        grid_spec=grid_spec,
        compiler_params=pltpu.CompilerParams(dimension_semantics=("arbitrary",),
                                             vmem_limit_bytes=TILES["vmem_expert"]),
        name="expert_ffn",
    )(block_e, xs, w_in, w_out)


def _combine_kernel(x_ref, y0_ref, y1_ref, w_ref, gate_ref, o_ref):
    w = w_ref[...]
    y = w[:, 0:1] * y0_ref[...].astype(F32) + w[:, 1:2] * y1_ref[...].astype(F32)
    o_ref[...] = x_ref[...] + gate_ref[...] * y


def moe_combine(xall, y0, y1, wts, mod3, *, layer, gate_idx, n_rows, geom):
    batch, seq, _ = geom
    d = xall.shape[1]
    tm = min(TILES["row"], seq)
    mrow = _mod_row_fn(batch * seq // tm, seq // tm, batch, layer)
    return pl.pallas_call(
        _combine_kernel,
        out_shape=jax.ShapeDtypeStruct((n_rows, d), F32),
        grid=(n_rows // tm,),
        in_specs=[
            pl.BlockSpec((tm, d), lambda m: (m, 0)),
            pl.BlockSpec((tm, d), lambda m: (m, 0)),
            pl.BlockSpec((tm, d), lambda m: (m, 0)),
            pl.BlockSpec((tm, LANES), lambda m: (m, 0)),
            pl.BlockSpec((None, 1, d), lambda m: (mrow(m), 0, gate_idx)),
        ],
        out_specs=pl.BlockSpec((tm, d), lambda m: (m, 0)),
        compiler_params=_cparams(("parallel",)),
        name="moe_combine",
    )(xall, y0, y1, wts, mod3)


def moe_dispatch(ids, counts, n_experts):
    blk = TILES["moe_block"]
    n = ids.shape[0]
    experts, rank = ids[:, :TOP_K], ids[:, TOP_K:2 * TOP_K]
    counts = counts[0, :n_experts].astype(jnp.int32)
    padded = (counts + blk - 1) // blk * blk
    pend = jnp.cumsum(padded)
    pstart = pend - padded
    eye = experts[:, :, None] == jnp.arange(n_experts, dtype=jnp.int32)[None, None, :]
    dest = jnp.sum(jnp.where(eye, pstart[None, None, :], 0), axis=-1) + rank
    n_blocks = -(-n * TOP_K // blk) + n_experts
    rows = n_blocks * blk
    tok = jnp.broadcast_to(jnp.arange(n, dtype=jnp.int32)[:, None], (n, TOP_K))
    row_tok = (jnp.arange(rows, dtype=jnp.int32) % n).at[dest.reshape(-1)].set(tok.reshape(-1))
    block_start = jnp.arange(n_blocks, dtype=jnp.int32) * blk
    block_e = jnp.minimum(jnp.sum(block_start[:, None] >= pend[None, :], axis=1), n_experts - 1).astype(jnp.int32)
    return row_tok, dest, block_e


def _rope_tables(seq, tm):
    inv = ROPE_THETA ** (-jnp.arange(0, AXIS_DIM, 2, dtype=F32) / AXIS_DIM)
    t = jnp.arange(seq)
    ang_r = (t // GRID_W).astype(F32)[:, None] * inv
    ang_c = (t % GRID_W).astype(F32)[:, None] * inv
    cos = jnp.concatenate([jnp.cos(ang_r), jnp.cos(ang_r), jnp.cos(ang_c), jnp.cos(ang_c)], axis=-1)
    sin = jnp.concatenate([-jnp.sin(ang_r), jnp.sin(ang_r), -jnp.sin(ang_c), jnp.sin(ang_c)], axis=-1)
    pad = LANES - ROPE_DIM
    cos = jnp.pad(cos, ((0, 0), (0, pad)))
    sin = jnp.pad(sin, ((0, 0), (0, pad)))
    ident_c = jnp.pad(jnp.ones((tm, ROPE_DIM), F32), ((0, 0), (0, pad)))
    return (jnp.concatenate([ident_c, cos], axis=0),
            jnp.concatenate([jnp.zeros((tm, LANES), F32), sin], axis=0))


def _partner():
    j = jnp.arange(ROPE_DIM)
    half = AXIS_DIM // 2
    return jnp.where((j % AXIS_DIM) < half, j + half, j - half)


def _pad_lanes(a):
    return jnp.pad(a, [(0, 0)] * (a.ndim - 1) + [(0, LANES - a.shape[-1])])


def _norm_gain_rows(g):
    rope = g[NOPE_DIM:]
    rows = jnp.stack([g[:NOPE_DIM], _pad_lanes(rope), _pad_lanes(rope[_partner()])])
    return jnp.pad(rows, ((0, 8 - rows.shape[0]), (0, 0))).astype(F32)


def _layer_weights(l, w_in, w_uq, w_ukv, q_lora, kv_lora):
    d = w_in.shape[1]
    heads = MLA_HEADS
    wl = w_in[l]
    gate_cols = 3 * d
    conv_a2 = (wl.shape[1] - gate_cols - (q_lora + kv_lora + ROPE_DIM)) // 5 * 2
    off_b = gate_cols + conv_a2
    off_c = off_b + q_lora + kv_lora + ROPE_DIM
    w_main = jnp.concatenate([wl[:, :off_b], wl[:, off_c:]], axis=1).astype(BF16)
    kpe = wl[:, off_b + q_lora + kv_lora:off_c]
    w_b = jnp.concatenate([wl[:, off_b:off_b + q_lora + kv_lora], _pad_lanes(kpe),
                           _pad_lanes(kpe[:, _partner()])], axis=1).astype(BF16)
    wq = w_uq[l].reshape(q_lora, heads, QK_DIM)
    wq_n = wq[:, :, :NOPE_DIM].reshape(q_lora, heads * LANES)
    wq_r = _pad_lanes(wq[:, :, NOPE_DIM:]).reshape(q_lora, heads * LANES)
    wq_s = _pad_lanes(wq[:, :, NOPE_DIM:][:, :, _partner()]).reshape(q_lora, heads * LANES)
    wq_all = jnp.concatenate([wq_n, wq_r, wq_s], axis=1).astype(BF16)
    wkv = w_ukv[l].reshape(kv_lora, heads, NOPE_DIM + V_DIM)
    wkv_all = jnp.concatenate([wkv[:, :, :NOPE_DIM].reshape(kv_lora, heads * NOPE_DIM),
                               wkv[:, :, NOPE_DIM:].reshape(kv_lora, heads * V_DIM)], axis=1).astype(BF16)
    return w_main, w_b, wq_all, wkv_all


def _router_layout(w_router, router_bias):
    d, n_experts = w_router.shape
    per_group = n_experts // N_GROUPS
    wr = w_router.reshape(d, N_GROUPS, per_group).transpose(0, 2, 1)
    wr_pad = _pad_lanes(wr).reshape(d, per_group * LANES).astype(F32)
    bias = _pad_lanes(router_bias.reshape(N_GROUPS, per_group).T).astype(F32)
    bias = jnp.pad(bias, ((0, 8 - per_group), (0, 0)))
    return wr_pad, bias, per_group


def kernel(x, c, ctx, c_ctx, w_mod, b_mod, g_norm1, g_norm2, w_in, conv_a, ln_a_g, ln_a_b, w_a_out, g_cq, g_ckv, w_uq, w_ukv, g_q, g_k, w_b_out, conv_c, w_c_out, w_o, w_router, router_bias, w_e_in, w_e_out):
    batch, seq, d = x.shape
    n_ctx = ctx.shape[1]
    depth = w_mod.shape[0]
    geom = (batch, seq, n_ctx)
    t_x = batch * seq
    t_all = t_x + batch * n_ctx
    q_lora, kv_lora = g_cq.shape[1], g_ckv.shape[1]
    ch_a, ch_c = conv_a.shape[2], conv_c.shape[2]
    n_experts = w_router.shape[1]
    assert batch + 1 <= MOD_ROWS and t_x % n_ctx == 0 and ch_a == ch_c and (3 * d) % ch_a == 0

    xall = jnp.concatenate([x.reshape(t_x, d), ctx.reshape(batch * n_ctx, d)], axis=0)
    cvec = jnp.concatenate([c, c_ctx[None, :], jnp.zeros((MOD_ROWS - batch - 1, d), F32)], axis=0)
    mod3 = modulation(cvec, w_mod, b_mod).reshape(depth * MOD_ROWS, 1, N_MOD * d)

    cos_tab, sin_tab = _rope_tables(seq, min(TILES["mla"], seq, n_ctx))
    wr_pad, bias_pad, per_group = _router_layout(w_router, router_bias)
    col0 = 3 * d // ch_a

    for l in range(depth):
        last = l == depth - 1
        rows_main = t_x if last else t_all
        w_main, w_b, wq_all, wkv_all = _layer_weights(l, w_in, w_uq, w_ukv, q_lora, kv_lora)

        u = norm_proj(xall, g_norm1[l], mod3, w_main, layer=l, shift_idx=0, n_rows=rows_main, geom=geom)
        ub = norm_proj(xall, g_norm1[l], mod3, w_b, layer=l, shift_idx=0, n_rows=t_all, geom=geom)
        q, kt, v = mla_project(ub, g_cq[l], g_ckv[l], wq_all, wkv_all, _norm_gain_rows(g_q[l]),
                               _norm_gain_rows(g_k[l]), cos_tab, sin_tab, geom=geom)
        ob = attention(q, kt, v, geom=geom, with_ctx_queries=not last)
        ha = depthwise_branch(u, conv_a[l], col_a=col0, col_b=col0 + 1, col_gate=None,
                              ln=(ln_a_g[l], ln_a_b[l]), n_rows=rows_main, geom=geom)
        hc = depthwise_branch(u, conv_c[l], col_a=col0 + 3, col_b=col0 + 4, col_gate=col0 + 2,
                              ln=None, n_rows=rows_main, geom=geom)
        mrg = branch_merge(ha, ob, hc, w_a_out[l].astype(BF16), w_b_out[l].astype(BF16),
                           w_c_out[l].astype(BF16), u, n_rows=rows_main, geom=geom)
        x1 = out_proj_residual(mrg, w_o[l].astype(BF16), xall, mod3, layer=l, gate_idx=2,
                               n_rows=rows_main, geom=geom)

        h2, ids, wts, counts = router(x1, g_norm2[l], mod3, wr_pad, bias_pad, layer=l, shift_idx=3,
                                      n_rows=rows_main, geom=geom, per_group=per_group)
        row_tok, dest, block_e = moe_dispatch(ids, counts, n_experts)
        ys = expert_ffn(jnp.take(h2, row_tok, axis=0), block_e, w_e_in, w_e_out, layer=l)
        y0 = jnp.take(ys, dest[:, 0], axis=0)
        y1 = jnp.take(ys, dest[:, 1], axis=0)
        xall = moe_combine(x1, y0, y1, wts, mod3, layer=l, gate_idx=5, n_rows=rows_main, geom=geom)

    return xall[:t_x].reshape(batch, seq, d)
```

```python
import functools
import math

import jax
import jax.numpy as jnp
from jax import lax
from jax.experimental import pallas as pl
from jax.experimental.pallas import tpu as pltpu

F32 = jnp.float32
BF16 = jnp.bfloat16

GRID_W = 64
MLA_HEADS = 8
NOPE_DIM = 128
ROPE_DIM = 64
V_DIM = 128
QK_DIM = NOPE_DIM + ROPE_DIM
AXIS_DIM = ROPE_DIM // 2
ROPE_THETA = 10000.0
N_GROUPS = 8
TOP_K = 2
N_MOD = 6
EPS = 1e-6

LANES = 128
HEAD_SLOT = 2 * LANES
HALO_ROWS = 16
MOD_ROWS = 8

TILES = dict(
    row=512,
    proj_row=1024,
    col=1024,
    mod_col=1024,
    mla=256,
    tq=512,
    tq_sub=512,
    conv=256,
    conv_rows=64,
    router=512,
    moe_block=256,
    vmem=56 * 1024 * 1024,
    vmem_expert=60 * 1024 * 1024,
)


def _cparams(sem):
    return pltpu.CompilerParams(dimension_semantics=sem, vmem_limit_bytes=TILES["vmem"])


def _sigmoid(x):
    return 1.0 / (1.0 + jnp.exp(-x))


def _mod_kernel(c_ref, w_ref, b_ref, o_ref):
    c = c_ref[...]
    s = (c * _sigmoid(c)).astype(BF16)
    o_ref[...] = jnp.dot(s, w_ref[...].astype(BF16), preferred_element_type=F32) + b_ref[...]


def modulation(cvec, w_mod, b_mod):
    n_layers, d, n = w_mod.shape
    tn = min(TILES["mod_col"], n)
    return pl.pallas_call(
        _mod_kernel,
        out_shape=jax.ShapeDtypeStruct((n_layers, MOD_ROWS, n), F32),
        grid=(n_layers, n // tn),
        in_specs=[
            pl.BlockSpec((MOD_ROWS, d), lambda l, j: (0, 0)),
            pl.BlockSpec((None, d, tn), lambda l, j: (l, 0, j)),
            pl.BlockSpec((None, 1, tn), lambda l, j: (l, 0, j)),
        ],
        out_specs=pl.BlockSpec((None, MOD_ROWS, tn), lambda l, j: (l, 0, j)),
        compiler_params=_cparams(("parallel", "parallel")),
        name="modulation",
    )(cvec, w_mod, b_mod.reshape(n_layers, 1, n))


def _norm_mod(x, g, shift, scale):
    ms = jnp.mean(x * x, axis=-1, keepdims=True)
    return x * lax.rsqrt(ms + EPS) * g * (1.0 + scale) + shift


def _norm_kernel(x_ref, g_ref, shift_ref, scale_ref, h_ref):
    h_ref[...] = _norm_mod(x_ref[...], g_ref[...], shift_ref[...], scale_ref[...]).astype(h_ref.dtype)


def _mod_row_fn(n_x_tiles, tiles_per_seq, batch, layer):
    def f(m):
        return layer * MOD_ROWS + jnp.where(m < n_x_tiles, m // tiles_per_seq, batch)
    return f


def norm_modulate(xall, g, mod3, *, layer, shift_idx, geom):
    batch, seq, _ = geom
    t, d = xall.shape
    tm = min(TILES["row"], seq)
    mrow = _mod_row_fn(batch * seq // tm, seq // tm, batch, layer)
    return pl.pallas_call(
        _norm_kernel,
        out_shape=jax.ShapeDtypeStruct((t, d), BF16),
        grid=(t // tm,),
        in_specs=[
            pl.BlockSpec((tm, d), lambda m: (m, 0)),
            pl.BlockSpec((1, d), lambda m: (0, 0)),
            pl.BlockSpec((None, 1, d), lambda m: (mrow(m), 0, shift_idx)),
            pl.BlockSpec((None, 1, d), lambda m: (mrow(m), 0, shift_idx + 1)),
        ],
        out_specs=pl.BlockSpec((tm, d), lambda m: (m, 0)),
        compiler_params=_cparams(("parallel",)),
        name="norm_modulate",
    )(xall, g.reshape(1, d), mod3, mod3)


def _proj_kernel(a_ref, w_ref, o_ref, w_bf):
    @pl.when(pl.program_id(1) == 0)
    def _():
        w_bf[...] = w_ref[...].astype(BF16)

    o_ref[...] = jnp.dot(a_ref[...], w_bf[...], preferred_element_type=F32).astype(o_ref.dtype)


def _w_spec(w, layer, tn, col0=0):
    if w.ndim == 3:
        return pl.BlockSpec((None, w.shape[1], tn), lambda j, m: (layer, 0, col0 + j))
    return pl.BlockSpec((w.shape[0], tn), lambda j, m: (0, col0 + j))


def project(a, w, *, n_cols, n_rows, layer=0):
    k = a.shape[1]
    tm = min(TILES["proj_row"], n_rows)
    tn = min(TILES["col"], n_cols)
    if n_cols % tn:
        tn = n_cols
    return pl.pallas_call(
        _proj_kernel,
        out_shape=jax.ShapeDtypeStruct((n_rows, n_cols), BF16),
        grid=(n_cols // tn, n_rows // tm),
        in_specs=[pl.BlockSpec((tm, k), lambda j, m: (m, 0)), _w_spec(w, layer, tn)],
        out_specs=pl.BlockSpec((tm, tn), lambda j, m: (m, j)),
        scratch_shapes=[pltpu.VMEM((k, tn), BF16)],
        compiler_params=_cparams(("parallel", "arbitrary")),
        name="project",
    )(a, w)


def _mla_kernel(ub_ref, gcq_ref, gckv_ref, wq_ref, wkv_ref, gq_ref, gk_ref, cos_ref, sin_ref,
                qt_ref, k_ref, vt_ref, *, q_lora, kv_lora, heads):
    hn = heads * LANES

    def rms(xv, g):
        return (xv * lax.rsqrt(jnp.mean(xv * xv, axis=-1, keepdims=True) + EPS) * g).astype(BF16)

    cqn = rms(ub_ref[:, 0:q_lora].astype(F32), gcq_ref[...])
    ckvn = rms(ub_ref[:, q_lora:q_lora + kv_lora].astype(F32), gckv_ref[...])
    off = q_lora + kv_lora
    kpe = ub_ref[:, off:off + LANES].astype(F32)
    kpes = ub_ref[:, off + LANES:off + 2 * LANES].astype(F32)

    qall = jnp.dot(cqn, wq_ref[...], preferred_element_type=F32)
    kvall = jnp.dot(ckvn, wkv_ref[...], preferred_element_type=F32)

    cos = cos_ref[...]
    sin = sin_ref[...]
    gq_n, gq_r, gq_s = gq_ref[0:1, :], gq_ref[1:2, :], gq_ref[2:3, :]
    gk_n, gk_r, gk_s = gk_ref[0:1, :], gk_ref[1:2, :], gk_ref[2:3, :]
    q_cos, q_sin = gq_r * cos, gq_s * sin
    k_rope = kpe * (gk_r * cos) + kpes * (gk_s * sin)
    ssq_pe = jnp.sum(kpe * kpe, axis=-1, keepdims=True)
    q_scale = QK_DIM ** -0.5 * math.log2(math.e)

    for h in range(heads):
        lo = h * LANES
        qn = qall[:, lo:lo + LANES]
        qr = qall[:, hn + lo:hn + lo + LANES]
        qs = qall[:, 2 * hn + lo:2 * hn + lo + LANES]
        ssq = jnp.sum(qn * qn, axis=-1, keepdims=True) + jnp.sum(qr * qr, axis=-1, keepdims=True)
        inv = lax.rsqrt(ssq * (1.0 / QK_DIM) + EPS) * q_scale
        qt_ref[h * HEAD_SLOT:h * HEAD_SLOT + LANES, :] = (qn * gq_n * inv).T.astype(BF16)
        qt_ref[h * HEAD_SLOT + LANES:(h + 1) * HEAD_SLOT, :] = ((qr * q_cos + qs * q_sin) * inv).T.astype(BF16)

        kn = kvall[:, lo:lo + LANES]
        invk = lax.rsqrt((jnp.sum(kn * kn, axis=-1, keepdims=True) + ssq_pe) * (1.0 / QK_DIM) + EPS)
        k_ref[:, h * HEAD_SLOT:h * HEAD_SLOT + LANES] = (kn * gk_n * invk).astype(BF16)
        k_ref[:, h * HEAD_SLOT + LANES:(h + 1) * HEAD_SLOT] = (k_rope * invk).astype(BF16)

        vt_ref[h * V_DIM:(h + 1) * V_DIM, :] = kvall[:, hn + h * V_DIM:hn + (h + 1) * V_DIM].T.astype(BF16)


def mla_project(ub, g_cq, g_ckv, wq, wkv, gq3, gk3, cos_tab, sin_tab, *, geom):
    batch, seq, ctx = geom
    t_all = ub.shape[0]
    q_lora, kv_lora = g_cq.shape[0], g_ckv.shape[0]
    heads = MLA_HEADS
    tm = min(TILES["mla"], seq, ctx)
    n_x = batch * seq // tm
    per_seq = seq // tm

    def pos(m):
        return jnp.where(m < n_x, 1 + m % per_seq, 0)

    kern = functools.partial(_mla_kernel, q_lora=q_lora, kv_lora=kv_lora, heads=heads)
    return pl.pallas_call(
        kern,
        out_shape=(
            jax.ShapeDtypeStruct((heads * HEAD_SLOT, t_all), BF16),
            jax.ShapeDtypeStruct((t_all, heads * HEAD_SLOT), BF16),
            jax.ShapeDtypeStruct((heads * V_DIM, t_all), BF16),
        ),
        grid=(t_all // tm,),
        in_specs=[
            pl.BlockSpec((tm, ub.shape[1]), lambda m: (m, 0)),
            pl.BlockSpec((1, q_lora), lambda m: (0, 0)),
            pl.BlockSpec((1, kv_lora), lambda m: (0, 0)),
            pl.BlockSpec(wq.shape, lambda m: (0, 0)),
            pl.BlockSpec(wkv.shape, lambda m: (0, 0)),
            pl.BlockSpec((8, LANES), lambda m: (0, 0)),
            pl.BlockSpec((8, LANES), lambda m: (0, 0)),
            pl.BlockSpec((tm, LANES), lambda m: (pos(m), 0)),
            pl.BlockSpec((tm, LANES), lambda m: (pos(m), 0)),
        ],
        out_specs=(
            pl.BlockSpec((heads * HEAD_SLOT, tm), lambda m: (0, m)),
            pl.BlockSpec((tm, heads * HEAD_SLOT), lambda m: (m, 0)),
            pl.BlockSpec((heads * V_DIM, tm), lambda m: (0, m)),
        ),
        compiler_params=_cparams(("parallel",)),
        name="mla_project",
    )(ub, g_cq.reshape(1, -1), g_ckv.reshape(1, -1), wq, wkv, gq3, gk3, cos_tab, sin_tab)


def _attn_kernel(*refs, sub, latent_keys):
    if latent_keys:
        qt_ref, kx_ref, kc_ref, vxt_ref, vct_ref, o_ref = refs
    else:
        qt_ref, kc_ref, vct_ref, o_ref = refs
    for j in range(qt_ref.shape[1] // sub):
        qt = qt_ref[:, j * sub:(j + 1) * sub]
        sc = jnp.dot(kc_ref[...], qt, preferred_element_type=F32)
        m = jnp.max(sc, axis=0, keepdims=True)
        if latent_keys:
            sx = jnp.dot(kx_ref[...], qt, preferred_element_type=F32)
            m = jnp.maximum(m, jnp.max(sx, axis=0, keepdims=True))
        pc = jnp.exp2(sc - m)
        denom = jnp.sum(pc, axis=0, keepdims=True)
        ot = jnp.dot(vct_ref[...], pc.astype(BF16), preferred_element_type=F32)
        if latent_keys:
            px = jnp.exp2(sx - m)
            denom = denom + jnp.sum(px, axis=0, keepdims=True)
            ot = ot + jnp.dot(vxt_ref[...], px.astype(BF16), preferred_element_type=F32)
        o_ref[j * sub:(j + 1) * sub, :] = (ot * (1.0 / denom)).T.astype(o_ref.dtype)


def attention_latent(qt, k, vt, *, geom):
    batch, seq, ctx = geom
    heads = MLA_HEADS
    tq = min(TILES["tq"], seq)
    sub = min(TILES["tq_sub"], tq)
    n_q = seq // tq
    ctx_blk0 = batch * seq // ctx
    kern = functools.partial(_attn_kernel, sub=sub, latent_keys=True)
    return pl.pallas_call(
        kern,
        out_shape=jax.ShapeDtypeStruct((batch * seq, heads * V_DIM), BF16),
        grid=(batch, heads, n_q),
        in_specs=[
            pl.BlockSpec((HEAD_SLOT, tq), lambda b, h, i: (h, b * n_q + i)),
            pl.BlockSpec((seq, HEAD_SLOT), lambda b, h, i: (b, h)),
            pl.BlockSpec((ctx, HEAD_SLOT), lambda b, h, i: (ctx_blk0 + b, h)),
            pl.BlockSpec((V_DIM, seq), lambda b, h, i: (h, b)),
            pl.BlockSpec((V_DIM, ctx), lambda b, h, i: (h, ctx_blk0 + b)),
        ],
        out_specs=pl.BlockSpec((tq, V_DIM), lambda b, h, i: (b * n_q + i, h)),
        compiler_params=_cparams(("parallel", "parallel", "arbitrary")),
        name="attention_latent",
    )(qt, k, k, vt, vt)


def attention_context(qt, k, vt, *, geom):
    batch, seq, ctx = geom
    heads = MLA_HEADS
    ctx_blk0 = batch * seq // ctx
    kern = functools.partial(_attn_kernel, sub=min(TILES["tq_sub"], ctx), latent_keys=False)
    return pl.pallas_call(
        kern,
        out_shape=jax.ShapeDtypeStruct((batch * ctx, heads * V_DIM), BF16),
        grid=(batch, heads),
        in_specs=[
            pl.BlockSpec((HEAD_SLOT, ctx), lambda b, h: (h, ctx_blk0 + b)),
            pl.BlockSpec((ctx, HEAD_SLOT), lambda b, h: (ctx_blk0 + b, h)),
            pl.BlockSpec((V_DIM, ctx), lambda b, h: (h, ctx_blk0 + b)),
        ],
        out_specs=pl.BlockSpec((ctx, V_DIM), lambda b, h: (b, h)),
        compiler_params=_cparams(("parallel", "parallel")),
        name="attention_context",
    )(qt, k, vt)


def _conv_kernel(*refs, width, conformer, tp, n_x_tiles, x_per_seq, c_per_seq):
    if conformer:
        (ap, bp, am, bm, an, bn, w_ref, lng_ref, lnb_ref, o_ref, buf, ybuf) = refs
        gate_ref = None
    else:
        (gate_ref, ap, bp, am, bm, an, bn, w_ref, o_ref, buf, ybuf) = refs
    nchunk = buf.shape[0]
    i = pl.program_id(0)
    in_x = i < n_x_tiles
    pos = jnp.where(in_x, i % x_per_seq, (i - n_x_tiles) % c_per_seq)
    per_seq = jnp.where(in_x, x_per_seq, c_per_seq)
    keep_prev = (pos > 0).astype(F32)
    keep_next = (pos < per_seq - 1).astype(F32)

    def pre(a, b):
        a = a.astype(F32)
        b = b.astype(F32)
        return a * _sigmoid(b) if conformer else a * b

    for c in range(nchunk):
        sl = slice(c * LANES, (c + 1) * LANES)
        buf[c, 0:HALO_ROWS, :] = pre(ap[:, sl], bp[:, sl]) * keep_prev
        buf[c, HALO_ROWS:HALO_ROWS + tp, :] = pre(am[:, sl], bm[:, sl])
        buf[c, HALO_ROWS + tp:2 * HALO_ROWS + tp, :] = pre(an[:, sl], bn[:, sl]) * keep_next

    rc = min(TILES["conv_rows"], tp)
    base = HALO_ROWS - width // 2

    def chunk(c, carry):
        wts = w_ref[c]
        for r0 in range(0, tp, rc):
            acc = jnp.zeros((rc, LANES), F32)
            for k in range(width):
                acc = acc + wts[k:k + 1, :] * buf[c, pl.ds(base + r0 + k, rc), :]
            ybuf[c, pl.ds(r0, rc), :] = acc
        return carry

    lax.fori_loop(0, nchunk, chunk, 0)

    if conformer:
        n_ch = nchunk * LANES
        tot = jnp.zeros((tp, 1), F32)
        for c in range(nchunk):
            tot = tot + jnp.sum(ybuf[c], axis=-1, keepdims=True)
        mu = tot * (1.0 / n_ch)
        tot2 = jnp.zeros((tp, 1), F32)
        for c in range(nchunk):
            dlt = ybuf[c] - mu
            tot2 = tot2 + jnp.sum(dlt * dlt, axis=-1, keepdims=True)
        inv = lax.rsqrt(tot2 * (1.0 / n_ch) + EPS)
        for c in range(nchunk):
            sl = slice(c * LANES, (c + 1) * LANES)
            y = (ybuf[c] - mu) * inv * lng_ref[:, sl] + lnb_ref[:, sl]
            o_ref[:, sl] = (y * _sigmoid(y)).astype(o_ref.dtype)
    else:
        for c in range(nchunk):
            sl = slice(c * LANES, (c + 1) * LANES)
            o_ref[:, sl] = (gate_ref[:, sl].astype(F32) * ybuf[c]).astype(o_ref.dtype)


def depthwise_branch(u, conv_w, *, col_a, col_b, col_gate, ln, n_rows, geom):
    batch, seq, ctx = geom
    width, ch = conv_w.shape
    conformer = ln is not None
    tp = min(TILES["conv"], seq, ctx)
    nchunk = ch // LANES
    hb = tp // HALO_ROWS
    last_halo = u.shape[0] // HALO_ROWS - 1
    wpad = -(-width // 8) * 8
    w3 = jnp.pad(conv_w.astype(F32), ((0, wpad - width), (0, 0))).reshape(wpad, nchunk, LANES).transpose(1, 0, 2)

    def prev_map(col):
        return lambda i: (jnp.maximum(i * hb - 1, 0), col)

    def main_map(col):
        return lambda i: (i, col)

    def next_map(col):
        return lambda i: (jnp.minimum((i + 1) * hb, last_halo), col)

    in_specs, args = [], []
    if not conformer:
        in_specs.append(pl.BlockSpec((tp, ch), main_map(col_gate)))
        args.append(u)
    for rows, mk in ((HALO_ROWS, prev_map), (tp, main_map), (HALO_ROWS, next_map)):
        for col in (col_a, col_b):
            in_specs.append(pl.BlockSpec((rows, ch), mk(col)))
            args.append(u)
    in_specs.append(pl.BlockSpec((nchunk, wpad, LANES), lambda i: (0, 0, 0)))
    args.append(w3)
    if conformer:
        for p in ln:
            in_specs.append(pl.BlockSpec((1, ch), lambda i: (0, 0)))
            args.append(p.reshape(1, ch).astype(F32))

    kern = functools.partial(
        _conv_kernel, width=width, conformer=conformer, tp=tp, n_x_tiles=batch * seq // tp,
        x_per_seq=seq // tp, c_per_seq=ctx // tp)
    return pl.pallas_call(
        kern,
        out_shape=jax.ShapeDtypeStruct((n_rows, ch), BF16),
        grid=(n_rows // tp,),
        in_specs=in_specs,
        out_specs=pl.BlockSpec((tp, ch), lambda i: (i, 0)),
        scratch_shapes=[pltpu.VMEM((nchunk, tp + 2 * HALO_ROWS, LANES), F32),
                        pltpu.VMEM((nchunk, tp, LANES), F32)],
        compiler_params=_cparams(("parallel",)),
        name="conformer_conv" if conformer else "short_conv",
    )(*args)


def _merge_kernel(ha_ref, obx_ref, obc_ref, hc_ref, wa_ref, wb_ref, wc_ref, ga_ref, gb_ref, gc_ref, o_ref,
                  wa_bf, wb_bf, wc_bf, *, n_x_tiles):
    m_idx = pl.program_id(1)

    @pl.when(m_idx == 0)
    def _():
        wa_bf[...] = wa_ref[...].astype(BF16)
        wb_bf[...] = wb_ref[...].astype(BF16)
        wc_bf[...] = wc_ref[...].astype(BF16)

    ob = jnp.where(m_idx < n_x_tiles, obx_ref[...], obc_ref[...])
    ya = jnp.dot(ha_ref[...], wa_bf[...], preferred_element_type=F32)
    yb = jnp.dot(ob, wb_bf[...], preferred_element_type=F32)
    yc = jnp.dot(hc_ref[...], wc_bf[...], preferred_element_type=F32)
    m = (_sigmoid(ga_ref[...].astype(F32)) * ya + _sigmoid(gb_ref[...].astype(F32)) * yb
         + _sigmoid(gc_ref[...].astype(F32)) * yc)
    o_ref[...] = m.astype(o_ref.dtype)


def branch_merge(ha, ob_x, ob_c, hc, wa, wb, wc, u, *, layer, n_rows, geom):
    batch, seq, _ = geom
    d = wa.shape[2]
    tm = min(TILES["row"], seq)
    tn = min(TILES["col"], d)
    ncol = d // tn
    n_x = batch * seq // tm
    row_spec = lambda a: pl.BlockSpec((tm, a.shape[1]), lambda j, m: (m, 0))
    gate_spec = lambda k: pl.BlockSpec((tm, tn), lambda j, m: (m, k * ncol + j))
    kern = functools.partial(_merge_kernel, n_x_tiles=n_x)
    return pl.pallas_call(
        kern,
        out_shape=jax.ShapeDtypeStruct((n_rows, d), BF16),
        grid=(ncol, n_rows // tm),
        in_specs=[row_spec(ha),
                  pl.BlockSpec((tm, ob_x.shape[1]), lambda j, m: (jnp.minimum(m, n_x - 1), 0)),
                  pl.BlockSpec((tm, ob_c.shape[1]), lambda j, m: (jnp.maximum(m - n_x, 0), 0)),
                  row_spec(hc), _w_spec(wa, layer, tn), _w_spec(wb, layer, tn), _w_spec(wc, layer, tn),
                  gate_spec(0), gate_spec(1), gate_spec(2)],
        out_specs=pl.BlockSpec((tm, tn), lambda j, m: (m, j)),
        scratch_shapes=[pltpu.VMEM((w.shape[1], tn), BF16) for w in (wa, wb, wc)],
        compiler_params=_cparams(("parallel", "arbitrary")),
        name="branch_merge",
    )(ha, ob_x, ob_c, hc, wa, wb, wc, u, u, u)


def _outproj_kernel(m_ref, w_ref, x_ref, gate_ref, o_ref, w_bf):
    @pl.when(pl.program_id(1) == 0)
    def _():
        w_bf[...] = w_ref[...].astype(BF16)

    y = jnp.dot(m_ref[...], w_bf[...], preferred_element_type=F32)
    o_ref[...] = x_ref[...] + gate_ref[...] * y


def out_proj_residual(mrg, w_o, xall, mod3, *, layer, gate_idx, n_rows, geom):
    batch, seq, _ = geom
    d = w_o.shape[2]
    tm = min(TILES["row"], seq)
    tn = min(TILES["col"], d)
    ncol = d // tn
    mrow = _mod_row_fn(batch * seq // tm, seq // tm, batch, layer)
    return pl.pallas_call(
        _outproj_kernel,
        out_shape=jax.ShapeDtypeStruct((n_rows, d), F32),
        grid=(ncol, n_rows // tm),
        in_specs=[
            pl.BlockSpec((tm, mrg.shape[1]), lambda j, m: (m, 0)),
            _w_spec(w_o, layer, tn),
            pl.BlockSpec((tm, tn), lambda j, m: (m, j)),
            pl.BlockSpec((None, 1, tn), lambda j, m: (mrow(m), 0, gate_idx * ncol + j)),
        ],
        out_specs=pl.BlockSpec((tm, tn), lambda j, m: (m, j)),
        scratch_shapes=[pltpu.VMEM((w_o.shape[1], tn), BF16)],
        compiler_params=_cparams(("parallel", "arbitrary")),
        name="out_proj_residual",
    )(mrg, w_o, xall, mod3)


def _router_kernel(x_ref, g_ref, shift_ref, scale_ref, wr_ref, bias_ref, h_ref, ids_ref, wts_ref, cnt_ref,
                   *, per_group):
    h = _norm_mod(x_ref[...], g_ref[...], shift_ref[...], scale_ref[...])
    h_ref[...] = h.astype(h_ref.dtype)
    h_hi = h.astype(BF16)
    h_lo = (h - h_hi.astype(F32)).astype(BF16)
    w = wr_ref[...]
    w_hi = w.astype(BF16)
    w_lo = (w - w_hi.astype(F32)).astype(BF16)
    logits = (jnp.dot(h_hi, w_hi, preferred_element_type=F32)
              + jnp.dot(h_hi, w_lo, preferred_element_type=F32)
              + jnp.dot(h_lo, w_hi, preferred_element_type=F32))
    lane_i = lax.broadcasted_iota(jnp.int32, (h.shape[0], LANES), 1)
    lane = lane_i.astype(F32)
    valid = lane_i < N_GROUPS
    neg = jnp.float32(-jnp.inf)
    score, sel = [], []
    for j in range(per_group):
        s = _sigmoid(logits[:, j * LANES:(j + 1) * LANES])
        score.append(s)
        sel.append(jnp.where(valid, s + bias_ref[j:j + 1, :], neg))
    assert per_group == 4
    hi1, lo1 = jnp.maximum(sel[0], sel[1]), jnp.minimum(sel[0], sel[1])
    hi2, lo2 = jnp.maximum(sel[2], sel[3]), jnp.minimum(sel[2], sel[3])
    top1 = jnp.maximum(hi1, hi2)
    top2 = jnp.maximum(jnp.minimum(hi1, hi2), jnp.maximum(lo1, lo2))
    gscore = jnp.where(valid, top1 + top2, neg)
    gmax = jnp.max(gscore, axis=-1, keepdims=True)
    grp = jnp.min(jnp.where(gscore == gmax, lane, float(LANES)), axis=-1, keepdims=True)
    pick = lane == grp
    zero = jnp.float32(0.0)
    sel_g = [jnp.sum(jnp.where(pick, s, zero), axis=-1, keepdims=True) for s in sel]
    sc_g = [jnp.sum(jnp.where(pick, s, zero), axis=-1, keepdims=True) for s in score]
    best, best_i, best_s = sel_g[0], jnp.zeros_like(grp), sc_g[0]
    for j in range(1, per_group):
        better = sel_g[j] > best
        best = jnp.where(better, sel_g[j], best)
        best_i = jnp.where(better, float(j), best_i)
        best_s = jnp.where(better, sc_g[j], best_s)
    second, second_i, second_s = jnp.full_like(best, neg), jnp.zeros_like(grp), jnp.zeros_like(best)
    for j in range(per_group):
        better = (sel_g[j] > second) & (best_i != float(j))
        second = jnp.where(better, sel_g[j], second)
        second_i = jnp.where(better, float(j), second_i)
        second_s = jnp.where(better, sc_g[j], second_s)
    tot = best_s + second_s
    e0 = grp * per_group + best_i
    e1 = grp * per_group + second_i
    @pl.when(pl.program_id(0) == 0)
    def _():
        cnt_ref[...] = jnp.zeros_like(cnt_ref)

    oh0 = lane == e0
    oh1 = lane == e1
    both = jnp.where(oh0 | oh1, jnp.float32(1.0), zero)
    tm = h.shape[0]
    earlier = (lax.broadcasted_iota(jnp.int32, (tm, tm), 1)
               < lax.broadcasted_iota(jnp.int32, (tm, tm), 0)).astype(BF16)
    before = jnp.dot(earlier, both.astype(BF16), preferred_element_type=F32) + cnt_ref[0:1, :]
    r0 = jnp.sum(jnp.where(oh0, before, zero), axis=-1, keepdims=True)
    r1 = jnp.sum(jnp.where(oh1, before, zero), axis=-1, keepdims=True)
    cnt_ref[...] = cnt_ref[...] + jnp.sum(both, axis=0, keepdims=True)
    ids = jnp.where(lane_i == 0, e0, jnp.where(lane_i == 1, e1, jnp.where(lane_i == 2, r0,
                    jnp.where(lane_i == 3, r1, zero))))
    ids_ref[...] = ids.astype(jnp.int32)
    wts_ref[...] = jnp.where(lane_i == 0, best_s / tot, jnp.where(lane_i == 1, second_s / tot, zero))


def router(xall, g, mod3, wr_pad, bias_pad, *, layer, shift_idx, n_rows, geom, per_group):
    batch, seq, _ = geom
    d = xall.shape[1]
    tm = min(TILES["router"], seq)
    mrow = _mod_row_fn(batch * seq // tm, seq // tm, batch, layer)
    kern = functools.partial(_router_kernel, per_group=per_group)
    return pl.pallas_call(
        kern,
        out_shape=(jax.ShapeDtypeStruct((n_rows, d), BF16),
                   jax.ShapeDtypeStruct((n_rows, LANES), jnp.int32),
                   jax.ShapeDtypeStruct((n_rows, LANES), F32),
                   jax.ShapeDtypeStruct((8, LANES), F32)),
        grid=(n_rows // tm,),
        in_specs=[
            pl.BlockSpec((tm, d), lambda m: (m, 0)),
            pl.BlockSpec((1, d), lambda m: (0, 0)),
            pl.BlockSpec((None, 1, d), lambda m: (mrow(m), 0, shift_idx)),
            pl.BlockSpec((None, 1, d), lambda m: (mrow(m), 0, shift_idx + 1)),
            pl.BlockSpec(wr_pad.shape, lambda m: (0, 0)),
            pl.BlockSpec(bias_pad.shape, lambda m: (0, 0)),
        ],
        out_specs=(pl.BlockSpec((tm, d), lambda m: (m, 0)),
                   pl.BlockSpec((tm, LANES), lambda m: (m, 0)),
                   pl.BlockSpec((tm, LANES), lambda m: (m, 0)),
                   pl.BlockSpec((8, LANES), lambda m: (0, 0))),
        compiler_params=_cparams(("arbitrary",)),
        name="router",
    )(xall, g.reshape(1, d), mod3, mod3, wr_pad, bias_pad)


def _expert_kernel(be_ref, x_ref, win_ref, wout_ref, o_ref, win_bf, wout_bf, *, ff):
    i = pl.program_id(0)

    @pl.when((i == 0) | (be_ref[i] != be_ref[jnp.maximum(i - 1, 0)]))
    def _():
        win_bf[...] = win_ref[...].astype(BF16)
        wout_bf[...] = wout_ref[...].astype(BF16)

    gu = jnp.dot(x_ref[...], win_bf[...], preferred_element_type=F32)
    g = gu[:, :ff]
    act = (g * _sigmoid(g) * gu[:, ff:]).astype(BF16)
    o_ref[...] = jnp.dot(act, wout_bf[...], preferred_element_type=F32).astype(o_ref.dtype)


def expert_ffn(xs, block_e, w_in, w_out, *, layer):
    rows, d = xs.shape
    blk = TILES["moe_block"]
    ff = w_out.shape[2]
    kern = functools.partial(_expert_kernel, ff=ff)
    grid_spec = pltpu.PrefetchScalarGridSpec(
        num_scalar_prefetch=1,
        grid=(rows // blk,),
        in_specs=[
            pl.BlockSpec((blk, d), lambda i, be: (i, 0)),
            pl.BlockSpec((None, None, d, 2 * ff), lambda i, be: (layer, be[i], 0, 0)),
            pl.BlockSpec((None, None, ff, d), lambda i, be: (layer, be[i], 0, 0)),
        ],
        out_specs=pl.BlockSpec((blk, d), lambda i, be: (i, 0)),
        scratch_shapes=[pltpu.VMEM((d, 2 * ff), BF16), pltpu.VMEM((ff, d), BF16)],
    )
    return pl.pallas_call(
        kern,
        out_shape=jax.ShapeDtypeStruct((rows, d), BF16),
        grid_spec=grid_spec,
        compiler_params=pltpu.CompilerParams(dimension_semantics=("arbitrary",),
                                             vmem_limit_bytes=TILES["vmem_expert"]),
        name="expert_ffn",
    )(block_e, xs, w_in, w_out)


def _combine_kernel(x_ref, y0_ref, y1_ref, w_ref, gate_ref, o_ref):
    w = w_ref[...]
    y = w[:, 0:1] * y0_ref[...].astype(F32) + w[:, 1:2] * y1_ref[...].astype(F32)
    o_ref[...] = x_ref[...] + gate_ref[...] * y


def moe_combine(xall, y0, y1, wts, mod3, *, layer, gate_idx, n_rows, geom):
    batch, seq, _ = geom
    d = xall.shape[1]
    tm = min(TILES["row"], seq)
    mrow = _mod_row_fn(batch * seq // tm, seq // tm, batch, layer)
    return pl.pallas_call(
        _combine_kernel,
        out_shape=jax.ShapeDtypeStruct((n_rows, d), F32),
        grid=(n_rows // tm,),
        in_specs=[
            pl.BlockSpec((tm, d), lambda m: (m, 0)),
            pl.BlockSpec((tm, d), lambda m: (m, 0)),
            pl.BlockSpec((tm, d), lambda m: (m, 0)),
            pl.BlockSpec((tm, LANES), lambda m: (m, 0)),
            pl.BlockSpec((None, 1, d), lambda m: (mrow(m), 0, gate_idx)),
        ],
        out_specs=pl.BlockSpec((tm, d), lambda m: (m, 0)),
        compiler_params=_cparams(("parallel",)),
        name="moe_combine",
    )(xall, y0, y1, wts, mod3)


def moe_dispatch(ids, counts, n_experts):
    blk = TILES["moe_block"]
    n = ids.shape[0]
    experts, rank = ids[:, :TOP_K], ids[:, TOP_K:2 * TOP_K]
    counts = counts[0, :n_experts].astype(jnp.int32)
    padded = (counts + blk - 1) // blk * blk
    pend = jnp.cumsum(padded)
    pstart = pend - padded
    eye = experts[:, :, None] == jnp.arange(n_experts, dtype=jnp.int32)[None, None, :]
    dest = jnp.sum(jnp.where(eye, pstart[None, None, :], 0), axis=-1) + rank
    n_blocks = -(-n * TOP_K // blk) + n_experts
    rows = n_blocks * blk
    tok = jnp.broadcast_to(jnp.arange(n, dtype=jnp.int32)[:, None], (n, TOP_K))
    row_tok = (jnp.arange(rows, dtype=jnp.int32) % n).at[dest.reshape(-1)].set(tok.reshape(-1))
    block_start = jnp.arange(n_blocks, dtype=jnp.int32) * blk
    block_e = jnp.minimum(jnp.sum(block_start[:, None] >= pend[None, :], axis=1), n_experts - 1).astype(jnp.int32)
    return row_tok, dest, block_e


def _rope_tables(seq, tm):
    inv = ROPE_THETA ** (-jnp.arange(0, AXIS_DIM, 2, dtype=F32) / AXIS_DIM)
    t = jnp.arange(seq)
    ang_r = (t // GRID_W).astype(F32)[:, None] * inv
    ang_c = (t % GRID_W).astype(F32)[:, None] * inv
    cos = jnp.concatenate([jnp.cos(ang_r), jnp.cos(ang_r), jnp.cos(ang_c), jnp.cos(ang_c)], axis=-1)
    sin = jnp.concatenate([-jnp.sin(ang_r), jnp.sin(ang_r), -jnp.sin(ang_c), jnp.sin(ang_c)], axis=-1)
    pad = LANES - ROPE_DIM
    cos = jnp.pad(cos, ((0, 0), (0, pad)))
    sin = jnp.pad(sin, ((0, 0), (0, pad)))
    ident_c = jnp.pad(jnp.ones((tm, ROPE_DIM), F32), ((0, 0), (0, pad)))
    return (jnp.concatenate([ident_c, cos], axis=0),
            jnp.concatenate([jnp.zeros((tm, LANES), F32), sin], axis=0))


def _partner():
    j = jnp.arange(ROPE_DIM)
    half = AXIS_DIM // 2
    return jnp.where((j % AXIS_DIM) < half, j + half, j - half)


def _pad_lanes(a):
    return jnp.pad(a, [(0, 0)] * (a.ndim - 1) + [(0, LANES - a.shape[-1])])


def _norm_gain_rows(g):
    rope = g[NOPE_DIM:]
    rows = jnp.stack([g[:NOPE_DIM], _pad_lanes(rope), _pad_lanes(rope[_partner()])])
    return jnp.pad(rows, ((0, 8 - rows.shape[0]), (0, 0))).astype(F32)


def _layer_weights(l, w_in, w_uq, w_ukv, q_lora, kv_lora):
    d = w_in.shape[1]
    heads = MLA_HEADS
    wl = w_in[l]
    gate_cols = 3 * d
    conv_a2 = (wl.shape[1] - gate_cols - (q_lora + kv_lora + ROPE_DIM)) // 5 * 2
    off_b = gate_cols + conv_a2
    off_c = off_b + q_lora + kv_lora + ROPE_DIM
    w_c = wl[:, off_c:]
    kpe = wl[:, off_b + q_lora + kv_lora:off_c]
    w_b = jnp.concatenate([wl[:, off_b:off_b + q_lora + kv_lora], _pad_lanes(kpe),
                           _pad_lanes(kpe[:, _partner()])], axis=1)
    wq = w_uq[l].reshape(q_lora, heads, QK_DIM)
    wq_n = wq[:, :, :NOPE_DIM].reshape(q_lora, heads * LANES)
    wq_r = _pad_lanes(wq[:, :, NOPE_DIM:]).reshape(q_lora, heads * LANES)
    wq_s = _pad_lanes(wq[:, :, NOPE_DIM:][:, :, _partner()]).reshape(q_lora, heads * LANES)
    wq_all = jnp.concatenate([wq_n, wq_r, wq_s], axis=1).astype(BF16)
    wkv = w_ukv[l].reshape(kv_lora, heads, NOPE_DIM + V_DIM)
    wkv_all = jnp.concatenate([wkv[:, :, :NOPE_DIM].reshape(kv_lora, heads * NOPE_DIM),
                               wkv[:, :, NOPE_DIM:].reshape(kv_lora, heads * V_DIM)], axis=1).astype(BF16)
    return off_b, w_c, w_b, wq_all, wkv_all


def _router_layout(w_router, router_bias):
    d, n_experts = w_router.shape
    per_group = n_experts // N_GROUPS
    wr = w_router.reshape(d, N_GROUPS, per_group).transpose(0, 2, 1)
    wr_pad = _pad_lanes(wr).reshape(d, per_group * LANES).astype(F32)
    bias = _pad_lanes(router_bias.reshape(N_GROUPS, per_group).T).astype(F32)
    bias = jnp.pad(bias, ((0, 8 - per_group), (0, 0)))
    return wr_pad, bias, per_group


def kernel(x, c, ctx, c_ctx, w_mod, b_mod, g_norm1, g_norm2, w_in, conv_a, ln_a_g, ln_a_b, w_a_out, g_cq, g_ckv, w_uq, w_ukv, g_q, g_k, w_b_out, conv_c, w_c_out, w_o, w_router, router_bias, w_e_in, w_e_out):
    batch, seq, d = x.shape
    n_ctx = ctx.shape[1]
    depth = w_mod.shape[0]
    geom = (batch, seq, n_ctx)
    t_x = batch * seq
    t_all = t_x + batch * n_ctx
    q_lora, kv_lora = g_cq.shape[1], g_ckv.shape[1]
    ch_a, ch_c = conv_a.shape[2], conv_c.shape[2]
    n_experts = w_router.shape[1]
    assert batch + 1 <= MOD_ROWS and t_x % n_ctx == 0 and ch_a == ch_c and (3 * d) % ch_a == 0

    xall = jnp.concatenate([x.reshape(t_x, d), ctx.reshape(batch * n_ctx, d)], axis=0)
    cvec = jnp.concatenate([c, c_ctx[None, :], jnp.zeros((MOD_ROWS - batch - 1, d), F32)], axis=0)
    mod3 = modulation(cvec, w_mod, b_mod).reshape(depth * MOD_ROWS, 1, N_MOD * d)

    cos_tab, sin_tab = _rope_tables(seq, min(TILES["mla"], seq, n_ctx))
    wr_pad, bias_pad, per_group = _router_layout(w_router, router_bias)
    col0 = 3 * d // ch_a

    for l in range(depth):
        last = l == depth - 1
        rows_main = t_x if last else t_all
        off_b, w_c, w_b, wq_all, wkv_all = _layer_weights(l, w_in, w_uq, w_ukv, q_lora, kv_lora)

        h = norm_modulate(xall, g_norm1[l], mod3, layer=l, shift_idx=0, geom=geom)
        u = project(h, w_in, n_cols=off_b, n_rows=rows_main, layer=l)
        uc = project(h, w_c, n_cols=w_c.shape[1], n_rows=rows_main)
        ub = project(h, w_b, n_cols=w_b.shape[1], n_rows=t_all)
        qt, k, vt = mla_project(ub, g_cq[l], g_ckv[l], wq_all, wkv_all, _norm_gain_rows(g_q[l]),
                                _norm_gain_rows(g_k[l]), cos_tab, sin_tab, geom=geom)
        ob_x = attention_latent(qt, k, vt, geom=geom)
        ob_c = ob_x if last else attention_context(qt, k, vt, geom=geom)
        ha = depthwise_branch(u, conv_a[l], col_a=col0, col_b=col0 + 1, col_gate=None,
                              ln=(ln_a_g[l], ln_a_b[l]), n_rows=rows_main, geom=geom)
        hc = depthwise_branch(uc, conv_c[l], col_a=1, col_b=2, col_gate=0,
                              ln=None, n_rows=rows_main, geom=geom)
        mrg = branch_merge(ha, ob_x, ob_c, hc, w_a_out, w_b_out, w_c_out, u, layer=l,
                           n_rows=rows_main, geom=geom)
        x1 = out_proj_residual(mrg, w_o, xall, mod3, layer=l, gate_idx=2, n_rows=rows_main, geom=geom)

        h2, ids, wts, counts = router(x1, g_norm2[l], mod3, wr_pad, bias_pad, layer=l, shift_idx=3,
                                      n_rows=rows_main, geom=geom, per_group=per_group)
        row_tok, dest, block_e = moe_dispatch(ids, counts, n_experts)
        ys = expert_ffn(h2.at[row_tok].get(mode="promise_in_bounds"), block_e, w_e_in, w_e_out, layer=l)
        y0 = ys.at[dest[:, 0]].get(mode="promise_in_bounds")
        y1 = ys.at[dest[:, 1]].get(mode="promise_in_bounds")
        xall = moe_combine(x1, y0, y1, wts, mod3, layer=l, gate_idx=5, n_rows=rows_main, geom=geom)

    return xall[:t_x].reshape(batch, seq, d)
```

```python
import functools
import math

import jax
import jax.numpy as jnp
from jax import lax
from jax.experimental import pallas as pl
from jax.experimental.pallas import tpu as pltpu

F32 = jnp.float32
BF16 = jnp.bfloat16

GRID_W = 64
MLA_HEADS = 8
NOPE_DIM = 128
ROPE_DIM = 64
V_DIM = 128
QK_DIM = NOPE_DIM + ROPE_DIM
AXIS_DIM = ROPE_DIM // 2
ROPE_THETA = 10000.0
N_GROUPS = 8
TOP_K = 2
N_MOD = 6
EPS = 1e-6

LANES = 128
HEAD_SLOT = 2 * LANES
HALO_ROWS = 16
MOD_ROWS = 8

TILES = dict(
    row=512,
    proj_row=1024,
    col=1024,
    mod_col=1024,
    mla=256,
    tq=512,
    tq_sub=512,
    attn_keys=512,
    conv=256,
    conv_rows=64,
    router=512,
    moe_block=256,
    vmem=56 * 1024 * 1024,
    vmem_expert=60 * 1024 * 1024,
)


def _cparams(sem):
    return pltpu.CompilerParams(dimension_semantics=sem, vmem_limit_bytes=TILES["vmem"])


def _sigmoid(x):
    return 1.0 / (1.0 + jnp.exp(-x))


def _mod_kernel(c_ref, w_ref, b_ref, o_ref):
    c = c_ref[...]
    s = (c * _sigmoid(c)).astype(BF16)
    o_ref[...] = jnp.dot(s, w_ref[...].astype(BF16), preferred_element_type=F32) + b_ref[...]


def modulation(cvec, w_mod, b_mod):
    n_layers, d, n = w_mod.shape
    tn = min(TILES["mod_col"], n)
    return pl.pallas_call(
        _mod_kernel,
        out_shape=jax.ShapeDtypeStruct((n_layers, MOD_ROWS, n), F32),
        grid=(n_layers, n // tn),
        in_specs=[
            pl.BlockSpec((MOD_ROWS, d), lambda l, j: (0, 0)),
            pl.BlockSpec((None, d, tn), lambda l, j: (l, 0, j)),
            pl.BlockSpec((None, 1, tn), lambda l, j: (l, 0, j)),
        ],
        out_specs=pl.BlockSpec((None, MOD_ROWS, tn), lambda l, j: (l, 0, j)),
        compiler_params=_cparams(("parallel", "parallel")),
        name="modulation",
    )(cvec, w_mod, b_mod.reshape(n_layers, 1, n))


def _norm_mod(x, g, shift, scale):
    ms = jnp.mean(x * x, axis=-1, keepdims=True)
    return x * lax.rsqrt(ms + EPS) * g * (1.0 + scale) + shift


def _stream_specs(stream, tm, n_x_tiles, tn=None, grid_rank=1):
    arrays = stream if isinstance(stream, tuple) else (stream,)
    width = arrays[0].shape[1] if tn is None else tn

    def spec(row_fn):
        if grid_rank == 1:
            return pl.BlockSpec((tm, width), lambda m: (row_fn(m), 0))
        return pl.BlockSpec((tm, width), lambda j, m: (row_fn(m), j))

    if len(arrays) == 1:
        return [spec(lambda m: m)], list(arrays)
    return [spec(lambda m: jnp.minimum(m, n_x_tiles - 1)),
            spec(lambda m: jnp.maximum(m - n_x_tiles, 0))], list(arrays)


def _stream_tile(refs, m, n_x_tiles):
    if len(refs) == 1:
        return refs[0][...]
    return jnp.where(m < n_x_tiles, refs[0][...], refs[1][...])


def _norm_kernel(*refs, n_x_tiles):
    x_refs, (g_ref, shift_ref, scale_ref, h_ref) = refs[:-4], refs[-4:]
    x = _stream_tile(x_refs, pl.program_id(0), n_x_tiles)
    h_ref[...] = _norm_mod(x, g_ref[...], shift_ref[...], scale_ref[...]).astype(h_ref.dtype)


def _mod_row_fn(n_x_tiles, tiles_per_seq, batch, layer):
    def f(m):
        return layer * MOD_ROWS + jnp.where(m < n_x_tiles, m // tiles_per_seq, batch)
    return f


def norm_modulate(stream, g, mod3, *, layer, shift_idx, n_rows, geom):
    batch, seq, _ = geom
    d = g.shape[0]
    tm = min(TILES["row"], seq)
    n_x = batch * seq // tm
    mrow = _mod_row_fn(n_x, seq // tm, batch, layer)
    x_specs, x_args = _stream_specs(stream, tm, n_x)
    return pl.pallas_call(
        functools.partial(_norm_kernel, n_x_tiles=n_x),
        out_shape=jax.ShapeDtypeStruct((n_rows, d), BF16),
        grid=(n_rows // tm,),
        in_specs=x_specs + [
            pl.BlockSpec((1, d), lambda m: (0, 0)),
            pl.BlockSpec((None, 1, d), lambda m: (mrow(m), 0, shift_idx)),
            pl.BlockSpec((None, 1, d), lambda m: (mrow(m), 0, shift_idx + 1)),
        ],
        out_specs=pl.BlockSpec((tm, d), lambda m: (m, 0)),
        compiler_params=_cparams(("parallel",)),
        name="norm_modulate",
    )(*x_args, g.reshape(1, d), mod3, mod3)


def _proj_kernel(a_ref, w_ref, o_ref, w_bf):
    @pl.when(pl.program_id(1) == 0)
    def _():
        w_bf[...] = w_ref[...].astype(BF16)

    o_ref[...] = jnp.dot(a_ref[...], w_bf[...], preferred_element_type=F32).astype(o_ref.dtype)


def _w_spec(w, layer, tn, col0=0):
    if w.ndim == 3:
        return pl.BlockSpec((None, w.shape[1], tn), lambda j, m: (layer, 0, col0 + j))
    return pl.BlockSpec((w.shape[0], tn), lambda j, m: (0, col0 + j))


def project(a, w, *, n_cols, n_rows, layer=0):
    k = a.shape[1]
    tm = min(TILES["proj_row"], n_rows)
    tn = min(TILES["col"], n_cols)
    if n_cols % tn:
        tn = n_cols
    return pl.pallas_call(
        _proj_kernel,
        out_shape=jax.ShapeDtypeStruct((n_rows, n_cols), BF16),
        grid=(n_cols // tn, n_rows // tm),
        in_specs=[pl.BlockSpec((tm, k), lambda j, m: (m, 0)), _w_spec(w, layer, tn)],
        out_specs=pl.BlockSpec((tm, tn), lambda j, m: (m, j)),
        scratch_shapes=[pltpu.VMEM((k, tn), BF16)],
        compiler_params=_cparams(("parallel", "arbitrary")),
        name="project",
    )(a, w)


def _mla_kernel(ub_ref, gcq_ref, gckv_ref, wq_ref, wkv_ref, gq_ref, gk_ref, cos_ref, sin_ref,
                qt_ref, k_ref, vt_ref, *, q_lora, kv_lora, heads):
    hn = heads * LANES

    def rms(xv, g):
        return (xv * lax.rsqrt(jnp.mean(xv * xv, axis=-1, keepdims=True) + EPS) * g).astype(BF16)

    cqn = rms(ub_ref[:, 0:q_lora].astype(F32), gcq_ref[...])
    ckvn = rms(ub_ref[:, q_lora:q_lora + kv_lora].astype(F32), gckv_ref[...])
    off = q_lora + kv_lora
    kpe = ub_ref[:, off:off + LANES].astype(F32)
    kpes = ub_ref[:, off + LANES:off + 2 * LANES].astype(F32)

    qall = jnp.dot(cqn, wq_ref[...], preferred_element_type=F32)
    kvall = jnp.dot(ckvn, wkv_ref[...], preferred_element_type=F32)

    cos = cos_ref[...]
    sin = sin_ref[...]
    gq_n, gq_r, gq_s = gq_ref[0:1, :], gq_ref[1:2, :], gq_ref[2:3, :]
    gk_n, gk_r, gk_s = gk_ref[0:1, :], gk_ref[1:2, :], gk_ref[2:3, :]
    q_cos, q_sin = gq_r * cos, gq_s * sin
    k_rope = kpe * (gk_r * cos) + kpes * (gk_s * sin)
    ssq_pe = jnp.sum(kpe * kpe, axis=-1, keepdims=True)
    q_scale = QK_DIM ** -0.5 * math.log2(math.e)

    for h in range(heads):
        lo = h * LANES
        qn = qall[:, lo:lo + LANES]
        qr = qall[:, hn + lo:hn + lo + LANES]
        qs = qall[:, 2 * hn + lo:2 * hn + lo + LANES]
        ssq = jnp.sum(qn * qn, axis=-1, keepdims=True) + jnp.sum(qr * qr, axis=-1, keepdims=True)
        inv = lax.rsqrt(ssq * (1.0 / QK_DIM) + EPS) * q_scale
        qt_ref[h * HEAD_SLOT:h * HEAD_SLOT + LANES, :] = (qn * gq_n * inv).T.astype(BF16)
        qt_ref[h * HEAD_SLOT + LANES:(h + 1) * HEAD_SLOT, :] = ((qr * q_cos + qs * q_sin) * inv).T.astype(BF16)

        kn = kvall[:, lo:lo + LANES]
        invk = lax.rsqrt((jnp.sum(kn * kn, axis=-1, keepdims=True) + ssq_pe) * (1.0 / QK_DIM) + EPS)
        k_ref[:, h * HEAD_SLOT:h * HEAD_SLOT + LANES] = (kn * gk_n * invk).astype(BF16)
        k_ref[:, h * HEAD_SLOT + LANES:(h + 1) * HEAD_SLOT] = (k_rope * invk).astype(BF16)

        vt_ref[h * V_DIM:(h + 1) * V_DIM, :] = kvall[:, hn + h * V_DIM:hn + (h + 1) * V_DIM].T.astype(BF16)


def mla_project(ub, g_cq, g_ckv, wq, wkv, gq3, gk3, cos_tab, sin_tab, *, geom):
    batch, seq, ctx = geom
    t_all = ub.shape[0]
    q_lora, kv_lora = g_cq.shape[0], g_ckv.shape[0]
    heads = MLA_HEADS
    tm = min(TILES["mla"], seq, ctx)
    n_x = batch * seq // tm
    per_seq = seq // tm

    def pos(m):
        return jnp.where(m < n_x, 1 + m % per_seq, 0)

    kern = functools.partial(_mla_kernel, q_lora=q_lora, kv_lora=kv_lora, heads=heads)
    return pl.pallas_call(
        kern,
        out_shape=(
            jax.ShapeDtypeStruct((heads * HEAD_SLOT, t_all), BF16),
            jax.ShapeDtypeStruct((t_all, heads * HEAD_SLOT), BF16),
            jax.ShapeDtypeStruct((heads * V_DIM, t_all), BF16),
        ),
        grid=(t_all // tm,),
        in_specs=[
            pl.BlockSpec((tm, ub.shape[1]), lambda m: (m, 0)),
            pl.BlockSpec((1, q_lora), lambda m: (0, 0)),
            pl.BlockSpec((1, kv_lora), lambda m: (0, 0)),
            pl.BlockSpec(wq.shape, lambda m: (0, 0)),
            pl.BlockSpec(wkv.shape, lambda m: (0, 0)),
            pl.BlockSpec((8, LANES), lambda m: (0, 0)),
            pl.BlockSpec((8, LANES), lambda m: (0, 0)),
            pl.BlockSpec((tm, LANES), lambda m: (pos(m), 0)),
            pl.BlockSpec((tm, LANES), lambda m: (pos(m), 0)),
        ],
        out_specs=(
            pl.BlockSpec((heads * HEAD_SLOT, tm), lambda m: (0, m)),
            pl.BlockSpec((tm, heads * HEAD_SLOT), lambda m: (m, 0)),
            pl.BlockSpec((heads * V_DIM, tm), lambda m: (0, m)),
        ),
        compiler_params=_cparams(("parallel",)),
        name="mla_project",
    )(ub, g_cq.reshape(1, -1), g_ckv.reshape(1, -1), wq, wkv, gq3, gk3, cos_tab, sin_tab)


def _attn_kernel(*refs, sub, latent_keys):
    if latent_keys:
        qt_ref, kx_ref, kc_ref, vxt_ref, vct_ref, o_ref = refs
    else:
        qt_ref, kc_ref, vct_ref, o_ref = refs
    for j in range(qt_ref.shape[1] // sub):
        qt = qt_ref[:, j * sub:(j + 1) * sub]
        sc = jnp.dot(kc_ref[...], qt, preferred_element_type=F32)
        m = jnp.max(sc, axis=0, keepdims=True)
        if latent_keys:
            sx = jnp.dot(kx_ref[...], qt, preferred_element_type=F32)
            m = jnp.maximum(m, jnp.max(sx, axis=0, keepdims=True))
        pc = jnp.exp2(sc - m)
        denom = jnp.sum(pc, axis=0, keepdims=True)
        ot = jnp.dot(vct_ref[...], pc.astype(BF16), preferred_element_type=F32)
        if latent_keys:
            px = jnp.exp2(sx - m)
            denom = denom + jnp.sum(px, axis=0, keepdims=True)
            ot = ot + jnp.dot(vxt_ref[...], px.astype(BF16), preferred_element_type=F32)
        o_ref[j * sub:(j + 1) * sub, :] = (ot * (1.0 / denom)).T.astype(o_ref.dtype)


def _attn_pipelined_kernel(qt_ref, kx_ref, kc_ref, vxt_ref, vct_ref, o_ref, s0_ref, s1_ref, m0_ref, m1_ref,
                           *, n_q, chunk):
    i = pl.program_id(2)
    seq, ctx = kx_ref.shape[0], kc_ref.shape[0]
    tq = qt_ref.shape[1]
    chunks = [(kc_ref, vct_ref, 0, ctx, 0)]
    chunks += [(kx_ref, vxt_ref, c0, chunk, ctx + c0) for c0 in range(0, seq, chunk)]
    slots = ((s0_ref, m0_ref), (s1_ref, m1_ref))

    def run(new, old):
        if new is not None:
            s_new, m_new = slots[new]
            qt = qt_ref[...]
            m = jnp.full((1, tq), -jnp.inf, F32)
        if old is not None:
            s_old, m_old_ref = slots[old]
            m_old = m_old_ref[...]
            acc = jnp.zeros((V_DIM, tq), F32)
            den = jnp.zeros((1, tq), F32)
        for kref, vtref, c0, size, row0 in chunks:
            if new is not None:
                s = jnp.dot(kref[c0:c0 + size, :], qt, preferred_element_type=F32)
                s_new[row0:row0 + size, :] = s
                m = jnp.maximum(m, jnp.max(s, axis=0, keepdims=True))
            if old is not None:
                p = jnp.exp2(s_old[row0:row0 + size, :] - m_old)
                den = den + jnp.sum(p, axis=0, keepdims=True)
                acc = acc + jnp.dot(vtref[:, c0:c0 + size], p.astype(BF16), preferred_element_type=F32)
        if new is not None:
            m_new[...] = m
        if old is not None:
            o_ref[...] = (acc * (1.0 / den)).T.astype(o_ref.dtype)

    pl.when(i == 0)(lambda: run(0, None))
    pl.when((i > 0) & (i < n_q) & (i % 2 == 1))(lambda: run(1, 0))
    pl.when((i > 0) & (i < n_q) & (i % 2 == 0))(lambda: run(0, 1))
    pl.when(i == n_q)(lambda: run(None, (n_q - 1) % 2))


def attention_latent(qt, k, vt, *, geom):
    batch, seq, ctx = geom
    heads = MLA_HEADS
    tq = min(TILES["tq"], seq)
    n_q = seq // tq
    ctx_blk0 = batch * seq // ctx
    kern = functools.partial(_attn_pipelined_kernel, n_q=n_q, chunk=min(TILES["attn_keys"], seq))
    return pl.pallas_call(
        kern,
        out_shape=jax.ShapeDtypeStruct((batch * seq, heads * V_DIM), BF16),
        grid=(batch, heads, n_q + 1),
        in_specs=[
            pl.BlockSpec((HEAD_SLOT, tq), lambda b, h, i: (h, b * n_q + jnp.minimum(i, n_q - 1))),
            pl.BlockSpec((seq, HEAD_SLOT), lambda b, h, i: (b, h)),
            pl.BlockSpec((ctx, HEAD_SLOT), lambda b, h, i: (ctx_blk0 + b, h)),
            pl.BlockSpec((V_DIM, seq), lambda b, h, i: (h, b)),
            pl.BlockSpec((V_DIM, ctx), lambda b, h, i: (h, ctx_blk0 + b)),
        ],
        out_specs=pl.BlockSpec((tq, V_DIM), lambda b, h, i: (b * n_q + jnp.maximum(i - 1, 0), h)),
        scratch_shapes=[pltpu.VMEM((ctx + seq, tq), F32), pltpu.VMEM((ctx + seq, tq), F32),
                        pltpu.VMEM((1, tq), F32), pltpu.VMEM((1, tq), F32)],
        compiler_params=_cparams(("parallel", "parallel", "arbitrary")),
        name="attention_latent",
    )(qt, k, k, vt, vt)


def attention_context(qt, k, vt, *, geom):
    batch, seq, ctx = geom
    heads = MLA_HEADS
    ctx_blk0 = batch * seq // ctx
    kern = functools.partial(_attn_kernel, sub=min(TILES["tq_sub"], ctx), latent_keys=False)
    return pl.pallas_call(
        kern,
        out_shape=jax.ShapeDtypeStruct((batch * ctx, heads * V_DIM), BF16),
        grid=(batch, heads),
        in_specs=[
            pl.BlockSpec((HEAD_SLOT, ctx), lambda b, h: (h, ctx_blk0 + b)),
            pl.BlockSpec((ctx, HEAD_SLOT), lambda b, h: (ctx_blk0 + b, h)),
            pl.BlockSpec((V_DIM, ctx), lambda b, h: (h, ctx_blk0 + b)),
        ],
        out_specs=pl.BlockSpec((ctx, V_DIM), lambda b, h: (b, h)),
        compiler_params=_cparams(("parallel", "parallel")),
        name="attention_context",
    )(qt, k, vt)


def _conv_kernel(*refs, width, conformer, tp, n_x_tiles, x_per_seq, c_per_seq):
    if conformer:
        (ap, bp, am, bm, an, bn, w_ref, lng_ref, lnb_ref, o_ref, buf, ybuf) = refs
        gate_ref = None
    else:
        (gate_ref, ap, bp, am, bm, an, bn, w_ref, o_ref, buf, ybuf) = refs
    nchunk = buf.shape[0]
    i = pl.program_id(0)
    in_x = i < n_x_tiles
    pos = jnp.where(in_x, i % x_per_seq, (i - n_x_tiles) % c_per_seq)
    per_seq = jnp.where(in_x, x_per_seq, c_per_seq)
    keep_prev = (pos > 0).astype(F32)
    keep_next = (pos < per_seq - 1).astype(F32)

    def pre(a, b):
        a = a.astype(F32)
        b = b.astype(F32)
        return a * _sigmoid(b) if conformer else a * b

    for c in range(nchunk):
        sl = slice(c * LANES, (c + 1) * LANES)
        buf[c, 0:HALO_ROWS, :] = pre(ap[:, sl], bp[:, sl]) * keep_prev
        buf[c, HALO_ROWS:HALO_ROWS + tp, :] = pre(am[:, sl], bm[:, sl])
        buf[c, HALO_ROWS + tp:2 * HALO_ROWS + tp, :] = pre(an[:, sl], bn[:, sl]) * keep_next

    rc = min(TILES["conv_rows"], tp)
    base = HALO_ROWS - width // 2

    def chunk(c, carry):
        wts = w_ref[c]
        for r0 in range(0, tp, rc):
            acc = jnp.zeros((rc, LANES), F32)
            for k in range(width):
                acc = acc + wts[k:k + 1, :] * buf[c, pl.ds(base + r0 + k, rc), :]
            ybuf[c, pl.ds(r0, rc), :] = acc
        return carry

    lax.fori_loop(0, nchunk, chunk, 0)

    if conformer:
        n_ch = nchunk * LANES
        tot = jnp.zeros((tp, 1), F32)
        for c in range(nchunk):
            tot = tot + jnp.sum(ybuf[c], axis=-1, keepdims=True)
        mu = tot * (1.0 / n_ch)
        tot2 = jnp.zeros((tp, 1), F32)
        for c in range(nchunk):
            dlt = ybuf[c] - mu
            tot2 = tot2 + jnp.sum(dlt * dlt, axis=-1, keepdims=True)
        inv = lax.rsqrt(tot2 * (1.0 / n_ch) + EPS)
        for c in range(nchunk):
            sl = slice(c * LANES, (c + 1) * LANES)
            y = (ybuf[c] - mu) * inv * lng_ref[:, sl] + lnb_ref[:, sl]
            o_ref[:, sl] = (y * _sigmoid(y)).astype(o_ref.dtype)
    else:
        for c in range(nchunk):
            sl = slice(c * LANES, (c + 1) * LANES)
            o_ref[:, sl] = (gate_ref[:, sl].astype(F32) * ybuf[c]).astype(o_ref.dtype)


def depthwise_branch(u, conv_w, *, col_a, col_b, col_gate, ln, n_rows, geom):
    batch, seq, ctx = geom
    width, ch = conv_w.shape
    conformer = ln is not None
    tp = min(TILES["conv"], seq, ctx)
    nchunk = ch // LANES
    hb = tp // HALO_ROWS
    last_halo = u.shape[0] // HALO_ROWS - 1
    wpad = -(-width // 8) * 8
    w3 = jnp.pad(conv_w.astype(F32), ((0, wpad - width), (0, 0))).reshape(wpad, nchunk, LANES).transpose(1, 0, 2)

    def prev_map(col):
        return lambda i: (jnp.maximum(i * hb - 1, 0), col)

    def main_map(col):
        return lambda i: (i, col)

    def next_map(col):
        return lambda i: (jnp.minimum((i + 1) * hb, last_halo), col)

    in_specs, args = [], []
    if not conformer:
        in_specs.append(pl.BlockSpec((tp, ch), main_map(col_gate)))
        args.append(u)
    for rows, mk in ((HALO_ROWS, prev_map), (tp, main_map), (HALO_ROWS, next_map)):
        for col in (col_a, col_b):
            in_specs.append(pl.BlockSpec((rows, ch), mk(col)))
            args.append(u)
    in_specs.append(pl.BlockSpec((nchunk, wpad, LANES), lambda i: (0, 0, 0)))
    args.append(w3)
    if conformer:
        for p in ln:
            in_specs.append(pl.BlockSpec((1, ch), lambda i: (0, 0)))
            args.append(p.reshape(1, ch).astype(F32))

    kern = functools.partial(
        _conv_kernel, width=width, conformer=conformer, tp=tp, n_x_tiles=batch * seq // tp,
        x_per_seq=seq // tp, c_per_seq=ctx // tp)
    return pl.pallas_call(
        kern,
        out_shape=jax.ShapeDtypeStruct((n_rows, ch), BF16),
        grid=(n_rows // tp,),
        in_specs=in_specs,
        out_specs=pl.BlockSpec((tp, ch), lambda i: (i, 0)),
        scratch_shapes=[pltpu.VMEM((nchunk, tp + 2 * HALO_ROWS, LANES), F32),
                        pltpu.VMEM((nchunk, tp, LANES), F32)],
        compiler_params=_cparams(("parallel",)),
        name="conformer_conv" if conformer else "short_conv",
    )(*args)


def _merge_kernel(ha_ref, obx_ref, obc_ref, hc_ref, wa_ref, wb_ref, wc_ref, ga_ref, gb_ref, gc_ref, o_ref,
                  wa_bf, wb_bf, wc_bf, *, n_x_tiles):
    m_idx = pl.program_id(1)

    @pl.when(m_idx == 0)
    def _():
        wa_bf[...] = wa_ref[...].astype(BF16)
        wb_bf[...] = wb_ref[...].astype(BF16)
        wc_bf[...] = wc_ref[...].astype(BF16)

    ob = jnp.where(m_idx < n_x_tiles, obx_ref[...], obc_ref[...])
    ya = jnp.dot(ha_ref[...], wa_bf[...], preferred_element_type=F32)
    yb = jnp.dot(ob, wb_bf[...], preferred_element_type=F32)
    yc = jnp.dot(hc_ref[...], wc_bf[...], preferred_element_type=F32)
    m = (_sigmoid(ga_ref[...].astype(F32)) * ya + _sigmoid(gb_ref[...].astype(F32)) * yb
         + _sigmoid(gc_ref[...].astype(F32)) * yc)
    o_ref[...] = m.astype(o_ref.dtype)


def branch_merge(ha, ob_x, ob_c, hc, wa, wb, wc, u, *, layer, n_rows, geom):
    batch, seq, _ = geom
    d = wa.shape[2]
    tm = min(TILES["row"], seq)
    tn = min(TILES["col"], d)
    ncol = d // tn
    n_x = batch * seq // tm
    row_spec = lambda a: pl.BlockSpec((tm, a.shape[1]), lambda j, m: (m, 0))
    gate_spec = lambda k: pl.BlockSpec((tm, tn), lambda j, m: (m, k * ncol + j))
    kern = functools.partial(_merge_kernel, n_x_tiles=n_x)
    return pl.pallas_call(
        kern,
        out_shape=jax.ShapeDtypeStruct((n_rows, d), BF16),
        grid=(ncol, n_rows // tm),
        in_specs=[row_spec(ha),
                  pl.BlockSpec((tm, ob_x.shape[1]), lambda j, m: (jnp.minimum(m, n_x - 1), 0)),
                  pl.BlockSpec((tm, ob_c.shape[1]), lambda j, m: (jnp.maximum(m - n_x, 0), 0)),
                  row_spec(hc), _w_spec(wa, layer, tn), _w_spec(wb, layer, tn), _w_spec(wc, layer, tn),
                  gate_spec(0), gate_spec(1), gate_spec(2)],
        out_specs=pl.BlockSpec((tm, tn), lambda j, m: (m, j)),
        scratch_shapes=[pltpu.VMEM((w.shape[1], tn), BF16) for w in (wa, wb, wc)],
        compiler_params=_cparams(("parallel", "arbitrary")),
        name="branch_merge",
    )(ha, ob_x, ob_c, hc, wa, wb, wc, u, u, u)


def _outproj_kernel(*refs, n_x_tiles):
    (m_ref, w_ref, gate_ref), x_refs, (o_ref, w_bf) = refs[:3], refs[3:-2], refs[-2:]
    m_idx = pl.program_id(1)

    @pl.when(m_idx == 0)
    def _():
        w_bf[...] = w_ref[...].astype(BF16)

    y = jnp.dot(m_ref[...], w_bf[...], preferred_element_type=F32)
    o_ref[...] = _stream_tile(x_refs, m_idx, n_x_tiles) + gate_ref[...] * y


def out_proj_residual(mrg, w_o, stream, mod3, *, layer, gate_idx, n_rows, geom):
    batch, seq, _ = geom
    d = w_o.shape[2]
    tm = min(TILES["row"], seq)
    tn = min(TILES["col"], d)
    ncol = d // tn
    n_x = batch * seq // tm
    mrow = _mod_row_fn(n_x, seq // tm, batch, layer)
    x_specs, x_args = _stream_specs(stream, tm, n_x, tn=tn, grid_rank=2)
    return pl.pallas_call(
        functools.partial(_outproj_kernel, n_x_tiles=n_x),
        out_shape=jax.ShapeDtypeStruct((n_rows, d), F32),
        grid=(ncol, n_rows // tm),
        in_specs=[
            pl.BlockSpec((tm, mrg.shape[1]), lambda j, m: (m, 0)),
            _w_spec(w_o, layer, tn),
            pl.BlockSpec((None, 1, tn), lambda j, m: (mrow(m), 0, gate_idx * ncol + j)),
        ] + x_specs,
        out_specs=pl.BlockSpec((tm, tn), lambda j, m: (m, j)),
        scratch_shapes=[pltpu.VMEM((w_o.shape[1], tn), BF16)],
        compiler_params=_cparams(("parallel", "arbitrary")),
        name="out_proj_residual",
    )(mrg, w_o, mod3, *x_args)


def _router_kernel(x_ref, g_ref, shift_ref, scale_ref, wr_ref, bias_ref, h_ref, ids_ref, wts_ref, cnt_ref,
                   *, per_group):
    h = _norm_mod(x_ref[...], g_ref[...], shift_ref[...], scale_ref[...])
    h_ref[...] = h.astype(h_ref.dtype)
    h_hi = h.astype(BF16)
    h_lo = (h - h_hi.astype(F32)).astype(BF16)
    w = wr_ref[...]
    w_hi = w.astype(BF16)
    w_lo = (w - w_hi.astype(F32)).astype(BF16)
    logits = (jnp.dot(h_hi, w_hi, preferred_element_type=F32)
              + jnp.dot(h_hi, w_lo, preferred_element_type=F32)
              + jnp.dot(h_lo, w_hi, preferred_element_type=F32))
    lane_i = lax.broadcasted_iota(jnp.int32, (h.shape[0], LANES), 1)
    lane = lane_i.astype(F32)
    valid = lane_i < N_GROUPS
    neg = jnp.float32(-jnp.inf)
    score, sel = [], []
    for j in range(per_group):
        s = _sigmoid(logits[:, j * LANES:(j + 1) * LANES])
        score.append(s)
        sel.append(jnp.where(valid, s + bias_ref[j:j + 1, :], neg))
    assert per_group == 4
    hi1, lo1 = jnp.maximum(sel[0], sel[1]), jnp.minimum(sel[0], sel[1])
    hi2, lo2 = jnp.maximum(sel[2], sel[3]), jnp.minimum(sel[2], sel[3])
    top1 = jnp.maximum(hi1, hi2)
    top2 = jnp.maximum(jnp.minimum(hi1, hi2), jnp.maximum(lo1, lo2))
    gscore = jnp.where(valid, top1 + top2, neg)
    gmax = jnp.max(gscore, axis=-1, keepdims=True)
    grp = jnp.min(jnp.where(gscore == gmax, lane, float(LANES)), axis=-1, keepdims=True)
    pick = lane == grp
    zero = jnp.float32(0.0)
    sel_g = [jnp.sum(jnp.where(pick, s, zero), axis=-1, keepdims=True) for s in sel]
    sc_g = [jnp.sum(jnp.where(pick, s, zero), axis=-1, keepdims=True) for s in score]
    best, best_i, best_s = sel_g[0], jnp.zeros_like(grp), sc_g[0]
    for j in range(1, per_group):
        better = sel_g[j] > best
        best = jnp.where(better, sel_g[j], best)
        best_i = jnp.where(better, float(j), best_i)
        best_s = jnp.where(better, sc_g[j], best_s)
    second, second_i, second_s = jnp.full_like(best, neg), jnp.zeros_like(grp), jnp.zeros_like(best)
    for j in range(per_group):
        better = (sel_g[j] > second) & (best_i != float(j))
        second = jnp.where(better, sel_g[j], second)
        second_i = jnp.where(better, float(j), second_i)
        second_s = jnp.where(better, sc_g[j], second_s)
    tot = best_s + second_s
    e0 = grp * per_group + best_i
    e1 = grp * per_group + second_i
    @pl.when(pl.program_id(0) == 0)
    def _():
        cnt_ref[...] = jnp.zeros_like(cnt_ref)

    oh0 = lane == e0
    oh1 = lane == e1
    both = jnp.where(oh0 | oh1, jnp.float32(1.0), zero)
    tm = h.shape[0]
    earlier = (lax.broadcasted_iota(jnp.int32, (tm, tm), 1)
               < lax.broadcasted_iota(jnp.int32, (tm, tm), 0)).astype(BF16)
    before = jnp.dot(earlier, both.astype(BF16), preferred_element_type=F32) + cnt_ref[0:1, :]
    r0 = jnp.sum(jnp.where(oh0, before, zero), axis=-1, keepdims=True)
    r1 = jnp.sum(jnp.where(oh1, before, zero), axis=-1, keepdims=True)
    cnt_ref[...] = cnt_ref[...] + jnp.sum(both, axis=0, keepdims=True)
    ids = jnp.where(lane_i == 0, e0, jnp.where(lane_i == 1, e1, jnp.where(lane_i == 2, r0,
                    jnp.where(lane_i == 3, r1, zero))))
    ids_ref[...] = ids.astype(jnp.int32)
    wts_ref[...] = jnp.where(lane_i == 0, best_s / tot, jnp.where(lane_i == 1, second_s / tot, zero))


def router(xall, g, mod3, wr_pad, bias_pad, *, layer, shift_idx, n_rows, geom, per_group):
    batch, seq, _ = geom
    d = xall.shape[1]
    tm = min(TILES["router"], seq)
    mrow = _mod_row_fn(batch * seq // tm, seq // tm, batch, layer)
    kern = functools.partial(_router_kernel, per_group=per_group)
    return pl.pallas_call(
        kern,
        out_shape=(jax.ShapeDtypeStruct((n_rows, d), BF16),
                   jax.ShapeDtypeStruct((n_rows, LANES), jnp.int32),
                   jax.ShapeDtypeStruct((n_rows, LANES), F32),
                   jax.ShapeDtypeStruct((8, LANES), F32)),
        grid=(n_rows // tm,),
        in_specs=[
            pl.BlockSpec((tm, d), lambda m: (m, 0)),
            pl.BlockSpec((1, d), lambda m: (0, 0)),
            pl.BlockSpec((None, 1, d), lambda m: (mrow(m), 0, shift_idx)),
            pl.BlockSpec((None, 1, d), lambda m: (mrow(m), 0, shift_idx + 1)),
            pl.BlockSpec(wr_pad.shape, lambda m: (0, 0)),
            pl.BlockSpec(bias_pad.shape, lambda m: (0, 0)),
        ],
        out_specs=(pl.BlockSpec((tm, d), lambda m: (m, 0)),
                   pl.BlockSpec((tm, LANES), lambda m: (m, 0)),
                   pl.BlockSpec((tm, LANES), lambda m: (m, 0)),
                   pl.BlockSpec((8, LANES), lambda m: (0, 0))),
        compiler_params=_cparams(("arbitrary",)),
        name="router",
    )(xall, g.reshape(1, d), mod3, mod3, wr_pad, bias_pad)


def _expert_kernel(be_ref, x_ref, win_ref, wout_ref, o_ref, win_bf, wout_bf, *, ff):
    i = pl.program_id(0)

    @pl.when((i == 0) | (be_ref[i] != be_ref[jnp.maximum(i - 1, 0)]))
    def _():
        win_bf[...] = win_ref[...].astype(BF16)
        wout_bf[...] = wout_ref[...].astype(BF16)

    gu = jnp.dot(x_ref[...], win_bf[...], preferred_element_type=F32)
    g = gu[:, :ff]
    act = (g * _sigmoid(g) * gu[:, ff:]).astype(BF16)
    o_ref[...] = jnp.dot(act, wout_bf[...], preferred_element_type=F32).astype(o_ref.dtype)


def expert_ffn(xs, block_e, w_in, w_out, *, layer):
    rows, d = xs.shape
    blk = TILES["moe_block"]
    ff = w_out.shape[2]
    kern = functools.partial(_expert_kernel, ff=ff)
    grid_spec = pltpu.PrefetchScalarGridSpec(
        num_scalar_prefetch=1,
        grid=(rows // blk,),
        in_specs=[
            pl.BlockSpec((blk, d), lambda i, be: (i, 0)),
            pl.BlockSpec((None, None, d, 2 * ff), lambda i, be: (layer, be[i], 0, 0)),
            pl.BlockSpec((None, None, ff, d), lambda i, be: (layer, be[i], 0, 0)),
        ],
        out_specs=pl.BlockSpec((blk, d), lambda i, be: (i, 0)),
        scratch_shapes=[pltpu.VMEM((d, 2 * ff), BF16), pltpu.VMEM((ff, d), BF16)],
    )
    return pl.pallas_call(
        kern,
        out_shape=jax.ShapeDtypeStruct((rows, d), BF16),
        grid_spec=grid_spec,
        compiler_params=pltpu.CompilerParams(dimension_semantics=("arbitrary",),
                                             vmem_limit_bytes=TILES["vmem_expert"]),
        name="expert_ffn",
    )(block_e, xs, w_in, w_out)


def _combine_kernel(x_ref, y_ref, w_ref, gate_ref, o_ref):
    w = w_ref[...]
    d = x_ref.shape[1]
    y = w[:, 0:1] * y_ref[:, :d].astype(F32) + w[:, 1:2] * y_ref[:, d:].astype(F32)
    o_ref[...] = x_ref[...] + gate_ref[...] * y


def moe_combine(xall, y_pair, wts, mod3, *, layer, gate_idx, n_rows, geom):
    batch, seq, _ = geom
    d = xall.shape[1]
    tm = min(TILES["row"], seq)
    mrow = _mod_row_fn(batch * seq // tm, seq // tm, batch, layer)
    return pl.pallas_call(
        _combine_kernel,
        out_shape=jax.ShapeDtypeStruct((n_rows, d), F32),
        grid=(n_rows // tm,),
        in_specs=[
            pl.BlockSpec((tm, d), lambda m: (m, 0)),
            pl.BlockSpec((tm, TOP_K * d), lambda m: (m, 0)),
            pl.BlockSpec((tm, LANES), lambda m: (m, 0)),
            pl.BlockSpec((None, 1, d), lambda m: (mrow(m), 0, gate_idx)),
        ],
        out_specs=pl.BlockSpec((tm, d), lambda m: (m, 0)),
        compiler_params=_cparams(("parallel",)),
        name="moe_combine",
    )(xall, y_pair, wts, mod3)


def moe_dispatch(ids, counts, n_experts):
    blk = TILES["moe_block"]
    n = ids.shape[0]
    experts, rank = ids[:, :TOP_K], ids[:, TOP_K:2 * TOP_K]
    counts = counts[0, :n_experts].astype(jnp.int32)
    padded = (counts + blk - 1) // blk * blk
    pend = jnp.cumsum(padded)
    pstart = pend - padded
    eye = experts[:, :, None] == jnp.arange(n_experts, dtype=jnp.int32)[None, None, :]
    dest = jnp.sum(jnp.where(eye, pstart[None, None, :], 0), axis=-1) + rank
    n_blocks = -(-n * TOP_K // blk) + n_experts
    rows = n_blocks * blk
    tok = jnp.broadcast_to(jnp.arange(n, dtype=jnp.int32)[:, None], (n, TOP_K))
    row_tok = (jnp.arange(rows, dtype=jnp.int32) % n).at[dest.reshape(-1)].set(tok.reshape(-1))
    block_start = jnp.arange(n_blocks, dtype=jnp.int32) * blk
    block_e = jnp.minimum(jnp.sum(block_start[:, None] >= pend[None, :], axis=1), n_experts - 1).astype(jnp.int32)
    return row_tok, dest, block_e


def _rope_tables(seq, tm):
    inv = ROPE_THETA ** (-jnp.arange(0, AXIS_DIM, 2, dtype=F32) / AXIS_DIM)
    t = jnp.arange(seq)
    ang_r = (t // GRID_W).astype(F32)[:, None] * inv
    ang_c = (t % GRID_W).astype(F32)[:, None] * inv
    cos = jnp.concatenate([jnp.cos(ang_r), jnp.cos(ang_r), jnp.cos(ang_c), jnp.cos(ang_c)], axis=-1)
    sin = jnp.concatenate([-jnp.sin(ang_r), jnp.sin(ang_r), -jnp.sin(ang_c), jnp.sin(ang_c)], axis=-1)
    pad = LANES - ROPE_DIM
    cos = jnp.pad(cos, ((0, 0), (0, pad)))
    sin = jnp.pad(sin, ((0, 0), (0, pad)))
    ident_c = jnp.pad(jnp.ones((tm, ROPE_DIM), F32), ((0, 0), (0, pad)))
    return (jnp.concatenate([ident_c, cos], axis=0),
            jnp.concatenate([jnp.zeros((tm, LANES), F32), sin], axis=0))


def _partner():
    j = jnp.arange(ROPE_DIM)
    half = AXIS_DIM // 2
    return jnp.where((j % AXIS_DIM) < half, j + half, j - half)


def _pad_lanes(a):
    return jnp.pad(a, [(0, 0)] * (a.ndim - 1) + [(0, LANES - a.shape[-1])])


def _norm_gain_rows(g):
    rope = g[NOPE_DIM:]
    rows = jnp.stack([g[:NOPE_DIM], _pad_lanes(rope), _pad_lanes(rope[_partner()])])
    return jnp.pad(rows, ((0, 8 - rows.shape[0]), (0, 0))).astype(F32)


def _layer_weights(l, w_in, w_uq, w_ukv, q_lora, kv_lora):
    d = w_in.shape[1]
    heads = MLA_HEADS
    wl = w_in[l]
    gate_cols = 3 * d
    conv_a2 = (wl.shape[1] - gate_cols - (q_lora + kv_lora + ROPE_DIM)) // 5 * 2
    off_b = gate_cols + conv_a2
    off_c = off_b + q_lora + kv_lora + ROPE_DIM
    w_c = wl[:, off_c:]
    kpe = wl[:, off_b + q_lora + kv_lora:off_c]
    w_b = jnp.concatenate([wl[:, off_b:off_b + q_lora + kv_lora], _pad_lanes(kpe),
                           _pad_lanes(kpe[:, _partner()])], axis=1)
    wq = w_uq[l].reshape(q_lora, heads, QK_DIM)
    wq_n = wq[:, :, :NOPE_DIM].reshape(q_lora, heads * LANES)
    wq_r = _pad_lanes(wq[:, :, NOPE_DIM:]).reshape(q_lora, heads * LANES)
    wq_s = _pad_lanes(wq[:, :, NOPE_DIM:][:, :, _partner()]).reshape(q_lora, heads * LANES)
    wq_all = jnp.concatenate([wq_n, wq_r, wq_s], axis=1).astype(BF16)
    wkv = w_ukv[l].reshape(kv_lora, heads, NOPE_DIM + V_DIM)
    wkv_all = jnp.concatenate([wkv[:, :, :NOPE_DIM].reshape(kv_lora, heads * NOPE_DIM),
                               wkv[:, :, NOPE_DIM:].reshape(kv_lora, heads * V_DIM)], axis=1).astype(BF16)
    return off_b, w_c, w_b, wq_all, wkv_all


def _router_layout(w_router, router_bias):
    d, n_experts = w_router.shape
    per_group = n_experts // N_GROUPS
    wr = w_router.reshape(d, N_GROUPS, per_group).transpose(0, 2, 1)
    wr_pad = _pad_lanes(wr).reshape(d, per_group * LANES).astype(F32)
    bias = _pad_lanes(router_bias.reshape(N_GROUPS, per_group).T).astype(F32)
    bias = jnp.pad(bias, ((0, 8 - per_group), (0, 0)))
    return wr_pad, bias, per_group


def kernel(x, c, ctx, c_ctx, w_mod, b_mod, g_norm1, g_norm2, w_in, conv_a, ln_a_g, ln_a_b, w_a_out, g_cq, g_ckv, w_uq, w_ukv, g_q, g_k, w_b_out, conv_c, w_c_out, w_o, w_router, router_bias, w_e_in, w_e_out):
    batch, seq, d = x.shape
    n_ctx = ctx.shape[1]
    depth = w_mod.shape[0]
    geom = (batch, seq, n_ctx)
    t_x = batch * seq
    t_all = t_x + batch * n_ctx
    q_lora, kv_lora = g_cq.shape[1], g_ckv.shape[1]
    ch_a, ch_c = conv_a.shape[2], conv_c.shape[2]
    n_experts = w_router.shape[1]
    assert batch + 1 <= MOD_ROWS and t_x % n_ctx == 0 and ch_a == ch_c and (3 * d) % ch_a == 0

    stream = (x.reshape(t_x, d), ctx.reshape(batch * n_ctx, d))
    cvec = jnp.concatenate([c, c_ctx[None, :], jnp.zeros((MOD_ROWS - batch - 1, d), F32)], axis=0)
    mod3 = modulation(cvec, w_mod, b_mod).reshape(depth * MOD_ROWS, 1, N_MOD * d)

    cos_tab, sin_tab = _rope_tables(seq, min(TILES["mla"], seq, n_ctx))
    wr_pad, bias_pad, per_group = _router_layout(w_router, router_bias)
    col0 = 3 * d // ch_a

    for l in range(depth):
        last = l == depth - 1
        rows_main = t_x if last else t_all
        off_b, w_c, w_b, wq_all, wkv_all = _layer_weights(l, w_in, w_uq, w_ukv, q_lora, kv_lora)

        h = norm_modulate(stream, g_norm1[l], mod3, layer=l, shift_idx=0, n_rows=t_all, geom=geom)
        u = project(h, w_in[l], n_cols=off_b, n_rows=rows_main)
        uc = project(h, w_c, n_cols=w_c.shape[1], n_rows=rows_main)
        ub = project(h, w_b, n_cols=w_b.shape[1], n_rows=t_all)
        qt, k, vt = mla_project(ub, g_cq[l], g_ckv[l], wq_all, wkv_all, _norm_gain_rows(g_q[l]),
                                _norm_gain_rows(g_k[l]), cos_tab, sin_tab, geom=geom)
        ob_x = attention_latent(qt, k, vt, geom=geom)
        ob_c = ob_x if last else attention_context(qt, k, vt, geom=geom)
        ha = depthwise_branch(u, conv_a[l], col_a=col0, col_b=col0 + 1, col_gate=None,
                              ln=(ln_a_g[l], ln_a_b[l]), n_rows=rows_main, geom=geom)
        hc = depthwise_branch(uc, conv_c[l], col_a=1, col_b=2, col_gate=0,
                              ln=None, n_rows=rows_main, geom=geom)
        mrg = branch_merge(ha, ob_x, ob_c, hc, w_a_out, w_b_out, w_c_out, u, layer=l,
                           n_rows=rows_main, geom=geom)
        x1 = out_proj_residual(mrg, w_o, stream, mod3, layer=l, gate_idx=2, n_rows=rows_main, geom=geom)

        h2, ids, wts, counts = router(x1, g_norm2[l], mod3, wr_pad, bias_pad, layer=l, shift_idx=3,
                                      n_rows=rows_main, geom=geom, per_group=per_group)
        row_tok, dest, block_e = moe_dispatch(ids, counts, n_experts)
        ys = expert_ffn(h2.at[row_tok].get(mode="promise_in_bounds"), block_e, w_e_in, w_e_out, layer=l)
        y_pair = ys.at[dest.reshape(-1)].get(mode="promise_in_bounds").reshape(rows_main, TOP_K * d)
        stream = moe_combine(x1, y_pair, wts, mod3, layer=l, gate_idx=5, n_rows=rows_main, geom=geom)

    return stream.reshape(batch, seq, d)
```

```python
import functools
import math

import jax
import jax.numpy as jnp
from jax import lax
from jax.experimental import pallas as pl
from jax.experimental.pallas import tpu as pltpu

F32 = jnp.float32
BF16 = jnp.bfloat16

GRID_W = 64
MLA_HEADS = 8
NOPE_DIM = 128
ROPE_DIM = 64
V_DIM = 128
QK_DIM = NOPE_DIM + ROPE_DIM
AXIS_DIM = ROPE_DIM // 2
ROPE_THETA = 10000.0
N_GROUPS = 8
TOP_K = 2
N_MOD = 6
EPS = 1e-6

LANES = 128
HEAD_SLOT = 2 * LANES
HALO_ROWS = 16
MOD_ROWS = 8

TILES = dict(
    row=512,
    proj_row=1024,
    col=1024,
    mod_col=1024,
    mla=256,
    tq=512,
    tq_sub=512,
    attn_keys=512,
    conv=256,
    conv_rows=64,
    router=512,
    moe_block=256,
    vmem=56 * 1024 * 1024,
    vmem_expert=60 * 1024 * 1024,
)


def _cparams(sem):
    return pltpu.CompilerParams(dimension_semantics=sem, vmem_limit_bytes=TILES["vmem"])


def _sigmoid(x):
    return 1.0 / (1.0 + jnp.exp(-x))


def _mod_kernel(c_ref, w_ref, b_ref, o_ref):
    c = c_ref[...]
    s = (c * _sigmoid(c)).astype(BF16)
    o_ref[...] = jnp.dot(s, w_ref[...].astype(BF16), preferred_element_type=F32) + b_ref[...]


def modulation(cvec, w_mod, b_mod):
    n_layers, d, n = w_mod.shape
    tn = min(TILES["mod_col"], n)
    return pl.pallas_call(
        _mod_kernel,
        out_shape=jax.ShapeDtypeStruct((n_layers, MOD_ROWS, n), F32),
        grid=(n_layers, n // tn),
        in_specs=[
            pl.BlockSpec((MOD_ROWS, d), lambda l, j: (0, 0)),
            pl.BlockSpec((None, d, tn), lambda l, j: (l, 0, j)),
            pl.BlockSpec((None, 1, tn), lambda l, j: (l, 0, j)),
        ],
        out_specs=pl.BlockSpec((None, MOD_ROWS, tn), lambda l, j: (l, 0, j)),
        compiler_params=_cparams(("parallel", "parallel")),
        name="modulation",
    )(cvec, w_mod, b_mod.reshape(n_layers, 1, n))


def _norm_mod(x, g, shift, scale):
    ms = jnp.mean(x * x, axis=-1, keepdims=True)
    return x * lax.rsqrt(ms + EPS) * g * (1.0 + scale) + shift


def _stream_specs(stream, tm, n_x_tiles, tn=None, grid_rank=1):
    arrays = stream if isinstance(stream, tuple) else (stream,)
    width = arrays[0].shape[1] if tn is None else tn

    def spec(row_fn):
        if grid_rank == 1:
            return pl.BlockSpec((tm, width), lambda m: (row_fn(m), 0))
        return pl.BlockSpec((tm, width), lambda j, m: (row_fn(m), j))

    if len(arrays) == 1:
        return [spec(lambda m: m)], list(arrays)
    return [spec(lambda m: jnp.minimum(m, n_x_tiles - 1)),
            spec(lambda m: jnp.maximum(m - n_x_tiles, 0))], list(arrays)


def _stream_tile(refs, m, n_x_tiles):
    if len(refs) == 1:
        return refs[0][...]
    return jnp.where(m < n_x_tiles, refs[0][...], refs[1][...])


def _norm_kernel(*refs, n_x_tiles):
    x_refs, (g_ref, shift_ref, scale_ref, h_ref) = refs[:-4], refs[-4:]
    x = _stream_tile(x_refs, pl.program_id(0), n_x_tiles)
    h_ref[...] = _norm_mod(x, g_ref[...], shift_ref[...], scale_ref[...]).astype(h_ref.dtype)


def _mod_row_fn(n_x_tiles, tiles_per_seq, batch, layer):
    def f(m):
        return layer * MOD_ROWS + jnp.where(m < n_x_tiles, m // tiles_per_seq, batch)
    return f


def norm_modulate(stream, g, mod3, *, layer, shift_idx, n_rows, geom):
    batch, seq, _ = geom
    d = g.shape[0]
    tm = min(TILES["row"], seq)
    n_x = batch * seq // tm
    mrow = _mod_row_fn(n_x, seq // tm, batch, layer)
    x_specs, x_args = _stream_specs(stream, tm, n_x)
    return pl.pallas_call(
        functools.partial(_norm_kernel, n_x_tiles=n_x),
        out_shape=jax.ShapeDtypeStruct((n_rows, d), BF16),
        grid=(n_rows // tm,),
        in_specs=x_specs + [
            pl.BlockSpec((1, d), lambda m: (0, 0)),
            pl.BlockSpec((None, 1, d), lambda m: (mrow(m), 0, shift_idx)),
            pl.BlockSpec((None, 1, d), lambda m: (mrow(m), 0, shift_idx + 1)),
        ],
        out_specs=pl.BlockSpec((tm, d), lambda m: (m, 0)),
        compiler_params=_cparams(("parallel",)),
        name="norm_modulate",
    )(*x_args, g.reshape(1, d), mod3, mod3)


def _proj_kernel(a_ref, w_ref, o_ref, w_bf):
    @pl.when(pl.program_id(1) == 0)
    def _():
        w_bf[...] = w_ref[...].astype(BF16)

    o_ref[...] = jnp.dot(a_ref[...], w_bf[...], preferred_element_type=F32).astype(o_ref.dtype)


def _w_spec(w, layer, tn, col0=0):
    if w.ndim == 3:
        return pl.BlockSpec((None, w.shape[1], tn), lambda j, m: (layer, 0, col0 + j))
    return pl.BlockSpec((w.shape[0], tn), lambda j, m: (0, col0 + j))


def project(a, w, *, n_cols, n_rows, layer=0):
    k = a.shape[1]
    tm = min(TILES["proj_row"], n_rows)
    tn = min(TILES["col"], n_cols)
    if n_cols % tn:
        tn = n_cols
    return pl.pallas_call(
        _proj_kernel,
        out_shape=jax.ShapeDtypeStruct((n_rows, n_cols), BF16),
        grid=(n_cols // tn, n_rows // tm),
        in_specs=[pl.BlockSpec((tm, k), lambda j, m: (m, 0)), _w_spec(w, layer, tn)],
        out_specs=pl.BlockSpec((tm, tn), lambda j, m: (m, j)),
        scratch_shapes=[pltpu.VMEM((k, tn), BF16)],
        compiler_params=_cparams(("parallel", "arbitrary")),
        name="project",
    )(a, w)


def _mla_kernel(ub_ref, gcq_ref, gckv_ref, wq_ref, wkv_ref, gq_ref, gk_ref, cos_ref, sin_ref,
                qt_ref, k_ref, vt_ref, *, q_lora, kv_lora, heads):
    hn = heads * LANES

    def rms(xv, g):
        return (xv * lax.rsqrt(jnp.mean(xv * xv, axis=-1, keepdims=True) + EPS) * g).astype(BF16)

    cqn = rms(ub_ref[:, 0:q_lora].astype(F32), gcq_ref[...])
    ckvn = rms(ub_ref[:, q_lora:q_lora + kv_lora].astype(F32), gckv_ref[...])
    off = q_lora + kv_lora
    kpe = ub_ref[:, off:off + LANES].astype(F32)
    kpes = ub_ref[:, off + LANES:off + 2 * LANES].astype(F32)

    qall = jnp.dot(cqn, wq_ref[...], preferred_element_type=F32)
    kvall = jnp.dot(ckvn, wkv_ref[...], preferred_element_type=F32)

    cos = cos_ref[...]
    sin = sin_ref[...]
    gq_n, gq_r, gq_s = gq_ref[0:1, :], gq_ref[1:2, :], gq_ref[2:3, :]
    gk_n, gk_r, gk_s = gk_ref[0:1, :], gk_ref[1:2, :], gk_ref[2:3, :]
    q_cos, q_sin = gq_r * cos, gq_s * sin
    k_rope = kpe * (gk_r * cos) + kpes * (gk_s * sin)
    ssq_pe = jnp.sum(kpe * kpe, axis=-1, keepdims=True)
    q_scale = QK_DIM ** -0.5 * math.log2(math.e)

    for h in range(heads):
        lo = h * LANES
        qn = qall[:, lo:lo + LANES]
        qr = qall[:, hn + lo:hn + lo + LANES]
        qs = qall[:, 2 * hn + lo:2 * hn + lo + LANES]
        ssq = jnp.sum(qn * qn, axis=-1, keepdims=True) + jnp.sum(qr * qr, axis=-1, keepdims=True)
        inv = lax.rsqrt(ssq * (1.0 / QK_DIM) + EPS) * q_scale
        qt_ref[h * HEAD_SLOT:h * HEAD_SLOT + LANES, :] = (qn * gq_n * inv).T.astype(BF16)
        qt_ref[h * HEAD_SLOT + LANES:(h + 1) * HEAD_SLOT, :] = ((qr * q_cos + qs * q_sin) * inv).T.astype(BF16)

        kn = kvall[:, lo:lo + LANES]
        invk = lax.rsqrt((jnp.sum(kn * kn, axis=-1, keepdims=True) + ssq_pe) * (1.0 / QK_DIM) + EPS)
        k_ref[:, h * HEAD_SLOT:h * HEAD_SLOT + LANES] = (kn * gk_n * invk).astype(BF16)
        k_ref[:, h * HEAD_SLOT + LANES:(h + 1) * HEAD_SLOT] = (k_rope * invk).astype(BF16)

        vt_ref[h * V_DIM:(h + 1) * V_DIM, :] = kvall[:, hn + h * V_DIM:hn + (h + 1) * V_DIM].T.astype(BF16)


def mla_project(ub, g_cq, g_ckv, wq, wkv, gq3, gk3, cos_tab, sin_tab, *, geom):
    batch, seq, ctx = geom
    t_all = ub.shape[0]
    q_lora, kv_lora = g_cq.shape[0], g_ckv.shape[0]
    heads = MLA_HEADS
    tm = min(TILES["mla"], seq, ctx)
    n_x = batch * seq // tm
    per_seq = seq // tm

    def pos(m):
        return jnp.where(m < n_x, 1 + m % per_seq, 0)

    kern = functools.partial(_mla_kernel, q_lora=q_lora, kv_lora=kv_lora, heads=heads)
    return pl.pallas_call(
        kern,
        out_shape=(
            jax.ShapeDtypeStruct((heads * HEAD_SLOT, t_all), BF16),
            jax.ShapeDtypeStruct((t_all, heads * HEAD_SLOT), BF16),
            jax.ShapeDtypeStruct((heads * V_DIM, t_all), BF16),
        ),
        grid=(t_all // tm,),
        in_specs=[
            pl.BlockSpec((tm, ub.shape[1]), lambda m: (m, 0)),
            pl.BlockSpec((1, q_lora), lambda m: (0, 0)),
            pl.BlockSpec((1, kv_lora), lambda m: (0, 0)),
            pl.BlockSpec(wq.shape, lambda m: (0, 0)),
            pl.BlockSpec(wkv.shape, lambda m: (0, 0)),
            pl.BlockSpec((8, LANES), lambda m: (0, 0)),
            pl.BlockSpec((8, LANES), lambda m: (0, 0)),
            pl.BlockSpec((tm, LANES), lambda m: (pos(m), 0)),
            pl.BlockSpec((tm, LANES), lambda m: (pos(m), 0)),
        ],
        out_specs=(
            pl.BlockSpec((heads * HEAD_SLOT, tm), lambda m: (0, m)),
            pl.BlockSpec((tm, heads * HEAD_SLOT), lambda m: (m, 0)),
            pl.BlockSpec((heads * V_DIM, tm), lambda m: (0, m)),
        ),
        compiler_params=_cparams(("parallel",)),
        name="mla_project",
    )(ub, g_cq.reshape(1, -1), g_ckv.reshape(1, -1), wq, wkv, gq3, gk3, cos_tab, sin_tab)


def _attn_kernel(*refs, sub, latent_keys):
    if latent_keys:
        qt_ref, kx_ref, kc_ref, vxt_ref, vct_ref, o_ref = refs
    else:
        qt_ref, kc_ref, vct_ref, o_ref = refs
    for j in range(qt_ref.shape[1] // sub):
        qt = qt_ref[:, j * sub:(j + 1) * sub]
        sc = jnp.dot(kc_ref[...], qt, preferred_element_type=F32)
        m = jnp.max(sc, axis=0, keepdims=True)
        if latent_keys:
            sx = jnp.dot(kx_ref[...], qt, preferred_element_type=F32)
            m = jnp.maximum(m, jnp.max(sx, axis=0, keepdims=True))
        pc = jnp.exp2(sc - m)
        denom = jnp.sum(pc, axis=0, keepdims=True)
        ot = jnp.dot(vct_ref[...], pc.astype(BF16), preferred_element_type=F32)
        if latent_keys:
            px = jnp.exp2(sx - m)
            denom = denom + jnp.sum(px, axis=0, keepdims=True)
            ot = ot + jnp.dot(vxt_ref[...], px.astype(BF16), preferred_element_type=F32)
        o_ref[j * sub:(j + 1) * sub, :] = (ot * (1.0 / denom)).T.astype(o_ref.dtype)


def _attn_pipelined_kernel(qt_ref, kx_ref, kc_ref, vxt_ref, vct_ref, o_ref, s0_ref, s1_ref, m0_ref, m1_ref,
                           *, n_q, chunk):
    i = pl.program_id(2)
    seq, ctx = kx_ref.shape[0], kc_ref.shape[0]
    tq = qt_ref.shape[1]
    chunks = [(kc_ref, vct_ref, 0, ctx, 0)]
    chunks += [(kx_ref, vxt_ref, c0, chunk, ctx + c0) for c0 in range(0, seq, chunk)]
    slots = ((s0_ref, m0_ref), (s1_ref, m1_ref))

    def run(new, old):
        if new is not None:
            s_new, m_new = slots[new]
            qt = qt_ref[...]
            m = jnp.full((1, tq), -jnp.inf, F32)
        if old is not None:
            s_old, m_old_ref = slots[old]
            m_old = m_old_ref[...]
            acc = jnp.zeros((V_DIM, tq), F32)
            den = jnp.zeros((1, tq), F32)
        for kref, vtref, c0, size, row0 in chunks:
            if new is not None:
                s = jnp.dot(kref[c0:c0 + size, :], qt, preferred_element_type=F32)
                s_new[row0:row0 + size, :] = s
                m = jnp.maximum(m, jnp.max(s, axis=0, keepdims=True))
            if old is not None:
                p = jnp.exp2(s_old[row0:row0 + size, :] - m_old)
                den = den + jnp.sum(p, axis=0, keepdims=True)
                acc = acc + jnp.dot(vtref[:, c0:c0 + size], p.astype(BF16), preferred_element_type=F32)
        if new is not None:
            m_new[...] = m
        if old is not None:
            o_ref[...] = (acc * (1.0 / den)).T.astype(o_ref.dtype)

    pl.when(i == 0)(lambda: run(0, None))
    pl.when((i > 0) & (i < n_q) & (i % 2 == 1))(lambda: run(1, 0))
    pl.when((i > 0) & (i < n_q) & (i % 2 == 0))(lambda: run(0, 1))
    pl.when(i == n_q)(lambda: run(None, (n_q - 1) % 2))


def attention_latent(qt, k, vt, *, geom):
    batch, seq, ctx = geom
    heads = MLA_HEADS
    tq = min(TILES["tq"], seq)
    n_q = seq // tq
    ctx_blk0 = batch * seq // ctx
    kern = functools.partial(_attn_pipelined_kernel, n_q=n_q, chunk=min(TILES["attn_keys"], seq))
    return pl.pallas_call(
        kern,
        out_shape=jax.ShapeDtypeStruct((batch * seq, heads * V_DIM), BF16),
        grid=(batch, heads, n_q + 1),
        in_specs=[
            pl.BlockSpec((HEAD_SLOT, tq), lambda b, h, i: (h, b * n_q + jnp.minimum(i, n_q - 1))),
            pl.BlockSpec((seq, HEAD_SLOT), lambda b, h, i: (b, h)),
            pl.BlockSpec((ctx, HEAD_SLOT), lambda b, h, i: (ctx_blk0 + b, h)),
            pl.BlockSpec((V_DIM, seq), lambda b, h, i: (h, b)),
            pl.BlockSpec((V_DIM, ctx), lambda b, h, i: (h, ctx_blk0 + b)),
        ],
        out_specs=pl.BlockSpec((tq, V_DIM), lambda b, h, i: (b * n_q + jnp.maximum(i - 1, 0), h)),
        scratch_shapes=[pltpu.VMEM((ctx + seq, tq), F32), pltpu.VMEM((ctx + seq, tq), F32),
                        pltpu.VMEM((1, tq), F32), pltpu.VMEM((1, tq), F32)],
        compiler_params=_cparams(("parallel", "parallel", "arbitrary")),
        name="attention_latent",
    )(qt, k, k, vt, vt)


def attention_context(qt, k, vt, *, geom):
    batch, seq, ctx = geom
    heads = MLA_HEADS
    ctx_blk0 = batch * seq // ctx
    kern = functools.partial(_attn_kernel, sub=min(TILES["tq_sub"], ctx), latent_keys=False)
    return pl.pallas_call(
        kern,
        out_shape=jax.ShapeDtypeStruct((batch * ctx, heads * V_DIM), BF16),
        grid=(batch, heads),
        in_specs=[
            pl.BlockSpec((HEAD_SLOT, ctx), lambda b, h: (h, ctx_blk0 + b)),
            pl.BlockSpec((ctx, HEAD_SLOT), lambda b, h: (ctx_blk0 + b, h)),
            pl.BlockSpec((V_DIM, ctx), lambda b, h: (h, ctx_blk0 + b)),
        ],
        out_specs=pl.BlockSpec((ctx, V_DIM), lambda b, h: (b, h)),
        compiler_params=_cparams(("parallel", "parallel")),
        name="attention_context",
    )(qt, k, vt)


def _conv_kernel(*refs, width, conformer, tp, n_x_tiles, x_per_seq, c_per_seq):
    if conformer:
        (ap, bp, am, bm, an, bn, w_ref, lng_ref, lnb_ref, o_ref, buf, ybuf) = refs
        gate_ref = None
    else:
        (gate_ref, ap, bp, am, bm, an, bn, w_ref, o_ref, buf, ybuf) = refs
    nchunk = buf.shape[0]
    i = pl.program_id(0)
    in_x = i < n_x_tiles
    pos = jnp.where(in_x, i % x_per_seq, (i - n_x_tiles) % c_per_seq)
    per_seq = jnp.where(in_x, x_per_seq, c_per_seq)
    keep_prev = (pos > 0).astype(F32)
    keep_next = (pos < per_seq - 1).astype(F32)

    def pre(a, b):
        a = a.astype(F32)
        b = b.astype(F32)
        return a * _sigmoid(b) if conformer else a * b

    for c in range(nchunk):
        sl = slice(c * LANES, (c + 1) * LANES)
        buf[c, 0:HALO_ROWS, :] = pre(ap[:, sl], bp[:, sl]) * keep_prev
        buf[c, HALO_ROWS:HALO_ROWS + tp, :] = pre(am[:, sl], bm[:, sl])
        buf[c, HALO_ROWS + tp:2 * HALO_ROWS + tp, :] = pre(an[:, sl], bn[:, sl]) * keep_next

    rc = min(TILES["conv_rows"], tp)
    base = HALO_ROWS - width // 2

    def chunk(c, carry):
        wts = w_ref[c]
        for r0 in range(0, tp, rc):
            acc = jnp.zeros((rc, LANES), F32)
            for k in range(width):
                acc = acc + wts[k:k + 1, :] * buf[c, pl.ds(base + r0 + k, rc), :]
            ybuf[c, pl.ds(r0, rc), :] = acc
        return carry

    lax.fori_loop(0, nchunk, chunk, 0)

    if conformer:
        n_ch = nchunk * LANES
        tot = jnp.zeros((tp, 1), F32)
        for c in range(nchunk):
            tot = tot + jnp.sum(ybuf[c], axis=-1, keepdims=True)
        mu = tot * (1.0 / n_ch)
        tot2 = jnp.zeros((tp, 1), F32)
        for c in range(nchunk):
            dlt = ybuf[c] - mu
            tot2 = tot2 + jnp.sum(dlt * dlt, axis=-1, keepdims=True)
        inv = lax.rsqrt(tot2 * (1.0 / n_ch) + EPS)
        for c in range(nchunk):
            sl = slice(c * LANES, (c + 1) * LANES)
            y = (ybuf[c] - mu) * inv * lng_ref[:, sl] + lnb_ref[:, sl]
            o_ref[:, sl] = (y * _sigmoid(y)).astype(o_ref.dtype)
    else:
        for c in range(nchunk):
            sl = slice(c * LANES, (c + 1) * LANES)
            o_ref[:, sl] = (gate_ref[:, sl].astype(F32) * ybuf[c]).astype(o_ref.dtype)


def depthwise_branch(u, conv_w, *, col_a, col_b, col_gate, ln, n_rows, geom):
    batch, seq, ctx = geom
    width, ch = conv_w.shape
    conformer = ln is not None
    tp = min(TILES["conv"], seq, ctx)
    nchunk = ch // LANES
    hb = tp // HALO_ROWS
    last_halo = u.shape[0] // HALO_ROWS - 1
    wpad = -(-width // 8) * 8
    w3 = jnp.pad(conv_w.astype(F32), ((0, wpad - width), (0, 0))).reshape(wpad, nchunk, LANES).transpose(1, 0, 2)

    def prev_map(col):
        return lambda i: (jnp.maximum(i * hb - 1, 0), col)

    def main_map(col):
        return lambda i: (i, col)

    def next_map(col):
        return lambda i: (jnp.minimum((i + 1) * hb, last_halo), col)

    in_specs, args = [], []
    if not conformer:
        in_specs.append(pl.BlockSpec((tp, ch), main_map(col_gate)))
        args.append(u)
    for rows, mk in ((HALO_ROWS, prev_map), (tp, main_map), (HALO_ROWS, next_map)):
        for col in (col_a, col_b):
            in_specs.append(pl.BlockSpec((rows, ch), mk(col)))
            args.append(u)
    in_specs.append(pl.BlockSpec((nchunk, wpad, LANES), lambda i: (0, 0, 0)))
    args.append(w3)
    if conformer:
        for p in ln:
            in_specs.append(pl.BlockSpec((1, ch), lambda i: (0, 0)))
            args.append(p.reshape(1, ch).astype(F32))

    kern = functools.partial(
        _conv_kernel, width=width, conformer=conformer, tp=tp, n_x_tiles=batch * seq // tp,
        x_per_seq=seq // tp, c_per_seq=ctx // tp)
    return pl.pallas_call(
        kern,
        out_shape=jax.ShapeDtypeStruct((n_rows, ch), BF16),
        grid=(n_rows // tp,),
        in_specs=in_specs,
        out_specs=pl.BlockSpec((tp, ch), lambda i: (i, 0)),
        scratch_shapes=[pltpu.VMEM((nchunk, tp + 2 * HALO_ROWS, LANES), F32),
                        pltpu.VMEM((nchunk, tp, LANES), F32)],
        compiler_params=_cparams(("parallel",)),
        name="conformer_conv" if conformer else "short_conv",
    )(*args)


def _merge_kernel(ha_ref, obx_ref, obc_ref, hc_ref, wa_ref, wb_ref, wc_ref, ga_ref, gb_ref, gc_ref, o_ref,
                  wa_bf, wb_bf, wc_bf, *, n_x_tiles):
    m_idx = pl.program_id(1)

    @pl.when(m_idx == 0)
    def _():
        wa_bf[...] = wa_ref[...].astype(BF16)
        wb_bf[...] = wb_ref[...].astype(BF16)
        wc_bf[...] = wc_ref[...].astype(BF16)

    ob = jnp.where(m_idx < n_x_tiles, obx_ref[...], obc_ref[...])
    ya = jnp.dot(ha_ref[...], wa_bf[...], preferred_element_type=F32)
    yb = jnp.dot(ob, wb_bf[...], preferred_element_type=F32)
    yc = jnp.dot(hc_ref[...], wc_bf[...], preferred_element_type=F32)
    m = (_sigmoid(ga_ref[...].astype(F32)) * ya + _sigmoid(gb_ref[...].astype(F32)) * yb
         + _sigmoid(gc_ref[...].astype(F32)) * yc)
    o_ref[...] = m.astype(o_ref.dtype)


def branch_merge(ha, ob_x, ob_c, hc, wa, wb, wc, u, *, layer, n_rows, geom):
    batch, seq, _ = geom
    d = wa.shape[2]
    tm = min(TILES["row"], seq)
    tn = min(TILES["col"], d)
    ncol = d // tn
    n_x = batch * seq // tm
    row_spec = lambda a: pl.BlockSpec((tm, a.shape[1]), lambda j, m: (m, 0))
    gate_spec = lambda k: pl.BlockSpec((tm, tn), lambda j, m: (m, k * ncol + j))
    kern = functools.partial(_merge_kernel, n_x_tiles=n_x)
    return pl.pallas_call(
        kern,
        out_shape=jax.ShapeDtypeStruct((n_rows, d), BF16),
        grid=(ncol, n_rows // tm),
        in_specs=[row_spec(ha),
                  pl.BlockSpec((tm, ob_x.shape[1]), lambda j, m: (jnp.minimum(m, n_x - 1), 0)),
                  pl.BlockSpec((tm, ob_c.shape[1]), lambda j, m: (jnp.maximum(m - n_x, 0), 0)),
                  row_spec(hc), _w_spec(wa, layer, tn), _w_spec(wb, layer, tn), _w_spec(wc, layer, tn),
                  gate_spec(0), gate_spec(1), gate_spec(2)],
        out_specs=pl.BlockSpec((tm, tn), lambda j, m: (m, j)),
        scratch_shapes=[pltpu.VMEM((w.shape[1], tn), BF16) for w in (wa, wb, wc)],
        compiler_params=_cparams(("parallel", "arbitrary")),
        name="branch_merge",
    )(ha, ob_x, ob_c, hc, wa, wb, wc, u, u, u)


def _outproj_kernel(*refs, n_x_tiles):
    (m_ref, w_ref, gate_ref), x_refs, (o_ref, w_bf) = refs[:3], refs[3:-2], refs[-2:]
    m_idx = pl.program_id(1)

    @pl.when(m_idx == 0)
    def _():
        w_bf[...] = w_ref[...].astype(BF16)

    y = jnp.dot(m_ref[...], w_bf[...], preferred_element_type=F32)
    o_ref[...] = _stream_tile(x_refs, m_idx, n_x_tiles) + gate_ref[...] * y


def out_proj_residual(mrg, w_o, stream, mod3, *, layer, gate_idx, n_rows, geom):
    batch, seq, _ = geom
    d = w_o.shape[2]
    tm = min(TILES["row"], seq)
    tn = min(TILES["col"], d)
    ncol = d // tn
    n_x = batch * seq // tm
    mrow = _mod_row_fn(n_x, seq // tm, batch, layer)
    x_specs, x_args = _stream_specs(stream, tm, n_x, tn=tn, grid_rank=2)
    return pl.pallas_call(
        functools.partial(_outproj_kernel, n_x_tiles=n_x),
        out_shape=jax.ShapeDtypeStruct((n_rows, d), F32),
        grid=(ncol, n_rows // tm),
        in_specs=[
            pl.BlockSpec((tm, mrg.shape[1]), lambda j, m: (m, 0)),
            _w_spec(w_o, layer, tn),
            pl.BlockSpec((None, 1, tn), lambda j, m: (mrow(m), 0, gate_idx * ncol + j)),
        ] + x_specs,
        out_specs=pl.BlockSpec((tm, tn), lambda j, m: (m, j)),
        scratch_shapes=[pltpu.VMEM((w_o.shape[1], tn), BF16)],
        compiler_params=_cparams(("parallel", "arbitrary")),
        name="out_proj_residual",
    )(mrg, w_o, mod3, *x_args)


def _router_kernel(x_ref, g_ref, shift_ref, scale_ref, wr_ref, bias_ref, h_ref, ids_ref, wts_ref, cnt_ref,
                   *, per_group):
    h = _norm_mod(x_ref[...], g_ref[...], shift_ref[...], scale_ref[...])
    h_ref[...] = h.astype(h_ref.dtype)
    h_hi = h.astype(BF16)
    h_lo = (h - h_hi.astype(F32)).astype(BF16)
    w = wr_ref[...]
    w_hi = w.astype(BF16)
    w_lo = (w - w_hi.astype(F32)).astype(BF16)
    logits = (jnp.dot(h_hi, w_hi, preferred_element_type=F32)
              + jnp.dot(h_hi, w_lo, preferred_element_type=F32)
              + jnp.dot(h_lo, w_hi, preferred_element_type=F32))
    lane_i = lax.broadcasted_iota(jnp.int32, (h.shape[0], LANES), 1)
    lane = lane_i.astype(F32)
    valid = lane_i < N_GROUPS
    neg = jnp.float32(-jnp.inf)
    score, sel = [], []
    for j in range(per_group):
        s = _sigmoid(logits[:, j * LANES:(j + 1) * LANES])
        score.append(s)
        sel.append(jnp.where(valid, s + bias_ref[j:j + 1, :], neg))
    assert per_group == 4
    hi1, lo1 = jnp.maximum(sel[0], sel[1]), jnp.minimum(sel[0], sel[1])
    hi2, lo2 = jnp.maximum(sel[2], sel[3]), jnp.minimum(sel[2], sel[3])
    top1 = jnp.maximum(hi1, hi2)
    top2 = jnp.maximum(jnp.minimum(hi1, hi2), jnp.maximum(lo1, lo2))
    gscore = jnp.where(valid, top1 + top2, neg)
    gmax = jnp.max(gscore, axis=-1, keepdims=True)
    grp = jnp.min(jnp.where(gscore == gmax, lane, float(LANES)), axis=-1, keepdims=True)
    pick = lane == grp
    zero = jnp.float32(0.0)
    sel_g = [jnp.sum(jnp.where(pick, s, zero), axis=-1, keepdims=True) for s in sel]
    sc_g = [jnp.sum(jnp.where(pick, s, zero), axis=-1, keepdims=True) for s in score]
    best, best_i, best_s = sel_g[0], jnp.zeros_like(grp), sc_g[0]
    for j in range(1, per_group):
        better = sel_g[j] > best
        best = jnp.where(better, sel_g[j], best)
        best_i = jnp.where(better, float(j), best_i)
        best_s = jnp.where(better, sc_g[j], best_s)
    second, second_i, second_s = jnp.full_like(best, neg), jnp.zeros_like(grp), jnp.zeros_like(best)
    for j in range(per_group):
        better = (sel_g[j] > second) & (best_i != float(j))
        second = jnp.where(better, sel_g[j], second)
        second_i = jnp.where(better, float(j), second_i)
        second_s = jnp.where(better, sc_g[j], second_s)
    tot = best_s + second_s
    e0 = grp * per_group + best_i
    e1 = grp * per_group + second_i
    @pl.when(pl.program_id(0) == 0)
    def _():
        cnt_ref[...] = jnp.zeros_like(cnt_ref)

    oh0 = lane == e0
    oh1 = lane == e1
    both = jnp.where(oh0 | oh1, jnp.float32(1.0), zero)
    tm = h.shape[0]
    earlier = (lax.broadcasted_iota(jnp.int32, (tm, tm), 1)
               < lax.broadcasted_iota(jnp.int32, (tm, tm), 0)).astype(BF16)
    before = jnp.dot(earlier, both.astype(BF16), preferred_element_type=F32) + cnt_ref[0:1, :]
    r0 = jnp.sum(jnp.where(oh0, before, zero), axis=-1, keepdims=True)
    r1 = jnp.sum(jnp.where(oh1, before, zero), axis=-1, keepdims=True)
    cnt_ref[...] = cnt_ref[...] + jnp.sum(both, axis=0, keepdims=True)
    ids = jnp.where(lane_i == 0, e0, jnp.where(lane_i == 1, e1, jnp.where(lane_i == 2, r0,
                    jnp.where(lane_i == 3, r1, zero))))
    ids_ref[...] = ids.astype(jnp.int32)
    wts_ref[...] = jnp.where(lane_i == 0, best_s / tot, jnp.where(lane_i == 1, second_s / tot, zero))


def router(xall, g, mod3, wr_pad, bias_pad, *, layer, shift_idx, n_rows, geom, per_group):
    batch, seq, _ = geom
    d = xall.shape[1]
    tm = min(TILES["router"], seq)
    mrow = _mod_row_fn(batch * seq // tm, seq // tm, batch, layer)
    kern = functools.partial(_router_kernel, per_group=per_group)
    return pl.pallas_call(
        kern,
        out_shape=(jax.ShapeDtypeStruct((n_rows, d), BF16),
                   jax.ShapeDtypeStruct((n_rows, LANES), jnp.int32),
                   jax.ShapeDtypeStruct((n_rows, LANES), F32),
                   jax.ShapeDtypeStruct((8, LANES), F32)),
        grid=(n_rows // tm,),
        in_specs=[
            pl.BlockSpec((tm, d), lambda m: (m, 0)),
            pl.BlockSpec((1, d), lambda m: (0, 0)),
            pl.BlockSpec((None, 1, d), lambda m: (mrow(m), 0, shift_idx)),
            pl.BlockSpec((None, 1, d), lambda m: (mrow(m), 0, shift_idx + 1)),
            pl.BlockSpec(wr_pad.shape, lambda m: (0, 0)),
            pl.BlockSpec(bias_pad.shape, lambda m: (0, 0)),
        ],
        out_specs=(pl.BlockSpec((tm, d), lambda m: (m, 0)),
                   pl.BlockSpec((tm, LANES), lambda m: (m, 0)),
                   pl.BlockSpec((tm, LANES), lambda m: (m, 0)),
                   pl.BlockSpec((8, LANES), lambda m: (0, 0))),
        compiler_params=_cparams(("arbitrary",)),
        name="router",
    )(xall, g.reshape(1, d), mod3, mod3, wr_pad, bias_pad)


def _expert_kernel(be_ref, x_ref, win_ref, wout_ref, o_ref, win_bf, wout_bf, *, ff):
    i = pl.program_id(0)

    @pl.when((i == 0) | (be_ref[i] != be_ref[jnp.maximum(i - 1, 0)]))
    def _():
        win_bf[...] = win_ref[...].astype(BF16)
        wout_bf[...] = wout_ref[...].astype(BF16)

    gu = jnp.dot(x_ref[...], win_bf[...], preferred_element_type=F32)
    g = gu[:, :ff]
    act = (g * _sigmoid(g) * gu[:, ff:]).astype(BF16)
    o_ref[...] = jnp.dot(act, wout_bf[...], preferred_element_type=F32).astype(o_ref.dtype)


def expert_ffn(xs, block_e, w_in, w_out, *, layer):
    rows, d = xs.shape
    blk = TILES["moe_block"]
    ff = w_out.shape[2]
    kern = functools.partial(_expert_kernel, ff=ff)
    grid_spec = pltpu.PrefetchScalarGridSpec(
        num_scalar_prefetch=1,
        grid=(rows // blk,),
        in_specs=[
            pl.BlockSpec((blk, d), lambda i, be: (i, 0)),
            pl.BlockSpec((None, None, d, 2 * ff), lambda i, be: (layer, be[i], 0, 0)),
            pl.BlockSpec((None, None, ff, d), lambda i, be: (layer, be[i], 0, 0)),
        ],
        out_specs=pl.BlockSpec((blk, d), lambda i, be: (i, 0)),
        scratch_shapes=[pltpu.VMEM((d, 2 * ff), BF16), pltpu.VMEM((ff, d), BF16)],
    )
    return pl.pallas_call(
        kern,
        out_shape=jax.ShapeDtypeStruct((rows, d), BF16),
        grid_spec=grid_spec,
        compiler_params=pltpu.CompilerParams(dimension_semantics=("arbitrary",),
                                             vmem_limit_bytes=TILES["vmem_expert"]),
        name="expert_ffn",
    )(block_e, xs, w_in, w_out)


def _combine_kernel(x_ref, y0_ref, y1_ref, w_ref, gate_ref, o_ref):
    w = w_ref[...]
    y = w[:, 0:1] * y0_ref[...].astype(F32) + w[:, 1:2] * y1_ref[...].astype(F32)
    o_ref[...] = x_ref[...] + gate_ref[...] * y


def moe_combine(xall, y_both, wts, mod3, *, layer, gate_idx, n_rows, geom):
    batch, seq, _ = geom
    d = xall.shape[1]
    tm = min(TILES["row"], seq)
    mrow = _mod_row_fn(batch * seq // tm, seq // tm, batch, layer)
    return pl.pallas_call(
        _combine_kernel,
        out_shape=jax.ShapeDtypeStruct((n_rows, d), F32),
        grid=(n_rows // tm,),
        in_specs=[
            pl.BlockSpec((tm, d), lambda m: (m, 0)),
            pl.BlockSpec((tm, d), lambda m: (m, 0)),
            pl.BlockSpec((tm, d), lambda m: (n_rows // tm + m, 0)),
            pl.BlockSpec((tm, LANES), lambda m: (m, 0)),
            pl.BlockSpec((None, 1, d), lambda m: (mrow(m), 0, gate_idx)),
        ],
        out_specs=pl.BlockSpec((tm, d), lambda m: (m, 0)),
        compiler_params=_cparams(("parallel",)),
        name="moe_combine",
    )(xall, y_both, y_both, wts, mod3)


def moe_dispatch(ids, counts, n_experts):
    blk = TILES["moe_block"]
    n = ids.shape[0]
    experts, rank = ids[:, :TOP_K], ids[:, TOP_K:2 * TOP_K]
    counts = counts[0, :n_experts].astype(jnp.int32)
    padded = (counts + blk - 1) // blk * blk
    pend = jnp.cumsum(padded)
    pstart = pend - padded
    eye = experts[:, :, None] == jnp.arange(n_experts, dtype=jnp.int32)[None, None, :]
    dest = jnp.sum(jnp.where(eye, pstart[None, None, :], 0), axis=-1) + rank
    n_blocks = -(-n * TOP_K // blk) + n_experts
    rows = n_blocks * blk
    tok = jnp.broadcast_to(jnp.arange(n, dtype=jnp.int32)[:, None], (n, TOP_K))
    row_tok = (jnp.arange(rows, dtype=jnp.int32) % n).at[dest.reshape(-1)].set(tok.reshape(-1))
    block_start = jnp.arange(n_blocks, dtype=jnp.int32) * blk
    block_e = jnp.minimum(jnp.sum(block_start[:, None] >= pend[None, :], axis=1), n_experts - 1).astype(jnp.int32)
    return row_tok, dest, block_e


def _rope_tables(seq, tm):
    inv = ROPE_THETA ** (-jnp.arange(0, AXIS_DIM, 2, dtype=F32) / AXIS_DIM)
    t = jnp.arange(seq)
    ang_r = (t // GRID_W).astype(F32)[:, None] * inv
    ang_c = (t % GRID_W).astype(F32)[:, None] * inv
    cos = jnp.concatenate([jnp.cos(ang_r), jnp.cos(ang_r), jnp.cos(ang_c), jnp.cos(ang_c)], axis=-1)
    sin = jnp.concatenate([-jnp.sin(ang_r), jnp.sin(ang_r), -jnp.sin(ang_c), jnp.sin(ang_c)], axis=-1)
    pad = LANES - ROPE_DIM
    cos = jnp.pad(cos, ((0, 0), (0, pad)))
    sin = jnp.pad(sin, ((0, 0), (0, pad)))
    ident_c = jnp.pad(jnp.ones((tm, ROPE_DIM), F32), ((0, 0), (0, pad)))
    return (jnp.concatenate([ident_c, cos], axis=0),
            jnp.concatenate([jnp.zeros((tm, LANES), F32), sin], axis=0))


def _partner():
    j = jnp.arange(ROPE_DIM)
    half = AXIS_DIM // 2
    return jnp.where((j % AXIS_DIM) < half, j + half, j - half)


def _pad_lanes(a):
    return jnp.pad(a, [(0, 0)] * (a.ndim - 1) + [(0, LANES - a.shape[-1])])


def _norm_gain_rows(g):
    rope = g[NOPE_DIM:]
    rows = jnp.stack([g[:NOPE_DIM], _pad_lanes(rope), _pad_lanes(rope[_partner()])])
    return jnp.pad(rows, ((0, 8 - rows.shape[0]), (0, 0))).astype(F32)


def _layer_weights(l, w_in, w_uq, w_ukv, q_lora, kv_lora):
    d = w_in.shape[1]
    heads = MLA_HEADS
    wl = w_in[l]
    gate_cols = 3 * d
    conv_a2 = (wl.shape[1] - gate_cols - (q_lora + kv_lora + ROPE_DIM)) // 5 * 2
    off_b = gate_cols + conv_a2
    off_c = off_b + q_lora + kv_lora + ROPE_DIM
    w_c = wl[:, off_c:]
    kpe = wl[:, off_b + q_lora + kv_lora:off_c]
    w_b = jnp.concatenate([wl[:, off_b:off_b + q_lora + kv_lora], _pad_lanes(kpe),
                           _pad_lanes(kpe[:, _partner()])], axis=1)
    wq = w_uq[l].reshape(q_lora, heads, QK_DIM)
    wq_n = wq[:, :, :NOPE_DIM].reshape(q_lora, heads * LANES)
    wq_r = _pad_lanes(wq[:, :, NOPE_DIM:]).reshape(q_lora, heads * LANES)
    wq_s = _pad_lanes(wq[:, :, NOPE_DIM:][:, :, _partner()]).reshape(q_lora, heads * LANES)
    wq_all = jnp.concatenate([wq_n, wq_r, wq_s], axis=1).astype(BF16)
    wkv = w_ukv[l].reshape(kv_lora, heads, NOPE_DIM + V_DIM)
    wkv_all = jnp.concatenate([wkv[:, :, :NOPE_DIM].reshape(kv_lora, heads * NOPE_DIM),
                               wkv[:, :, NOPE_DIM:].reshape(kv_lora, heads * V_DIM)], axis=1).astype(BF16)
    return off_b, w_c, w_b, wq_all, wkv_all


def _router_layout(w_router, router_bias):
    d, n_experts = w_router.shape
    per_group = n_experts // N_GROUPS
    wr = w_router.reshape(d, N_GROUPS, per_group).transpose(0, 2, 1)
    wr_pad = _pad_lanes(wr).reshape(d, per_group * LANES).astype(F32)
    bias = _pad_lanes(router_bias.reshape(N_GROUPS, per_group).T).astype(F32)
    bias = jnp.pad(bias, ((0, 8 - per_group), (0, 0)))
    return wr_pad, bias, per_group


def kernel(x, c, ctx, c_ctx, w_mod, b_mod, g_norm1, g_norm2, w_in, conv_a, ln_a_g, ln_a_b, w_a_out, g_cq, g_ckv, w_uq, w_ukv, g_q, g_k, w_b_out, conv_c, w_c_out, w_o, w_router, router_bias, w_e_in, w_e_out):
    batch, seq, d = x.shape
    n_ctx = ctx.shape[1]
    depth = w_mod.shape[0]
    geom = (batch, seq, n_ctx)
    t_x = batch * seq
    t_all = t_x + batch * n_ctx
    q_lora, kv_lora = g_cq.shape[1], g_ckv.shape[1]
    ch_a, ch_c = conv_a.shape[2], conv_c.shape[2]
    n_experts = w_router.shape[1]
    assert batch + 1 <= MOD_ROWS and t_x % n_ctx == 0 and ch_a == ch_c and (3 * d) % ch_a == 0

    stream = (x.reshape(t_x, d), ctx.reshape(batch * n_ctx, d))
    cvec = jnp.concatenate([c, c_ctx[None, :], jnp.zeros((MOD_ROWS - batch - 1, d), F32)], axis=0)
    mod3 = modulation(cvec, w_mod, b_mod).reshape(depth * MOD_ROWS, 1, N_MOD * d)

    cos_tab, sin_tab = _rope_tables(seq, min(TILES["mla"], seq, n_ctx))
    wr_pad, bias_pad, per_group = _router_layout(w_router, router_bias)
    col0 = 3 * d // ch_a

    for l in range(depth):
        last = l == depth - 1
        rows_main = t_x if last else t_all
        off_b, w_c, w_b, wq_all, wkv_all = _layer_weights(l, w_in, w_uq, w_ukv, q_lora, kv_lora)

        h = norm_modulate(stream, g_norm1[l], mod3, layer=l, shift_idx=0, n_rows=t_all, geom=geom)
        u = project(h, w_in[l], n_cols=off_b, n_rows=rows_main)
        uc = project(h, w_c, n_cols=w_c.shape[1], n_rows=rows_main)
        ub = project(h, w_b, n_cols=w_b.shape[1], n_rows=t_all)
        qt, k, vt = mla_project(ub, g_cq[l], g_ckv[l], wq_all, wkv_all, _norm_gain_rows(g_q[l]),
                                _norm_gain_rows(g_k[l]), cos_tab, sin_tab, geom=geom)
        ob_x = attention_latent(qt, k, vt, geom=geom)
        ob_c = ob_x if last else attention_context(qt, k, vt, geom=geom)
        ha = depthwise_branch(u, conv_a[l], col_a=col0, col_b=col0 + 1, col_gate=None,
                              ln=(ln_a_g[l], ln_a_b[l]), n_rows=rows_main, geom=geom)
        hc = depthwise_branch(uc, conv_c[l], col_a=1, col_b=2, col_gate=0,
                              ln=None, n_rows=rows_main, geom=geom)
        mrg = branch_merge(ha, ob_x, ob_c, hc, w_a_out, w_b_out, w_c_out, u, layer=l,
                           n_rows=rows_main, geom=geom)
        x1 = out_proj_residual(mrg, w_o, stream, mod3, layer=l, gate_idx=2, n_rows=rows_main, geom=geom)

        h2, ids, wts, counts = router(x1, g_norm2[l], mod3, wr_pad, bias_pad, layer=l, shift_idx=3,
                                      n_rows=rows_main, geom=geom, per_group=per_group)
        row_tok, dest, block_e = moe_dispatch(ids, counts, n_experts)
        ys = expert_ffn(h2.at[row_tok].get(mode="promise_in_bounds"), block_e, w_e_in, w_e_out, layer=l)
        y_both = ys.at[dest.T.reshape(-1)].get(mode="promise_in_bounds")
        stream = moe_combine(x1, y_both, wts, mod3, layer=l, gate_idx=5, n_rows=rows_main, geom=geom)

    return stream.reshape(batch, seq, d)
```

```python
import functools
import math

import jax
import jax.numpy as jnp
from jax import lax
from jax.experimental import pallas as pl
from jax.experimental.pallas import tpu as pltpu

F32 = jnp.float32
BF16 = jnp.bfloat16

GRID_W = 64
MLA_HEADS = 8
NOPE_DIM = 128
ROPE_DIM = 64
V_DIM = 128
QK_DIM = NOPE_DIM + ROPE_DIM
AXIS_DIM = ROPE_DIM // 2
ROPE_THETA = 10000.0
N_GROUPS = 8
TOP_K = 2
N_MOD = 6
EPS = 1e-6

LANES = 128
HEAD_SLOT = 2 * LANES
HALO_ROWS = 16
MOD_ROWS = 8

TILES = dict(
    row=512,
    proj_row=1024,
    col=1024,
    mod_col=1024,
    mla=256,
    tq=512,
    tq_sub=512,
    attn_keys=512,
    conv=256,
    conv_rows=64,
    router=512,
    moe_block=256,
    vmem=56 * 1024 * 1024,
    vmem_expert=60 * 1024 * 1024,
)


def _cparams(sem):
    return pltpu.CompilerParams(dimension_semantics=sem, vmem_limit_bytes=TILES["vmem"])


def _sigmoid(x):
    return 1.0 / (1.0 + jnp.exp(-x))


def _mod_kernel(c_ref, w_ref, b_ref, o_ref):
    c = c_ref[...]
    s = (c * _sigmoid(c)).astype(BF16)
    o_ref[...] = jnp.dot(s, w_ref[...].astype(BF16), preferred_element_type=F32) + b_ref[...]


def modulation(cvec, w_mod, b_mod):
    n_layers, d, n = w_mod.shape
    tn = min(TILES["mod_col"], n)
    return pl.pallas_call(
        _mod_kernel,
        out_shape=jax.ShapeDtypeStruct((n_layers, MOD_ROWS, n), F32),
        grid=(n_layers, n // tn),
        in_specs=[
            pl.BlockSpec((MOD_ROWS, d), lambda l, j: (0, 0)),
            pl.BlockSpec((None, d, tn), lambda l, j: (l, 0, j)),
            pl.BlockSpec((None, 1, tn), lambda l, j: (l, 0, j)),
        ],
        out_specs=pl.BlockSpec((None, MOD_ROWS, tn), lambda l, j: (l, 0, j)),
        compiler_params=_cparams(("parallel", "parallel")),
        name="modulation",
    )(cvec, w_mod, b_mod.reshape(n_layers, 1, n))


def _norm_mod(x, g, shift, scale):
    ms = jnp.mean(x * x, axis=-1, keepdims=True)
    return x * lax.rsqrt(ms + EPS) * g * (1.0 + scale) + shift


def _stream_specs(stream, tm, n_x_tiles, tn=None, grid_rank=1):
    arrays = stream if isinstance(stream, tuple) else (stream,)
    width = arrays[0].shape[1] if tn is None else tn

    def spec(row_fn):
        if grid_rank == 1:
            return pl.BlockSpec((tm, width), lambda m: (row_fn(m), 0))
        return pl.BlockSpec((tm, width), lambda j, m: (row_fn(m), j))

    if len(arrays) == 1:
        return [spec(lambda m: m)], list(arrays)
    return [spec(lambda m: jnp.minimum(m, n_x_tiles - 1)),
            spec(lambda m: jnp.maximum(m - n_x_tiles, 0))], list(arrays)


def _stream_tile(refs, m, n_x_tiles):
    if len(refs) == 1:
        return refs[0][...]
    return jnp.where(m < n_x_tiles, refs[0][...], refs[1][...])


def _norm_kernel(*refs, n_x_tiles):
    x_refs, (g_ref, shift_ref, scale_ref, h_ref) = refs[:-4], refs[-4:]
    x = _stream_tile(x_refs, pl.program_id(0), n_x_tiles)
    h_ref[...] = _norm_mod(x, g_ref[...], shift_ref[...], scale_ref[...]).astype(h_ref.dtype)


def _mod_row_fn(n_x_tiles, tiles_per_seq, batch, layer):
    def f(m):
        return layer * MOD_ROWS + jnp.where(m < n_x_tiles, m // tiles_per_seq, batch)
    return f


def norm_modulate(stream, g, mod3, *, layer, shift_idx, n_rows, geom):
    batch, seq, _ = geom
    d = g.shape[0]
    tm = min(TILES["row"], seq)
    n_x = batch * seq // tm
    mrow = _mod_row_fn(n_x, seq // tm, batch, layer)
    x_specs, x_args = _stream_specs(stream, tm, n_x)
    return pl.pallas_call(
        functools.partial(_norm_kernel, n_x_tiles=n_x),
        out_shape=jax.ShapeDtypeStruct((n_rows, d), BF16),
        grid=(n_rows // tm,),
        in_specs=x_specs + [
            pl.BlockSpec((1, d), lambda m: (0, 0)),
            pl.BlockSpec((None, 1, d), lambda m: (mrow(m), 0, shift_idx)),
            pl.BlockSpec((None, 1, d), lambda m: (mrow(m), 0, shift_idx + 1)),
        ],
        out_specs=pl.BlockSpec((tm, d), lambda m: (m, 0)),
        compiler_params=_cparams(("parallel",)),
        name="norm_modulate",
    )(*x_args, g.reshape(1, d), mod3, mod3)


def _proj_kernel(a_ref, w_ref, o_ref, w_bf):
    @pl.when(pl.program_id(1) == 0)
    def _():
        w_bf[...] = w_ref[...].astype(BF16)

    o_ref[...] = jnp.dot(a_ref[...], w_bf[...], preferred_element_type=F32).astype(o_ref.dtype)


def _w_spec(w, layer, tn, col0=0):
    if w.ndim == 3:
        return pl.BlockSpec((None, w.shape[1], tn), lambda j, m: (layer, 0, col0 + j))
    return pl.BlockSpec((w.shape[0], tn), lambda j, m: (0, col0 + j))


def project(a, w, *, n_cols, n_rows, layer=0):
    k = a.shape[1]
    tm = min(TILES["proj_row"], n_rows)
    tn = min(TILES["col"], n_cols)
    if n_cols % tn:
        tn = n_cols
    return pl.pallas_call(
        _proj_kernel,
        out_shape=jax.ShapeDtypeStruct((n_rows, n_cols), BF16),
        grid=(n_cols // tn, n_rows // tm),
        in_specs=[pl.BlockSpec((tm, k), lambda j, m: (m, 0)), _w_spec(w, layer, tn)],
        out_specs=pl.BlockSpec((tm, tn), lambda j, m: (m, j)),
        scratch_shapes=[pltpu.VMEM((k, tn), BF16)],
        compiler_params=_cparams(("parallel", "arbitrary")),
        name="project",
    )(a, w)


def _mla_kernel(ub_ref, gcq_ref, gckv_ref, wq_ref, wkv_ref, gq_ref, gk_ref, cos_ref, sin_ref,
                qt_ref, k_ref, vt_ref, *, q_lora, kv_lora, heads):
    hn = heads * LANES

    def rms(xv, g):
        return (xv * lax.rsqrt(jnp.mean(xv * xv, axis=-1, keepdims=True) + EPS) * g).astype(BF16)

    cqn = rms(ub_ref[:, 0:q_lora].astype(F32), gcq_ref[...])
    ckvn = rms(ub_ref[:, q_lora:q_lora + kv_lora].astype(F32), gckv_ref[...])
    off = q_lora + kv_lora
    kpe = ub_ref[:, off:off + LANES].astype(F32)
    kpes = ub_ref[:, off + LANES:off + 2 * LANES].astype(F32)

    qall = jnp.dot(cqn, wq_ref[...], preferred_element_type=F32)
    kvall = jnp.dot(ckvn, wkv_ref[...], preferred_element_type=F32)

    cos = cos_ref[...]
    sin = sin_ref[...]
    gq_n, gq_r, gq_s = gq_ref[0:1, :], gq_ref[1:2, :], gq_ref[2:3, :]
    gk_n, gk_r, gk_s = gk_ref[0:1, :], gk_ref[1:2, :], gk_ref[2:3, :]
    q_cos, q_sin = gq_r * cos, gq_s * sin
    k_rope = kpe * (gk_r * cos) + kpes * (gk_s * sin)
    ssq_pe = jnp.sum(kpe * kpe, axis=-1, keepdims=True)
    q_scale = QK_DIM ** -0.5 * math.log2(math.e)

    for h in range(heads):
        lo = h * LANES
        qn = qall[:, lo:lo + LANES]
        qr = qall[:, hn + lo:hn + lo + LANES]
        qs = qall[:, 2 * hn + lo:2 * hn + lo + LANES]
        ssq = jnp.sum(qn * qn, axis=-1, keepdims=True) + jnp.sum(qr * qr, axis=-1, keepdims=True)
        inv = lax.rsqrt(ssq * (1.0 / QK_DIM) + EPS) * q_scale
        qt_ref[h * HEAD_SLOT:h * HEAD_SLOT + LANES, :] = (qn * gq_n * inv).T.astype(BF16)
        qt_ref[h * HEAD_SLOT + LANES:(h + 1) * HEAD_SLOT, :] = ((qr * q_cos + qs * q_sin) * inv).T.astype(BF16)

        kn = kvall[:, lo:lo + LANES]
        invk = lax.rsqrt((jnp.sum(kn * kn, axis=-1, keepdims=True) + ssq_pe) * (1.0 / QK_DIM) + EPS)
        k_ref[:, h * HEAD_SLOT:h * HEAD_SLOT + LANES] = (kn * gk_n * invk).astype(BF16)
        k_ref[:, h * HEAD_SLOT + LANES:(h + 1) * HEAD_SLOT] = (k_rope * invk).astype(BF16)

        vt_ref[h * V_DIM:(h + 1) * V_DIM, :] = kvall[:, hn + h * V_DIM:hn + (h + 1) * V_DIM].T.astype(BF16)


def mla_project(ub, g_cq, g_ckv, wq, wkv, gq3, gk3, cos_tab, sin_tab, *, geom):
    batch, seq, ctx = geom
    t_all = ub.shape[0]
    q_lora, kv_lora = g_cq.shape[0], g_ckv.shape[0]
    heads = MLA_HEADS
    tm = min(TILES["mla"], seq, ctx)
    n_x = batch * seq // tm
    per_seq = seq // tm

    def pos(m):
        return jnp.where(m < n_x, 1 + m % per_seq, 0)

    kern = functools.partial(_mla_kernel, q_lora=q_lora, kv_lora=kv_lora, heads=heads)
    return pl.pallas_call(
        kern,
        out_shape=(
            jax.ShapeDtypeStruct((heads * HEAD_SLOT, t_all), BF16),
            jax.ShapeDtypeStruct((t_all, heads * HEAD_SLOT), BF16),
            jax.ShapeDtypeStruct((heads * V_DIM, t_all), BF16),
        ),
        grid=(t_all // tm,),
        in_specs=[
            pl.BlockSpec((tm, ub.shape[1]), lambda m: (m, 0)),
            pl.BlockSpec((1, q_lora), lambda m: (0, 0)),
            pl.BlockSpec((1, kv_lora), lambda m: (0, 0)),
            pl.BlockSpec(wq.shape, lambda m: (0, 0)),
            pl.BlockSpec(wkv.shape, lambda m: (0, 0)),
            pl.BlockSpec((8, LANES), lambda m: (0, 0)),
            pl.BlockSpec((8, LANES), lambda m: (0, 0)),
            pl.BlockSpec((tm, LANES), lambda m: (pos(m), 0)),
            pl.BlockSpec((tm, LANES), lambda m: (pos(m), 0)),
        ],
        out_specs=(
            pl.BlockSpec((heads * HEAD_SLOT, tm), lambda m: (0, m)),
            pl.BlockSpec((tm, heads * HEAD_SLOT), lambda m: (m, 0)),
            pl.BlockSpec((heads * V_DIM, tm), lambda m: (0, m)),
        ),
        compiler_params=_cparams(("parallel",)),
        name="mla_project",
    )(ub, g_cq.reshape(1, -1), g_ckv.reshape(1, -1), wq, wkv, gq3, gk3, cos_tab, sin_tab)


def _attn_kernel(*refs, sub, latent_keys):
    if latent_keys:
        qt_ref, kx_ref, kc_ref, vxt_ref, vct_ref, o_ref = refs
    else:
        qt_ref, kc_ref, vct_ref, o_ref = refs
    for j in range(qt_ref.shape[1] // sub):
        qt = qt_ref[:, j * sub:(j + 1) * sub]
        sc = jnp.dot(kc_ref[...], qt, preferred_element_type=F32)
        m = jnp.max(sc, axis=0, keepdims=True)
        if latent_keys:
            sx = jnp.dot(kx_ref[...], qt, preferred_element_type=F32)
            m = jnp.maximum(m, jnp.max(sx, axis=0, keepdims=True))
        pc = jnp.exp2(sc - m)
        denom = jnp.sum(pc, axis=0, keepdims=True)
        ot = jnp.dot(vct_ref[...], pc.astype(BF16), preferred_element_type=F32)
        if latent_keys:
            px = jnp.exp2(sx - m)
            denom = denom + jnp.sum(px, axis=0, keepdims=True)
            ot = ot + jnp.dot(vxt_ref[...], px.astype(BF16), preferred_element_type=F32)
        o_ref[j * sub:(j + 1) * sub, :] = (ot * (1.0 / denom)).T.astype(o_ref.dtype)


def _attn_pipelined_kernel(qt_ref, kx_ref, kc_ref, vxt_ref, vct_ref, o_ref, s0_ref, s1_ref, m0_ref, m1_ref,
                           *, n_q, chunk):
    i = pl.program_id(2)
    seq, ctx = kx_ref.shape[0], kc_ref.shape[0]
    tq = qt_ref.shape[1]
    chunks = [(kc_ref, vct_ref, 0, ctx, 0)]
    chunks += [(kx_ref, vxt_ref, c0, chunk, ctx + c0) for c0 in range(0, seq, chunk)]
    slots = ((s0_ref, m0_ref), (s1_ref, m1_ref))

    def run(new, old):
        if new is not None:
            s_new, m_new = slots[new]
            qt = qt_ref[...]
            m = jnp.full((1, tq), -jnp.inf, F32)
        if old is not None:
            s_old, m_old_ref = slots[old]
            m_old = m_old_ref[...]
            acc = jnp.zeros((V_DIM, tq), F32)
            den = jnp.zeros((1, tq), F32)
        for kref, vtref, c0, size, row0 in chunks:
            if new is not None:
                s = jnp.dot(kref[c0:c0 + size, :], qt, preferred_element_type=F32)
                s_new[row0:row0 + size, :] = s
                m = jnp.maximum(m, jnp.max(s, axis=0, keepdims=True))
            if old is not None:
                p = jnp.exp2(s_old[row0:row0 + size, :] - m_old)
                den = den + jnp.sum(p, axis=0, keepdims=True)
                acc = acc + jnp.dot(vtref[:, c0:c0 + size], p.astype(BF16), preferred_element_type=F32)
        if new is not None:
            m_new[...] = m
        if old is not None:
            o_ref[...] = (acc * (1.0 / den)).T.astype(o_ref.dtype)

    pl.when(i == 0)(lambda: run(0, None))
    pl.when((i > 0) & (i < n_q) & (i % 2 == 1))(lambda: run(1, 0))
    pl.when((i > 0) & (i < n_q) & (i % 2 == 0))(lambda: run(0, 1))
    pl.when(i == n_q)(lambda: run(None, (n_q - 1) % 2))


def attention_latent(qt, k, vt, *, geom):
    batch, seq, ctx = geom
    heads = MLA_HEADS
    tq = min(TILES["tq"], seq)
    n_q = seq // tq
    ctx_blk0 = batch * seq // ctx
    kern = functools.partial(_attn_pipelined_kernel, n_q=n_q, chunk=min(TILES["attn_keys"], seq))
    return pl.pallas_call(
        kern,
        out_shape=jax.ShapeDtypeStruct((batch * seq, heads * V_DIM), BF16),
        grid=(batch, heads, n_q + 1),
        in_specs=[
            pl.BlockSpec((HEAD_SLOT, tq), lambda b, h, i: (h, b * n_q + jnp.minimum(i, n_q - 1))),
            pl.BlockSpec((seq, HEAD_SLOT), lambda b, h, i: (b, h)),
            pl.BlockSpec((ctx, HEAD_SLOT), lambda b, h, i: (ctx_blk0 + b, h)),
            pl.BlockSpec((V_DIM, seq), lambda b, h, i: (h, b)),
            pl.BlockSpec((V_DIM, ctx), lambda b, h, i: (h, ctx_blk0 + b)),
        ],
        out_specs=pl.BlockSpec((tq, V_DIM), lambda b, h, i: (b * n_q + jnp.maximum(i - 1, 0), h)),
        scratch_shapes=[pltpu.VMEM((ctx + seq, tq), F32), pltpu.VMEM((ctx + seq, tq), F32),
                        pltpu.VMEM((1, tq), F32), pltpu.VMEM((1, tq), F32)],
        compiler_params=_cparams(("parallel", "parallel", "arbitrary")),
        name="attention_latent",
    )(qt, k, k, vt, vt)


def attention_context(qt, k, vt, *, geom):
    batch, seq, ctx = geom
    heads = MLA_HEADS
    ctx_blk0 = batch * seq // ctx
    kern = functools.partial(_attn_kernel, sub=min(TILES["tq_sub"], ctx), latent_keys=False)
    return pl.pallas_call(
        kern,
        out_shape=jax.ShapeDtypeStruct((batch * ctx, heads * V_DIM), BF16),
        grid=(batch, heads),
        in_specs=[
            pl.BlockSpec((HEAD_SLOT, ctx), lambda b, h: (h, ctx_blk0 + b)),
            pl.BlockSpec((ctx, HEAD_SLOT), lambda b, h: (ctx_blk0 + b, h)),
            pl.BlockSpec((V_DIM, ctx), lambda b, h: (h, ctx_blk0 + b)),
        ],
        out_specs=pl.BlockSpec((ctx, V_DIM), lambda b, h: (b, h)),
        compiler_params=_cparams(("parallel", "parallel")),
        name="attention_context",
    )(qt, k, vt)


def _conv_kernel(*refs, width, conformer, tp, n_x_tiles, x_per_seq, c_per_seq):
    if conformer:
        (ap, bp, am, bm, an, bn, w_ref, lng_ref, lnb_ref, o_ref, buf, ybuf) = refs
        gate_ref = None
    else:
        (gate_ref, ap, bp, am, bm, an, bn, w_ref, o_ref, buf, ybuf) = refs
    nchunk = buf.shape[0]
    i = pl.program_id(0)
    in_x = i < n_x_tiles
    pos = jnp.where(in_x, i % x_per_seq, (i - n_x_tiles) % c_per_seq)
    per_seq = jnp.where(in_x, x_per_seq, c_per_seq)
    keep_prev = (pos > 0).astype(F32)
    keep_next = (pos < per_seq - 1).astype(F32)

    def pre(a, b):
        a = a.astype(F32)
        b = b.astype(F32)
        return a * _sigmoid(b) if conformer else a * b

    for c in range(nchunk):
        sl = slice(c * LANES, (c + 1) * LANES)
        buf[c, 0:HALO_ROWS, :] = pre(ap[:, sl], bp[:, sl]) * keep_prev
        buf[c, HALO_ROWS:HALO_ROWS + tp, :] = pre(am[:, sl], bm[:, sl])
        buf[c, HALO_ROWS + tp:2 * HALO_ROWS + tp, :] = pre(an[:, sl], bn[:, sl]) * keep_next

    rc = min(TILES["conv_rows"], tp)
    base = HALO_ROWS - width // 2

    def chunk(c, carry):
        wts = w_ref[c]
        for r0 in range(0, tp, rc):
            acc = jnp.zeros((rc, LANES), F32)
            for k in range(width):
                acc = acc + wts[k:k + 1, :] * buf[c, pl.ds(base + r0 + k, rc), :]
            ybuf[c, pl.ds(r0, rc), :] = acc
        return carry

    lax.fori_loop(0, nchunk, chunk, 0)

    if conformer:
        n_ch = nchunk * LANES
        tot = jnp.zeros((tp, 1), F32)
        for c in range(nchunk):
            tot = tot + jnp.sum(ybuf[c], axis=-1, keepdims=True)
        mu = tot * (1.0 / n_ch)
        tot2 = jnp.zeros((tp, 1), F32)
        for c in range(nchunk):
            dlt = ybuf[c] - mu
            tot2 = tot2 + jnp.sum(dlt * dlt, axis=-1, keepdims=True)
        inv = lax.rsqrt(tot2 * (1.0 / n_ch) + EPS)
        for c in range(nchunk):
            sl = slice(c * LANES, (c + 1) * LANES)
            y = (ybuf[c] - mu) * inv * lng_ref[:, sl] + lnb_ref[:, sl]
            o_ref[:, sl] = (y * _sigmoid(y)).astype(o_ref.dtype)
    else:
        for c in range(nchunk):
            sl = slice(c * LANES, (c + 1) * LANES)
            o_ref[:, sl] = (gate_ref[:, sl].astype(F32) * ybuf[c]).astype(o_ref.dtype)


def depthwise_branch(u, conv_w, *, col_a, col_b, col_gate, ln, n_rows, geom):
    batch, seq, ctx = geom
    width, ch = conv_w.shape
    conformer = ln is not None
    tp = min(TILES["conv"], seq, ctx)
    nchunk = ch // LANES
    hb = tp // HALO_ROWS
    last_halo = u.shape[0] // HALO_ROWS - 1
    wpad = -(-width // 8) * 8
    w3 = jnp.pad(conv_w.astype(F32), ((0, wpad - width), (0, 0))).reshape(wpad, nchunk, LANES).transpose(1, 0, 2)

    def prev_map(col):
        return lambda i: (jnp.maximum(i * hb - 1, 0), col)

    def main_map(col):
        return lambda i: (i, col)

    def next_map(col):
        return lambda i: (jnp.minimum((i + 1) * hb, last_halo), col)

    in_specs, args = [], []
    if not conformer:
        in_specs.append(pl.BlockSpec((tp, ch), main_map(col_gate)))
        args.append(u)
    for rows, mk in ((HALO_ROWS, prev_map), (tp, main_map), (HALO_ROWS, next_map)):
        for col in (col_a, col_b):
            in_specs.append(pl.BlockSpec((rows, ch), mk(col)))
            args.append(u)
    in_specs.append(pl.BlockSpec((nchunk, wpad, LANES), lambda i: (0, 0, 0)))
    args.append(w3)
    if conformer:
        for p in ln:
            in_specs.append(pl.BlockSpec((1, ch), lambda i: (0, 0)))
            args.append(p.reshape(1, ch).astype(F32))

    kern = functools.partial(
        _conv_kernel, width=width, conformer=conformer, tp=tp, n_x_tiles=batch * seq // tp,
        x_per_seq=seq // tp, c_per_seq=ctx // tp)
    return pl.pallas_call(
        kern,
        out_shape=jax.ShapeDtypeStruct((n_rows, ch), BF16),
        grid=(n_rows // tp,),
        in_specs=in_specs,
        out_specs=pl.BlockSpec((tp, ch), lambda i: (i, 0)),
        scratch_shapes=[pltpu.VMEM((nchunk, tp + 2 * HALO_ROWS, LANES), F32),
                        pltpu.VMEM((nchunk, tp, LANES), F32)],
        compiler_params=_cparams(("parallel",)),
        name="conformer_conv" if conformer else "short_conv",
    )(*args)


def _merge_kernel(ha_ref, obx_ref, obc_ref, hc_ref, wa_ref, wb_ref, wc_ref, ga_ref, gb_ref, gc_ref, o_ref,
                  wa_bf, wb_bf, wc_bf, *, n_x_tiles):
    m_idx = pl.program_id(1)

    @pl.when(m_idx == 0)
    def _():
        wa_bf[...] = wa_ref[...].astype(BF16)
        wb_bf[...] = wb_ref[...].astype(BF16)
        wc_bf[...] = wc_ref[...].astype(BF16)

    ob = jnp.where(m_idx < n_x_tiles, obx_ref[...], obc_ref[...])
    ya = jnp.dot(ha_ref[...], wa_bf[...], preferred_element_type=F32)
    yb = jnp.dot(ob, wb_bf[...], preferred_element_type=F32)
    yc = jnp.dot(hc_ref[...], wc_bf[...], preferred_element_type=F32)
    m = (_sigmoid(ga_ref[...].astype(F32)) * ya + _sigmoid(gb_ref[...].astype(F32)) * yb
         + _sigmoid(gc_ref[...].astype(F32)) * yc)
    o_ref[...] = m.astype(o_ref.dtype)


def branch_merge(ha, ob_x, ob_c, hc, wa, wb, wc, u, *, layer, n_rows, geom):
    batch, seq, _ = geom
    d = wa.shape[2]
    tm = min(TILES["row"], seq)
    tn = min(TILES["col"], d)
    ncol = d // tn
    n_x = batch * seq // tm
    row_spec = lambda a: pl.BlockSpec((tm, a.shape[1]), lambda j, m: (m, 0))
    gate_spec = lambda k: pl.BlockSpec((tm, tn), lambda j, m: (m, k * ncol + j))
    kern = functools.partial(_merge_kernel, n_x_tiles=n_x)
    return pl.pallas_call(
        kern,
        out_shape=jax.ShapeDtypeStruct((n_rows, d), BF16),
        grid=(ncol, n_rows // tm),
        in_specs=[row_spec(ha),
                  pl.BlockSpec((tm, ob_x.shape[1]), lambda j, m: (jnp.minimum(m, n_x - 1), 0)),
                  pl.BlockSpec((tm, ob_c.shape[1]), lambda j, m: (jnp.maximum(m - n_x, 0), 0)),
                  row_spec(hc), _w_spec(wa, layer, tn), _w_spec(wb, layer, tn), _w_spec(wc, layer, tn),
                  gate_spec(0), gate_spec(1), gate_spec(2)],
        out_specs=pl.BlockSpec((tm, tn), lambda j, m: (m, j)),
        scratch_shapes=[pltpu.VMEM((w.shape[1], tn), BF16) for w in (wa, wb, wc)],
        compiler_params=_cparams(("parallel", "arbitrary")),
        name="branch_merge",
    )(ha, ob_x, ob_c, hc, wa, wb, wc, u, u, u)


def _outproj_kernel(*refs, n_x_tiles):
    (m_ref, w_ref, gate_ref), x_refs, (o_ref, w_bf) = refs[:3], refs[3:-2], refs[-2:]
    m_idx = pl.program_id(1)

    @pl.when(m_idx == 0)
    def _():
        w_bf[...] = w_ref[...].astype(BF16)

    y = jnp.dot(m_ref[...], w_bf[...], preferred_element_type=F32)
    o_ref[...] = _stream_tile(x_refs, m_idx, n_x_tiles) + gate_ref[...] * y


def out_proj_residual(mrg, w_o, stream, mod3, *, layer, gate_idx, n_rows, geom):
    batch, seq, _ = geom
    d = w_o.shape[2]
    tm = min(TILES["row"], seq)
    tn = min(TILES["col"], d)
    ncol = d // tn
    n_x = batch * seq // tm
    mrow = _mod_row_fn(n_x, seq // tm, batch, layer)
    x_specs, x_args = _stream_specs(stream, tm, n_x, tn=tn, grid_rank=2)
    return pl.pallas_call(
        functools.partial(_outproj_kernel, n_x_tiles=n_x),
        out_shape=jax.ShapeDtypeStruct((n_rows, d), F32),
        grid=(ncol, n_rows // tm),
        in_specs=[
            pl.BlockSpec((tm, mrg.shape[1]), lambda j, m: (m, 0)),
            _w_spec(w_o, layer, tn),
            pl.BlockSpec((None, 1, tn), lambda j, m: (mrow(m), 0, gate_idx * ncol + j)),
        ] + x_specs,
        out_specs=pl.BlockSpec((tm, tn), lambda j, m: (m, j)),
        scratch_shapes=[pltpu.VMEM((w_o.shape[1], tn), BF16)],
        compiler_params=_cparams(("parallel", "arbitrary")),
        name="out_proj_residual",
    )(mrg, w_o, mod3, *x_args)


def _router_kernel(x_ref, g_ref, shift_ref, scale_ref, wr_ref, bias_ref, h_ref, ids_ref, wts_ref, cnt_ref,
                   *, per_group):
    h = _norm_mod(x_ref[...], g_ref[...], shift_ref[...], scale_ref[...])
    h_ref[...] = h.astype(h_ref.dtype)
    h_hi = h.astype(BF16)
    h_lo = (h - h_hi.astype(F32)).astype(BF16)
    w = wr_ref[...]
    w_hi = w.astype(BF16)
    w_lo = (w - w_hi.astype(F32)).astype(BF16)
    logits = (jnp.dot(h_hi, w_hi, preferred_element_type=F32)
              + jnp.dot(h_hi, w_lo, preferred_element_type=F32)
              + jnp.dot(h_lo, w_hi, preferred_element_type=F32))
    lane_i = lax.broadcasted_iota(jnp.int32, (h.shape[0], LANES), 1)
    lane = lane_i.astype(F32)
    valid = lane_i < N_GROUPS
    neg = jnp.float32(-jnp.inf)
    score, sel = [], []
    for j in range(per_group):
        s = _sigmoid(logits[:, j * LANES:(j + 1) * LANES])
        score.append(s)
        sel.append(jnp.where(valid, s + bias_ref[j:j + 1, :], neg))
    assert per_group == 4
    hi1, lo1 = jnp.maximum(sel[0], sel[1]), jnp.minimum(sel[0], sel[1])
    hi2, lo2 = jnp.maximum(sel[2], sel[3]), jnp.minimum(sel[2], sel[3])
    top1 = jnp.maximum(hi1, hi2)
    top2 = jnp.maximum(jnp.minimum(hi1, hi2), jnp.maximum(lo1, lo2))
    gscore = jnp.where(valid, top1 + top2, neg)
    gmax = jnp.max(gscore, axis=-1, keepdims=True)
    grp = jnp.min(jnp.where(gscore == gmax, lane, float(LANES)), axis=-1, keepdims=True)
    pick = lane == grp
    zero = jnp.float32(0.0)
    sel_g = [jnp.sum(jnp.where(pick, s, zero), axis=-1, keepdims=True) for s in sel]
    sc_g = [jnp.sum(jnp.where(pick, s, zero), axis=-1, keepdims=True) for s in score]
    best, best_i, best_s = sel_g[0], jnp.zeros_like(grp), sc_g[0]
    for j in range(1, per_group):
        better = sel_g[j] > best
        best = jnp.where(better, sel_g[j], best)
        best_i = jnp.where(better, float(j), best_i)
        best_s = jnp.where(better, sc_g[j], best_s)
    second, second_i, second_s = jnp.full_like(best, neg), jnp.zeros_like(grp), jnp.zeros_like(best)
    for j in range(per_group):
        better = (sel_g[j] > second) & (best_i != float(j))
        second = jnp.where(better, sel_g[j], second)
        second_i = jnp.where(better, float(j), second_i)
        second_s = jnp.where(better, sc_g[j], second_s)
    tot = best_s + second_s
    e0 = grp * per_group + best_i
    e1 = grp * per_group + second_i
    @pl.when(pl.program_id(0) == 0)
    def _():
        cnt_ref[...] = jnp.zeros_like(cnt_ref)

    oh0 = lane == e0
    oh1 = lane == e1
    both = jnp.where(oh0 | oh1, jnp.float32(1.0), zero)
    tm = h.shape[0]
    earlier = (lax.broadcasted_iota(jnp.int32, (tm, tm), 1)
               < lax.broadcasted_iota(jnp.int32, (tm, tm), 0)).astype(BF16)
    before = jnp.dot(earlier, both.astype(BF16), preferred_element_type=F32) + cnt_ref[0:1, :]
    r0 = jnp.sum(jnp.where(oh0, before, zero), axis=-1, keepdims=True)
    r1 = jnp.sum(jnp.where(oh1, before, zero), axis=-1, keepdims=True)
    cnt_ref[...] = cnt_ref[...] + jnp.sum(both, axis=0, keepdims=True)
    ids = jnp.where(lane_i == 0, e0, jnp.where(lane_i == 1, e1, jnp.where(lane_i == 2, r0,
                    jnp.where(lane_i == 3, r1, zero))))
    ids_ref[...] = ids.astype(jnp.int32)
    wts_ref[...] = jnp.where(lane_i == 0, best_s / tot, jnp.where(lane_i == 1, second_s / tot, zero))


def router(xall, g, mod3, wr_pad, bias_pad, *, layer, shift_idx, n_rows, geom, per_group):
    batch, seq, _ = geom
    d = xall.shape[1]
    tm = min(TILES["router"], seq)
    mrow = _mod_row_fn(batch * seq // tm, seq // tm, batch, layer)
    kern = functools.partial(_router_kernel, per_group=per_group)
    return pl.pallas_call(
        kern,
        out_shape=(jax.ShapeDtypeStruct((n_rows, d), F32),
                   jax.ShapeDtypeStruct((n_rows, LANES), jnp.int32),
                   jax.ShapeDtypeStruct((n_rows, LANES), F32),
                   jax.ShapeDtypeStruct((8, LANES), F32)),
        grid=(n_rows // tm,),
        in_specs=[
            pl.BlockSpec((tm, d), lambda m: (m, 0)),
            pl.BlockSpec((1, d), lambda m: (0, 0)),
            pl.BlockSpec((None, 1, d), lambda m: (mrow(m), 0, shift_idx)),
            pl.BlockSpec((None, 1, d), lambda m: (mrow(m), 0, shift_idx + 1)),
            pl.BlockSpec(wr_pad.shape, lambda m: (0, 0)),
            pl.BlockSpec(bias_pad.shape, lambda m: (0, 0)),
        ],
        out_specs=(pl.BlockSpec((tm, d), lambda m: (m, 0)),
                   pl.BlockSpec((tm, LANES), lambda m: (m, 0)),
                   pl.BlockSpec((tm, LANES), lambda m: (m, 0)),
                   pl.BlockSpec((8, LANES), lambda m: (0, 0))),
        compiler_params=_cparams(("arbitrary",)),
        name="router",
    )(xall, g.reshape(1, d), mod3, mod3, wr_pad, bias_pad)


def _expert_kernel(be_ref, tok_cur, tok_next, dst_prev, dst_cur, h_hbm, win_ref, wout_ref, y_hbm,
                   xbuf0, xbuf1, ybuf0, ybuf1, win_bf, wout_bf, gsem, ssem, *, ff, n_blocks):
    i = pl.program_id(0)
    blk = xbuf0.shape[0]
    xbufs, ybufs = (xbuf0, xbuf1), (ybuf0, ybuf1)

    def gather(tok_ref, p):
        for r in range(blk):
            pltpu.make_async_copy(h_hbm.at[pl.ds(tok_ref[0, r], 1)], xbufs[p].at[pl.ds(r, 1)],
                                  gsem.at[p]).start()

    def scatter(dst_ref, p):
        for r in range(blk):
            pltpu.make_async_copy(ybufs[p].at[pl.ds(r, 1)], y_hbm.at[pl.ds(dst_ref[0, r], 1)],
                                  ssem.at[p]).start()

    def wait_gather(p):
        pltpu.make_async_copy(h_hbm.at[pl.ds(0, blk)], xbufs[p], gsem.at[p]).wait()

    def wait_scatter(p):
        pltpu.make_async_copy(ybufs[p], y_hbm.at[pl.ds(0, blk)], ssem.at[p]).wait()

    @pl.when((i == 0) | (be_ref[i] != be_ref[jnp.maximum(i - 1, 0)]))
    def _():
        win_bf[...] = win_ref[...].astype(BF16)
        wout_bf[...] = wout_ref[...].astype(BF16)

    @pl.when(i == 0)
    def _():
        ybuf1[...] = jnp.zeros_like(ybuf1)
        gather(tok_cur, 0)
        wait_gather(0)

    def step(p):
        gu = jnp.dot(xbufs[p][...].astype(BF16), win_bf[...], preferred_element_type=F32)
        g = gu[:, :ff]
        act = (g * _sigmoid(g) * gu[:, ff:]).astype(BF16)
        y = jnp.dot(act, wout_bf[...], preferred_element_type=F32)
        gather(tok_next, 1 - p)
        scatter(dst_prev, 1 - p)
        pl.when(i >= 1)(lambda: wait_scatter(p))
        ybufs[p][...] = y
        wait_gather(1 - p)

        @pl.when(i == n_blocks - 1)
        def _():
            scatter(dst_cur, p)
            wait_scatter(p)
            wait_scatter(1 - p)

    pl.when(i % 2 == 0)(lambda: step(0))
    pl.when(i % 2 == 1)(lambda: step(1))


def expert_ffn(h2, row_tok, row_dst, block_e, w_in, w_out, *, layer):
    d = h2.shape[1]
    rows = row_tok.shape[0]
    blk = TILES["moe_block"]
    n_blocks = rows // blk
    ff = w_out.shape[2]
    tok3 = row_tok.reshape(n_blocks, 1, blk)
    spare = rows + jnp.arange(blk, dtype=jnp.int32)
    dst3 = jnp.concatenate([spare, row_dst]).reshape(n_blocks + 1, 1, blk)
    smem_blk = lambda f: pl.BlockSpec((None, 1, blk), f, memory_space=pltpu.SMEM)
    kern = functools.partial(_expert_kernel, ff=ff, n_blocks=n_blocks)
    grid_spec = pltpu.PrefetchScalarGridSpec(
        num_scalar_prefetch=1,
        grid=(n_blocks,),
        in_specs=[
            smem_blk(lambda i, be: (i, 0, 0)),
            smem_blk(lambda i, be: (jnp.minimum(i + 1, n_blocks - 1), 0, 0)),
            smem_blk(lambda i, be: (i, 0, 0)),
            smem_blk(lambda i, be: (i + 1, 0, 0)),
            pl.BlockSpec(memory_space=pl.ANY),
            pl.BlockSpec((None, None, d, 2 * ff), lambda i, be: (layer, be[i], 0, 0)),
            pl.BlockSpec((None, None, ff, d), lambda i, be: (layer, be[i], 0, 0)),
        ],
        out_specs=pl.BlockSpec(memory_space=pl.ANY),
        scratch_shapes=[pltpu.VMEM((blk, d), F32), pltpu.VMEM((blk, d), F32),
                        pltpu.VMEM((blk, d), F32), pltpu.VMEM((blk, d), F32),
                        pltpu.VMEM((d, 2 * ff), BF16), pltpu.VMEM((ff, d), BF16),
                        pltpu.SemaphoreType.DMA((2,)), pltpu.SemaphoreType.DMA((2,))],
    )
    return pl.pallas_call(
        kern,
        out_shape=jax.ShapeDtypeStruct((rows + blk, d), F32),
        grid_spec=grid_spec,
        compiler_params=pltpu.CompilerParams(dimension_semantics=("arbitrary",),
                                             vmem_limit_bytes=TILES["vmem_expert"]),
        name="expert_ffn",
    )(block_e, tok3, tok3, dst3, dst3, h2, w_in, w_out)


def _combine_kernel(x_ref, y0_ref, y1_ref, w_ref, gate_ref, o_ref):
    w = w_ref[...]
    y = w[:, 0:1] * y0_ref[...].astype(F32) + w[:, 1:2] * y1_ref[...].astype(F32)
    o_ref[...] = x_ref[...] + gate_ref[...] * y


def moe_combine(xall, y_both, wts, mod3, *, layer, gate_idx, n_rows, geom):
    batch, seq, _ = geom
    d = xall.shape[1]
    tm = min(TILES["row"], seq)
    mrow = _mod_row_fn(batch * seq // tm, seq // tm, batch, layer)
    return pl.pallas_call(
        _combine_kernel,
        out_shape=jax.ShapeDtypeStruct((n_rows, d), F32),
        grid=(n_rows // tm,),
        in_specs=[
            pl.BlockSpec((tm, d), lambda m: (m, 0)),
            pl.BlockSpec((tm, d), lambda m: (m, 0)),
            pl.BlockSpec((tm, d), lambda m: (n_rows // tm + m, 0)),
            pl.BlockSpec((tm, LANES), lambda m: (m, 0)),
            pl.BlockSpec((None, 1, d), lambda m: (mrow(m), 0, gate_idx)),
        ],
        out_specs=pl.BlockSpec((tm, d), lambda m: (m, 0)),
        compiler_params=_cparams(("parallel",)),
        name="moe_combine",
    )(xall, y_both, y_both, wts, mod3)


def moe_dispatch(ids, counts, n_experts):
    blk = TILES["moe_block"]
    n = ids.shape[0]
    experts, rank = ids[:, :TOP_K], ids[:, TOP_K:2 * TOP_K]
    counts = counts[0, :n_experts].astype(jnp.int32)
    padded = (counts + blk - 1) // blk * blk
    pend = jnp.cumsum(padded)
    pstart = pend - padded
    eye = experts[:, :, None] == jnp.arange(n_experts, dtype=jnp.int32)[None, None, :]
    dest = jnp.sum(jnp.where(eye, pstart[None, None, :], 0), axis=-1) + rank
    n_blocks = -(-n * TOP_K // blk) + n_experts
    rows = n_blocks * blk
    slot = jnp.full((rows,), -1, jnp.int32).at[dest.T.reshape(-1)].set(jnp.arange(n * TOP_K, dtype=jnp.int32))
    pad = slot < 0
    pad_rank = jnp.cumsum(pad.astype(jnp.int32)) - 1
    row_dst = jnp.where(pad, n * TOP_K + pad_rank, slot)
    row_tok = jnp.where(pad, jnp.arange(rows, dtype=jnp.int32) % n, slot % n)
    block_start = jnp.arange(n_blocks, dtype=jnp.int32) * blk
    block_e = jnp.minimum(jnp.sum(block_start[:, None] >= pend[None, :], axis=1), n_experts - 1).astype(jnp.int32)
    return row_tok, row_dst, block_e


def _rope_tables(seq, tm):
    inv = ROPE_THETA ** (-jnp.arange(0, AXIS_DIM, 2, dtype=F32) / AXIS_DIM)
    t = jnp.arange(seq)
    ang_r = (t // GRID_W).astype(F32)[:, None] * inv
    ang_c = (t % GRID_W).astype(F32)[:, None] * inv
    cos = jnp.concatenate([jnp.cos(ang_r), jnp.cos(ang_r), jnp.cos(ang_c), jnp.cos(ang_c)], axis=-1)
    sin = jnp.concatenate([-jnp.sin(ang_r), jnp.sin(ang_r), -jnp.sin(ang_c), jnp.sin(ang_c)], axis=-1)
    pad = LANES - ROPE_DIM
    cos = jnp.pad(cos, ((0, 0), (0, pad)))
    sin = jnp.pad(sin, ((0, 0), (0, pad)))
    ident_c = jnp.pad(jnp.ones((tm, ROPE_DIM), F32), ((0, 0), (0, pad)))
    return (jnp.concatenate([ident_c, cos], axis=0),
            jnp.concatenate([jnp.zeros((tm, LANES), F32), sin], axis=0))


def _partner():
    j = jnp.arange(ROPE_DIM)
    half = AXIS_DIM // 2
    return jnp.where((j % AXIS_DIM) < half, j + half, j - half)


def _pad_lanes(a):
    return jnp.pad(a, [(0, 0)] * (a.ndim - 1) + [(0, LANES - a.shape[-1])])


def _norm_gain_rows(g):
    rope = g[NOPE_DIM:]
    rows = jnp.stack([g[:NOPE_DIM], _pad_lanes(rope), _pad_lanes(rope[_partner()])])
    return jnp.pad(rows, ((0, 8 - rows.shape[0]), (0, 0))).astype(F32)


def _layer_weights(l, w_in, w_uq, w_ukv, q_lora, kv_lora):
    d = w_in.shape[1]
    heads = MLA_HEADS
    wl = w_in[l]
    gate_cols = 3 * d
    conv_a2 = (wl.shape[1] - gate_cols - (q_lora + kv_lora + ROPE_DIM)) // 5 * 2
    off_b = gate_cols + conv_a2
    off_c = off_b + q_lora + kv_lora + ROPE_DIM
    w_c = wl[:, off_c:]
    kpe = wl[:, off_b + q_lora + kv_lora:off_c]
    w_b = jnp.concatenate([wl[:, off_b:off_b + q_lora + kv_lora], _pad_lanes(kpe),
                           _pad_lanes(kpe[:, _partner()])], axis=1)
    wq = w_uq[l].reshape(q_lora, heads, QK_DIM)
    wq_n = wq[:, :, :NOPE_DIM].reshape(q_lora, heads * LANES)
    wq_r = _pad_lanes(wq[:, :, NOPE_DIM:]).reshape(q_lora, heads * LANES)
    wq_s = _pad_lanes(wq[:, :, NOPE_DIM:][:, :, _partner()]).reshape(q_lora, heads * LANES)
    wq_all = jnp.concatenate([wq_n, wq_r, wq_s], axis=1).astype(BF16)
    wkv = w_ukv[l].reshape(kv_lora, heads, NOPE_DIM + V_DIM)
    wkv_all = jnp.concatenate([wkv[:, :, :NOPE_DIM].reshape(kv_lora, heads * NOPE_DIM),
                               wkv[:, :, NOPE_DIM:].reshape(kv_lora, heads * V_DIM)], axis=1).astype(BF16)
    return off_b, w_c, w_b, wq_all, wkv_all


def _router_layout(w_router, router_bias):
    d, n_experts = w_router.shape
    per_group = n_experts // N_GROUPS
    wr = w_router.reshape(d, N_GROUPS, per_group).transpose(0, 2, 1)
    wr_pad = _pad_lanes(wr).reshape(d, per_group * LANES).astype(F32)
    bias = _pad_lanes(router_bias.reshape(N_GROUPS, per_group).T).astype(F32)
    bias = jnp.pad(bias, ((0, 8 - per_group), (0, 0)))
    return wr_pad, bias, per_group


def kernel(x, c, ctx, c_ctx, w_mod, b_mod, g_norm1, g_norm2, w_in, conv_a, ln_a_g, ln_a_b, w_a_out, g_cq, g_ckv, w_uq, w_ukv, g_q, g_k, w_b_out, conv_c, w_c_out, w_o, w_router, router_bias, w_e_in, w_e_out):
    batch, seq, d = x.shape
    n_ctx = ctx.shape[1]
    depth = w_mod.shape[0]
    geom = (batch, seq, n_ctx)
    t_x = batch * seq
    t_all = t_x + batch * n_ctx
    q_lora, kv_lora = g_cq.shape[1], g_ckv.shape[1]
    ch_a, ch_c = conv_a.shape[2], conv_c.shape[2]
    n_experts = w_router.shape[1]
    assert batch + 1 <= MOD_ROWS and t_x % n_ctx == 0 and ch_a == ch_c and (3 * d) % ch_a == 0

    stream = (x.reshape(t_x, d), ctx.reshape(batch * n_ctx, d))
    cvec = jnp.concatenate([c, c_ctx[None, :], jnp.zeros((MOD_ROWS - batch - 1, d), F32)], axis=0)
    mod3 = modulation(cvec, w_mod, b_mod).reshape(depth * MOD_ROWS, 1, N_MOD * d)

    cos_tab, sin_tab = _rope_tables(seq, min(TILES["mla"], seq, n_ctx))
    wr_pad, bias_pad, per_group = _router_layout(w_router, router_bias)
    col0 = 3 * d // ch_a

    for l in range(depth):
        last = l == depth - 1
        rows_main = t_x if last else t_all
        off_b, w_c, w_b, wq_all, wkv_all = _layer_weights(l, w_in, w_uq, w_ukv, q_lora, kv_lora)

        h = norm_modulate(stream, g_norm1[l], mod3, layer=l, shift_idx=0, n_rows=t_all, geom=geom)
        u = project(h, w_in[l], n_cols=off_b, n_rows=rows_main)
        uc = project(h, w_c, n_cols=w_c.shape[1], n_rows=rows_main)
        ub = project(h, w_b, n_cols=w_b.shape[1], n_rows=t_all)
        qt, k, vt = mla_project(ub, g_cq[l], g_ckv[l], wq_all, wkv_all, _norm_gain_rows(g_q[l]),
                                _norm_gain_rows(g_k[l]), cos_tab, sin_tab, geom=geom)
        ob_x = attention_latent(qt, k, vt, geom=geom)
        ob_c = ob_x if last else attention_context(qt, k, vt, geom=geom)
        ha = depthwise_branch(u, conv_a[l], col_a=col0, col_b=col0 + 1, col_gate=None,
                              ln=(ln_a_g[l], ln_a_b[l]), n_rows=rows_main, geom=geom)
        hc = depthwise_branch(uc, conv_c[l], col_a=1, col_b=2, col_gate=0,
                              ln=None, n_rows=rows_main, geom=geom)
        mrg = branch_merge(ha, ob_x, ob_c, hc, w_a_out, w_b_out, w_c_out, u, layer=l,
                           n_rows=rows_main, geom=geom)
        x1 = out_proj_residual(mrg, w_o, stream, mod3, layer=l, gate_idx=2, n_rows=rows_main, geom=geom)

        h2, ids, wts, counts = router(x1, g_norm2[l], mod3, wr_pad, bias_pad, layer=l, shift_idx=3,
                                      n_rows=rows_main, geom=geom, per_group=per_group)
        row_tok, row_dst, block_e = moe_dispatch(ids, counts, n_experts)
        y_both = expert_ffn(h2, row_tok, row_dst, block_e, w_e_in, w_e_out, layer=l)
        stream = moe_combine(x1, y_both, wts, mod3, layer=l, gate_idx=5, n_rows=rows_main, geom=geom)

    return stream.reshape(batch, seq, d)
```

```python
import functools
import math

import jax
import jax.numpy as jnp
from jax import lax
from jax.experimental import pallas as pl
from jax.experimental.pallas import tpu as pltpu

F32 = jnp.float32
BF16 = jnp.bfloat16

GRID_W = 64
MLA_HEADS = 8
NOPE_DIM = 128
ROPE_DIM = 64
V_DIM = 128
QK_DIM = NOPE_DIM + ROPE_DIM
AXIS_DIM = ROPE_DIM // 2
ROPE_THETA = 10000.0
N_GROUPS = 8
TOP_K = 2
N_MOD = 6
EPS = 1e-6

LANES = 128
HEAD_SLOT = 2 * LANES
HALO_ROWS = 16
MOD_ROWS = 8

TILES = dict(
    row=512,
    proj_row=1024,
    col=1024,
    mod_col=1024,
    mla=256,
    tq=512,
    tq_sub=512,
    attn_keys=512,
    conv=256,
    conv_rows=64,
    router=512,
    moe_block=256,
    vmem=56 * 1024 * 1024,
    vmem_expert=60 * 1024 * 1024,
)


def _cparams(sem):
    return pltpu.CompilerParams(dimension_semantics=sem, vmem_limit_bytes=TILES["vmem"])


def _sigmoid(x):
    return 1.0 / (1.0 + jnp.exp(-x))


def _mod_kernel(c_ref, w_ref, b_ref, o_ref):
    c = c_ref[...]
    s = (c * _sigmoid(c)).astype(BF16)
    o_ref[...] = jnp.dot(s, w_ref[...].astype(BF16), preferred_element_type=F32) + b_ref[...]


def modulation(cvec, w_mod, b_mod):
    n_layers, d, n = w_mod.shape
    tn = min(TILES["mod_col"], n)
    return pl.pallas_call(
        _mod_kernel,
        out_shape=jax.ShapeDtypeStruct((n_layers, MOD_ROWS, n), F32),
        grid=(n_layers, n // tn),
        in_specs=[
            pl.BlockSpec((MOD_ROWS, d), lambda l, j: (0, 0)),
            pl.BlockSpec((None, d, tn), lambda l, j: (l, 0, j)),
            pl.BlockSpec((None, 1, tn), lambda l, j: (l, 0, j)),
        ],
        out_specs=pl.BlockSpec((None, MOD_ROWS, tn), lambda l, j: (l, 0, j)),
        compiler_params=_cparams(("parallel", "parallel")),
        name="modulation",
    )(cvec, w_mod, b_mod.reshape(n_layers, 1, n))


def _norm_mod(x, g, shift, scale):
    ms = jnp.mean(x * x, axis=-1, keepdims=True)
    return x * lax.rsqrt(ms + EPS) * g * (1.0 + scale) + shift


def _stream_specs(stream, tm, n_x_tiles, tn=None, grid_rank=1):
    arrays = stream if isinstance(stream, tuple) else (stream,)
    width = arrays[0].shape[1] if tn is None else tn

    def spec(row_fn):
        if grid_rank == 1:
            return pl.BlockSpec((tm, width), lambda m: (row_fn(m), 0))
        return pl.BlockSpec((tm, width), lambda j, m: (row_fn(m), j))

    if len(arrays) == 1:
        return [spec(lambda m: m)], list(arrays)
    return [spec(lambda m: jnp.minimum(m, n_x_tiles - 1)),
            spec(lambda m: jnp.maximum(m - n_x_tiles, 0))], list(arrays)


def _stream_tile(refs, m, n_x_tiles):
    if len(refs) == 1:
        return refs[0][...]
    return jnp.where(m < n_x_tiles, refs[0][...], refs[1][...])


def _norm_kernel(*refs, n_x_tiles):
    x_refs, (g_ref, shift_ref, scale_ref, h_ref) = refs[:-4], refs[-4:]
    x = _stream_tile(x_refs, pl.program_id(0), n_x_tiles)
    h_ref[...] = _norm_mod(x, g_ref[...], shift_ref[...], scale_ref[...]).astype(h_ref.dtype)


def _mod_row_fn(n_x_tiles, tiles_per_seq, batch, layer):
    def f(m):
        return layer * MOD_ROWS + jnp.where(m < n_x_tiles, m // tiles_per_seq, batch)
    return f


def norm_modulate(stream, g, mod3, *, layer, shift_idx, n_rows, geom):
    batch, seq, _ = geom
    d = g.shape[0]
    tm = min(TILES["row"], seq)
    n_x = batch * seq // tm
    mrow = _mod_row_fn(n_x, seq // tm, batch, layer)
    x_specs, x_args = _stream_specs(stream, tm, n_x)
    return pl.pallas_call(
        functools.partial(_norm_kernel, n_x_tiles=n_x),
        out_shape=jax.ShapeDtypeStruct((n_rows, d), BF16),
        grid=(n_rows // tm,),
        in_specs=x_specs + [
            pl.BlockSpec((1, d), lambda m: (0, 0)),
            pl.BlockSpec((None, 1, d), lambda m: (mrow(m), 0, shift_idx)),
            pl.BlockSpec((None, 1, d), lambda m: (mrow(m), 0, shift_idx + 1)),
        ],
        out_specs=pl.BlockSpec((tm, d), lambda m: (m, 0)),
        compiler_params=_cparams(("parallel",)),
        name="norm_modulate",
    )(*x_args, g.reshape(1, d), mod3, mod3)


def _proj_kernel(a_ref, w_ref, o_ref, w_bf):
    @pl.when(pl.program_id(1) == 0)
    def _():
        w_bf[...] = w_ref[...].astype(BF16)

    o_ref[...] = jnp.dot(a_ref[...], w_bf[...], preferred_element_type=F32).astype(o_ref.dtype)


def _w_spec(w, layer, tn, col0=0):
    if w.ndim == 3:
        return pl.BlockSpec((None, w.shape[1], tn), lambda j, m: (layer, 0, col0 + j))
    return pl.BlockSpec((w.shape[0], tn), lambda j, m: (0, col0 + j))


def project(a, w, *, n_cols, n_rows, layer=0):
    k = a.shape[1]
    tm = min(TILES["proj_row"], n_rows)
    tn = min(TILES["col"], n_cols)
    if n_cols % tn:
        tn = n_cols
    return pl.pallas_call(
        _proj_kernel,
        out_shape=jax.ShapeDtypeStruct((n_rows, n_cols), BF16),
        grid=(n_cols // tn, n_rows // tm),
        in_specs=[pl.BlockSpec((tm, k), lambda j, m: (m, 0)), _w_spec(w, layer, tn)],
        out_specs=pl.BlockSpec((tm, tn), lambda j, m: (m, j)),
        scratch_shapes=[pltpu.VMEM((k, tn), BF16)],
        compiler_params=_cparams(("parallel", "arbitrary")),
        name="project",
    )(a, w)


def _mla_kernel(ub_ref, gcq_ref, gckv_ref, wq_ref, wkv_ref, gq_ref, gk_ref, cos_ref, sin_ref,
                qt_ref, k_ref, vt_ref, *, q_lora, kv_lora, heads):
    hn = heads * LANES

    def rms(xv, g):
        return (xv * lax.rsqrt(jnp.mean(xv * xv, axis=-1, keepdims=True) + EPS) * g).astype(BF16)

    cqn = rms(ub_ref[:, 0:q_lora].astype(F32), gcq_ref[...])
    ckvn = rms(ub_ref[:, q_lora:q_lora + kv_lora].astype(F32), gckv_ref[...])
    off = q_lora + kv_lora
    kpe = ub_ref[:, off:off + LANES].astype(F32)
    kpes = ub_ref[:, off + LANES:off + 2 * LANES].astype(F32)

    qall = jnp.dot(cqn, wq_ref[...], preferred_element_type=F32)
    kvall = jnp.dot(ckvn, wkv_ref[...], preferred_element_type=F32)

    cos = cos_ref[...]
    sin = sin_ref[...]
    gq_n, gq_r, gq_s = gq_ref[0:1, :], gq_ref[1:2, :], gq_ref[2:3, :]
    gk_n, gk_r, gk_s = gk_ref[0:1, :], gk_ref[1:2, :], gk_ref[2:3, :]
    q_cos, q_sin = gq_r * cos, gq_s * sin
    k_rope = kpe * (gk_r * cos) + kpes * (gk_s * sin)
    ssq_pe = jnp.sum(kpe * kpe, axis=-1, keepdims=True)
    q_scale = QK_DIM ** -0.5 * math.log2(math.e)

    for h in range(heads):
        lo = h * LANES
        qn = qall[:, lo:lo + LANES]
        qr = qall[:, hn + lo:hn + lo + LANES]
        qs = qall[:, 2 * hn + lo:2 * hn + lo + LANES]
        ssq = jnp.sum(qn * qn, axis=-1, keepdims=True) + jnp.sum(qr * qr, axis=-1, keepdims=True)
        inv = lax.rsqrt(ssq * (1.0 / QK_DIM) + EPS) * q_scale
        qt_ref[h * HEAD_SLOT:h * HEAD_SLOT + LANES, :] = (qn * gq_n * inv).T.astype(BF16)
        qt_ref[h * HEAD_SLOT + LANES:(h + 1) * HEAD_SLOT, :] = ((qr * q_cos + qs * q_sin) * inv).T.astype(BF16)

        kn = kvall[:, lo:lo + LANES]
        invk = lax.rsqrt((jnp.sum(kn * kn, axis=-1, keepdims=True) + ssq_pe) * (1.0 / QK_DIM) + EPS)
        k_ref[:, h * HEAD_SLOT:h * HEAD_SLOT + LANES] = (kn * gk_n * invk).astype(BF16)
        k_ref[:, h * HEAD_SLOT + LANES:(h + 1) * HEAD_SLOT] = (k_rope * invk).astype(BF16)

        vt_ref[h * V_DIM:(h + 1) * V_DIM, :] = kvall[:, hn + h * V_DIM:hn + (h + 1) * V_DIM].T.astype(BF16)


def mla_project(ub, g_cq, g_ckv, wq, wkv, gq3, gk3, cos_tab, sin_tab, *, geom):
    batch, seq, ctx = geom
    t_all = ub.shape[0]
    q_lora, kv_lora = g_cq.shape[0], g_ckv.shape[0]
    heads = MLA_HEADS
    tm = min(TILES["mla"], seq, ctx)
    n_x = batch * seq // tm
    per_seq = seq // tm

    def pos(m):
        return jnp.where(m < n_x, 1 + m % per_seq, 0)

    kern = functools.partial(_mla_kernel, q_lora=q_lora, kv_lora=kv_lora, heads=heads)
    return pl.pallas_call(
        kern,
        out_shape=(
            jax.ShapeDtypeStruct((heads * HEAD_SLOT, t_all), BF16),
            jax.ShapeDtypeStruct((t_all, heads * HEAD_SLOT), BF16),
            jax.ShapeDtypeStruct((heads * V_DIM, t_all), BF16),
        ),
        grid=(t_all // tm,),
        in_specs=[
            pl.BlockSpec((tm, ub.shape[1]), lambda m: (m, 0)),
            pl.BlockSpec((1, q_lora), lambda m: (0, 0)),
            pl.BlockSpec((1, kv_lora), lambda m: (0, 0)),
            pl.BlockSpec(wq.shape, lambda m: (0, 0)),
            pl.BlockSpec(wkv.shape, lambda m: (0, 0)),
            pl.BlockSpec((8, LANES), lambda m: (0, 0)),
            pl.BlockSpec((8, LANES), lambda m: (0, 0)),
            pl.BlockSpec((tm, LANES), lambda m: (pos(m), 0)),
            pl.BlockSpec((tm, LANES), lambda m: (pos(m), 0)),
        ],
        out_specs=(
            pl.BlockSpec((heads * HEAD_SLOT, tm), lambda m: (0, m)),
            pl.BlockSpec((tm, heads * HEAD_SLOT), lambda m: (m, 0)),
            pl.BlockSpec((heads * V_DIM, tm), lambda m: (0, m)),
        ),
        compiler_params=_cparams(("parallel",)),
        name="mla_project",
    )(ub, g_cq.reshape(1, -1), g_ckv.reshape(1, -1), wq, wkv, gq3, gk3, cos_tab, sin_tab)


def _attn_kernel(*refs, sub, latent_keys):
    if latent_keys:
        qt_ref, kx_ref, kc_ref, vxt_ref, vct_ref, o_ref = refs
    else:
        qt_ref, kc_ref, vct_ref, o_ref = refs
    for j in range(qt_ref.shape[1] // sub):
        qt = qt_ref[:, j * sub:(j + 1) * sub]
        sc = jnp.dot(kc_ref[...], qt, preferred_element_type=F32)
        m = jnp.max(sc, axis=0, keepdims=True)
        if latent_keys:
            sx = jnp.dot(kx_ref[...], qt, preferred_element_type=F32)
            m = jnp.maximum(m, jnp.max(sx, axis=0, keepdims=True))
        pc = jnp.exp2(sc - m)
        denom = jnp.sum(pc, axis=0, keepdims=True)
        ot = jnp.dot(vct_ref[...], pc.astype(BF16), preferred_element_type=F32)
        if latent_keys:
            px = jnp.exp2(sx - m)
            denom = denom + jnp.sum(px, axis=0, keepdims=True)
            ot = ot + jnp.dot(vxt_ref[...], px.astype(BF16), preferred_element_type=F32)
        o_ref[j * sub:(j + 1) * sub, :] = (ot * (1.0 / denom)).T.astype(o_ref.dtype)


def _attn_pipelined_kernel(qt_ref, kx_ref, kc_ref, vxt_ref, vct_ref, o_ref, s0_ref, s1_ref, m0_ref, m1_ref,
                           *, n_q, chunk):
    i = pl.program_id(2)
    seq, ctx = kx_ref.shape[0], kc_ref.shape[0]
    tq = qt_ref.shape[1]
    chunks = [(kc_ref, vct_ref, 0, ctx, 0)]
    chunks += [(kx_ref, vxt_ref, c0, chunk, ctx + c0) for c0 in range(0, seq, chunk)]
    slots = ((s0_ref, m0_ref), (s1_ref, m1_ref))

    def run(new, old):
        if new is not None:
            s_new, m_new = slots[new]
            qt = qt_ref[...]
            m = jnp.full((1, tq), -jnp.inf, F32)
        if old is not None:
            s_old, m_old_ref = slots[old]
            m_old = m_old_ref[...]
            acc = jnp.zeros((V_DIM, tq), F32)
            den = jnp.zeros((1, tq), F32)
        for kref, vtref, c0, size, row0 in chunks:
            if new is not None:
                s = jnp.dot(kref[c0:c0 + size, :], qt, preferred_element_type=F32)
                s_new[row0:row0 + size, :] = s
                m = jnp.maximum(m, jnp.max(s, axis=0, keepdims=True))
            if old is not None:
                p = jnp.exp2(s_old[row0:row0 + size, :] - m_old)
                den = den + jnp.sum(p, axis=0, keepdims=True)
                acc = acc + jnp.dot(vtref[:, c0:c0 + size], p.astype(BF16), preferred_element_type=F32)
        if new is not None:
            m_new[...] = m
        if old is not None:
            o_ref[...] = (acc * (1.0 / den)).T.astype(o_ref.dtype)

    pl.when(i == 0)(lambda: run(0, None))
    pl.when((i > 0) & (i < n_q) & (i % 2 == 1))(lambda: run(1, 0))
    pl.when((i > 0) & (i < n_q) & (i % 2 == 0))(lambda: run(0, 1))
    pl.when(i == n_q)(lambda: run(None, (n_q - 1) % 2))


def attention_latent(qt, k, vt, *, geom):
    batch, seq, ctx = geom
    heads = MLA_HEADS
    tq = min(TILES["tq"], seq)
    n_q = seq // tq
    ctx_blk0 = batch * seq // ctx
    kern = functools.partial(_attn_pipelined_kernel, n_q=n_q, chunk=min(TILES["attn_keys"], seq))
    return pl.pallas_call(
        kern,
        out_shape=jax.ShapeDtypeStruct((batch * seq, heads * V_DIM), BF16),
        grid=(batch, heads, n_q + 1),
        in_specs=[
            pl.BlockSpec((HEAD_SLOT, tq), lambda b, h, i: (h, b * n_q + jnp.minimum(i, n_q - 1))),
            pl.BlockSpec((seq, HEAD_SLOT), lambda b, h, i: (b, h)),
            pl.BlockSpec((ctx, HEAD_SLOT), lambda b, h, i: (ctx_blk0 + b, h)),
            pl.BlockSpec((V_DIM, seq), lambda b, h, i: (h, b)),
            pl.BlockSpec((V_DIM, ctx), lambda b, h, i: (h, ctx_blk0 + b)),
        ],
        out_specs=pl.BlockSpec((tq, V_DIM), lambda b, h, i: (b * n_q + jnp.maximum(i - 1, 0), h)),
        scratch_shapes=[pltpu.VMEM((ctx + seq, tq), F32), pltpu.VMEM((ctx + seq, tq), F32),
                        pltpu.VMEM((1, tq), F32), pltpu.VMEM((1, tq), F32)],
        compiler_params=_cparams(("parallel", "parallel", "arbitrary")),
        name="attention_latent",
    )(qt, k, k, vt, vt)


def attention_context(qt, k, vt, *, geom):
    batch, seq, ctx = geom
    heads = MLA_HEADS
    ctx_blk0 = batch * seq // ctx
    kern = functools.partial(_attn_kernel, sub=min(TILES["tq_sub"], ctx), latent_keys=False)
    return pl.pallas_call(
        kern,
        out_shape=jax.ShapeDtypeStruct((batch * ctx, heads * V_DIM), BF16),
        grid=(batch, heads),
        in_specs=[
            pl.BlockSpec((HEAD_SLOT, ctx), lambda b, h: (h, ctx_blk0 + b)),
            pl.BlockSpec((ctx, HEAD_SLOT), lambda b, h: (ctx_blk0 + b, h)),
            pl.BlockSpec((V_DIM, ctx), lambda b, h: (h, ctx_blk0 + b)),
        ],
        out_specs=pl.BlockSpec((ctx, V_DIM), lambda b, h: (b, h)),
        compiler_params=_cparams(("parallel", "parallel")),
        name="attention_context",
    )(qt, k, vt)


def _conv_kernel(*refs, width, conformer, tp, n_x_tiles, x_per_seq, c_per_seq):
    if conformer:
        (ap, bp, am, bm, an, bn, w_ref, lng_ref, lnb_ref, o_ref, buf, ybuf) = refs
        gate_ref = None
    else:
        (gate_ref, ap, bp, am, bm, an, bn, w_ref, o_ref, buf, ybuf) = refs
    nchunk = buf.shape[0]
    i = pl.program_id(0)
    in_x = i < n_x_tiles
    pos = jnp.where(in_x, i % x_per_seq, (i - n_x_tiles) % c_per_seq)
    per_seq = jnp.where(in_x, x_per_seq, c_per_seq)
    keep_prev = (pos > 0).astype(F32)
    keep_next = (pos < per_seq - 1).astype(F32)

    def pre(a, b):
        a = a.astype(F32)
        b = b.astype(F32)
        return a * _sigmoid(b) if conformer else a * b

    for c in range(nchunk):
        sl = slice(c * LANES, (c + 1) * LANES)
        buf[c, 0:HALO_ROWS, :] = pre(ap[:, sl], bp[:, sl]) * keep_prev
        buf[c, HALO_ROWS:HALO_ROWS + tp, :] = pre(am[:, sl], bm[:, sl])
        buf[c, HALO_ROWS + tp:2 * HALO_ROWS + tp, :] = pre(an[:, sl], bn[:, sl]) * keep_next

    rc = min(TILES["conv_rows"], tp)
    base = HALO_ROWS - width // 2

    def chunk(c, carry):
        wts = w_ref[c]
        for r0 in range(0, tp, rc):
            acc = jnp.zeros((rc, LANES), F32)
            for k in range(width):
                acc = acc + wts[k:k + 1, :] * buf[c, pl.ds(base + r0 + k, rc), :]
            ybuf[c, pl.ds(r0, rc), :] = acc
        return carry

    lax.fori_loop(0, nchunk, chunk, 0)

    if conformer:
        n_ch = nchunk * LANES
        tot = jnp.zeros((tp, 1), F32)
        for c in range(nchunk):
            tot = tot + jnp.sum(ybuf[c], axis=-1, keepdims=True)
        mu = tot * (1.0 / n_ch)
        tot2 = jnp.zeros((tp, 1), F32)
        for c in range(nchunk):
            dlt = ybuf[c] - mu
            tot2 = tot2 + jnp.sum(dlt * dlt, axis=-1, keepdims=True)
        inv = lax.rsqrt(tot2 * (1.0 / n_ch) + EPS)
        for c in range(nchunk):
            sl = slice(c * LANES, (c + 1) * LANES)
            y = (ybuf[c] - mu) * inv * lng_ref[:, sl] + lnb_ref[:, sl]
            o_ref[:, sl] = (y * _sigmoid(y)).astype(o_ref.dtype)
    else:
        for c in range(nchunk):
            sl = slice(c * LANES, (c + 1) * LANES)
            o_ref[:, sl] = (gate_ref[:, sl].astype(F32) * ybuf[c]).astype(o_ref.dtype)


def depthwise_branch(u, conv_w, *, col_a, col_b, col_gate, ln, n_rows, geom):
    batch, seq, ctx = geom
    width, ch = conv_w.shape
    conformer = ln is not None
    tp = min(TILES["conv"], seq, ctx)
    nchunk = ch // LANES
    hb = tp // HALO_ROWS
    last_halo = u.shape[0] // HALO_ROWS - 1
    wpad = -(-width // 8) * 8
    w3 = jnp.pad(conv_w.astype(F32), ((0, wpad - width), (0, 0))).reshape(wpad, nchunk, LANES).transpose(1, 0, 2)

    def prev_map(col):
        return lambda i: (jnp.maximum(i * hb - 1, 0), col)

    def main_map(col):
        return lambda i: (i, col)

    def next_map(col):
        return lambda i: (jnp.minimum((i + 1) * hb, last_halo), col)

    in_specs, args = [], []
    if not conformer:
        in_specs.append(pl.BlockSpec((tp, ch), main_map(col_gate)))
        args.append(u)
    for rows, mk in ((HALO_ROWS, prev_map), (tp, main_map), (HALO_ROWS, next_map)):
        for col in (col_a, col_b):
            in_specs.append(pl.BlockSpec((rows, ch), mk(col)))
            args.append(u)
    in_specs.append(pl.BlockSpec((nchunk, wpad, LANES), lambda i: (0, 0, 0)))
    args.append(w3)
    if conformer:
        for p in ln:
            in_specs.append(pl.BlockSpec((1, ch), lambda i: (0, 0)))
            args.append(p.reshape(1, ch).astype(F32))

    kern = functools.partial(
        _conv_kernel, width=width, conformer=conformer, tp=tp, n_x_tiles=batch * seq // tp,
        x_per_seq=seq // tp, c_per_seq=ctx // tp)
    return pl.pallas_call(
        kern,
        out_shape=jax.ShapeDtypeStruct((n_rows, ch), BF16),
        grid=(n_rows // tp,),
        in_specs=in_specs,
        out_specs=pl.BlockSpec((tp, ch), lambda i: (i, 0)),
        scratch_shapes=[pltpu.VMEM((nchunk, tp + 2 * HALO_ROWS, LANES), F32),
                        pltpu.VMEM((nchunk, tp, LANES), F32)],
        compiler_params=_cparams(("parallel",)),
        name="conformer_conv" if conformer else "short_conv",
    )(*args)


def _merge_kernel(ha_ref, obx_ref, obc_ref, hc_ref, wa_ref, wb_ref, wc_ref, ga_ref, gb_ref, gc_ref, o_ref,
                  wa_bf, wb_bf, wc_bf, *, n_x_tiles):
    m_idx = pl.program_id(1)

    @pl.when(m_idx == 0)
    def _():
        wa_bf[...] = wa_ref[...].astype(BF16)
        wb_bf[...] = wb_ref[...].astype(BF16)
        wc_bf[...] = wc_ref[...].astype(BF16)

    ob = jnp.where(m_idx < n_x_tiles, obx_ref[...], obc_ref[...])
    ya = jnp.dot(ha_ref[...], wa_bf[...], preferred_element_type=F32)
    yb = jnp.dot(ob, wb_bf[...], preferred_element_type=F32)
    yc = jnp.dot(hc_ref[...], wc_bf[...], preferred_element_type=F32)
    m = (_sigmoid(ga_ref[...].astype(F32)) * ya + _sigmoid(gb_ref[...].astype(F32)) * yb
         + _sigmoid(gc_ref[...].astype(F32)) * yc)
    o_ref[...] = m.astype(o_ref.dtype)


def branch_merge(ha, ob_x, ob_c, hc, wa, wb, wc, u, *, layer, n_rows, geom):
    batch, seq, _ = geom
    d = wa.shape[2]
    tm = min(TILES["row"], seq)
    tn = min(TILES["col"], d)
    ncol = d // tn
    n_x = batch * seq // tm
    row_spec = lambda a: pl.BlockSpec((tm, a.shape[1]), lambda j, m: (m, 0))
    gate_spec = lambda k: pl.BlockSpec((tm, tn), lambda j, m: (m, k * ncol + j))
    kern = functools.partial(_merge_kernel, n_x_tiles=n_x)
    return pl.pallas_call(
        kern,
        out_shape=jax.ShapeDtypeStruct((n_rows, d), BF16),
        grid=(ncol, n_rows // tm),
        in_specs=[row_spec(ha),
                  pl.BlockSpec((tm, ob_x.shape[1]), lambda j, m: (jnp.minimum(m, n_x - 1), 0)),
                  pl.BlockSpec((tm, ob_c.shape[1]), lambda j, m: (jnp.maximum(m - n_x, 0), 0)),
                  row_spec(hc), _w_spec(wa, layer, tn), _w_spec(wb, layer, tn), _w_spec(wc, layer, tn),
                  gate_spec(0), gate_spec(1), gate_spec(2)],
        out_specs=pl.BlockSpec((tm, tn), lambda j, m: (m, j)),
        scratch_shapes=[pltpu.VMEM((w.shape[1], tn), BF16) for w in (wa, wb, wc)],
        compiler_params=_cparams(("parallel", "arbitrary")),
        name="branch_merge",
    )(ha, ob_x, ob_c, hc, wa, wb, wc, u, u, u)


def _outproj_kernel(*refs, n_x_tiles):
    (m_ref, w_ref, gate_ref), x_refs, (o_ref, w_bf) = refs[:3], refs[3:-2], refs[-2:]
    m_idx = pl.program_id(1)

    @pl.when(m_idx == 0)
    def _():
        w_bf[...] = w_ref[...].astype(BF16)

    y = jnp.dot(m_ref[...], w_bf[...], preferred_element_type=F32)
    o_ref[...] = _stream_tile(x_refs, m_idx, n_x_tiles) + gate_ref[...] * y


def out_proj_residual(mrg, w_o, stream, mod3, *, layer, gate_idx, n_rows, geom):
    batch, seq, _ = geom
    d = w_o.shape[2]
    tm = min(TILES["row"], seq)
    tn = min(TILES["col"], d)
    ncol = d // tn
    n_x = batch * seq // tm
    mrow = _mod_row_fn(n_x, seq // tm, batch, layer)
    x_specs, x_args = _stream_specs(stream, tm, n_x, tn=tn, grid_rank=2)
    return pl.pallas_call(
        functools.partial(_outproj_kernel, n_x_tiles=n_x),
        out_shape=jax.ShapeDtypeStruct((n_rows, d), F32),
        grid=(ncol, n_rows // tm),
        in_specs=[
            pl.BlockSpec((tm, mrg.shape[1]), lambda j, m: (m, 0)),
            _w_spec(w_o, layer, tn),
            pl.BlockSpec((None, 1, tn), lambda j, m: (mrow(m), 0, gate_idx * ncol + j)),
        ] + x_specs,
        out_specs=pl.BlockSpec((tm, tn), lambda j, m: (m, j)),
        scratch_shapes=[pltpu.VMEM((w_o.shape[1], tn), BF16)],
        compiler_params=_cparams(("parallel", "arbitrary")),
        name="out_proj_residual",
    )(mrg, w_o, mod3, *x_args)


def _pack_halves(v):
    n = v.shape[1] // 2
    lo = lax.bitcast_convert_type(v[:, :n].astype(BF16).astype(F32), jnp.int32)
    hi = lax.bitcast_convert_type(v[:, n:].astype(BF16).astype(F32), jnp.int32)
    return lax.shift_right_logical(lo, jnp.int32(16)) | (hi & jnp.int32(-65536))


def _unpack_halves(p):
    lo = lax.bitcast_convert_type(lax.shift_left(p, jnp.int32(16)), F32)
    hi = lax.bitcast_convert_type(p & jnp.int32(-65536), F32)
    return lo, hi


def _router_kernel(x_ref, g_ref, shift_ref, scale_ref, wr_ref, bias_ref, h_ref, ids_ref, wts_ref, cnt_ref,
                   *, per_group):
    h = _norm_mod(x_ref[...], g_ref[...], shift_ref[...], scale_ref[...])
    h_ref[...] = _pack_halves(h)
    h_hi = h.astype(BF16)
    h_lo = (h - h_hi.astype(F32)).astype(BF16)
    w = wr_ref[...]
    w_hi = w.astype(BF16)
    w_lo = (w - w_hi.astype(F32)).astype(BF16)
    logits = (jnp.dot(h_hi, w_hi, preferred_element_type=F32)
              + jnp.dot(h_hi, w_lo, preferred_element_type=F32)
              + jnp.dot(h_lo, w_hi, preferred_element_type=F32))
    lane_i = lax.broadcasted_iota(jnp.int32, (h.shape[0], LANES), 1)
    lane = lane_i.astype(F32)
    valid = lane_i < N_GROUPS
    neg = jnp.float32(-jnp.inf)
    score, sel = [], []
    for j in range(per_group):
        s = _sigmoid(logits[:, j * LANES:(j + 1) * LANES])
        score.append(s)
        sel.append(jnp.where(valid, s + bias_ref[j:j + 1, :], neg))
    assert per_group == 4
    hi1, lo1 = jnp.maximum(sel[0], sel[1]), jnp.minimum(sel[0], sel[1])
    hi2, lo2 = jnp.maximum(sel[2], sel[3]), jnp.minimum(sel[2], sel[3])
    top1 = jnp.maximum(hi1, hi2)
    top2 = jnp.maximum(jnp.minimum(hi1, hi2), jnp.maximum(lo1, lo2))
    gscore = jnp.where(valid, top1 + top2, neg)
    gmax = jnp.max(gscore, axis=-1, keepdims=True)
    grp = jnp.min(jnp.where(gscore == gmax, lane, float(LANES)), axis=-1, keepdims=True)
    pick = lane == grp
    zero = jnp.float32(0.0)
    sel_g = [jnp.sum(jnp.where(pick, s, zero), axis=-1, keepdims=True) for s in sel]
    sc_g = [jnp.sum(jnp.where(pick, s, zero), axis=-1, keepdims=True) for s in score]
    best, best_i, best_s = sel_g[0], jnp.zeros_like(grp), sc_g[0]
    for j in range(1, per_group):
        better = sel_g[j] > best
        best = jnp.where(better, sel_g[j], best)
        best_i = jnp.where(better, float(j), best_i)
        best_s = jnp.where(better, sc_g[j], best_s)
    second, second_i, second_s = jnp.full_like(best, neg), jnp.zeros_like(grp), jnp.zeros_like(best)
    for j in range(per_group):
        better = (sel_g[j] > second) & (best_i != float(j))
        second = jnp.where(better, sel_g[j], second)
        second_i = jnp.where(better, float(j), second_i)
        second_s = jnp.where(better, sc_g[j], second_s)
    tot = best_s + second_s
    e0 = grp * per_group + best_i
    e1 = grp * per_group + second_i
    @pl.when(pl.program_id(0) == 0)
    def _():
        cnt_ref[...] = jnp.zeros_like(cnt_ref)

    oh0 = lane == e0
    oh1 = lane == e1
    both = jnp.where(oh0 | oh1, jnp.float32(1.0), zero)
    tm = h.shape[0]
    earlier = (lax.broadcasted_iota(jnp.int32, (tm, tm), 1)
               < lax.broadcasted_iota(jnp.int32, (tm, tm), 0)).astype(BF16)
    before = jnp.dot(earlier, both.astype(BF16), preferred_element_type=F32) + cnt_ref[0:1, :]
    r0 = jnp.sum(jnp.where(oh0, before, zero), axis=-1, keepdims=True)
    r1 = jnp.sum(jnp.where(oh1, before, zero), axis=-1, keepdims=True)
    cnt_ref[...] = cnt_ref[...] + jnp.sum(both, axis=0, keepdims=True)
    ids = jnp.where(lane_i == 0, e0, jnp.where(lane_i == 1, e1, jnp.where(lane_i == 2, r0,
                    jnp.where(lane_i == 3, r1, zero))))
    ids_ref[...] = ids.astype(jnp.int32)
    wts_ref[...] = jnp.where(lane_i == 0, best_s / tot, jnp.where(lane_i == 1, second_s / tot, zero))


def router(xall, g, mod3, wr_pad, bias_pad, *, layer, shift_idx, n_rows, geom, per_group):
    batch, seq, _ = geom
    d = xall.shape[1]
    tm = min(TILES["router"], seq)
    mrow = _mod_row_fn(batch * seq // tm, seq // tm, batch, layer)
    kern = functools.partial(_router_kernel, per_group=per_group)
    return pl.pallas_call(
        kern,
        out_shape=(jax.ShapeDtypeStruct((n_rows, d // 2), jnp.int32),
                   jax.ShapeDtypeStruct((n_rows, LANES), jnp.int32),
                   jax.ShapeDtypeStruct((n_rows, LANES), F32),
                   jax.ShapeDtypeStruct((8, LANES), F32)),
        grid=(n_rows // tm,),
        in_specs=[
            pl.BlockSpec((tm, d), lambda m: (m, 0)),
            pl.BlockSpec((1, d), lambda m: (0, 0)),
            pl.BlockSpec((None, 1, d), lambda m: (mrow(m), 0, shift_idx)),
            pl.BlockSpec((None, 1, d), lambda m: (mrow(m), 0, shift_idx + 1)),
            pl.BlockSpec(wr_pad.shape, lambda m: (0, 0)),
            pl.BlockSpec(bias_pad.shape, lambda m: (0, 0)),
        ],
        out_specs=(pl.BlockSpec((tm, d // 2), lambda m: (m, 0)),
                   pl.BlockSpec((tm, LANES), lambda m: (m, 0)),
                   pl.BlockSpec((tm, LANES), lambda m: (m, 0)),
                   pl.BlockSpec((8, LANES), lambda m: (0, 0))),
        compiler_params=_cparams(("arbitrary",)),
        name="router",
    )(xall, g.reshape(1, d), mod3, mod3, wr_pad, bias_pad)


def _expert_kernel(be_ref, nxt_ref, tok_cur, tok_next, dst_prev, dst_cur, h_hbm, win_hbm, wout_hbm, y_hbm,
                   xbuf0, xbuf1, ybuf0, ybuf1, win_f32, wout_f32, win_bf, wout_bf, gsem, ssem, wsem,
                   *, ff, n_blocks, layer):
    i = pl.program_id(0)
    blk = xbuf0.shape[0]
    xbufs, ybufs = (xbuf0, xbuf1), (ybuf0, ybuf1)

    def weight_copies(e):
        return (pltpu.make_async_copy(win_hbm.at[layer, e], win_f32, wsem.at[0]),
                pltpu.make_async_copy(wout_hbm.at[layer, e], wout_f32, wsem.at[1]))

    @pl.when(i == 0)
    def _():
        for cp in weight_copies(be_ref[0]):
            cp.start()

    @pl.when((i == 0) | (be_ref[i] != be_ref[jnp.maximum(i - 1, 0)]))
    def _():
        for cp in weight_copies(be_ref[i]):
            cp.wait()
        win_bf[...] = win_f32[...].astype(BF16)
        wout_bf[...] = wout_f32[...].astype(BF16)
        for cp in weight_copies(nxt_ref[i]):
            cp.start()

    def gather(tok_ref, p):
        for r in range(blk):
            pltpu.make_async_copy(h_hbm.at[pl.ds(tok_ref[0, r], 1)], xbufs[p].at[pl.ds(r, 1)],
                                  gsem.at[p]).start()

    def scatter(dst_ref, p):
        for r in range(blk):
            pltpu.make_async_copy(ybufs[p].at[pl.ds(r, 1)], y_hbm.at[pl.ds(dst_ref[0, r], 1)],
                                  ssem.at[p]).start()

    def wait_gather(p):
        pltpu.make_async_copy(h_hbm.at[pl.ds(0, blk)], xbufs[p], gsem.at[p]).wait()

    def wait_scatter(p):
        pltpu.make_async_copy(ybufs[p], y_hbm.at[pl.ds(0, blk)], ssem.at[p]).wait()

    @pl.when(i == 0)
    def _():
        ybuf1[...] = jnp.zeros_like(ybuf1)
        gather(tok_cur, 0)
        wait_gather(0)

    def step(p):
        x_lo, x_hi = _unpack_halves(xbufs[p][...])
        x = jnp.concatenate([x_lo, x_hi], axis=1).astype(BF16)
        gu = jnp.dot(x, win_bf[...], preferred_element_type=F32)
        g = gu[:, :ff]
        act = (g * _sigmoid(g) * gu[:, ff:]).astype(BF16)
        y = jnp.dot(act, wout_bf[...], preferred_element_type=F32)
        gather(tok_next, 1 - p)
        scatter(dst_prev, 1 - p)
        pl.when(i >= 1)(lambda: wait_scatter(p))
        ybufs[p][...] = _pack_halves(y)
        wait_gather(1 - p)

        @pl.when(i == n_blocks - 1)
        def _():
            scatter(dst_cur, p)
            wait_scatter(p)
            wait_scatter(1 - p)
            for cp in weight_copies(be_ref[i]):
                cp.wait()

    pl.when(i % 2 == 0)(lambda: step(0))
    pl.when(i % 2 == 1)(lambda: step(1))


def expert_ffn(h2, row_tok, row_dst, block_e, next_e, w_in, w_out, *, layer):
    dh = h2.shape[1]
    d = 2 * dh
    rows = row_tok.shape[0]
    blk = TILES["moe_block"]
    n_blocks = rows // blk
    ff = w_out.shape[2]
    tok3 = row_tok.reshape(n_blocks, 1, blk)
    spare = rows + jnp.arange(blk, dtype=jnp.int32)
    dst3 = jnp.concatenate([spare, row_dst]).reshape(n_blocks + 1, 1, blk)
    smem_blk = lambda f: pl.BlockSpec((None, 1, blk), f, memory_space=pltpu.SMEM)
    any_spec = pl.BlockSpec(memory_space=pl.ANY)
    kern = functools.partial(_expert_kernel, ff=ff, n_blocks=n_blocks, layer=layer)
    grid_spec = pltpu.PrefetchScalarGridSpec(
        num_scalar_prefetch=2,
        grid=(n_blocks,),
        in_specs=[
            smem_blk(lambda i, be, nx: (i, 0, 0)),
            smem_blk(lambda i, be, nx: (jnp.minimum(i + 1, n_blocks - 1), 0, 0)),
            smem_blk(lambda i, be, nx: (i, 0, 0)),
            smem_blk(lambda i, be, nx: (i + 1, 0, 0)),
            any_spec, any_spec, any_spec,
        ],
        out_specs=any_spec,
        scratch_shapes=[pltpu.VMEM((blk, dh), jnp.int32), pltpu.VMEM((blk, dh), jnp.int32),
                        pltpu.VMEM((blk, dh), jnp.int32), pltpu.VMEM((blk, dh), jnp.int32),
                        pltpu.VMEM((d, 2 * ff), F32), pltpu.VMEM((ff, d), F32),
                        pltpu.VMEM((d, 2 * ff), BF16), pltpu.VMEM((ff, d), BF16),
                        pltpu.SemaphoreType.DMA((2,)), pltpu.SemaphoreType.DMA((2,)),
                        pltpu.SemaphoreType.DMA((2,))],
    )
    return pl.pallas_call(
        kern,
        out_shape=jax.ShapeDtypeStruct((rows + blk, dh), jnp.int32),
        grid_spec=grid_spec,
        compiler_params=pltpu.CompilerParams(dimension_semantics=("arbitrary",),
                                             vmem_limit_bytes=TILES["vmem_expert"]),
        name="expert_ffn",
    )(block_e, next_e, tok3, tok3, dst3, dst3, h2, w_in, w_out)


def _combine_kernel(x_ref, y0_ref, y1_ref, w_ref, gate_ref, *rest, next_norm):
    w = w_ref[...]
    y0_lo, y0_hi = _unpack_halves(y0_ref[...])
    y1_lo, y1_hi = _unpack_halves(y1_ref[...])
    y = jnp.concatenate([w[:, 0:1] * y0_lo + w[:, 1:2] * y1_lo, w[:, 0:1] * y0_hi + w[:, 1:2] * y1_hi], axis=1)
    out = x_ref[...] + gate_ref[...] * y
    if next_norm:
        g_ref, shift_ref, scale_ref, o_ref, h_ref = rest
        h_ref[...] = _norm_mod(out, g_ref[...], shift_ref[...], scale_ref[...]).astype(h_ref.dtype)
    else:
        (o_ref,) = rest
    o_ref[...] = out


def moe_combine(xall, y_both, wts, mod3, *, layer, gate_idx, n_rows, geom, next_g=None):
    batch, seq, _ = geom
    d = xall.shape[1]
    tm = min(TILES["row"], seq)
    n_x = batch * seq // tm
    mrow = _mod_row_fn(n_x, seq // tm, batch, layer)
    in_specs = [
        pl.BlockSpec((tm, d), lambda m: (m, 0)),
        pl.BlockSpec((tm, d // 2), lambda m: (m, 0)),
        pl.BlockSpec((tm, d // 2), lambda m: (n_rows // tm + m, 0)),
        pl.BlockSpec((tm, LANES), lambda m: (m, 0)),
        pl.BlockSpec((None, 1, d), lambda m: (mrow(m), 0, gate_idx)),
    ]
    args = [xall, y_both, y_both, wts, mod3]
    out_shape = [jax.ShapeDtypeStruct((n_rows, d), F32)]
    out_specs = [pl.BlockSpec((tm, d), lambda m: (m, 0))]
    if next_g is not None:
        nrow = _mod_row_fn(n_x, seq // tm, batch, layer + 1)
        in_specs += [pl.BlockSpec((1, d), lambda m: (0, 0)),
                     pl.BlockSpec((None, 1, d), lambda m: (nrow(m), 0, 0)),
                     pl.BlockSpec((None, 1, d), lambda m: (nrow(m), 0, 1))]
        args += [next_g.reshape(1, d), mod3, mod3]
        out_shape.append(jax.ShapeDtypeStruct((n_rows, d), BF16))
        out_specs.append(pl.BlockSpec((tm, d), lambda m: (m, 0)))
    return pl.pallas_call(
        functools.partial(_combine_kernel, next_norm=next_g is not None),
        out_shape=tuple(out_shape),
        grid=(n_rows // tm,),
        in_specs=in_specs,
        out_specs=tuple(out_specs),
        compiler_params=_cparams(("parallel",)),
        name="moe_combine",
    )(*args)


def moe_dispatch(ids, counts, n_experts):
    blk = TILES["moe_block"]
    n = ids.shape[0]
    experts, rank = ids[:, :TOP_K], ids[:, TOP_K:2 * TOP_K]
    counts = counts[0, :n_experts].astype(jnp.int32)
    padded = (counts + blk - 1) // blk * blk
    pend = jnp.cumsum(padded)
    pstart = pend - padded
    eye = experts[:, :, None] == jnp.arange(n_experts, dtype=jnp.int32)[None, None, :]
    dest = jnp.sum(jnp.where(eye, pstart[None, None, :], 0), axis=-1) + rank
    n_blocks = -(-n * TOP_K // blk) + n_experts
    rows = n_blocks * blk
    slot = jnp.full((rows,), -1, jnp.int32).at[dest.T.reshape(-1)].set(jnp.arange(n * TOP_K, dtype=jnp.int32))
    pad = slot < 0
    pad_rank = jnp.cumsum(pad.astype(jnp.int32)) - 1
    row_dst = jnp.where(pad, n * TOP_K + pad_rank, slot)
    row_tok = jnp.where(pad, jnp.arange(rows, dtype=jnp.int32) % n, slot % n)
    block_start = jnp.arange(n_blocks, dtype=jnp.int32) * blk
    block_e = jnp.minimum(jnp.sum(block_start[:, None] >= pend[None, :], axis=1), n_experts - 1).astype(jnp.int32)
    e_ids = jnp.arange(n_experts, dtype=jnp.int32)
    present = jnp.any(block_e[:, None] == e_ids[None, :], axis=0)
    later = present[None, :] & (e_ids[None, :] > e_ids[:, None])
    next_present = jnp.where(jnp.any(later, axis=1), jnp.argmax(later, axis=1).astype(jnp.int32), e_ids)
    next_e = next_present[block_e]
    return row_tok, row_dst, block_e, next_e


def _rope_tables(seq, tm):
    inv = ROPE_THETA ** (-jnp.arange(0, AXIS_DIM, 2, dtype=F32) / AXIS_DIM)
    t = jnp.arange(seq)
    ang_r = (t // GRID_W).astype(F32)[:, None] * inv
    ang_c = (t % GRID_W).astype(F32)[:, None] * inv
    cos = jnp.concatenate([jnp.cos(ang_r), jnp.cos(ang_r), jnp.cos(ang_c), jnp.cos(ang_c)], axis=-1)
    sin = jnp.concatenate([-jnp.sin(ang_r), jnp.sin(ang_r), -jnp.sin(ang_c), jnp.sin(ang_c)], axis=-1)
    pad = LANES - ROPE_DIM
    cos = jnp.pad(cos, ((0, 0), (0, pad)))
    sin = jnp.pad(sin, ((0, 0), (0, pad)))
    ident_c = jnp.pad(jnp.ones((tm, ROPE_DIM), F32), ((0, 0), (0, pad)))
    return (jnp.concatenate([ident_c, cos], axis=0),
            jnp.concatenate([jnp.zeros((tm, LANES), F32), sin], axis=0))


def _partner():
    j = jnp.arange(ROPE_DIM)
    half = AXIS_DIM // 2
    return jnp.where((j % AXIS_DIM) < half, j + half, j - half)


def _pad_lanes(a):
    return jnp.pad(a, [(0, 0)] * (a.ndim - 1) + [(0, LANES - a.shape[-1])])


def _norm_gain_rows(g):
    rope = g[NOPE_DIM:]
    rows = jnp.stack([g[:NOPE_DIM], _pad_lanes(rope), _pad_lanes(rope[_partner()])])
    return jnp.pad(rows, ((0, 8 - rows.shape[0]), (0, 0))).astype(F32)


def _layer_weights(l, w_in, w_uq, w_ukv, q_lora, kv_lora):
    d = w_in.shape[1]
    heads = MLA_HEADS
    wl = w_in[l]
    gate_cols = 3 * d
    conv_a2 = (wl.shape[1] - gate_cols - (q_lora + kv_lora + ROPE_DIM)) // 5 * 2
    off_b = gate_cols + conv_a2
    off_c = off_b + q_lora + kv_lora + ROPE_DIM
    w_c = wl[:, off_c:]
    kpe = wl[:, off_b + q_lora + kv_lora:off_c]
    w_b = jnp.concatenate([wl[:, off_b:off_b + q_lora + kv_lora], _pad_lanes(kpe),
                           _pad_lanes(kpe[:, _partner()])], axis=1)
    wq = w_uq[l].reshape(q_lora, heads, QK_DIM)
    wq_n = wq[:, :, :NOPE_DIM].reshape(q_lora, heads * LANES)
    wq_r = _pad_lanes(wq[:, :, NOPE_DIM:]).reshape(q_lora, heads * LANES)
    wq_s = _pad_lanes(wq[:, :, NOPE_DIM:][:, :, _partner()]).reshape(q_lora, heads * LANES)
    wq_all = jnp.concatenate([wq_n, wq_r, wq_s], axis=1).astype(BF16)
    wkv = w_ukv[l].reshape(kv_lora, heads, NOPE_DIM + V_DIM)
    wkv_all = jnp.concatenate([wkv[:, :, :NOPE_DIM].reshape(kv_lora, heads * NOPE_DIM),
                               wkv[:, :, NOPE_DIM:].reshape(kv_lora, heads * V_DIM)], axis=1).astype(BF16)
    return off_b, w_c, w_b, wq_all, wkv_all


def _router_layout(w_router, router_bias):
    d, n_experts = w_router.shape
    per_group = n_experts // N_GROUPS
    wr = w_router.reshape(d, N_GROUPS, per_group).transpose(0, 2, 1)
    wr_pad = _pad_lanes(wr).reshape(d, per_group * LANES).astype(F32)
    bias = _pad_lanes(router_bias.reshape(N_GROUPS, per_group).T).astype(F32)
    bias = jnp.pad(bias, ((0, 8 - per_group), (0, 0)))
    return wr_pad, bias, per_group


def kernel(x, c, ctx, c_ctx, w_mod, b_mod, g_norm1, g_norm2, w_in, conv_a, ln_a_g, ln_a_b, w_a_out, g_cq, g_ckv, w_uq, w_ukv, g_q, g_k, w_b_out, conv_c, w_c_out, w_o, w_router, router_bias, w_e_in, w_e_out):
    batch, seq, d = x.shape
    n_ctx = ctx.shape[1]
    depth = w_mod.shape[0]
    geom = (batch, seq, n_ctx)
    t_x = batch * seq
    t_all = t_x + batch * n_ctx
    q_lora, kv_lora = g_cq.shape[1], g_ckv.shape[1]
    ch_a, ch_c = conv_a.shape[2], conv_c.shape[2]
    n_experts = w_router.shape[1]
    assert batch + 1 <= MOD_ROWS and t_x % n_ctx == 0 and ch_a == ch_c and (3 * d) % ch_a == 0

    stream = (x.reshape(t_x, d), ctx.reshape(batch * n_ctx, d))
    cvec = jnp.concatenate([c, c_ctx[None, :], jnp.zeros((MOD_ROWS - batch - 1, d), F32)], axis=0)
    mod3 = modulation(cvec, w_mod, b_mod).reshape(depth * MOD_ROWS, 1, N_MOD * d)

    cos_tab, sin_tab = _rope_tables(seq, min(TILES["mla"], seq, n_ctx))
    wr_pad, bias_pad, per_group = _router_layout(w_router, router_bias)
    col0 = 3 * d // ch_a

    for l in range(depth):
        last = l == depth - 1
        rows_main = t_x if last else t_all
        off_b, w_c, w_b, wq_all, wkv_all = _layer_weights(l, w_in, w_uq, w_ukv, q_lora, kv_lora)

        if l == 0:
            h = norm_modulate(stream, g_norm1[l], mod3, layer=l, shift_idx=0, n_rows=t_all, geom=geom)
        u = project(h, w_in[l], n_cols=off_b, n_rows=rows_main)
        uc = project(h, w_c, n_cols=w_c.shape[1], n_rows=rows_main)
        ub = project(h, w_b, n_cols=w_b.shape[1], n_rows=t_all)
        qt, k, vt = mla_project(ub, g_cq[l], g_ckv[l], wq_all, wkv_all, _norm_gain_rows(g_q[l]),
                                _norm_gain_rows(g_k[l]), cos_tab, sin_tab, geom=geom)
        ob_x = attention_latent(qt, k, vt, geom=geom)
        ob_c = ob_x if last else attention_context(qt, k, vt, geom=geom)
        ha = depthwise_branch(u, conv_a[l], col_a=col0, col_b=col0 + 1, col_gate=None,
                              ln=(ln_a_g[l], ln_a_b[l]), n_rows=rows_main, geom=geom)
        hc = depthwise_branch(uc, conv_c[l], col_a=1, col_b=2, col_gate=0,
                              ln=None, n_rows=rows_main, geom=geom)
        mrg = branch_merge(ha, ob_x, ob_c, hc, w_a_out, w_b_out, w_c_out, u, layer=l,
                           n_rows=rows_main, geom=geom)
        x1 = out_proj_residual(mrg, w_o, stream, mod3, layer=l, gate_idx=2, n_rows=rows_main, geom=geom)

        h2, ids, wts, counts = router(x1, g_norm2[l], mod3, wr_pad, bias_pad, layer=l, shift_idx=3,
                                      n_rows=rows_main, geom=geom, per_group=per_group)
        row_tok, row_dst, block_e, next_e = moe_dispatch(ids, counts, n_experts)
        y_both = expert_ffn(h2, row_tok, row_dst, block_e, next_e, w_e_in, w_e_out, layer=l)
        if last:
            (stream,) = moe_combine(x1, y_both, wts, mod3, layer=l, gate_idx=5, n_rows=rows_main, geom=geom)
        else:
            stream, h = moe_combine(x1, y_both, wts, mod3, layer=l, gate_idx=5, n_rows=rows_main,
                                    geom=geom, next_g=g_norm1[l + 1])

    return stream.reshape(batch, seq, d)
```

```python
import functools
import math

import jax
import jax.numpy as jnp
from jax import lax
from jax.experimental import pallas as pl
from jax.experimental.pallas import tpu as pltpu

F32 = jnp.float32
BF16 = jnp.bfloat16

GRID_W = 64
MLA_HEADS = 8
NOPE_DIM = 128
ROPE_DIM = 64
V_DIM = 128
QK_DIM = NOPE_DIM + ROPE_DIM
AXIS_DIM = ROPE_DIM // 2
ROPE_THETA = 10000.0
N_GROUPS = 8
TOP_K = 2
N_MOD = 6
EPS = 1e-6

LANES = 128
HEAD_SLOT = 2 * LANES
HALO_ROWS = 16
MOD_ROWS = 8

TILES = dict(
    row=512,
    proj_row=1024,
    col=1024,
    mod_col=1024,
    mla=256,
    tq=512,
    tq_sub=512,
    attn_keys=512,
    conv=256,
    conv_rows=64,
    router=512,
    moe_block=256,
    vmem=56 * 1024 * 1024,
    vmem_expert=60 * 1024 * 1024,
)


def _cparams(sem):
    return pltpu.CompilerParams(dimension_semantics=sem, vmem_limit_bytes=TILES["vmem"])


def _sigmoid(x):
    return 1.0 / (1.0 + jnp.exp(-x))


def _mod_kernel(c_ref, w_ref, b_ref, o_ref):
    c = c_ref[...]
    s = (c * _sigmoid(c)).astype(BF16)
    o_ref[...] = jnp.dot(s, w_ref[...].astype(BF16), preferred_element_type=F32) + b_ref[...]


def modulation(cvec, w_mod, b_mod):
    n_layers, d, n = w_mod.shape
    tn = min(TILES["mod_col"], n)
    return pl.pallas_call(
        _mod_kernel,
        out_shape=jax.ShapeDtypeStruct((n_layers, MOD_ROWS, n), F32),
        grid=(n_layers, n // tn),
        in_specs=[
            pl.BlockSpec((MOD_ROWS, d), lambda l, j: (0, 0)),
            pl.BlockSpec((None, d, tn), lambda l, j: (l, 0, j)),
            pl.BlockSpec((None, 1, tn), lambda l, j: (l, 0, j)),
        ],
        out_specs=pl.BlockSpec((None, MOD_ROWS, tn), lambda l, j: (l, 0, j)),
        compiler_params=_cparams(("parallel", "parallel")),
        name="modulation",
    )(cvec, w_mod, b_mod.reshape(n_layers, 1, n))


def _norm_mod(x, g, shift, scale):
    ms = jnp.mean(x * x, axis=-1, keepdims=True)
    return x * lax.rsqrt(ms + EPS) * g * (1.0 + scale) + shift


def _stream_specs(stream, tm, n_x_tiles, tn=None, grid_rank=1):
    arrays = stream if isinstance(stream, tuple) else (stream,)
    width = arrays[0].shape[1] if tn is None else tn

    def spec(row_fn):
        if grid_rank == 1:
            return pl.BlockSpec((tm, width), lambda m: (row_fn(m), 0))
        return pl.BlockSpec((tm, width), lambda j, m: (row_fn(m), j))

    if len(arrays) == 1:
        return [spec(lambda m: m)], list(arrays)
    return [spec(lambda m: jnp.minimum(m, n_x_tiles - 1)),
            spec(lambda m: jnp.maximum(m - n_x_tiles, 0))], list(arrays)


def _stream_tile(refs, m, n_x_tiles):
    if len(refs) == 1:
        return refs[0][...]
    return jnp.where(m < n_x_tiles, refs[0][...], refs[1][...])


def _norm_kernel(*refs, n_x_tiles):
    x_refs, (g_ref, shift_ref, scale_ref, h_ref) = refs[:-4], refs[-4:]
    x = _stream_tile(x_refs, pl.program_id(0), n_x_tiles)
    h_ref[...] = _norm_mod(x, g_ref[...], shift_ref[...], scale_ref[...]).astype(h_ref.dtype)


def _mod_row_fn(n_x_tiles, tiles_per_seq, batch, layer):
    def f(m):
        return layer * MOD_ROWS + jnp.where(m < n_x_tiles, m // tiles_per_seq, batch)
    return f


def norm_modulate(stream, g, mod3, *, layer, shift_idx, n_rows, geom):
    batch, seq, _ = geom
    d = g.shape[0]
    tm = min(TILES["row"], seq)
    n_x = batch * seq // tm
    mrow = _mod_row_fn(n_x, seq // tm, batch, layer)
    x_specs, x_args = _stream_specs(stream, tm, n_x)
    return pl.pallas_call(
        functools.partial(_norm_kernel, n_x_tiles=n_x),
        out_shape=jax.ShapeDtypeStruct((n_rows, d), BF16),
        grid=(n_rows // tm,),
        in_specs=x_specs + [
            pl.BlockSpec((1, d), lambda m: (0, 0)),
            pl.BlockSpec((None, 1, d), lambda m: (mrow(m), 0, shift_idx)),
            pl.BlockSpec((None, 1, d), lambda m: (mrow(m), 0, shift_idx + 1)),
        ],
        out_specs=pl.BlockSpec((tm, d), lambda m: (m, 0)),
        compiler_params=_cparams(("parallel",)),
        name="norm_modulate",
    )(*x_args, g.reshape(1, d), mod3, mod3)


def _proj_kernel(a_ref, w_ref, o_ref, w_bf):
    @pl.when(pl.program_id(1) == 0)
    def _():
        w_bf[...] = w_ref[...].astype(BF16)

    o_ref[...] = jnp.dot(a_ref[...], w_bf[...], preferred_element_type=F32).astype(o_ref.dtype)


def _w_spec(w, layer, tn, col0=0):
    if w.ndim == 3:
        return pl.BlockSpec((None, w.shape[1], tn), lambda j, m: (layer, 0, col0 + j))
    return pl.BlockSpec((w.shape[0], tn), lambda j, m: (0, col0 + j))


def project(a, w, *, n_cols, n_rows, row_block=0):
    k = a.shape[1]
    tm = min(TILES["proj_row"], n_rows)
    tn = min(TILES["col"], n_cols)
    if n_cols % tn:
        tn = n_cols
    return pl.pallas_call(
        _proj_kernel,
        out_shape=jax.ShapeDtypeStruct((n_rows, n_cols), BF16),
        grid=(n_cols // tn, n_rows // tm),
        in_specs=[pl.BlockSpec((tm, k), lambda j, m: (m, 0)),
                  pl.BlockSpec((k, tn), lambda j, m: (row_block, j))],
        out_specs=pl.BlockSpec((tm, tn), lambda j, m: (m, j)),
        scratch_shapes=[pltpu.VMEM((k, tn), BF16)],
        compiler_params=_cparams(("parallel", "arbitrary")),
        name="project",
    )(a, w)


def _mla_kernel(ub_ref, gcq_ref, gckv_ref, wq_ref, wkv_ref, gq_ref, gk_ref, cos_ref, sin_ref,
                qt_ref, k_ref, vt_ref, *, q_lora, kv_lora, heads):
    hn = heads * LANES

    def rms(xv, g):
        return (xv * lax.rsqrt(jnp.mean(xv * xv, axis=-1, keepdims=True) + EPS) * g).astype(BF16)

    cqn = rms(ub_ref[:, 0:q_lora].astype(F32), gcq_ref[...])
    ckvn = rms(ub_ref[:, q_lora:q_lora + kv_lora].astype(F32), gckv_ref[...])
    off = q_lora + kv_lora
    kpe = ub_ref[:, off:off + LANES].astype(F32)
    kpes = ub_ref[:, off + LANES:off + 2 * LANES].astype(F32)

    qall = jnp.dot(cqn, wq_ref[...], preferred_element_type=F32)
    kvall = jnp.dot(ckvn, wkv_ref[...], preferred_element_type=F32)

    cos = cos_ref[...]
    sin = sin_ref[...]
    gq_n, gq_r, gq_s = gq_ref[0:1, :], gq_ref[1:2, :], gq_ref[2:3, :]
    gk_n, gk_r, gk_s = gk_ref[0:1, :], gk_ref[1:2, :], gk_ref[2:3, :]
    q_cos, q_sin = gq_r * cos, gq_s * sin
    k_rope = kpe * (gk_r * cos) + kpes * (gk_s * sin)
    ssq_pe = jnp.sum(kpe * kpe, axis=-1, keepdims=True)
    q_scale = QK_DIM ** -0.5 * math.log2(math.e)

    for h in range(heads):
        lo = h * LANES
        qn = qall[:, lo:lo + LANES]
        qr = qall[:, hn + lo:hn + lo + LANES]
        qs = qall[:, 2 * hn + lo:2 * hn + lo + LANES]
        ssq = jnp.sum(qn * qn, axis=-1, keepdims=True) + jnp.sum(qr * qr, axis=-1, keepdims=True)
        inv = lax.rsqrt(ssq * (1.0 / QK_DIM) + EPS) * q_scale
        qt_ref[h * HEAD_SLOT:h * HEAD_SLOT + LANES, :] = (qn * gq_n * inv).T.astype(BF16)
        qt_ref[h * HEAD_SLOT + LANES:(h + 1) * HEAD_SLOT, :] = ((qr * q_cos + qs * q_sin) * inv).T.astype(BF16)

        kn = kvall[:, lo:lo + LANES]
        invk = lax.rsqrt((jnp.sum(kn * kn, axis=-1, keepdims=True) + ssq_pe) * (1.0 / QK_DIM) + EPS)
        k_ref[:, h * HEAD_SLOT:h * HEAD_SLOT + LANES] = (kn * gk_n * invk).astype(BF16)
        k_ref[:, h * HEAD_SLOT + LANES:(h + 1) * HEAD_SLOT] = (k_rope * invk).astype(BF16)

        vt_ref[h * V_DIM:(h + 1) * V_DIM, :] = kvall[:, hn + h * V_DIM:hn + (h + 1) * V_DIM].T.astype(BF16)


def mla_project(ub, g_cq, g_ckv, wq, wkv, gq3, gk3, cos_tab, sin_tab, *, geom):
    batch, seq, ctx = geom
    t_all = ub.shape[0]
    q_lora, kv_lora = g_cq.shape[0], g_ckv.shape[0]
    heads = MLA_HEADS
    tm = min(TILES["mla"], seq, ctx)
    n_x = batch * seq // tm
    per_seq = seq // tm

    def pos(m):
        return jnp.where(m < n_x, 1 + m % per_seq, 0)

    kern = functools.partial(_mla_kernel, q_lora=q_lora, kv_lora=kv_lora, heads=heads)
    return pl.pallas_call(
        kern,
        out_shape=(
            jax.ShapeDtypeStruct((heads * HEAD_SLOT, t_all), BF16),
            jax.ShapeDtypeStruct((t_all, heads * HEAD_SLOT), BF16),
            jax.ShapeDtypeStruct((heads * V_DIM, t_all), BF16),
        ),
        grid=(t_all // tm,),
        in_specs=[
            pl.BlockSpec((tm, ub.shape[1]), lambda m: (m, 0)),
            pl.BlockSpec((1, q_lora), lambda m: (0, 0)),
            pl.BlockSpec((1, kv_lora), lambda m: (0, 0)),
            pl.BlockSpec(wq.shape, lambda m: (0, 0)),
            pl.BlockSpec(wkv.shape, lambda m: (0, 0)),
            pl.BlockSpec((8, LANES), lambda m: (0, 0)),
            pl.BlockSpec((8, LANES), lambda m: (0, 0)),
            pl.BlockSpec((tm, LANES), lambda m: (pos(m), 0)),
            pl.BlockSpec((tm, LANES), lambda m: (pos(m), 0)),
        ],
        out_specs=(
            pl.BlockSpec((heads * HEAD_SLOT, tm), lambda m: (0, m)),
            pl.BlockSpec((tm, heads * HEAD_SLOT), lambda m: (m, 0)),
            pl.BlockSpec((heads * V_DIM, tm), lambda m: (0, m)),
        ),
        compiler_params=_cparams(("parallel",)),
        name="mla_project",
    )(ub, g_cq.reshape(1, -1), g_ckv.reshape(1, -1), wq, wkv, gq3, gk3, cos_tab, sin_tab)


def _attn_kernel(*refs, sub, latent_keys):
    if latent_keys:
        qt_ref, kx_ref, kc_ref, vxt_ref, vct_ref, o_ref = refs
    else:
        qt_ref, kc_ref, vct_ref, o_ref = refs
    for j in range(qt_ref.shape[1] // sub):
        qt = qt_ref[:, j * sub:(j + 1) * sub]
        sc = jnp.dot(kc_ref[...], qt, preferred_element_type=F32)
        m = jnp.max(sc, axis=0, keepdims=True)
        if latent_keys:
            sx = jnp.dot(kx_ref[...], qt, preferred_element_type=F32)
            m = jnp.maximum(m, jnp.max(sx, axis=0, keepdims=True))
        pc = jnp.exp2(sc - m)
        denom = jnp.sum(pc, axis=0, keepdims=True)
        ot = jnp.dot(vct_ref[...], pc.astype(BF16), preferred_element_type=F32)
        if latent_keys:
            px = jnp.exp2(sx - m)
            denom = denom + jnp.sum(px, axis=0, keepdims=True)
            ot = ot + jnp.dot(vxt_ref[...], px.astype(BF16), preferred_element_type=F32)
        o_ref[j * sub:(j + 1) * sub, :] = (ot * (1.0 / denom)).T.astype(o_ref.dtype)


def _attn_pipelined_kernel(qt_ref, kx_ref, kc_ref, vxt_ref, vct_ref, o_ref, s0_ref, s1_ref, m0_ref, m1_ref,
                           *, n_q, chunk):
    i = pl.program_id(2)
    seq, ctx = kx_ref.shape[0], kc_ref.shape[0]
    tq = qt_ref.shape[1]
    chunks = [(kc_ref, vct_ref, 0, ctx, 0)]
    chunks += [(kx_ref, vxt_ref, c0, chunk, ctx + c0) for c0 in range(0, seq, chunk)]
    slots = ((s0_ref, m0_ref), (s1_ref, m1_ref))

    def run(new, old):
        if new is not None:
            s_new, m_new = slots[new]
            qt = qt_ref[...]
            m = jnp.full((1, tq), -jnp.inf, F32)
        if old is not None:
            s_old, m_old_ref = slots[old]
            m_old = m_old_ref[...]
            acc = jnp.zeros((V_DIM, tq), F32)
            den = jnp.zeros((1, tq), F32)
        for kref, vtref, c0, size, row0 in chunks:
            if new is not None:
                s = jnp.dot(kref[c0:c0 + size, :], qt, preferred_element_type=F32)
                s_new[row0:row0 + size, :] = s
                m = jnp.maximum(m, jnp.max(s, axis=0, keepdims=True))
            if old is not None:
                p = jnp.exp2(s_old[row0:row0 + size, :] - m_old)
                den = den + jnp.sum(p, axis=0, keepdims=True)
                acc = acc + jnp.dot(vtref[:, c0:c0 + size], p.astype(BF16), preferred_element_type=F32)
        if new is not None:
            m_new[...] = m
        if old is not None:
            o_ref[...] = (acc * (1.0 / den)).T.astype(o_ref.dtype)

    pl.when(i == 0)(lambda: run(0, None))
    pl.when((i > 0) & (i < n_q) & (i % 2 == 1))(lambda: run(1, 0))
    pl.when((i > 0) & (i < n_q) & (i % 2 == 0))(lambda: run(0, 1))
    pl.when(i == n_q)(lambda: run(None, (n_q - 1) % 2))


def attention_latent(qt, k, vt, *, geom):
    batch, seq, ctx = geom
    heads = MLA_HEADS
    tq = min(TILES["tq"], seq)
    n_q = seq // tq
    ctx_blk0 = batch * seq // ctx
    kern = functools.partial(_attn_pipelined_kernel, n_q=n_q, chunk=min(TILES["attn_keys"], seq))
    return pl.pallas_call(
        kern,
        out_shape=jax.ShapeDtypeStruct((batch * seq, heads * V_DIM), BF16),
        grid=(batch, heads, n_q + 1),
        in_specs=[
            pl.BlockSpec((HEAD_SLOT, tq), lambda b, h, i: (h, b * n_q + jnp.minimum(i, n_q - 1))),
            pl.BlockSpec((seq, HEAD_SLOT), lambda b, h, i: (b, h)),
            pl.BlockSpec((ctx, HEAD_SLOT), lambda b, h, i: (ctx_blk0 + b, h)),
            pl.BlockSpec((V_DIM, seq), lambda b, h, i: (h, b)),
            pl.BlockSpec((V_DIM, ctx), lambda b, h, i: (h, ctx_blk0 + b)),
        ],
        out_specs=pl.BlockSpec((tq, V_DIM), lambda b, h, i: (b * n_q + jnp.maximum(i - 1, 0), h)),
        scratch_shapes=[pltpu.VMEM((ctx + seq, tq), F32), pltpu.VMEM((ctx + seq, tq), F32),
                        pltpu.VMEM((1, tq), F32), pltpu.VMEM((1, tq), F32)],
        compiler_params=_cparams(("parallel", "parallel", "arbitrary")),
        name="attention_latent",
    )(qt, k, k, vt, vt)


def attention_context(qt, k, vt, *, geom):
    batch, seq, ctx = geom
    heads = MLA_HEADS
    ctx_blk0 = batch * seq // ctx
    kern = functools.partial(_attn_kernel, sub=min(TILES["tq_sub"], ctx), latent_keys=False)
    return pl.pallas_call(
        kern,
        out_shape=jax.ShapeDtypeStruct((batch * ctx, heads * V_DIM), BF16),
        grid=(batch, heads),
        in_specs=[
            pl.BlockSpec((HEAD_SLOT, ctx), lambda b, h: (h, ctx_blk0 + b)),
            pl.BlockSpec((ctx, HEAD_SLOT), lambda b, h: (ctx_blk0 + b, h)),
            pl.BlockSpec((V_DIM, ctx), lambda b, h: (h, ctx_blk0 + b)),
        ],
        out_specs=pl.BlockSpec((ctx, V_DIM), lambda b, h: (b, h)),
        compiler_params=_cparams(("parallel", "parallel")),
        name="attention_context",
    )(qt, k, vt)


def _conv_kernel(*refs, width, conformer, tp, n_x_tiles, x_per_seq, c_per_seq):
    if conformer:
        (ap, bp, am, bm, an, bn, w_ref, lng_ref, lnb_ref, o_ref, buf, ybuf) = refs
        gate_ref = None
    else:
        (gate_ref, ap, bp, am, bm, an, bn, w_ref, o_ref, buf, ybuf) = refs
    nchunk = buf.shape[0]
    i = pl.program_id(0)
    in_x = i < n_x_tiles
    pos = jnp.where(in_x, i % x_per_seq, (i - n_x_tiles) % c_per_seq)
    per_seq = jnp.where(in_x, x_per_seq, c_per_seq)
    keep_prev = (pos > 0).astype(F32)
    keep_next = (pos < per_seq - 1).astype(F32)

    def pre(a, b):
        a = a.astype(F32)
        b = b.astype(F32)
        return a * _sigmoid(b) if conformer else a * b

    for c in range(nchunk):
        sl = slice(c * LANES, (c + 1) * LANES)
        buf[c, 0:HALO_ROWS, :] = pre(ap[:, sl], bp[:, sl]) * keep_prev
        buf[c, HALO_ROWS:HALO_ROWS + tp, :] = pre(am[:, sl], bm[:, sl])
        buf[c, HALO_ROWS + tp:2 * HALO_ROWS + tp, :] = pre(an[:, sl], bn[:, sl]) * keep_next

    rc = min(TILES["conv_rows"], tp)
    base = HALO_ROWS - width // 2

    def chunk(c, carry):
        wts = w_ref[c]
        for r0 in range(0, tp, rc):
            acc = jnp.zeros((rc, LANES), F32)
            for k in range(width):
                acc = acc + wts[k:k + 1, :] * buf[c, pl.ds(base + r0 + k, rc), :]
            ybuf[c, pl.ds(r0, rc), :] = acc
        return carry

    lax.fori_loop(0, nchunk, chunk, 0)

    if conformer:
        n_ch = nchunk * LANES
        tot = jnp.zeros((tp, 1), F32)
        for c in range(nchunk):
            tot = tot + jnp.sum(ybuf[c], axis=-1, keepdims=True)
        mu = tot * (1.0 / n_ch)
        tot2 = jnp.zeros((tp, 1), F32)
        for c in range(nchunk):
            dlt = ybuf[c] - mu
            tot2 = tot2 + jnp.sum(dlt * dlt, axis=-1, keepdims=True)
        inv = lax.rsqrt(tot2 * (1.0 / n_ch) + EPS)
        for c in range(nchunk):
            sl = slice(c * LANES, (c + 1) * LANES)
            y = (ybuf[c] - mu) * inv * lng_ref[:, sl] + lnb_ref[:, sl]
            o_ref[:, sl] = (y * _sigmoid(y)).astype(o_ref.dtype)
    else:
        for c in range(nchunk):
            sl = slice(c * LANES, (c + 1) * LANES)
            o_ref[:, sl] = (gate_ref[:, sl].astype(F32) * ybuf[c]).astype(o_ref.dtype)


def depthwise_branch(u, conv_w, *, col_a, col_b, col_gate, ln, n_rows, geom):
    batch, seq, ctx = geom
    width, ch = conv_w.shape
    conformer = ln is not None
    tp = min(TILES["conv"], seq, ctx)
    nchunk = ch // LANES
    hb = tp // HALO_ROWS
    last_halo = u.shape[0] // HALO_ROWS - 1
    wpad = -(-width // 8) * 8
    w3 = jnp.pad(conv_w.astype(F32), ((0, wpad - width), (0, 0))).reshape(wpad, nchunk, LANES).transpose(1, 0, 2)

    def prev_map(col):
        return lambda i: (jnp.maximum(i * hb - 1, 0), col)

    def main_map(col):
        return lambda i: (i, col)

    def next_map(col):
        return lambda i: (jnp.minimum((i + 1) * hb, last_halo), col)

    in_specs, args = [], []
    if not conformer:
        in_specs.append(pl.BlockSpec((tp, ch), main_map(col_gate)))
        args.append(u)
    for rows, mk in ((HALO_ROWS, prev_map), (tp, main_map), (HALO_ROWS, next_map)):
        for col in (col_a, col_b):
            in_specs.append(pl.BlockSpec((rows, ch), mk(col)))
            args.append(u)
    in_specs.append(pl.BlockSpec((nchunk, wpad, LANES), lambda i: (0, 0, 0)))
    args.append(w3)
    if conformer:
        for p in ln:
            in_specs.append(pl.BlockSpec((1, ch), lambda i: (0, 0)))
            args.append(p.reshape(1, ch).astype(F32))

    kern = functools.partial(
        _conv_kernel, width=width, conformer=conformer, tp=tp, n_x_tiles=batch * seq // tp,
        x_per_seq=seq // tp, c_per_seq=ctx // tp)
    return pl.pallas_call(
        kern,
        out_shape=jax.ShapeDtypeStruct((n_rows, ch), BF16),
        grid=(n_rows // tp,),
        in_specs=in_specs,
        out_specs=pl.BlockSpec((tp, ch), lambda i: (i, 0)),
        scratch_shapes=[pltpu.VMEM((nchunk, tp + 2 * HALO_ROWS, LANES), F32),
                        pltpu.VMEM((nchunk, tp, LANES), F32)],
        compiler_params=_cparams(("parallel",)),
        name="conformer_conv" if conformer else "short_conv",
    )(*args)


def _merge_kernel(ha_ref, obx_ref, obc_ref, hc_ref, wa_ref, wb_ref, wc_ref, ga_ref, gb_ref, gc_ref, o_ref,
                  wa_bf, wb_bf, wc_bf, *, n_x_tiles):
    m_idx = pl.program_id(1)

    @pl.when(m_idx == 0)
    def _():
        wa_bf[...] = wa_ref[...].astype(BF16)
        wb_bf[...] = wb_ref[...].astype(BF16)
        wc_bf[...] = wc_ref[...].astype(BF16)

    ob = jnp.where(m_idx < n_x_tiles, obx_ref[...], obc_ref[...])
    ya = jnp.dot(ha_ref[...], wa_bf[...], preferred_element_type=F32)
    yb = jnp.dot(ob, wb_bf[...], preferred_element_type=F32)
    yc = jnp.dot(hc_ref[...], wc_bf[...], preferred_element_type=F32)
    m = (_sigmoid(ga_ref[...].astype(F32)) * ya + _sigmoid(gb_ref[...].astype(F32)) * yb
         + _sigmoid(gc_ref[...].astype(F32)) * yc)
    o_ref[...] = m.astype(o_ref.dtype)


def branch_merge(ha, ob_x, ob_c, hc, wa, wb, wc, u, *, layer, n_rows, geom):
    batch, seq, _ = geom
    d = wa.shape[2]
    tm = min(TILES["row"], seq)
    tn = min(TILES["col"], d)
    ncol = d // tn
    n_x = batch * seq // tm
    row_spec = lambda a: pl.BlockSpec((tm, a.shape[1]), lambda j, m: (m, 0))
    gate_spec = lambda k: pl.BlockSpec((tm, tn), lambda j, m: (m, k * ncol + j))
    kern = functools.partial(_merge_kernel, n_x_tiles=n_x)
    return pl.pallas_call(
        kern,
        out_shape=jax.ShapeDtypeStruct((n_rows, d), BF16),
        grid=(ncol, n_rows // tm),
        in_specs=[row_spec(ha),
                  pl.BlockSpec((tm, ob_x.shape[1]), lambda j, m: (jnp.minimum(m, n_x - 1), 0)),
                  pl.BlockSpec((tm, ob_c.shape[1]), lambda j, m: (jnp.maximum(m - n_x, 0), 0)),
                  row_spec(hc), _w_spec(wa, layer, tn), _w_spec(wb, layer, tn), _w_spec(wc, layer, tn),
                  gate_spec(0), gate_spec(1), gate_spec(2)],
        out_specs=pl.BlockSpec((tm, tn), lambda j, m: (m, j)),
        scratch_shapes=[pltpu.VMEM((w.shape[1], tn), BF16) for w in (wa, wb, wc)],
        compiler_params=_cparams(("parallel", "arbitrary")),
        name="branch_merge",
    )(ha, ob_x, ob_c, hc, wa, wb, wc, u, u, u)


def _outproj_kernel(*refs, n_x_tiles):
    (m_ref, w_ref, gate_ref), x_refs, (o_ref, w_bf) = refs[:3], refs[3:-2], refs[-2:]
    m_idx = pl.program_id(1)

    @pl.when(m_idx == 0)
    def _():
        w_bf[...] = w_ref[...].astype(BF16)

    y = jnp.dot(m_ref[...], w_bf[...], preferred_element_type=F32)
    o_ref[...] = _stream_tile(x_refs, m_idx, n_x_tiles) + gate_ref[...] * y


def out_proj_residual(mrg, w_o, stream, mod3, *, layer, gate_idx, n_rows, geom):
    batch, seq, _ = geom
    d = w_o.shape[2]
    tm = min(TILES["row"], seq)
    tn = min(TILES["col"], d)
    ncol = d // tn
    n_x = batch * seq // tm
    mrow = _mod_row_fn(n_x, seq // tm, batch, layer)
    x_specs, x_args = _stream_specs(stream, tm, n_x, tn=tn, grid_rank=2)
    return pl.pallas_call(
        functools.partial(_outproj_kernel, n_x_tiles=n_x),
        out_shape=jax.ShapeDtypeStruct((n_rows, d), F32),
        grid=(ncol, n_rows // tm),
        in_specs=[
            pl.BlockSpec((tm, mrg.shape[1]), lambda j, m: (m, 0)),
            _w_spec(w_o, layer, tn),
            pl.BlockSpec((None, 1, tn), lambda j, m: (mrow(m), 0, gate_idx * ncol + j)),
        ] + x_specs,
        out_specs=pl.BlockSpec((tm, tn), lambda j, m: (m, j)),
        scratch_shapes=[pltpu.VMEM((w_o.shape[1], tn), BF16)],
        compiler_params=_cparams(("parallel", "arbitrary")),
        name="out_proj_residual",
    )(mrg, w_o, mod3, *x_args)


def _pack_halves(v):
    n = v.shape[1] // 2
    lo = lax.bitcast_convert_type(v[:, :n].astype(BF16).astype(F32), jnp.int32)
    hi = lax.bitcast_convert_type(v[:, n:].astype(BF16).astype(F32), jnp.int32)
    return lax.shift_right_logical(lo, jnp.int32(16)) | (hi & jnp.int32(-65536))


def _unpack_halves(p):
    lo = lax.bitcast_convert_type(lax.shift_left(p, jnp.int32(16)), F32)
    hi = lax.bitcast_convert_type(p & jnp.int32(-65536), F32)
    return lo, hi


def _router_kernel(x_ref, g_ref, shift_ref, scale_ref, wr_ref, bias_ref, h_ref, ids_ref, wts_ref, cnt_ref,
                   *, per_group):
    h = _norm_mod(x_ref[...], g_ref[...], shift_ref[...], scale_ref[...])
    h_ref[...] = _pack_halves(h)
    h_hi = h.astype(BF16)
    h_lo = (h - h_hi.astype(F32)).astype(BF16)
    w = wr_ref[...]
    w_hi = w.astype(BF16)
    w_lo = (w - w_hi.astype(F32)).astype(BF16)
    logits = (jnp.dot(h_hi, w_hi, preferred_element_type=F32)
              + jnp.dot(h_hi, w_lo, preferred_element_type=F32)
              + jnp.dot(h_lo, w_hi, preferred_element_type=F32))
    lane_i = lax.broadcasted_iota(jnp.int32, (h.shape[0], LANES), 1)
    lane = lane_i.astype(F32)
    valid = lane_i < N_GROUPS
    neg = jnp.float32(-jnp.inf)
    score, sel = [], []
    for j in range(per_group):
        s = _sigmoid(logits[:, j * LANES:(j + 1) * LANES])
        score.append(s)
        sel.append(jnp.where(valid, s + bias_ref[j:j + 1, :], neg))
    assert per_group == 4
    hi1, lo1 = jnp.maximum(sel[0], sel[1]), jnp.minimum(sel[0], sel[1])
    hi2, lo2 = jnp.maximum(sel[2], sel[3]), jnp.minimum(sel[2], sel[3])
    top1 = jnp.maximum(hi1, hi2)
    top2 = jnp.maximum(jnp.minimum(hi1, hi2), jnp.maximum(lo1, lo2))
    gscore = jnp.where(valid, top1 + top2, neg)
    gmax = jnp.max(gscore, axis=-1, keepdims=True)
    grp = jnp.min(jnp.where(gscore == gmax, lane, float(LANES)), axis=-1, keepdims=True)
    pick = lane == grp
    zero = jnp.float32(0.0)
    sel_g = [jnp.sum(jnp.where(pick, s, zero), axis=-1, keepdims=True) for s in sel]
    sc_g = [jnp.sum(jnp.where(pick, s, zero), axis=-1, keepdims=True) for s in score]
    best, best_i, best_s = sel_g[0], jnp.zeros_like(grp), sc_g[0]
    for j in range(1, per_group):
        better = sel_g[j] > best
        best = jnp.where(better, sel_g[j], best)
        best_i = jnp.where(better, float(j), best_i)
        best_s = jnp.where(better, sc_g[j], best_s)
    second, second_i, second_s = jnp.full_like(best, neg), jnp.zeros_like(grp), jnp.zeros_like(best)
    for j in range(per_group):
        better = (sel_g[j] > second) & (best_i != float(j))
        second = jnp.where(better, sel_g[j], second)
        second_i = jnp.where(better, float(j), second_i)
        second_s = jnp.where(better, sc_g[j], second_s)
    tot = best_s + second_s
    e0 = grp * per_group + best_i
    e1 = grp * per_group + second_i
    @pl.when(pl.program_id(0) == 0)
    def _():
        cnt_ref[...] = jnp.zeros_like(cnt_ref)

    oh0 = lane == e0
    oh1 = lane == e1
    both = jnp.where(oh0 | oh1, jnp.float32(1.0), zero)
    tm = h.shape[0]
    earlier = (lax.broadcasted_iota(jnp.int32, (tm, tm), 1)
               < lax.broadcasted_iota(jnp.int32, (tm, tm), 0)).astype(BF16)
    before = jnp.dot(earlier, both.astype(BF16), preferred_element_type=F32) + cnt_ref[0:1, :]
    r0 = jnp.sum(jnp.where(oh0, before, zero), axis=-1, keepdims=True)
    r1 = jnp.sum(jnp.where(oh1, before, zero), axis=-1, keepdims=True)
    cnt_ref[...] = cnt_ref[...] + jnp.sum(both, axis=0, keepdims=True)
    ids = jnp.where(lane_i == 0, e0, jnp.where(lane_i == 1, e1, jnp.where(lane_i == 2, r0,
                    jnp.where(lane_i == 3, r1, zero))))
    ids_ref[...] = ids.astype(jnp.int32)
    wts_ref[...] = jnp.where(lane_i == 0, best_s / tot, jnp.where(lane_i == 1, second_s / tot, zero))


def router(xall, g, mod3, wr_pad, bias_pad, *, layer, shift_idx, n_rows, geom, per_group):
    batch, seq, _ = geom
    d = xall.shape[1]
    tm = min(TILES["router"], seq)
    mrow = _mod_row_fn(batch * seq // tm, seq // tm, batch, layer)
    kern = functools.partial(_router_kernel, per_group=per_group)
    return pl.pallas_call(
        kern,
        out_shape=(jax.ShapeDtypeStruct((n_rows, d // 2), jnp.int32),
                   jax.ShapeDtypeStruct((n_rows, LANES), jnp.int32),
                   jax.ShapeDtypeStruct((n_rows, LANES), F32),
                   jax.ShapeDtypeStruct((8, LANES), F32)),
        grid=(n_rows // tm,),
        in_specs=[
            pl.BlockSpec((tm, d), lambda m: (m, 0)),
            pl.BlockSpec((1, d), lambda m: (0, 0)),
            pl.BlockSpec((None, 1, d), lambda m: (mrow(m), 0, shift_idx)),
            pl.BlockSpec((None, 1, d), lambda m: (mrow(m), 0, shift_idx + 1)),
            pl.BlockSpec(wr_pad.shape, lambda m: (0, 0)),
            pl.BlockSpec(bias_pad.shape, lambda m: (0, 0)),
        ],
        out_specs=(pl.BlockSpec((tm, d // 2), lambda m: (m, 0)),
                   pl.BlockSpec((tm, LANES), lambda m: (m, 0)),
                   pl.BlockSpec((tm, LANES), lambda m: (m, 0)),
                   pl.BlockSpec((8, LANES), lambda m: (0, 0))),
        compiler_params=_cparams(("arbitrary",)),
        name="router",
    )(xall, g.reshape(1, d), mod3, mod3, wr_pad, bias_pad)


def _expert_kernel(be_ref, nxt_ref, tok_cur, tok_next, dst_prev, dst_cur, h_hbm, win_hbm, wout_hbm, y_hbm,
                   xbuf0, xbuf1, ybuf0, ybuf1, win_f32, wout_f32, win_bf, wout_bf, gsem, ssem, wsem,
                   *, ff, n_blocks, layer):
    i = pl.program_id(0)
    blk = xbuf0.shape[0]
    xbufs, ybufs = (xbuf0, xbuf1), (ybuf0, ybuf1)

    def weight_copies(e):
        return (pltpu.make_async_copy(win_hbm.at[layer, e], win_f32, wsem.at[0]),
                pltpu.make_async_copy(wout_hbm.at[layer, e], wout_f32, wsem.at[1]))

    @pl.when(i == 0)
    def _():
        for cp in weight_copies(be_ref[0]):
            cp.start(priority=1)

    @pl.when((i == 0) | (be_ref[i] != be_ref[jnp.maximum(i - 1, 0)]))
    def _():
        for cp in weight_copies(be_ref[i]):
            cp.wait()
        win_bf[...] = win_f32[...].astype(BF16)
        wout_bf[...] = wout_f32[...].astype(BF16)
        for cp in weight_copies(nxt_ref[i]):
            cp.start(priority=1)

    def gather(tok_ref, p):
        for r in range(blk):
            pltpu.make_async_copy(h_hbm.at[pl.ds(tok_ref[0, r], 1)], xbufs[p].at[pl.ds(r, 1)],
                                  gsem.at[p]).start()

    def scatter(dst_ref, p):
        for r in range(blk):
            pltpu.make_async_copy(ybufs[p].at[pl.ds(r, 1)], y_hbm.at[pl.ds(dst_ref[0, r], 1)],
                                  ssem.at[p]).start()

    def wait_gather(p):
        pltpu.make_async_copy(h_hbm.at[pl.ds(0, blk)], xbufs[p], gsem.at[p]).wait()

    def wait_scatter(p):
        pltpu.make_async_copy(ybufs[p], y_hbm.at[pl.ds(0, blk)], ssem.at[p]).wait()

    @pl.when(i == 0)
    def _():
        ybuf1[...] = jnp.zeros_like(ybuf1)
        gather(tok_cur, 0)
        wait_gather(0)

    def step(p):
        x_lo, x_hi = _unpack_halves(xbufs[p][...])
        x = jnp.concatenate([x_lo, x_hi], axis=1).astype(BF16)
        gu = jnp.dot(x, win_bf[...], preferred_element_type=F32)
        g = gu[:, :ff]
        act = (g * _sigmoid(g) * gu[:, ff:]).astype(BF16)
        y = jnp.dot(act, wout_bf[...], preferred_element_type=F32)
        gather(tok_next, 1 - p)
        scatter(dst_prev, 1 - p)
        pl.when(i >= 1)(lambda: wait_scatter(p))
        ybufs[p][...] = _pack_halves(y)
        wait_gather(1 - p)

        @pl.when(i == n_blocks - 1)
        def _():
            scatter(dst_cur, p)
            wait_scatter(p)
            wait_scatter(1 - p)
            for cp in weight_copies(be_ref[i]):
                cp.wait()

    pl.when(i % 2 == 0)(lambda: step(0))
    pl.when(i % 2 == 1)(lambda: step(1))


def expert_ffn(h2, row_tok, row_dst, block_e, next_e, w_in, w_out, *, layer):
    dh = h2.shape[1]
    d = 2 * dh
    rows = row_tok.shape[0]
    blk = TILES["moe_block"]
    n_blocks = rows // blk
    ff = w_out.shape[2]
    tok3 = row_tok.reshape(n_blocks, 1, blk)
    spare = rows + jnp.arange(blk, dtype=jnp.int32)
    dst3 = jnp.concatenate([spare, row_dst]).reshape(n_blocks + 1, 1, blk)
    smem_blk = lambda f: pl.BlockSpec((None, 1, blk), f, memory_space=pltpu.SMEM)
    any_spec = pl.BlockSpec(memory_space=pl.ANY)
    kern = functools.partial(_expert_kernel, ff=ff, n_blocks=n_blocks, layer=layer)
    grid_spec = pltpu.PrefetchScalarGridSpec(
        num_scalar_prefetch=2,
        grid=(n_blocks,),
        in_specs=[
            smem_blk(lambda i, be, nx: (i, 0, 0)),
            smem_blk(lambda i, be, nx: (jnp.minimum(i + 1, n_blocks - 1), 0, 0)),
            smem_blk(lambda i, be, nx: (i, 0, 0)),
            smem_blk(lambda i, be, nx: (i + 1, 0, 0)),
            any_spec, any_spec, any_spec,
        ],
        out_specs=any_spec,
        scratch_shapes=[pltpu.VMEM((blk, dh), jnp.int32), pltpu.VMEM((blk, dh), jnp.int32),
                        pltpu.VMEM((blk, dh), jnp.int32), pltpu.VMEM((blk, dh), jnp.int32),
                        pltpu.VMEM((d, 2 * ff), F32), pltpu.VMEM((ff, d), F32),
                        pltpu.VMEM((d, 2 * ff), BF16), pltpu.VMEM((ff, d), BF16),
                        pltpu.SemaphoreType.DMA((2,)), pltpu.SemaphoreType.DMA((2,)),
                        pltpu.SemaphoreType.DMA((2,))],
    )
    return pl.pallas_call(
        kern,
        out_shape=jax.ShapeDtypeStruct((rows + blk, dh), jnp.int32),
        grid_spec=grid_spec,
        compiler_params=pltpu.CompilerParams(dimension_semantics=("arbitrary",),
                                             vmem_limit_bytes=TILES["vmem_expert"]),
        name="expert_ffn",
    )(block_e, next_e, tok3, tok3, dst3, dst3, h2, w_in, w_out)


def _combine_kernel(x_ref, y0_ref, y1_ref, w_ref, gate_ref, *rest, next_norm):
    w = w_ref[...]
    y0_lo, y0_hi = _unpack_halves(y0_ref[...])
    y1_lo, y1_hi = _unpack_halves(y1_ref[...])
    y = jnp.concatenate([w[:, 0:1] * y0_lo + w[:, 1:2] * y1_lo, w[:, 0:1] * y0_hi + w[:, 1:2] * y1_hi], axis=1)
    out = x_ref[...] + gate_ref[...] * y
    if next_norm:
        g_ref, shift_ref, scale_ref, o_ref, h_ref = rest
        h_ref[...] = _norm_mod(out, g_ref[...], shift_ref[...], scale_ref[...]).astype(h_ref.dtype)
    else:
        (o_ref,) = rest
    o_ref[...] = out


def moe_combine(xall, y_both, wts, mod3, *, layer, gate_idx, n_rows, geom, next_g=None):
    batch, seq, _ = geom
    d = xall.shape[1]
    tm = min(TILES["row"], seq)
    n_x = batch * seq // tm
    mrow = _mod_row_fn(n_x, seq // tm, batch, layer)
    in_specs = [
        pl.BlockSpec((tm, d), lambda m: (m, 0)),
        pl.BlockSpec((tm, d // 2), lambda m: (m, 0)),
        pl.BlockSpec((tm, d // 2), lambda m: (n_rows // tm + m, 0)),
        pl.BlockSpec((tm, LANES), lambda m: (m, 0)),
        pl.BlockSpec((None, 1, d), lambda m: (mrow(m), 0, gate_idx)),
    ]
    args = [xall, y_both, y_both, wts, mod3]
    out_shape = [jax.ShapeDtypeStruct((n_rows, d), F32)]
    out_specs = [pl.BlockSpec((tm, d), lambda m: (m, 0))]
    if next_g is not None:
        nrow = _mod_row_fn(n_x, seq // tm, batch, layer + 1)
        in_specs += [pl.BlockSpec((1, d), lambda m: (0, 0)),
                     pl.BlockSpec((None, 1, d), lambda m: (nrow(m), 0, 0)),
                     pl.BlockSpec((None, 1, d), lambda m: (nrow(m), 0, 1))]
        args += [next_g.reshape(1, d), mod3, mod3]
        out_shape.append(jax.ShapeDtypeStruct((n_rows, d), BF16))
        out_specs.append(pl.BlockSpec((tm, d), lambda m: (m, 0)))
    return pl.pallas_call(
        functools.partial(_combine_kernel, next_norm=next_g is not None),
        out_shape=tuple(out_shape),
        grid=(n_rows // tm,),
        in_specs=in_specs,
        out_specs=tuple(out_specs),
        compiler_params=_cparams(("parallel",)),
        name="moe_combine",
    )(*args)


def moe_dispatch(ids, counts, n_experts):
    blk = TILES["moe_block"]
    n = ids.shape[0]
    experts, rank = ids[:, :TOP_K], ids[:, TOP_K:2 * TOP_K]
    counts = counts[0, :n_experts].astype(jnp.int32)
    padded = (counts + blk - 1) // blk * blk
    pend = jnp.cumsum(padded)
    pstart = pend - padded
    eye = experts[:, :, None] == jnp.arange(n_experts, dtype=jnp.int32)[None, None, :]
    dest = jnp.sum(jnp.where(eye, pstart[None, None, :], 0), axis=-1) + rank
    n_blocks = -(-n * TOP_K // blk) + n_experts
    rows = n_blocks * blk
    slot = jnp.full((rows,), -1, jnp.int32).at[dest.T.reshape(-1)].set(jnp.arange(n * TOP_K, dtype=jnp.int32))
    pad = slot < 0
    pad_rank = jnp.cumsum(pad.astype(jnp.int32)) - 1
    row_dst = jnp.where(pad, n * TOP_K + pad_rank, slot)
    row_tok = jnp.where(pad, jnp.arange(rows, dtype=jnp.int32) % n, slot % n)
    block_start = jnp.arange(n_blocks, dtype=jnp.int32) * blk
    block_e = jnp.minimum(jnp.sum(block_start[:, None] >= pend[None, :], axis=1), n_experts - 1).astype(jnp.int32)
    e_ids = jnp.arange(n_experts, dtype=jnp.int32)
    present = jnp.any(block_e[:, None] == e_ids[None, :], axis=0)
    later = present[None, :] & (e_ids[None, :] > e_ids[:, None])
    next_present = jnp.where(jnp.any(later, axis=1), jnp.argmax(later, axis=1).astype(jnp.int32), e_ids)
    next_e = next_present[block_e]
    return row_tok, row_dst, block_e, next_e


def _rope_tables(seq, tm):
    inv = ROPE_THETA ** (-jnp.arange(0, AXIS_DIM, 2, dtype=F32) / AXIS_DIM)
    t = jnp.arange(seq)
    ang_r = (t // GRID_W).astype(F32)[:, None] * inv
    ang_c = (t % GRID_W).astype(F32)[:, None] * inv
    cos = jnp.concatenate([jnp.cos(ang_r), jnp.cos(ang_r), jnp.cos(ang_c), jnp.cos(ang_c)], axis=-1)
    sin = jnp.concatenate([-jnp.sin(ang_r), jnp.sin(ang_r), -jnp.sin(ang_c), jnp.sin(ang_c)], axis=-1)
    pad = LANES - ROPE_DIM
    cos = jnp.pad(cos, ((0, 0), (0, pad)))
    sin = jnp.pad(sin, ((0, 0), (0, pad)))
    ident_c = jnp.pad(jnp.ones((tm, ROPE_DIM), F32), ((0, 0), (0, pad)))
    return (jnp.concatenate([ident_c, cos], axis=0),
            jnp.concatenate([jnp.zeros((tm, LANES), F32), sin], axis=0))


def _partner():
    j = jnp.arange(ROPE_DIM)
    half = AXIS_DIM // 2
    return jnp.where((j % AXIS_DIM) < half, j + half, j - half)


def _pad_lanes(a):
    return jnp.pad(a, [(0, 0)] * (a.ndim - 1) + [(0, LANES - a.shape[-1])])


def _norm_gain_rows(g):
    rope = g[NOPE_DIM:]
    rows = jnp.stack([g[:NOPE_DIM], _pad_lanes(rope), _pad_lanes(rope[_partner()])])
    return jnp.pad(rows, ((0, 8 - rows.shape[0]), (0, 0))).astype(F32)


def _layer_weights(l, w_in, w_uq, w_ukv, q_lora, kv_lora):
    d = w_in.shape[1]
    heads = MLA_HEADS
    wl = w_in[l]
    gate_cols = 3 * d
    conv_a2 = (wl.shape[1] - gate_cols - (q_lora + kv_lora + ROPE_DIM)) // 5 * 2
    off_b = gate_cols + conv_a2
    off_c = off_b + q_lora + kv_lora + ROPE_DIM
    w_c = wl[:, off_c:]
    kpe = wl[:, off_b + q_lora + kv_lora:off_c]
    w_b = jnp.concatenate([wl[:, off_b:off_b + q_lora + kv_lora], _pad_lanes(kpe),
                           _pad_lanes(kpe[:, _partner()])], axis=1)
    wq = w_uq[l].reshape(q_lora, heads, QK_DIM)
    wq_n = wq[:, :, :NOPE_DIM].reshape(q_lora, heads * LANES)
    wq_r = _pad_lanes(wq[:, :, NOPE_DIM:]).reshape(q_lora, heads * LANES)
    wq_s = _pad_lanes(wq[:, :, NOPE_DIM:][:, :, _partner()]).reshape(q_lora, heads * LANES)
    wq_all = jnp.concatenate([wq_n, wq_r, wq_s], axis=1).astype(BF16)
    wkv = w_ukv[l].reshape(kv_lora, heads, NOPE_DIM + V_DIM)
    wkv_all = jnp.concatenate([wkv[:, :, :NOPE_DIM].reshape(kv_lora, heads * NOPE_DIM),
                               wkv[:, :, NOPE_DIM:].reshape(kv_lora, heads * V_DIM)], axis=1).astype(BF16)
    return off_b, w_c, w_b, wq_all, wkv_all


def _router_layout(w_router, router_bias):
    d, n_experts = w_router.shape
    per_group = n_experts // N_GROUPS
    wr = w_router.reshape(d, N_GROUPS, per_group).transpose(0, 2, 1)
    wr_pad = _pad_lanes(wr).reshape(d, per_group * LANES).astype(F32)
    bias = _pad_lanes(router_bias.reshape(N_GROUPS, per_group).T).astype(F32)
    bias = jnp.pad(bias, ((0, 8 - per_group), (0, 0)))
    return wr_pad, bias, per_group


def kernel(x, c, ctx, c_ctx, w_mod, b_mod, g_norm1, g_norm2, w_in, conv_a, ln_a_g, ln_a_b, w_a_out, g_cq, g_ckv, w_uq, w_ukv, g_q, g_k, w_b_out, conv_c, w_c_out, w_o, w_router, router_bias, w_e_in, w_e_out):
    batch, seq, d = x.shape
    n_ctx = ctx.shape[1]
    depth = w_mod.shape[0]
    geom = (batch, seq, n_ctx)
    t_x = batch * seq
    t_all = t_x + batch * n_ctx
    q_lora, kv_lora = g_cq.shape[1], g_ckv.shape[1]
    ch_a, ch_c = conv_a.shape[2], conv_c.shape[2]
    n_experts = w_router.shape[1]
    assert batch + 1 <= MOD_ROWS and t_x % n_ctx == 0 and ch_a == ch_c and (3 * d) % ch_a == 0

    stream = (x.reshape(t_x, d), ctx.reshape(batch * n_ctx, d))
    cvec = jnp.concatenate([c, c_ctx[None, :], jnp.zeros((MOD_ROWS - batch - 1, d), F32)], axis=0)
    mod3 = modulation(cvec, w_mod, b_mod).reshape(depth * MOD_ROWS, 1, N_MOD * d)

    cos_tab, sin_tab = _rope_tables(seq, min(TILES["mla"], seq, n_ctx))
    wr_pad, bias_pad, per_group = _router_layout(w_router, router_bias)
    col0 = 3 * d // ch_a

    for l in range(depth):
        last = l == depth - 1
        rows_main = t_x if last else t_all
        off_b, w_c, w_b, wq_all, wkv_all = _layer_weights(l, w_in, w_uq, w_ukv, q_lora, kv_lora)

        if l == 0:
            h = norm_modulate(stream, g_norm1[l], mod3, layer=l, shift_idx=0, n_rows=t_all, geom=geom)
        u = project(h, w_in.reshape(depth * d, -1), n_cols=off_b, n_rows=rows_main, row_block=l)
        uc = project(h, w_c, n_cols=w_c.shape[1], n_rows=rows_main)
        ub = project(h, w_b, n_cols=w_b.shape[1], n_rows=t_all)
        qt, k, vt = mla_project(ub, g_cq[l], g_ckv[l], wq_all, wkv_all, _norm_gain_rows(g_q[l]),
                                _norm_gain_rows(g_k[l]), cos_tab, sin_tab, geom=geom)
        ob_x = attention_latent(qt, k, vt, geom=geom)
        ob_c = ob_x if last else attention_context(qt, k, vt, geom=geom)
        ha = depthwise_branch(u, conv_a[l], col_a=col0, col_b=col0 + 1, col_gate=None,
                              ln=(ln_a_g[l], ln_a_b[l]), n_rows=rows_main, geom=geom)
        hc = depthwise_branch(uc, conv_c[l], col_a=1, col_b=2, col_gate=0,
                              ln=None, n_rows=rows_main, geom=geom)
        mrg = branch_merge(ha, ob_x, ob_c, hc, w_a_out, w_b_out, w_c_out, u, layer=l,
                           n_rows=rows_main, geom=geom)
        x1 = out_proj_residual(mrg, w_o, stream, mod3, layer=l, gate_idx=2, n_rows=rows_main, geom=geom)

        h2, ids, wts, counts = router(x1, g_norm2[l], mod3, wr_pad, bias_pad, layer=l, shift_idx=3,
                                      n_rows=rows_main, geom=geom, per_group=per_group)
        row_tok, row_dst, block_e, next_e = moe_dispatch(ids, counts, n_experts)
        y_both = expert_ffn(h2, row_tok, row_dst, block_e, next_e, w_e_in, w_e_out, layer=l)
        if last:
            (stream,) = moe_combine(x1, y_both, wts, mod3, layer=l, gate_idx=5, n_rows=rows_main, geom=geom)
        else:
            stream, h = moe_combine(x1, y_both, wts, mod3, layer=l, gate_idx=5, n_rows=rows_main,
                                    geom=geom, next_g=g_norm1[l + 1])

    return stream.reshape(batch, seq, d)
```

```python
import functools
import math

import jax
import jax.numpy as jnp
from jax import lax
from jax.experimental import pallas as pl
from jax.experimental.pallas import tpu as pltpu

F32 = jnp.float32
BF16 = jnp.bfloat16

GRID_W = 64
MLA_HEADS = 8
NOPE_DIM = 128
ROPE_DIM = 64
V_DIM = 128
QK_DIM = NOPE_DIM + ROPE_DIM
AXIS_DIM = ROPE_DIM // 2
ROPE_THETA = 10000.0
N_GROUPS = 8
TOP_K = 2
N_MOD = 6
EPS = 1e-6

LANES = 128
HEAD_SLOT = 2 * LANES
HALO_ROWS = 16
MOD_ROWS = 8

TILES = dict(
    row=512,
    proj_row=1024,
    col=1024,
    mod_col=1024,
    mla=256,
    tq=512,
    tq_sub=512,
    attn_keys=512,
    conv=256,
    conv_rows=64,
    router=512,
    moe_block=256,
    vmem=56 * 1024 * 1024,
    vmem_expert=60 * 1024 * 1024,
)


def _cparams(sem):
    return pltpu.CompilerParams(dimension_semantics=sem, vmem_limit_bytes=TILES["vmem"])


def _sigmoid(x):
    return 1.0 / (1.0 + jnp.exp(-x))


def _mod_kernel(c_ref, w_ref, b_ref, o_ref):
    c = c_ref[...]
    s = (c * _sigmoid(c)).astype(BF16)
    o_ref[...] = jnp.dot(s, w_ref[...].astype(BF16), preferred_element_type=F32) + b_ref[...]


def modulation(cvec, w_mod, b_mod):
    n_layers, d, n = w_mod.shape
    tn = min(TILES["mod_col"], n)
    return pl.pallas_call(
        _mod_kernel,
        out_shape=jax.ShapeDtypeStruct((n_layers, MOD_ROWS, n), F32),
        grid=(n_layers, n // tn),
        in_specs=[
            pl.BlockSpec((MOD_ROWS, d), lambda l, j: (0, 0)),
            pl.BlockSpec((None, d, tn), lambda l, j: (l, 0, j)),
            pl.BlockSpec((None, 1, tn), lambda l, j: (l, 0, j)),
        ],
        out_specs=pl.BlockSpec((None, MOD_ROWS, tn), lambda l, j: (l, 0, j)),
        compiler_params=_cparams(("parallel", "parallel")),
        name="modulation",
    )(cvec, w_mod, b_mod.reshape(n_layers, 1, n))


def _norm_mod(x, g, shift, scale):
    ms = jnp.mean(x * x, axis=-1, keepdims=True)
    return x * lax.rsqrt(ms + EPS) * g * (1.0 + scale) + shift


def _stream_specs(stream, tm, n_x_tiles, tn=None, grid_rank=1):
    arrays = stream if isinstance(stream, tuple) else (stream,)
    width = arrays[0].shape[1] if tn is None else tn

    def spec(row_fn):
        if grid_rank == 1:
            return pl.BlockSpec((tm, width), lambda m: (row_fn(m), 0))
        return pl.BlockSpec((tm, width), lambda j, m: (row_fn(m), j))

    if len(arrays) == 1:
        return [spec(lambda m: m)], list(arrays)
    return [spec(lambda m: jnp.minimum(m, n_x_tiles - 1)),
            spec(lambda m: jnp.maximum(m - n_x_tiles, 0))], list(arrays)


def _stream_tile(refs, m, n_x_tiles):
    if len(refs) == 1:
        return refs[0][...]
    return jnp.where(m < n_x_tiles, refs[0][...], refs[1][...])


def _norm_kernel(*refs, n_x_tiles):
    x_refs, (g_ref, shift_ref, scale_ref, h_ref) = refs[:-4], refs[-4:]
    x = _stream_tile(x_refs, pl.program_id(0), n_x_tiles)
    h_ref[...] = _norm_mod(x, g_ref[...], shift_ref[...], scale_ref[...]).astype(h_ref.dtype)


def _mod_row_fn(n_x_tiles, tiles_per_seq, batch, layer):
    def f(m):
        return layer * MOD_ROWS + jnp.where(m < n_x_tiles, m // tiles_per_seq, batch)
    return f


def norm_modulate(stream, g, mod3, *, layer, shift_idx, n_rows, geom):
    batch, seq, _ = geom
    d = g.shape[0]
    tm = min(TILES["row"], seq)
    n_x = batch * seq // tm
    mrow = _mod_row_fn(n_x, seq // tm, batch, layer)
    x_specs, x_args = _stream_specs(stream, tm, n_x)
    return pl.pallas_call(
        functools.partial(_norm_kernel, n_x_tiles=n_x),
        out_shape=jax.ShapeDtypeStruct((n_rows, d), BF16),
        grid=(n_rows // tm,),
        in_specs=x_specs + [
            pl.BlockSpec((1, d), lambda m: (0, 0)),
            pl.BlockSpec((None, 1, d), lambda m: (mrow(m), 0, shift_idx)),
            pl.BlockSpec((None, 1, d), lambda m: (mrow(m), 0, shift_idx + 1)),
        ],
        out_specs=pl.BlockSpec((tm, d), lambda m: (m, 0)),
        compiler_params=_cparams(("parallel",)),
        name="norm_modulate",
    )(*x_args, g.reshape(1, d), mod3, mod3)


def _proj_kernel(a_ref, w_ref, o_ref, w_bf):
    @pl.when(pl.program_id(1) == 0)
    def _():
        w_bf[...] = w_ref[...].astype(BF16)

    o_ref[...] = jnp.dot(a_ref[...], w_bf[...], preferred_element_type=F32).astype(o_ref.dtype)


def _w_spec(w, layer, tn, col0=0):
    if w.ndim == 3:
        return pl.BlockSpec((None, w.shape[1], tn), lambda j, m: (layer, 0, col0 + j))
    return pl.BlockSpec((w.shape[0], tn), lambda j, m: (0, col0 + j))


def project(a, w, *, n_cols, n_rows, row_block=0):
    k = a.shape[1]
    tm = min(TILES["proj_row"], n_rows)
    tn = min(TILES["col"], n_cols)
    if n_cols % tn:
        tn = n_cols
    return pl.pallas_call(
        _proj_kernel,
        out_shape=jax.ShapeDtypeStruct((n_rows, n_cols), BF16),
        grid=(n_cols // tn, n_rows // tm),
        in_specs=[pl.BlockSpec((tm, k), lambda j, m: (m, 0)),
                  pl.BlockSpec((k, tn), lambda j, m: (row_block, j))],
        out_specs=pl.BlockSpec((tm, tn), lambda j, m: (m, j)),
        scratch_shapes=[pltpu.VMEM((k, tn), BF16)],
        compiler_params=_cparams(("parallel", "arbitrary")),
        name="project",
    )(a, w)


def _proj_window_kernel(a_ref, w1_ref, w2_ref, o_ref, w_bf, *, shift):
    @pl.when(pl.program_id(1) == 0)
    def _():
        n1 = w1_ref.shape[1]
        rows = 256
        for r0 in range(0, w1_ref.shape[0], rows):
            cat = jnp.concatenate([w1_ref[r0:r0 + rows, :], w2_ref[r0:r0 + rows, :]], axis=1)
            if shift:
                cat = pltpu.roll(cat, cat.shape[1] - shift, axis=1)
            w_bf[r0:r0 + rows, :] = cat[:, :w_bf.shape[1]].astype(BF16)

    o_ref[...] = jnp.dot(a_ref[...], w_bf[...], preferred_element_type=F32).astype(o_ref.dtype)


def project_window(a, w, *, col0, n_cols, n_rows, row_block, keep_tail=False):
    k = a.shape[1]
    shift = col0 % LANES
    base = col0 - shift
    tm = min(TILES["proj_row"], n_rows)
    assert n_cols % LANES == 0 and not (keep_tail and shift)
    tn = n_cols if keep_tail else min(TILES["col"], n_cols)
    while n_cols % tn or base % tn:
        tn -= LANES
    assert tn == n_cols or not keep_tail
    out_w = tn + LANES if keep_tail else tn
    return pl.pallas_call(
        functools.partial(_proj_window_kernel, shift=shift),
        out_shape=jax.ShapeDtypeStruct((n_rows, n_cols // tn * out_w), BF16),
        grid=(n_cols // tn, n_rows // tm),
        in_specs=[pl.BlockSpec((tm, k), lambda j, m: (m, 0)),
                  pl.BlockSpec((k, tn), lambda j, m: (row_block, base // tn + j)),
                  pl.BlockSpec((k, LANES), lambda j, m: (row_block, (base + (j + 1) * tn) // LANES))],
        out_specs=pl.BlockSpec((tm, out_w), lambda j, m: (m, j)),
        scratch_shapes=[pltpu.VMEM((k, out_w), BF16)],
        compiler_params=_cparams(("parallel", "arbitrary")),
        name="project_window",
    )(a, w, w)


def _mla_kernel(ub_ref, gcq_ref, gckv_ref, wq_ref, wkv_ref, gq_ref, gk_ref, cos_ref, sin_ref, perm_ref,
                qt_ref, k_ref, vt_ref, *, q_lora, kv_lora, heads):
    hn = heads * LANES

    def rms(xv, g):
        return (xv * lax.rsqrt(jnp.mean(xv * xv, axis=-1, keepdims=True) + EPS) * g).astype(BF16)

    cqn = rms(ub_ref[:, 0:q_lora].astype(F32), gcq_ref[...])
    ckvn = rms(ub_ref[:, q_lora:q_lora + kv_lora].astype(F32), gckv_ref[...])
    off = q_lora + kv_lora
    rope_lane = lax.broadcasted_iota(jnp.int32, (ub_ref.shape[0], LANES), 1) < ROPE_DIM
    kpe_bf = jnp.where(rope_lane, ub_ref[:, off:off + LANES], jnp.zeros((), BF16))
    kpe = kpe_bf.astype(F32)
    kpes = jnp.dot(kpe_bf, perm_ref[...], preferred_element_type=F32)

    qall = jnp.dot(cqn, wq_ref[...], preferred_element_type=F32)
    kvall = jnp.dot(ckvn, wkv_ref[...], preferred_element_type=F32)

    cos = cos_ref[...]
    sin = sin_ref[...]
    gq_n, gq_r, gq_s = gq_ref[0:1, :], gq_ref[1:2, :], gq_ref[2:3, :]
    gk_n, gk_r, gk_s = gk_ref[0:1, :], gk_ref[1:2, :], gk_ref[2:3, :]
    q_cos, q_sin = gq_r * cos, gq_s * sin
    k_rope = kpe * (gk_r * cos) + kpes * (gk_s * sin)
    ssq_pe = jnp.sum(kpe * kpe, axis=-1, keepdims=True)
    q_scale = QK_DIM ** -0.5 * math.log2(math.e)

    for h in range(heads):
        lo = h * LANES
        qn = qall[:, lo:lo + LANES]
        qr = qall[:, hn + lo:hn + lo + LANES]
        qs = qall[:, 2 * hn + lo:2 * hn + lo + LANES]
        ssq = jnp.sum(qn * qn, axis=-1, keepdims=True) + jnp.sum(qr * qr, axis=-1, keepdims=True)
        inv = lax.rsqrt(ssq * (1.0 / QK_DIM) + EPS) * q_scale
        qt_ref[h * HEAD_SLOT:h * HEAD_SLOT + LANES, :] = (qn * gq_n * inv).T.astype(BF16)
        qt_ref[h * HEAD_SLOT + LANES:(h + 1) * HEAD_SLOT, :] = ((qr * q_cos + qs * q_sin) * inv).T.astype(BF16)

        kn = kvall[:, lo:lo + LANES]
        invk = lax.rsqrt((jnp.sum(kn * kn, axis=-1, keepdims=True) + ssq_pe) * (1.0 / QK_DIM) + EPS)
        k_ref[:, h * HEAD_SLOT:h * HEAD_SLOT + LANES] = (kn * gk_n * invk).astype(BF16)
        k_ref[:, h * HEAD_SLOT + LANES:(h + 1) * HEAD_SLOT] = (k_rope * invk).astype(BF16)

        vt_ref[h * V_DIM:(h + 1) * V_DIM, :] = kvall[:, hn + h * V_DIM:hn + (h + 1) * V_DIM].T.astype(BF16)


def mla_project(ub, g_cq, g_ckv, wq, wkv, gq3, gk3, cos_tab, sin_tab, *, geom):
    batch, seq, ctx = geom
    t_all = ub.shape[0]
    q_lora, kv_lora = g_cq.shape[0], g_ckv.shape[0]
    heads = MLA_HEADS
    tm = min(TILES["mla"], seq, ctx)
    n_x = batch * seq // tm
    per_seq = seq // tm

    def pos(m):
        return jnp.where(m < n_x, 1 + m % per_seq, 0)

    kern = functools.partial(_mla_kernel, q_lora=q_lora, kv_lora=kv_lora, heads=heads)
    return pl.pallas_call(
        kern,
        out_shape=(
            jax.ShapeDtypeStruct((heads * HEAD_SLOT, t_all), BF16),
            jax.ShapeDtypeStruct((t_all, heads * HEAD_SLOT), BF16),
            jax.ShapeDtypeStruct((heads * V_DIM, t_all), BF16),
        ),
        grid=(t_all // tm,),
        in_specs=[
            pl.BlockSpec((tm, ub.shape[1]), lambda m: (m, 0)),
            pl.BlockSpec((1, q_lora), lambda m: (0, 0)),
            pl.BlockSpec((1, kv_lora), lambda m: (0, 0)),
            pl.BlockSpec(wq.shape, lambda m: (0, 0)),
            pl.BlockSpec(wkv.shape, lambda m: (0, 0)),
            pl.BlockSpec((8, LANES), lambda m: (0, 0)),
            pl.BlockSpec((8, LANES), lambda m: (0, 0)),
            pl.BlockSpec((tm, LANES), lambda m: (pos(m), 0)),
            pl.BlockSpec((tm, LANES), lambda m: (pos(m), 0)),
            pl.BlockSpec((LANES, LANES), lambda m: (0, 0)),
        ],
        out_specs=(
            pl.BlockSpec((heads * HEAD_SLOT, tm), lambda m: (0, m)),
            pl.BlockSpec((tm, heads * HEAD_SLOT), lambda m: (m, 0)),
            pl.BlockSpec((heads * V_DIM, tm), lambda m: (0, m)),
        ),
        compiler_params=_cparams(("parallel",)),
        name="mla_project",
    )(ub, g_cq.reshape(1, -1), g_ckv.reshape(1, -1), wq, wkv, gq3, gk3, cos_tab, sin_tab, _partner_matrix())


def _attn_kernel(*refs, sub, latent_keys):
    if latent_keys:
        qt_ref, kx_ref, kc_ref, vxt_ref, vct_ref, o_ref = refs
    else:
        qt_ref, kc_ref, vct_ref, o_ref = refs
    for j in range(qt_ref.shape[1] // sub):
        qt = qt_ref[:, j * sub:(j + 1) * sub]
        sc = jnp.dot(kc_ref[...], qt, preferred_element_type=F32)
        m = jnp.max(sc, axis=0, keepdims=True)
        if latent_keys:
            sx = jnp.dot(kx_ref[...], qt, preferred_element_type=F32)
            m = jnp.maximum(m, jnp.max(sx, axis=0, keepdims=True))
        pc = jnp.exp2(sc - m)
        denom = jnp.sum(pc, axis=0, keepdims=True)
        ot = jnp.dot(vct_ref[...], pc.astype(BF16), preferred_element_type=F32)
        if latent_keys:
            px = jnp.exp2(sx - m)
            denom = denom + jnp.sum(px, axis=0, keepdims=True)
            ot = ot + jnp.dot(vxt_ref[...], px.astype(BF16), preferred_element_type=F32)
        o_ref[j * sub:(j + 1) * sub, :] = (ot * (1.0 / denom)).T.astype(o_ref.dtype)


def _attn_pipelined_kernel(qt_ref, kx_ref, kc_ref, vxt_ref, vct_ref, o_ref, s0_ref, s1_ref, m0_ref, m1_ref,
                           *, n_q, chunk):
    i = pl.program_id(2)
    seq, ctx = kx_ref.shape[0], kc_ref.shape[0]
    tq = qt_ref.shape[1]
    chunks = [(kc_ref, vct_ref, 0, ctx, 0)]
    chunks += [(kx_ref, vxt_ref, c0, chunk, ctx + c0) for c0 in range(0, seq, chunk)]
    slots = ((s0_ref, m0_ref), (s1_ref, m1_ref))

    def run(new, old):
        if new is not None:
            s_new, m_new = slots[new]
            qt = qt_ref[...]
            m = jnp.full((1, tq), -jnp.inf, F32)
        if old is not None:
            s_old, m_old_ref = slots[old]
            m_old = m_old_ref[...]
            acc = jnp.zeros((V_DIM, tq), F32)
            den = jnp.zeros((1, tq), F32)
        for kref, vtref, c0, size, row0 in chunks:
            if new is not None:
                s = jnp.dot(kref[c0:c0 + size, :], qt, preferred_element_type=F32)
                s_new[row0:row0 + size, :] = s
                m = jnp.maximum(m, jnp.max(s, axis=0, keepdims=True))
            if old is not None:
                p = jnp.exp2(s_old[row0:row0 + size, :] - m_old)
                den = den + jnp.sum(p, axis=0, keepdims=True)
                acc = acc + jnp.dot(vtref[:, c0:c0 + size], p.astype(BF16), preferred_element_type=F32)
        if new is not None:
            m_new[...] = m
        if old is not None:
            o_ref[...] = (acc * (1.0 / den)).T.astype(o_ref.dtype)

    pl.when(i == 0)(lambda: run(0, None))
    pl.when((i > 0) & (i < n_q) & (i % 2 == 1))(lambda: run(1, 0))
    pl.when((i > 0) & (i < n_q) & (i % 2 == 0))(lambda: run(0, 1))
    pl.when(i == n_q)(lambda: run(None, (n_q - 1) % 2))


def attention_latent(qt, k, vt, *, geom):
    batch, seq, ctx = geom
    heads = MLA_HEADS
    tq = min(TILES["tq"], seq)
    n_q = seq // tq
    ctx_blk0 = batch * seq // ctx
    kern = functools.partial(_attn_pipelined_kernel, n_q=n_q, chunk=min(TILES["attn_keys"], seq))
    return pl.pallas_call(
        kern,
        out_shape=jax.ShapeDtypeStruct((batch * seq, heads * V_DIM), BF16),
        grid=(batch, heads, n_q + 1),
        in_specs=[
            pl.BlockSpec((HEAD_SLOT, tq), lambda b, h, i: (h, b * n_q + jnp.minimum(i, n_q - 1))),
            pl.BlockSpec((seq, HEAD_SLOT), lambda b, h, i: (b, h)),
            pl.BlockSpec((ctx, HEAD_SLOT), lambda b, h, i: (ctx_blk0 + b, h)),
            pl.BlockSpec((V_DIM, seq), lambda b, h, i: (h, b)),
            pl.BlockSpec((V_DIM, ctx), lambda b, h, i: (h, ctx_blk0 + b)),
        ],
        out_specs=pl.BlockSpec((tq, V_DIM), lambda b, h, i: (b * n_q + jnp.maximum(i - 1, 0), h)),
        scratch_shapes=[pltpu.VMEM((ctx + seq, tq), F32), pltpu.VMEM((ctx + seq, tq), F32),
                        pltpu.VMEM((1, tq), F32), pltpu.VMEM((1, tq), F32)],
        compiler_params=_cparams(("parallel", "parallel", "arbitrary")),
        name="attention_latent",
    )(qt, k, k, vt, vt)


def attention_context(qt, k, vt, *, geom):
    batch, seq, ctx = geom
    heads = MLA_HEADS
    ctx_blk0 = batch * seq // ctx
    kern = functools.partial(_attn_kernel, sub=min(TILES["tq_sub"], ctx), latent_keys=False)
    return pl.pallas_call(
        kern,
        out_shape=jax.ShapeDtypeStruct((batch * ctx, heads * V_DIM), BF16),
        grid=(batch, heads),
        in_specs=[
            pl.BlockSpec((HEAD_SLOT, ctx), lambda b, h: (h, ctx_blk0 + b)),
            pl.BlockSpec((ctx, HEAD_SLOT), lambda b, h: (ctx_blk0 + b, h)),
            pl.BlockSpec((V_DIM, ctx), lambda b, h: (h, ctx_blk0 + b)),
        ],
        out_specs=pl.BlockSpec((ctx, V_DIM), lambda b, h: (b, h)),
        compiler_params=_cparams(("parallel", "parallel")),
        name="attention_context",
    )(qt, k, vt)


def _conv_kernel(*refs, width, conformer, tp, n_x_tiles, x_per_seq, c_per_seq):
    if conformer:
        (ap, bp, am, bm, an, bn, w_ref, lng_ref, lnb_ref, o_ref, buf, ybuf) = refs
        gate_ref = None
    else:
        (gate_ref, ap, bp, am, bm, an, bn, w_ref, o_ref, buf, ybuf) = refs
    nchunk = buf.shape[0]
    i = pl.program_id(0)
    in_x = i < n_x_tiles
    pos = jnp.where(in_x, i % x_per_seq, (i - n_x_tiles) % c_per_seq)
    per_seq = jnp.where(in_x, x_per_seq, c_per_seq)
    keep_prev = (pos > 0).astype(F32)
    keep_next = (pos < per_seq - 1).astype(F32)

    def pre(a, b):
        a = a.astype(F32)
        b = b.astype(F32)
        return a * _sigmoid(b) if conformer else a * b

    for c in range(nchunk):
        sl = slice(c * LANES, (c + 1) * LANES)
        buf[c, 0:HALO_ROWS, :] = pre(ap[:, sl], bp[:, sl]) * keep_prev
        buf[c, HALO_ROWS:HALO_ROWS + tp, :] = pre(am[:, sl], bm[:, sl])
        buf[c, HALO_ROWS + tp:2 * HALO_ROWS + tp, :] = pre(an[:, sl], bn[:, sl]) * keep_next

    rc = min(TILES["conv_rows"], tp)
    base = HALO_ROWS - width // 2

    def chunk(c, carry):
        wts = w_ref[c]
        for r0 in range(0, tp, rc):
            acc = jnp.zeros((rc, LANES), F32)
            for k in range(width):
                acc = acc + wts[k:k + 1, :] * buf[c, pl.ds(base + r0 + k, rc), :]
            ybuf[c, pl.ds(r0, rc), :] = acc
        return carry

    lax.fori_loop(0, nchunk, chunk, 0)

    if conformer:
        n_ch = nchunk * LANES
        tot = jnp.zeros((tp, 1), F32)
        for c in range(nchunk):
            tot = tot + jnp.sum(ybuf[c], axis=-1, keepdims=True)
        mu = tot * (1.0 / n_ch)
        tot2 = jnp.zeros((tp, 1), F32)
        for c in range(nchunk):
            dlt = ybuf[c] - mu
            tot2 = tot2 + jnp.sum(dlt * dlt, axis=-1, keepdims=True)
        inv = lax.rsqrt(tot2 * (1.0 / n_ch) + EPS)
        for c in range(nchunk):
            sl = slice(c * LANES, (c + 1) * LANES)
            y = (ybuf[c] - mu) * inv * lng_ref[:, sl] + lnb_ref[:, sl]
            o_ref[:, sl] = (y * _sigmoid(y)).astype(o_ref.dtype)
    else:
        for c in range(nchunk):
            sl = slice(c * LANES, (c + 1) * LANES)
            o_ref[:, sl] = (gate_ref[:, sl].astype(F32) * ybuf[c]).astype(o_ref.dtype)


def depthwise_branch(u, conv_w, *, col_a, col_b, col_gate, ln, n_rows, geom):
    batch, seq, ctx = geom
    width, ch = conv_w.shape
    conformer = ln is not None
    tp = min(TILES["conv"], seq, ctx)
    nchunk = ch // LANES
    hb = tp // HALO_ROWS
    last_halo = u.shape[0] // HALO_ROWS - 1
    wpad = -(-width // 8) * 8
    w3 = jnp.pad(conv_w.astype(F32), ((0, wpad - width), (0, 0))).reshape(wpad, nchunk, LANES).transpose(1, 0, 2)

    def prev_map(col):
        return lambda i: (jnp.maximum(i * hb - 1, 0), col)

    def main_map(col):
        return lambda i: (i, col)

    def next_map(col):
        return lambda i: (jnp.minimum((i + 1) * hb, last_halo), col)

    in_specs, args = [], []
    if not conformer:
        in_specs.append(pl.BlockSpec((tp, ch), main_map(col_gate)))
        args.append(u)
    for rows, mk in ((HALO_ROWS, prev_map), (tp, main_map), (HALO_ROWS, next_map)):
        for col in (col_a, col_b):
            in_specs.append(pl.BlockSpec((rows, ch), mk(col)))
            args.append(u)
    in_specs.append(pl.BlockSpec((nchunk, wpad, LANES), lambda i: (0, 0, 0)))
    args.append(w3)
    if conformer:
        for p in ln:
            in_specs.append(pl.BlockSpec((1, ch), lambda i: (0, 0)))
            args.append(p.reshape(1, ch).astype(F32))

    kern = functools.partial(
        _conv_kernel, width=width, conformer=conformer, tp=tp, n_x_tiles=batch * seq // tp,
        x_per_seq=seq // tp, c_per_seq=ctx // tp)
    return pl.pallas_call(
        kern,
        out_shape=jax.ShapeDtypeStruct((n_rows, ch), BF16),
        grid=(n_rows // tp,),
        in_specs=in_specs,
        out_specs=pl.BlockSpec((tp, ch), lambda i: (i, 0)),
        scratch_shapes=[pltpu.VMEM((nchunk, tp + 2 * HALO_ROWS, LANES), F32),
                        pltpu.VMEM((nchunk, tp, LANES), F32)],
        compiler_params=_cparams(("parallel",)),
        name="conformer_conv" if conformer else "short_conv",
    )(*args)


def _merge_kernel(ha_ref, obx_ref, obc_ref, hc_ref, wa_ref, wb_ref, wc_ref, ga_ref, gb_ref, gc_ref, o_ref,
                  wa_bf, wb_bf, wc_bf, *, n_x_tiles):
    m_idx = pl.program_id(1)

    @pl.when(m_idx == 0)
    def _():
        wa_bf[...] = wa_ref[...].astype(BF16)
        wb_bf[...] = wb_ref[...].astype(BF16)
        wc_bf[...] = wc_ref[...].astype(BF16)

    ob = jnp.where(m_idx < n_x_tiles, obx_ref[...], obc_ref[...])
    ya = jnp.dot(ha_ref[...], wa_bf[...], preferred_element_type=F32)
    yb = jnp.dot(ob, wb_bf[...], preferred_element_type=F32)
    yc = jnp.dot(hc_ref[...], wc_bf[...], preferred_element_type=F32)
    m = (_sigmoid(ga_ref[...].astype(F32)) * ya + _sigmoid(gb_ref[...].astype(F32)) * yb
         + _sigmoid(gc_ref[...].astype(F32)) * yc)
    o_ref[...] = m.astype(o_ref.dtype)


def branch_merge(ha, ob_x, ob_c, hc, wa, wb, wc, u, *, layer, n_rows, geom):
    batch, seq, _ = geom
    d = wa.shape[2]
    tm = min(TILES["row"], seq)
    tn = min(TILES["col"], d)
    ncol = d // tn
    n_x = batch * seq // tm
    row_spec = lambda a: pl.BlockSpec((tm, a.shape[1]), lambda j, m: (m, 0))
    gate_spec = lambda k: pl.BlockSpec((tm, tn), lambda j, m: (m, k * ncol + j))
    kern = functools.partial(_merge_kernel, n_x_tiles=n_x)
    return pl.pallas_call(
        kern,
        out_shape=jax.ShapeDtypeStruct((n_rows, d), BF16),
        grid=(ncol, n_rows // tm),
        in_specs=[row_spec(ha),
                  pl.BlockSpec((tm, ob_x.shape[1]), lambda j, m: (jnp.minimum(m, n_x - 1), 0)),
                  pl.BlockSpec((tm, ob_c.shape[1]), lambda j, m: (jnp.maximum(m - n_x, 0), 0)),
                  row_spec(hc), _w_spec(wa, layer, tn), _w_spec(wb, layer, tn), _w_spec(wc, layer, tn),
                  gate_spec(0), gate_spec(1), gate_spec(2)],
        out_specs=pl.BlockSpec((tm, tn), lambda j, m: (m, j)),
        scratch_shapes=[pltpu.VMEM((w.shape[1], tn), BF16) for w in (wa, wb, wc)],
        compiler_params=_cparams(("parallel", "arbitrary")),
        name="branch_merge",
    )(ha, ob_x, ob_c, hc, wa, wb, wc, u, u, u)


def _outproj_kernel(*refs, n_x_tiles):
    (m_ref, w_ref, gate_ref), x_refs, (o_ref, w_bf) = refs[:3], refs[3:-2], refs[-2:]
    m_idx = pl.program_id(1)

    @pl.when(m_idx == 0)
    def _():
        w_bf[...] = w_ref[...].astype(BF16)

    y = jnp.dot(m_ref[...], w_bf[...], preferred_element_type=F32)
    o_ref[...] = _stream_tile(x_refs, m_idx, n_x_tiles) + gate_ref[...] * y


def out_proj_residual(mrg, w_o, stream, mod3, *, layer, gate_idx, n_rows, geom):
    batch, seq, _ = geom
    d = w_o.shape[2]
    tm = min(TILES["row"], seq)
    tn = min(TILES["col"], d)
    ncol = d // tn
    n_x = batch * seq // tm
    mrow = _mod_row_fn(n_x, seq // tm, batch, layer)
    x_specs, x_args = _stream_specs(stream, tm, n_x, tn=tn, grid_rank=2)
    return pl.pallas_call(
        functools.partial(_outproj_kernel, n_x_tiles=n_x),
        out_shape=jax.ShapeDtypeStruct((n_rows, d), F32),
        grid=(ncol, n_rows // tm),
        in_specs=[
            pl.BlockSpec((tm, mrg.shape[1]), lambda j, m: (m, 0)),
            _w_spec(w_o, layer, tn),
            pl.BlockSpec((None, 1, tn), lambda j, m: (mrow(m), 0, gate_idx * ncol + j)),
        ] + x_specs,
        out_specs=pl.BlockSpec((tm, tn), lambda j, m: (m, j)),
        scratch_shapes=[pltpu.VMEM((w_o.shape[1], tn), BF16)],
        compiler_params=_cparams(("parallel", "arbitrary")),
        name="out_proj_residual",
    )(mrg, w_o, mod3, *x_args)


def _pack_halves(v):
    n = v.shape[1] // 2
    lo = lax.bitcast_convert_type(v[:, :n].astype(BF16).astype(F32), jnp.int32)
    hi = lax.bitcast_convert_type(v[:, n:].astype(BF16).astype(F32), jnp.int32)
    return lax.shift_right_logical(lo, jnp.int32(16)) | (hi & jnp.int32(-65536))


def _unpack_halves(p):
    lo = lax.bitcast_convert_type(lax.shift_left(p, jnp.int32(16)), F32)
    hi = lax.bitcast_convert_type(p & jnp.int32(-65536), F32)
    return lo, hi


def _router_kernel(x_ref, g_ref, shift_ref, scale_ref, wr_ref, bias_ref, h_ref, ids_ref, wts_ref, cnt_ref,
                   *, per_group):
    h = _norm_mod(x_ref[...], g_ref[...], shift_ref[...], scale_ref[...])
    h_ref[...] = _pack_halves(h)
    h_hi = h.astype(BF16)
    h_lo = (h - h_hi.astype(F32)).astype(BF16)
    w = wr_ref[...]
    w_hi = w.astype(BF16)
    w_lo = (w - w_hi.astype(F32)).astype(BF16)
    logits = (jnp.dot(h_hi, w_hi, preferred_element_type=F32)
              + jnp.dot(h_hi, w_lo, preferred_element_type=F32)
              + jnp.dot(h_lo, w_hi, preferred_element_type=F32))
    lane_i = lax.broadcasted_iota(jnp.int32, (h.shape[0], LANES), 1)
    lane = lane_i.astype(F32)
    valid = lane_i < N_GROUPS
    neg = jnp.float32(-jnp.inf)
    score, sel = [], []
    for j in range(per_group):
        s = _sigmoid(logits[:, j * LANES:(j + 1) * LANES])
        score.append(s)
        sel.append(jnp.where(valid, s + bias_ref[j:j + 1, :], neg))
    assert per_group == 4
    hi1, lo1 = jnp.maximum(sel[0], sel[1]), jnp.minimum(sel[0], sel[1])
    hi2, lo2 = jnp.maximum(sel[2], sel[3]), jnp.minimum(sel[2], sel[3])
    top1 = jnp.maximum(hi1, hi2)
    top2 = jnp.maximum(jnp.minimum(hi1, hi2), jnp.maximum(lo1, lo2))
    gscore = jnp.where(valid, top1 + top2, neg)
    gmax = jnp.max(gscore, axis=-1, keepdims=True)
    grp = jnp.min(jnp.where(gscore == gmax, lane, float(LANES)), axis=-1, keepdims=True)
    pick = lane == grp
    zero = jnp.float32(0.0)
    sel_g = [jnp.sum(jnp.where(pick, s, zero), axis=-1, keepdims=True) for s in sel]
    sc_g = [jnp.sum(jnp.where(pick, s, zero), axis=-1, keepdims=True) for s in score]
    best, best_i, best_s = sel_g[0], jnp.zeros_like(grp), sc_g[0]
    for j in range(1, per_group):
        better = sel_g[j] > best
        best = jnp.where(better, sel_g[j], best)
        best_i = jnp.where(better, float(j), best_i)
        best_s = jnp.where(better, sc_g[j], best_s)
    second, second_i, second_s = jnp.full_like(best, neg), jnp.zeros_like(grp), jnp.zeros_like(best)
    for j in range(per_group):
        better = (sel_g[j] > second) & (best_i != float(j))
        second = jnp.where(better, sel_g[j], second)
        second_i = jnp.where(better, float(j), second_i)
        second_s = jnp.where(better, sc_g[j], second_s)
    tot = best_s + second_s
    e0 = grp * per_group + best_i
    e1 = grp * per_group + second_i
    @pl.when(pl.program_id(0) == 0)
    def _():
        cnt_ref[...] = jnp.zeros_like(cnt_ref)

    oh0 = lane == e0
    oh1 = lane == e1
    both = jnp.where(oh0 | oh1, jnp.float32(1.0), zero)
    tm = h.shape[0]
    earlier = (lax.broadcasted_iota(jnp.int32, (tm, tm), 1)
               < lax.broadcasted_iota(jnp.int32, (tm, tm), 0)).astype(BF16)
    before = jnp.dot(earlier, both.astype(BF16), preferred_element_type=F32) + cnt_ref[0:1, :]
    r0 = jnp.sum(jnp.where(oh0, before, zero), axis=-1, keepdims=True)
    r1 = jnp.sum(jnp.where(oh1, before, zero), axis=-1, keepdims=True)
    cnt_ref[...] = cnt_ref[...] + jnp.sum(both, axis=0, keepdims=True)
    ids = jnp.where(lane_i == 0, e0, jnp.where(lane_i == 1, e1, jnp.where(lane_i == 2, r0,
                    jnp.where(lane_i == 3, r1, zero))))
    ids_ref[...] = ids.astype(jnp.int32)
    wts_ref[...] = jnp.where(lane_i == 0, best_s / tot, jnp.where(lane_i == 1, second_s / tot, zero))


def router(xall, g, mod3, wr_pad, bias_pad, *, layer, shift_idx, n_rows, geom, per_group):
    batch, seq, _ = geom
    d = xall.shape[1]
    tm = min(TILES["router"], seq)
    mrow = _mod_row_fn(batch * seq // tm, seq // tm, batch, layer)
    kern = functools.partial(_router_kernel, per_group=per_group)
    return pl.pallas_call(
        kern,
        out_shape=(jax.ShapeDtypeStruct((n_rows, d // 2), jnp.int32),
                   jax.ShapeDtypeStruct((n_rows, LANES), jnp.int32),
                   jax.ShapeDtypeStruct((n_rows, LANES), F32),
                   jax.ShapeDtypeStruct((8, LANES), F32)),
        grid=(n_rows // tm,),
        in_specs=[
            pl.BlockSpec((tm, d), lambda m: (m, 0)),
            pl.BlockSpec((1, d), lambda m: (0, 0)),
            pl.BlockSpec((None, 1, d), lambda m: (mrow(m), 0, shift_idx)),
            pl.BlockSpec((None, 1, d), lambda m: (mrow(m), 0, shift_idx + 1)),
            pl.BlockSpec(wr_pad.shape, lambda m: (0, 0)),
            pl.BlockSpec(bias_pad.shape, lambda m: (0, 0)),
        ],
        out_specs=(pl.BlockSpec((tm, d // 2), lambda m: (m, 0)),
                   pl.BlockSpec((tm, LANES), lambda m: (m, 0)),
                   pl.BlockSpec((tm, LANES), lambda m: (m, 0)),
                   pl.BlockSpec((8, LANES), lambda m: (0, 0))),
        compiler_params=_cparams(("arbitrary",)),
        name="router",
    )(xall, g.reshape(1, d), mod3, mod3, wr_pad, bias_pad)


def _expert_kernel(be_ref, nxt_ref, tok_cur, tok_next, dst_prev, dst_cur, h_hbm, win_hbm, wout_hbm, y_hbm,
                   xbuf0, xbuf1, ybuf0, ybuf1, win_f32, wout_f32, win_bf, wout_bf, gsem, ssem, wsem,
                   *, ff, n_blocks, layer):
    i = pl.program_id(0)
    blk = xbuf0.shape[0]
    xbufs, ybufs = (xbuf0, xbuf1), (ybuf0, ybuf1)

    def weight_copies(e):
        return (pltpu.make_async_copy(win_hbm.at[layer, e], win_f32, wsem.at[0]),
                pltpu.make_async_copy(wout_hbm.at[layer, e], wout_f32, wsem.at[1]))

    @pl.when(i == 0)
    def _():
        for cp in weight_copies(be_ref[0]):
            cp.start(priority=1)

    @pl.when((i == 0) | (be_ref[i] != be_ref[jnp.maximum(i - 1, 0)]))
    def _():
        for cp in weight_copies(be_ref[i]):
            cp.wait()
        win_bf[...] = win_f32[...].astype(BF16)
        wout_bf[...] = wout_f32[...].astype(BF16)
        for cp in weight_copies(nxt_ref[i]):
            cp.start(priority=1)

    def gather(tok_ref, p):
        for r in range(blk):
            pltpu.make_async_copy(h_hbm.at[pl.ds(tok_ref[0, r], 1)], xbufs[p].at[pl.ds(r, 1)],
                                  gsem.at[p]).start()

    def scatter(dst_ref, p):
        for r in range(blk):
            pltpu.make_async_copy(ybufs[p].at[pl.ds(r, 1)], y_hbm.at[pl.ds(dst_ref[0, r], 1)],
                                  ssem.at[p]).start()

    def wait_gather(p):
        pltpu.make_async_copy(h_hbm.at[pl.ds(0, blk)], xbufs[p], gsem.at[p]).wait()

    def wait_scatter(p):
        pltpu.make_async_copy(ybufs[p], y_hbm.at[pl.ds(0, blk)], ssem.at[p]).wait()

    @pl.when(i == 0)
    def _():
        ybuf1[...] = jnp.zeros_like(ybuf1)
        gather(tok_cur, 0)
        wait_gather(0)

    def step(p):
        x_lo, x_hi = _unpack_halves(xbufs[p][...])
        x = jnp.concatenate([x_lo, x_hi], axis=1).astype(BF16)
        gu = jnp.dot(x, win_bf[...], preferred_element_type=F32)
        g = gu[:, :ff]
        act = (g * _sigmoid(g) * gu[:, ff:]).astype(BF16)
        y = jnp.dot(act, wout_bf[...], preferred_element_type=F32)
        gather(tok_next, 1 - p)
        scatter(dst_prev, 1 - p)
        pl.when(i >= 1)(lambda: wait_scatter(p))
        ybufs[p][...] = _pack_halves(y)
        wait_gather(1 - p)

        @pl.when(i == n_blocks - 1)
        def _():
            scatter(dst_cur, p)
            wait_scatter(p)
            wait_scatter(1 - p)
            for cp in weight_copies(be_ref[i]):
                cp.wait()

    pl.when(i % 2 == 0)(lambda: step(0))
    pl.when(i % 2 == 1)(lambda: step(1))


def expert_ffn(h2, row_tok, row_dst, block_e, next_e, w_in, w_out, *, layer):
    dh = h2.shape[1]
    d = 2 * dh
    rows = row_tok.shape[0]
    blk = TILES["moe_block"]
    n_blocks = rows // blk
    ff = w_out.shape[2]
    tok3 = row_tok.reshape(n_blocks, 1, blk)
    spare = rows + jnp.arange(blk, dtype=jnp.int32)
    dst3 = jnp.concatenate([spare, row_dst]).reshape(n_blocks + 1, 1, blk)
    smem_blk = lambda f: pl.BlockSpec((None, 1, blk), f, memory_space=pltpu.SMEM)
    any_spec = pl.BlockSpec(memory_space=pl.ANY)
    kern = functools.partial(_expert_kernel, ff=ff, n_blocks=n_blocks, layer=layer)
    grid_spec = pltpu.PrefetchScalarGridSpec(
        num_scalar_prefetch=2,
        grid=(n_blocks,),
        in_specs=[
            smem_blk(lambda i, be, nx: (i, 0, 0)),
            smem_blk(lambda i, be, nx: (jnp.minimum(i + 1, n_blocks - 1), 0, 0)),
            smem_blk(lambda i, be, nx: (i, 0, 0)),
            smem_blk(lambda i, be, nx: (i + 1, 0, 0)),
            any_spec, any_spec, any_spec,
        ],
        out_specs=any_spec,
        scratch_shapes=[pltpu.VMEM((blk, dh), jnp.int32), pltpu.VMEM((blk, dh), jnp.int32),
                        pltpu.VMEM((blk, dh), jnp.int32), pltpu.VMEM((blk, dh), jnp.int32),
                        pltpu.VMEM((d, 2 * ff), F32), pltpu.VMEM((ff, d), F32),
                        pltpu.VMEM((d, 2 * ff), BF16), pltpu.VMEM((ff, d), BF16),
                        pltpu.SemaphoreType.DMA((2,)), pltpu.SemaphoreType.DMA((2,)),
                        pltpu.SemaphoreType.DMA((2,))],
    )
    return pl.pallas_call(
        kern,
        out_shape=jax.ShapeDtypeStruct((rows + blk, dh), jnp.int32),
        grid_spec=grid_spec,
        compiler_params=pltpu.CompilerParams(dimension_semantics=("arbitrary",),
                                             vmem_limit_bytes=TILES["vmem_expert"]),
        name="expert_ffn",
    )(block_e, next_e, tok3, tok3, dst3, dst3, h2, w_in, w_out)


def _combine_kernel(x_ref, y0_ref, y1_ref, w_ref, gate_ref, *rest, next_norm):
    w = w_ref[...]
    y0_lo, y0_hi = _unpack_halves(y0_ref[...])
    y1_lo, y1_hi = _unpack_halves(y1_ref[...])
    y = jnp.concatenate([w[:, 0:1] * y0_lo + w[:, 1:2] * y1_lo, w[:, 0:1] * y0_hi + w[:, 1:2] * y1_hi], axis=1)
    out = x_ref[...] + gate_ref[...] * y
    if next_norm:
        g_ref, shift_ref, scale_ref, o_ref, h_ref = rest
        h_ref[...] = _norm_mod(out, g_ref[...], shift_ref[...], scale_ref[...]).astype(h_ref.dtype)
    else:
        (o_ref,) = rest
    o_ref[...] = out


def moe_combine(xall, y_both, wts, mod3, *, layer, gate_idx, n_rows, geom, next_g=None):
    batch, seq, _ = geom
    d = xall.shape[1]
    tm = min(TILES["row"], seq)
    n_x = batch * seq // tm
    mrow = _mod_row_fn(n_x, seq // tm, batch, layer)
    in_specs = [
        pl.BlockSpec((tm, d), lambda m: (m, 0)),
        pl.BlockSpec((tm, d // 2), lambda m: (m, 0)),
        pl.BlockSpec((tm, d // 2), lambda m: (n_rows // tm + m, 0)),
        pl.BlockSpec((tm, LANES), lambda m: (m, 0)),
        pl.BlockSpec((None, 1, d), lambda m: (mrow(m), 0, gate_idx)),
    ]
    args = [xall, y_both, y_both, wts, mod3]
    out_shape = [jax.ShapeDtypeStruct((n_rows, d), F32)]
    out_specs = [pl.BlockSpec((tm, d), lambda m: (m, 0))]
    if next_g is not None:
        nrow = _mod_row_fn(n_x, seq // tm, batch, layer + 1)
        in_specs += [pl.BlockSpec((1, d), lambda m: (0, 0)),
                     pl.BlockSpec((None, 1, d), lambda m: (nrow(m), 0, 0)),
                     pl.BlockSpec((None, 1, d), lambda m: (nrow(m), 0, 1))]
        args += [next_g.reshape(1, d), mod3, mod3]
        out_shape.append(jax.ShapeDtypeStruct((n_rows, d), BF16))
        out_specs.append(pl.BlockSpec((tm, d), lambda m: (m, 0)))
    return pl.pallas_call(
        functools.partial(_combine_kernel, next_norm=next_g is not None),
        out_shape=tuple(out_shape),
        grid=(n_rows // tm,),
        in_specs=in_specs,
        out_specs=tuple(out_specs),
        compiler_params=_cparams(("parallel",)),
        name="moe_combine",
    )(*args)


def moe_dispatch(ids, counts, n_experts):
    blk = TILES["moe_block"]
    n = ids.shape[0]
    experts, rank = ids[:, :TOP_K], ids[:, TOP_K:2 * TOP_K]
    counts = counts[0, :n_experts].astype(jnp.int32)
    padded = (counts + blk - 1) // blk * blk
    pend = jnp.cumsum(padded)
    pstart = pend - padded
    eye = experts[:, :, None] == jnp.arange(n_experts, dtype=jnp.int32)[None, None, :]
    dest = jnp.sum(jnp.where(eye, pstart[None, None, :], 0), axis=-1) + rank
    n_blocks = -(-n * TOP_K // blk) + n_experts
    rows = n_blocks * blk
    slot = jnp.full((rows,), -1, jnp.int32).at[dest.T.reshape(-1)].set(jnp.arange(n * TOP_K, dtype=jnp.int32))
    pad = slot < 0
    pad_rank = jnp.cumsum(pad.astype(jnp.int32)) - 1
    row_dst = jnp.where(pad, n * TOP_K + pad_rank, slot)
    row_tok = jnp.where(pad, jnp.arange(rows, dtype=jnp.int32) % n, slot % n)
    block_start = jnp.arange(n_blocks, dtype=jnp.int32) * blk
    block_e = jnp.minimum(jnp.sum(block_start[:, None] >= pend[None, :], axis=1), n_experts - 1).astype(jnp.int32)
    e_ids = jnp.arange(n_experts, dtype=jnp.int32)
    present = jnp.any(block_e[:, None] == e_ids[None, :], axis=0)
    later = present[None, :] & (e_ids[None, :] > e_ids[:, None])
    next_present = jnp.where(jnp.any(later, axis=1), jnp.argmax(later, axis=1).astype(jnp.int32), e_ids)
    next_e = next_present[block_e]
    return row_tok, row_dst, block_e, next_e


def _rope_tables(seq, tm):
    inv = ROPE_THETA ** (-jnp.arange(0, AXIS_DIM, 2, dtype=F32) / AXIS_DIM)
    t = jnp.arange(seq)
    ang_r = (t // GRID_W).astype(F32)[:, None] * inv
    ang_c = (t % GRID_W).astype(F32)[:, None] * inv
    cos = jnp.concatenate([jnp.cos(ang_r), jnp.cos(ang_r), jnp.cos(ang_c), jnp.cos(ang_c)], axis=-1)
    sin = jnp.concatenate([-jnp.sin(ang_r), jnp.sin(ang_r), -jnp.sin(ang_c), jnp.sin(ang_c)], axis=-1)
    pad = LANES - ROPE_DIM
    cos = jnp.pad(cos, ((0, 0), (0, pad)))
    sin = jnp.pad(sin, ((0, 0), (0, pad)))
    ident_c = jnp.pad(jnp.ones((tm, ROPE_DIM), F32), ((0, 0), (0, pad)))
    return (jnp.concatenate([ident_c, cos], axis=0),
            jnp.concatenate([jnp.zeros((tm, LANES), F32), sin], axis=0))


def _partner():
    j = jnp.arange(ROPE_DIM)
    half = AXIS_DIM // 2
    return jnp.where((j % AXIS_DIM) < half, j + half, j - half)


def _partner_matrix():
    rows = jnp.arange(LANES)[:, None]
    cols = jnp.arange(LANES)[None, :]
    partner = jnp.pad(_partner(), (0, LANES - ROPE_DIM), constant_values=-1)
    return (rows == partner[None, :]).astype(BF16) * (cols < ROPE_DIM).astype(BF16)


def _pad_lanes(a):
    return jnp.pad(a, [(0, 0)] * (a.ndim - 1) + [(0, LANES - a.shape[-1])])


def _norm_gain_rows(g):
    rope = g[NOPE_DIM:]
    rows = jnp.stack([g[:NOPE_DIM], _pad_lanes(rope), _pad_lanes(rope[_partner()])])
    return jnp.pad(rows, ((0, 8 - rows.shape[0]), (0, 0))).astype(F32)


def _layer_weights(l, w_uq, w_ukv, q_lora, kv_lora):
    heads = MLA_HEADS
    wq = w_uq[l].reshape(q_lora, heads, QK_DIM)
    wq_n = wq[:, :, :NOPE_DIM].reshape(q_lora, heads * LANES)
    wq_r = _pad_lanes(wq[:, :, NOPE_DIM:]).reshape(q_lora, heads * LANES)
    wq_s = _pad_lanes(wq[:, :, NOPE_DIM:][:, :, _partner()]).reshape(q_lora, heads * LANES)
    wq_all = jnp.concatenate([wq_n, wq_r, wq_s], axis=1).astype(BF16)
    wkv = w_ukv[l].reshape(kv_lora, heads, NOPE_DIM + V_DIM)
    wkv_all = jnp.concatenate([wkv[:, :, :NOPE_DIM].reshape(kv_lora, heads * NOPE_DIM),
                               wkv[:, :, NOPE_DIM:].reshape(kv_lora, heads * V_DIM)], axis=1).astype(BF16)
    return wq_all, wkv_all


def _router_layout(w_router, router_bias):
    d, n_experts = w_router.shape
    per_group = n_experts // N_GROUPS
    wr = w_router.reshape(d, N_GROUPS, per_group).transpose(0, 2, 1)
    wr_pad = _pad_lanes(wr).reshape(d, per_group * LANES).astype(F32)
    bias = _pad_lanes(router_bias.reshape(N_GROUPS, per_group).T).astype(F32)
    bias = jnp.pad(bias, ((0, 8 - per_group), (0, 0)))
    return wr_pad, bias, per_group


def kernel(x, c, ctx, c_ctx, w_mod, b_mod, g_norm1, g_norm2, w_in, conv_a, ln_a_g, ln_a_b, w_a_out, g_cq, g_ckv, w_uq, w_ukv, g_q, g_k, w_b_out, conv_c, w_c_out, w_o, w_router, router_bias, w_e_in, w_e_out):
    batch, seq, d = x.shape
    n_ctx = ctx.shape[1]
    depth = w_mod.shape[0]
    geom = (batch, seq, n_ctx)
    t_x = batch * seq
    t_all = t_x + batch * n_ctx
    q_lora, kv_lora = g_cq.shape[1], g_ckv.shape[1]
    ch_a, ch_c = conv_a.shape[2], conv_c.shape[2]
    n_experts = w_router.shape[1]
    assert batch + 1 <= MOD_ROWS and t_x % n_ctx == 0 and ch_a == ch_c and (3 * d) % ch_a == 0

    stream = (x.reshape(t_x, d), ctx.reshape(batch * n_ctx, d))
    cvec = jnp.concatenate([c, c_ctx[None, :], jnp.zeros((MOD_ROWS - batch - 1, d), F32)], axis=0)
    mod3 = modulation(cvec, w_mod, b_mod).reshape(depth * MOD_ROWS, 1, N_MOD * d)

    cos_tab, sin_tab = _rope_tables(seq, min(TILES["mla"], seq, n_ctx))
    wr_pad, bias_pad, per_group = _router_layout(w_router, router_bias)
    col0 = 3 * d // ch_a
    off_b = 3 * d + 2 * ch_a
    off_c = off_b + q_lora + kv_lora + ROPE_DIM
    w_in2 = w_in.reshape(depth * d, w_in.shape[2])

    for l in range(depth):
        last = l == depth - 1
        rows_main = t_x if last else t_all
        wq_all, wkv_all = _layer_weights(l, w_uq, w_ukv, q_lora, kv_lora)

        if l == 0:
            h = norm_modulate(stream, g_norm1[l], mod3, layer=l, shift_idx=0, n_rows=t_all, geom=geom)
        u = project(h, w_in2, n_cols=off_b, n_rows=rows_main, row_block=l)
        uc = project_window(h, w_in2, col0=off_c, n_cols=3 * ch_c, n_rows=rows_main, row_block=l)
        ub = project_window(h, w_in2, col0=off_b, n_cols=q_lora + kv_lora, n_rows=t_all, row_block=l,
                            keep_tail=True)
        qt, k, vt = mla_project(ub, g_cq[l], g_ckv[l], wq_all, wkv_all, _norm_gain_rows(g_q[l]),
                                _norm_gain_rows(g_k[l]), cos_tab, sin_tab, geom=geom)
        ob_x = attention_latent(qt, k, vt, geom=geom)
        ob_c = ob_x if last else attention_context(qt, k, vt, geom=geom)
        ha = depthwise_branch(u, conv_a[l], col_a=col0, col_b=col0 + 1, col_gate=None,
                              ln=(ln_a_g[l], ln_a_b[l]), n_rows=rows_main, geom=geom)
        hc = depthwise_branch(uc, conv_c[l], col_a=1, col_b=2, col_gate=0,
                              ln=None, n_rows=rows_main, geom=geom)
        mrg = branch_merge(ha, ob_x, ob_c, hc, w_a_out, w_b_out, w_c_out, u, layer=l,
                           n_rows=rows_main, geom=geom)
        x1 = out_proj_residual(mrg, w_o, stream, mod3, layer=l, gate_idx=2, n_rows=rows_main, geom=geom)

        h2, ids, wts, counts = router(x1, g_norm2[l], mod3, wr_pad, bias_pad, layer=l, shift_idx=3,
                                      n_rows=rows_main, geom=geom, per_group=per_group)
        row_tok, row_dst, block_e, next_e = moe_dispatch(ids, counts, n_experts)
        y_both = expert_ffn(h2, row_tok, row_dst, block_e, next_e, w_e_in, w_e_out, layer=l)
        if last:
            (stream,) = moe_combine(x1, y_both, wts, mod3, layer=l, gate_idx=5, n_rows=rows_main, geom=geom)
        else:
            stream, h = moe_combine(x1, y_both, wts, mod3, layer=l, gate_idx=5, n_rows=rows_main,
                                    geom=geom, next_g=g_norm1[l + 1])

    return stream.reshape(batch, seq, d)
```

```python
import functools
import math

import jax
import jax.numpy as jnp
from jax import lax
from jax.experimental import pallas as pl
from jax.experimental.pallas import tpu as pltpu

F32 = jnp.float32
BF16 = jnp.bfloat16

GRID_W = 64
MLA_HEADS = 8
NOPE_DIM = 128
ROPE_DIM = 64
V_DIM = 128
QK_DIM = NOPE_DIM + ROPE_DIM
AXIS_DIM = ROPE_DIM // 2
ROPE_THETA = 10000.0
N_GROUPS = 8
TOP_K = 2
N_MOD = 6
EPS = 1e-6

LANES = 128
HEAD_SLOT = 2 * LANES
HALO_ROWS = 16
MOD_ROWS = 8

TILES = dict(
    row=512,
    proj_row=1024,
    col=1024,
    mod_col=1024,
    mla=512,
    tq=512,
    tq_sub=512,
    attn_keys=512,
    conv=256,
    conv_rows=64,
    router=512,
    moe_block=256,
    vmem=56 * 1024 * 1024,
    vmem_expert=60 * 1024 * 1024,
)


def _cparams(sem):
    return pltpu.CompilerParams(dimension_semantics=sem, vmem_limit_bytes=TILES["vmem"])


def _sigmoid(x):
    return 1.0 / (1.0 + jnp.exp(-x))


def _mod_kernel(c_ref, w_ref, b_ref, o_ref):
    c = c_ref[...]
    s = (c * _sigmoid(c)).astype(BF16)
    o_ref[...] = jnp.dot(s, w_ref[...].astype(BF16), preferred_element_type=F32) + b_ref[...]


def modulation(cvec, w_mod, b_mod):
    n_layers, d, n = w_mod.shape
    tn = min(TILES["mod_col"], n)
    return pl.pallas_call(
        _mod_kernel,
        out_shape=jax.ShapeDtypeStruct((n_layers, MOD_ROWS, n), F32),
        grid=(n_layers, n // tn),
        in_specs=[
            pl.BlockSpec((MOD_ROWS, d), lambda l, j: (0, 0)),
            pl.BlockSpec((None, d, tn), lambda l, j: (l, 0, j)),
            pl.BlockSpec((None, 1, tn), lambda l, j: (l, 0, j)),
        ],
        out_specs=pl.BlockSpec((None, MOD_ROWS, tn), lambda l, j: (l, 0, j)),
        compiler_params=_cparams(("parallel", "parallel")),
        name="modulation",
    )(cvec, w_mod, b_mod.reshape(n_layers, 1, n))


def _norm_mod(x, g, shift, scale):
    ms = jnp.mean(x * x, axis=-1, keepdims=True)
    return x * lax.rsqrt(ms + EPS) * g * (1.0 + scale) + shift


def _stream_specs(stream, tm, n_x_tiles, tn=None, grid_rank=1):
    arrays = stream if isinstance(stream, tuple) else (stream,)
    width = arrays[0].shape[1] if tn is None else tn

    def spec(row_fn):
        if grid_rank == 1:
            return pl.BlockSpec((tm, width), lambda m: (row_fn(m), 0))
        return pl.BlockSpec((tm, width), lambda j, m: (row_fn(m), j))

    if len(arrays) == 1:
        return [spec(lambda m: m)], list(arrays)
    return [spec(lambda m: jnp.minimum(m, n_x_tiles - 1)),
            spec(lambda m: jnp.maximum(m - n_x_tiles, 0))], list(arrays)


def _stream_tile(refs, m, n_x_tiles):
    if len(refs) == 1:
        return refs[0][...]
    return jnp.where(m < n_x_tiles, refs[0][...], refs[1][...])


def _norm_kernel(*refs, n_x_tiles):
    x_refs, (g_ref, shift_ref, scale_ref, h_ref) = refs[:-4], refs[-4:]
    x = _stream_tile(x_refs, pl.program_id(0), n_x_tiles)
    h_ref[...] = _norm_mod(x, g_ref[...], shift_ref[...], scale_ref[...]).astype(h_ref.dtype)


def _mod_row_fn(n_x_tiles, tiles_per_seq, batch, layer):
    def f(m):
        return layer * MOD_ROWS + jnp.where(m < n_x_tiles, m // tiles_per_seq, batch)
    return f


def norm_modulate(stream, g, mod3, *, layer, shift_idx, n_rows, geom):
    batch, seq, _ = geom
    d = g.shape[0]
    tm = min(TILES["row"], seq)
    n_x = batch * seq // tm
    mrow = _mod_row_fn(n_x, seq // tm, batch, layer)
    x_specs, x_args = _stream_specs(stream, tm, n_x)
    return pl.pallas_call(
        functools.partial(_norm_kernel, n_x_tiles=n_x),
        out_shape=jax.ShapeDtypeStruct((n_rows, d), BF16),
        grid=(n_rows // tm,),
        in_specs=x_specs + [
            pl.BlockSpec((1, d), lambda m: (0, 0)),
            pl.BlockSpec((None, 1, d), lambda m: (mrow(m), 0, shift_idx)),
            pl.BlockSpec((None, 1, d), lambda m: (mrow(m), 0, shift_idx + 1)),
        ],
        out_specs=pl.BlockSpec((tm, d), lambda m: (m, 0)),
        compiler_params=_cparams(("parallel",)),
        name="norm_modulate",
    )(*x_args, g.reshape(1, d), mod3, mod3)


def _proj_kernel(a_ref, w_ref, o_ref, w_bf):
    @pl.when(pl.program_id(1) == 0)
    def _():
        w_bf[...] = w_ref[...].astype(BF16)

    o_ref[...] = jnp.dot(a_ref[...], w_bf[...], preferred_element_type=F32).astype(o_ref.dtype)


def _w_spec(w, layer, tn, col0=0):
    if w.ndim == 3:
        return pl.BlockSpec((None, w.shape[1], tn), lambda j, m: (layer, 0, col0 + j))
    return pl.BlockSpec((w.shape[0], tn), lambda j, m: (0, col0 + j))


def project(a, w, *, n_cols, n_rows, row_block=0):
    k = a.shape[1]
    tm = min(TILES["proj_row"], n_rows)
    tn = min(TILES["col"], n_cols)
    if n_cols % tn:
        tn = n_cols
    return pl.pallas_call(
        _proj_kernel,
        out_shape=jax.ShapeDtypeStruct((n_rows, n_cols), BF16),
        grid=(n_cols // tn, n_rows // tm),
        in_specs=[pl.BlockSpec((tm, k), lambda j, m: (m, 0)),
                  pl.BlockSpec((k, tn), lambda j, m: (row_block, j))],
        out_specs=pl.BlockSpec((tm, tn), lambda j, m: (m, j)),
        scratch_shapes=[pltpu.VMEM((k, tn), BF16)],
        compiler_params=_cparams(("parallel", "arbitrary")),
        name="project",
    )(a, w)


def _proj_window_kernel(a_ref, w1_ref, w2_ref, o_ref, w_bf, *, shift):
    @pl.when(pl.program_id(1) == 0)
    def _():
        n1 = w1_ref.shape[1]
        rows = 256
        for r0 in range(0, w1_ref.shape[0], rows):
            cat = jnp.concatenate([w1_ref[r0:r0 + rows, :], w2_ref[r0:r0 + rows, :]], axis=1)
            if shift:
                cat = pltpu.roll(cat, cat.shape[1] - shift, axis=1)
            w_bf[r0:r0 + rows, :] = cat[:, :w_bf.shape[1]].astype(BF16)

    o_ref[...] = jnp.dot(a_ref[...], w_bf[...], preferred_element_type=F32).astype(o_ref.dtype)


def project_window(a, w, *, col0, n_cols, n_rows, row_block, keep_tail=False):
    k = a.shape[1]
    shift = col0 % LANES
    base = col0 - shift
    tm = min(TILES["proj_row"], n_rows)
    assert n_cols % LANES == 0 and not (keep_tail and shift)
    tn = n_cols if keep_tail else min(TILES["col"], n_cols)
    while n_cols % tn or base % tn:
        tn -= LANES
    assert tn == n_cols or not keep_tail
    out_w = tn + LANES if keep_tail else tn
    return pl.pallas_call(
        functools.partial(_proj_window_kernel, shift=shift),
        out_shape=jax.ShapeDtypeStruct((n_rows, n_cols // tn * out_w), BF16),
        grid=(n_cols // tn, n_rows // tm),
        in_specs=[pl.BlockSpec((tm, k), lambda j, m: (m, 0)),
                  pl.BlockSpec((k, tn), lambda j, m: (row_block, base // tn + j)),
                  pl.BlockSpec((k, LANES), lambda j, m: (row_block, (base + (j + 1) * tn) // LANES))],
        out_specs=pl.BlockSpec((tm, out_w), lambda j, m: (m, j)),
        scratch_shapes=[pltpu.VMEM((k, out_w), BF16)],
        compiler_params=_cparams(("parallel", "arbitrary")),
        name="project_window",
    )(a, w, w)


def _mla_kernel(ub_ref, gcq_ref, gckv_ref, wq_ref, wkv_ref, gq_ref, gk_ref, cos_ref, sin_ref, perm_ref,
                qt_ref, k_ref, vt_ref, *, q_lora, kv_lora, heads):
    hn = heads * LANES

    def rms(xv, g):
        return (xv * lax.rsqrt(jnp.mean(xv * xv, axis=-1, keepdims=True) + EPS) * g).astype(BF16)

    cqn = rms(ub_ref[:, 0:q_lora].astype(F32), gcq_ref[...])
    ckvn = rms(ub_ref[:, q_lora:q_lora + kv_lora].astype(F32), gckv_ref[...])
    off = q_lora + kv_lora
    rope_lane = lax.broadcasted_iota(jnp.int32, (ub_ref.shape[0], LANES), 1) < ROPE_DIM
    kpe_bf = jnp.where(rope_lane, ub_ref[:, off:off + LANES], jnp.zeros((), BF16))
    kpe = kpe_bf.astype(F32)
    kpes = jnp.dot(kpe_bf, perm_ref[...], preferred_element_type=F32)

    qall = jnp.dot(cqn, wq_ref[...], preferred_element_type=F32)
    kvall = jnp.dot(ckvn, wkv_ref[...], preferred_element_type=F32)

    cos = cos_ref[...]
    sin = sin_ref[...]
    gq_n, gq_r, gq_s = gq_ref[0:1, :], gq_ref[1:2, :], gq_ref[2:3, :]
    gk_n, gk_r, gk_s = gk_ref[0:1, :], gk_ref[1:2, :], gk_ref[2:3, :]
    q_cos, q_sin = gq_r * cos, gq_s * sin
    k_rope = kpe * (gk_r * cos) + kpes * (gk_s * sin)
    ssq_pe = jnp.sum(kpe * kpe, axis=-1, keepdims=True)
    q_scale = QK_DIM ** -0.5 * math.log2(math.e)

    for h in range(heads):
        lo = h * LANES
        qn = qall[:, lo:lo + LANES]
        qr = qall[:, hn + lo:hn + lo + LANES]
        qs = qall[:, 2 * hn + lo:2 * hn + lo + LANES]
        ssq = jnp.sum(qn * qn, axis=-1, keepdims=True) + jnp.sum(qr * qr, axis=-1, keepdims=True)
        inv = lax.rsqrt(ssq * (1.0 / QK_DIM) + EPS) * q_scale
        qt_ref[h * HEAD_SLOT:h * HEAD_SLOT + LANES, :] = (qn * gq_n * inv).T.astype(BF16)
        qt_ref[h * HEAD_SLOT + LANES:(h + 1) * HEAD_SLOT, :] = ((qr * q_cos + qs * q_sin) * inv).T.astype(BF16)

        kn = kvall[:, lo:lo + LANES]
        invk = lax.rsqrt((jnp.sum(kn * kn, axis=-1, keepdims=True) + ssq_pe) * (1.0 / QK_DIM) + EPS)
        k_ref[:, h * HEAD_SLOT:h * HEAD_SLOT + LANES] = (kn * gk_n * invk).astype(BF16)
        k_ref[:, h * HEAD_SLOT + LANES:(h + 1) * HEAD_SLOT] = (k_rope * invk).astype(BF16)

        vt_ref[h * V_DIM:(h + 1) * V_DIM, :] = kvall[:, hn + h * V_DIM:hn + (h + 1) * V_DIM].T.astype(BF16)


def _mla_tile(geom):
    batch, seq, ctx = geom
    tm = min(TILES["mla"], seq, batch * ctx)
    assert seq % tm == 0 and (batch * ctx) % tm == 0
    return tm


def mla_project(ub, g_cq, g_ckv, wq, wkv, gq3, gk3, cos_tab, sin_tab, *, geom):
    batch, seq, ctx = geom
    t_all = ub.shape[0]
    q_lora, kv_lora = g_cq.shape[0], g_ckv.shape[0]
    heads = MLA_HEADS
    tm = _mla_tile(geom)
    n_x = batch * seq // tm
    per_seq = seq // tm

    def pos(m):
        return jnp.where(m < n_x, 1 + m % per_seq, 0)

    kern = functools.partial(_mla_kernel, q_lora=q_lora, kv_lora=kv_lora, heads=heads)
    return pl.pallas_call(
        kern,
        out_shape=(
            jax.ShapeDtypeStruct((heads * HEAD_SLOT, t_all), BF16),
            jax.ShapeDtypeStruct((t_all, heads * HEAD_SLOT), BF16),
            jax.ShapeDtypeStruct((heads * V_DIM, t_all), BF16),
        ),
        grid=(t_all // tm,),
        in_specs=[
            pl.BlockSpec((tm, ub.shape[1]), lambda m: (m, 0)),
            pl.BlockSpec((1, q_lora), lambda m: (0, 0)),
            pl.BlockSpec((1, kv_lora), lambda m: (0, 0)),
            pl.BlockSpec(wq.shape, lambda m: (0, 0)),
            pl.BlockSpec(wkv.shape, lambda m: (0, 0)),
            pl.BlockSpec((8, LANES), lambda m: (0, 0)),
            pl.BlockSpec((8, LANES), lambda m: (0, 0)),
            pl.BlockSpec((tm, LANES), lambda m: (pos(m), 0)),
            pl.BlockSpec((tm, LANES), lambda m: (pos(m), 0)),
            pl.BlockSpec((LANES, LANES), lambda m: (0, 0)),
        ],
        out_specs=(
            pl.BlockSpec((heads * HEAD_SLOT, tm), lambda m: (0, m)),
            pl.BlockSpec((tm, heads * HEAD_SLOT), lambda m: (m, 0)),
            pl.BlockSpec((heads * V_DIM, tm), lambda m: (0, m)),
        ),
        compiler_params=_cparams(("parallel",)),
        name="mla_project",
    )(ub, g_cq.reshape(1, -1), g_ckv.reshape(1, -1), wq, wkv, gq3, gk3, cos_tab, sin_tab, _partner_matrix())


def _attn_kernel(*refs, sub, latent_keys):
    if latent_keys:
        qt_ref, kx_ref, kc_ref, vxt_ref, vct_ref, o_ref = refs
    else:
        qt_ref, kc_ref, vct_ref, o_ref = refs
    for j in range(qt_ref.shape[1] // sub):
        qt = qt_ref[:, j * sub:(j + 1) * sub]
        sc = jnp.dot(kc_ref[...], qt, preferred_element_type=F32)
        m = jnp.max(sc, axis=0, keepdims=True)
        if latent_keys:
            sx = jnp.dot(kx_ref[...], qt, preferred_element_type=F32)
            m = jnp.maximum(m, jnp.max(sx, axis=0, keepdims=True))
        pc = jnp.exp2(sc - m)
        denom = jnp.sum(pc, axis=0, keepdims=True)
        ot = jnp.dot(vct_ref[...], pc.astype(BF16), preferred_element_type=F32)
        if latent_keys:
            px = jnp.exp2(sx - m)
            denom = denom + jnp.sum(px, axis=0, keepdims=True)
            ot = ot + jnp.dot(vxt_ref[...], px.astype(BF16), preferred_element_type=F32)
        o_ref[j * sub:(j + 1) * sub, :] = (ot * (1.0 / denom)).T.astype(o_ref.dtype)


def _attn_pipelined_kernel(qt_ref, kx_ref, kc_ref, vxt_ref, vct_ref, o_ref, s0_ref, s1_ref, m0_ref, m1_ref,
                           *, n_q, chunk):
    i = pl.program_id(2)
    seq, ctx = kx_ref.shape[0], kc_ref.shape[0]
    tq = qt_ref.shape[1]
    chunks = [(kc_ref, vct_ref, 0, ctx, 0)]
    chunks += [(kx_ref, vxt_ref, c0, chunk, ctx + c0) for c0 in range(0, seq, chunk)]
    slots = ((s0_ref, m0_ref), (s1_ref, m1_ref))

    def run(new, old):
        if new is not None:
            s_new, m_new = slots[new]
            qt = qt_ref[...]
            m = jnp.full((1, tq), -jnp.inf, F32)
        if old is not None:
            s_old, m_old_ref = slots[old]
            m_old = m_old_ref[...]
            acc = jnp.zeros((V_DIM, tq), F32)
            den = jnp.zeros((1, tq), F32)
        for kref, vtref, c0, size, row0 in chunks:
            if new is not None:
                s = jnp.dot(kref[c0:c0 + size, :], qt, preferred_element_type=F32)
                s_new[row0:row0 + size, :] = s
                m = jnp.maximum(m, jnp.max(s, axis=0, keepdims=True))
            if old is not None:
                p = jnp.exp2(s_old[row0:row0 + size, :] - m_old)
                den = den + jnp.sum(p, axis=0, keepdims=True)
                acc = acc + jnp.dot(vtref[:, c0:c0 + size], p.astype(BF16), preferred_element_type=F32)
        if new is not None:
            m_new[...] = m
        if old is not None:
            o_ref[...] = (acc * (1.0 / den)).T.astype(o_ref.dtype)

    pl.when(i == 0)(lambda: run(0, None))
    pl.when((i > 0) & (i < n_q) & (i % 2 == 1))(lambda: run(1, 0))
    pl.when((i > 0) & (i < n_q) & (i % 2 == 0))(lambda: run(0, 1))
    pl.when(i == n_q)(lambda: run(None, (n_q - 1) % 2))


def attention_latent(qt, k, vt, *, geom):
    batch, seq, ctx = geom
    heads = MLA_HEADS
    tq = min(TILES["tq"], seq)
    n_q = seq // tq
    ctx_blk0 = batch * seq // ctx
    kern = functools.partial(_attn_pipelined_kernel, n_q=n_q, chunk=min(TILES["attn_keys"], seq))
    return pl.pallas_call(
        kern,
        out_shape=jax.ShapeDtypeStruct((batch * seq, heads * V_DIM), BF16),
        grid=(batch, heads, n_q + 1),
        in_specs=[
            pl.BlockSpec((HEAD_SLOT, tq), lambda b, h, i: (h, b * n_q + jnp.minimum(i, n_q - 1))),
            pl.BlockSpec((seq, HEAD_SLOT), lambda b, h, i: (b, h)),
            pl.BlockSpec((ctx, HEAD_SLOT), lambda b, h, i: (ctx_blk0 + b, h)),
            pl.BlockSpec((V_DIM, seq), lambda b, h, i: (h, b)),
            pl.BlockSpec((V_DIM, ctx), lambda b, h, i: (h, ctx_blk0 + b)),
        ],
        out_specs=pl.BlockSpec((tq, V_DIM), lambda b, h, i: (b * n_q + jnp.maximum(i - 1, 0), h)),
        scratch_shapes=[pltpu.VMEM((ctx + seq, tq), F32), pltpu.VMEM((ctx + seq, tq), F32),
                        pltpu.VMEM((1, tq), F32), pltpu.VMEM((1, tq), F32)],
        compiler_params=_cparams(("parallel", "parallel", "arbitrary")),
        name="attention_latent",
    )(qt, k, k, vt, vt)


def attention_context(qt, k, vt, *, geom):
    batch, seq, ctx = geom
    heads = MLA_HEADS
    ctx_blk0 = batch * seq // ctx
    kern = functools.partial(_attn_kernel, sub=min(TILES["tq_sub"], ctx), latent_keys=False)
    return pl.pallas_call(
        kern,
        out_shape=jax.ShapeDtypeStruct((batch * ctx, heads * V_DIM), BF16),
        grid=(batch, heads),
        in_specs=[
            pl.BlockSpec((HEAD_SLOT, ctx), lambda b, h: (h, ctx_blk0 + b)),
            pl.BlockSpec((ctx, HEAD_SLOT), lambda b, h: (ctx_blk0 + b, h)),
            pl.BlockSpec((V_DIM, ctx), lambda b, h: (h, ctx_blk0 + b)),
        ],
        out_specs=pl.BlockSpec((ctx, V_DIM), lambda b, h: (b, h)),
        compiler_params=_cparams(("parallel", "parallel")),
        name="attention_context",
    )(qt, k, vt)


def _conv_kernel(*refs, width, conformer, tp, n_x_tiles, x_per_seq, c_per_seq):
    if conformer:
        (ap, bp, am, bm, an, bn, w_ref, lng_ref, lnb_ref, o_ref, buf, ybuf) = refs
        gate_ref = None
    else:
        (gate_ref, ap, bp, am, bm, an, bn, w_ref, o_ref, buf, ybuf) = refs
    nchunk = buf.shape[0]
    i = pl.program_id(0)
    in_x = i < n_x_tiles
    pos = jnp.where(in_x, i % x_per_seq, (i - n_x_tiles) % c_per_seq)
    per_seq = jnp.where(in_x, x_per_seq, c_per_seq)
    keep_prev = (pos > 0).astype(F32)
    keep_next = (pos < per_seq - 1).astype(F32)

    def pre(a, b):
        a = a.astype(F32)
        b = b.astype(F32)
        return a * _sigmoid(b) if conformer else a * b

    for c in range(nchunk):
        sl = slice(c * LANES, (c + 1) * LANES)
        buf[c, 0:HALO_ROWS, :] = pre(ap[:, sl], bp[:, sl]) * keep_prev
        buf[c, HALO_ROWS:HALO_ROWS + tp, :] = pre(am[:, sl], bm[:, sl])
        buf[c, HALO_ROWS + tp:2 * HALO_ROWS + tp, :] = pre(an[:, sl], bn[:, sl]) * keep_next

    rc = min(TILES["conv_rows"], tp)
    base = HALO_ROWS - width // 2

    def chunk(c, carry):
        wts = w_ref[c]
        for r0 in range(0, tp, rc):
            acc = jnp.zeros((rc, LANES), F32)
            for k in range(width):
                acc = acc + wts[k:k + 1, :] * buf[c, pl.ds(base + r0 + k, rc), :]
            ybuf[c, pl.ds(r0, rc), :] = acc
        return carry

    lax.fori_loop(0, nchunk, chunk, 0)

    if conformer:
        n_ch = nchunk * LANES
        tot = jnp.zeros((tp, 1), F32)
        for c in range(nchunk):
            tot = tot + jnp.sum(ybuf[c], axis=-1, keepdims=True)
        mu = tot * (1.0 / n_ch)
        tot2 = jnp.zeros((tp, 1), F32)
        for c in range(nchunk):
            dlt = ybuf[c] - mu
            tot2 = tot2 + jnp.sum(dlt * dlt, axis=-1, keepdims=True)
        inv = lax.rsqrt(tot2 * (1.0 / n_ch) + EPS)
        for c in range(nchunk):
            sl = slice(c * LANES, (c + 1) * LANES)
            y = (ybuf[c] - mu) * inv * lng_ref[:, sl] + lnb_ref[:, sl]
            o_ref[:, sl] = (y * _sigmoid(y)).astype(o_ref.dtype)
    else:
        for c in range(nchunk):
            sl = slice(c * LANES, (c + 1) * LANES)
            o_ref[:, sl] = (gate_ref[:, sl].astype(F32) * ybuf[c]).astype(o_ref.dtype)


def depthwise_branch(u, conv_w, *, col_a, col_b, col_gate, ln, n_rows, geom):
    batch, seq, ctx = geom
    width, ch = conv_w.shape
    conformer = ln is not None
    tp = min(TILES["conv"], seq, ctx)
    nchunk = ch // LANES
    hb = tp // HALO_ROWS
    last_halo = u.shape[0] // HALO_ROWS - 1
    wpad = -(-width // 8) * 8
    w3 = jnp.pad(conv_w.astype(F32), ((0, wpad - width), (0, 0))).reshape(wpad, nchunk, LANES).transpose(1, 0, 2)

    def prev_map(col):
        return lambda i: (jnp.maximum(i * hb - 1, 0), col)

    def main_map(col):
        return lambda i: (i, col)

    def next_map(col):
        return lambda i: (jnp.minimum((i + 1) * hb, last_halo), col)

    in_specs, args = [], []
    if not conformer:
        in_specs.append(pl.BlockSpec((tp, ch), main_map(col_gate)))
        args.append(u)
    for rows, mk in ((HALO_ROWS, prev_map), (tp, main_map), (HALO_ROWS, next_map)):
        for col in (col_a, col_b):
            in_specs.append(pl.BlockSpec((rows, ch), mk(col)))
            args.append(u)
    in_specs.append(pl.BlockSpec((nchunk, wpad, LANES), lambda i: (0, 0, 0)))
    args.append(w3)
    if conformer:
        for p in ln:
            in_specs.append(pl.BlockSpec((1, ch), lambda i: (0, 0)))
            args.append(p.reshape(1, ch).astype(F32))

    kern = functools.partial(
        _conv_kernel, width=width, conformer=conformer, tp=tp, n_x_tiles=batch * seq // tp,
        x_per_seq=seq // tp, c_per_seq=ctx // tp)
    return pl.pallas_call(
        kern,
        out_shape=jax.ShapeDtypeStruct((n_rows, ch), BF16),
        grid=(n_rows // tp,),
        in_specs=in_specs,
        out_specs=pl.BlockSpec((tp, ch), lambda i: (i, 0)),
        scratch_shapes=[pltpu.VMEM((nchunk, tp + 2 * HALO_ROWS, LANES), F32),
                        pltpu.VMEM((nchunk, tp, LANES), F32)],
        compiler_params=_cparams(("parallel",)),
        name="conformer_conv" if conformer else "short_conv",
    )(*args)


def _merge_kernel(ha_ref, obx_ref, obc_ref, hc_ref, wa_ref, wb_ref, wc_ref, ga_ref, gb_ref, gc_ref, o_ref,
                  wa_bf, wb_bf, wc_bf, *, n_x_tiles):
    m_idx = pl.program_id(1)

    @pl.when(m_idx == 0)
    def _():
        wa_bf[...] = wa_ref[...].astype(BF16)
        wb_bf[...] = wb_ref[...].astype(BF16)
        wc_bf[...] = wc_ref[...].astype(BF16)

    ob = jnp.where(m_idx < n_x_tiles, obx_ref[...], obc_ref[...])
    ya = jnp.dot(ha_ref[...], wa_bf[...], preferred_element_type=F32)
    yb = jnp.dot(ob, wb_bf[...], preferred_element_type=F32)
    yc = jnp.dot(hc_ref[...], wc_bf[...], preferred_element_type=F32)
    m = (_sigmoid(ga_ref[...].astype(F32)) * ya + _sigmoid(gb_ref[...].astype(F32)) * yb
         + _sigmoid(gc_ref[...].astype(F32)) * yc)
    o_ref[...] = m.astype(o_ref.dtype)


def branch_merge(ha, ob_x, ob_c, hc, wa, wb, wc, u, *, layer, n_rows, geom):
    batch, seq, _ = geom
    d = wa.shape[2]
    tm = min(TILES["row"], seq)
    tn = min(TILES["col"], d)
    ncol = d // tn
    n_x = batch * seq // tm
    row_spec = lambda a: pl.BlockSpec((tm, a.shape[1]), lambda j, m: (m, 0))
    gate_spec = lambda k: pl.BlockSpec((tm, tn), lambda j, m: (m, k * ncol + j))
    kern = functools.partial(_merge_kernel, n_x_tiles=n_x)
    return pl.pallas_call(
        kern,
        out_shape=jax.ShapeDtypeStruct((n_rows, d), BF16),
        grid=(ncol, n_rows // tm),
        in_specs=[row_spec(ha),
                  pl.BlockSpec((tm, ob_x.shape[1]), lambda j, m: (jnp.minimum(m, n_x - 1), 0)),
                  pl.BlockSpec((tm, ob_c.shape[1]), lambda j, m: (jnp.maximum(m - n_x, 0), 0)),
                  row_spec(hc), _w_spec(wa, layer, tn), _w_spec(wb, layer, tn), _w_spec(wc, layer, tn),
                  gate_spec(0), gate_spec(1), gate_spec(2)],
        out_specs=pl.BlockSpec((tm, tn), lambda j, m: (m, j)),
        scratch_shapes=[pltpu.VMEM((w.shape[1], tn), BF16) for w in (wa, wb, wc)],
        compiler_params=_cparams(("parallel", "arbitrary")),
        name="branch_merge",
    )(ha, ob_x, ob_c, hc, wa, wb, wc, u, u, u)


def _outproj_kernel(*refs, n_x_tiles):
    (m_ref, w_ref, gate_ref), x_refs, (o_ref, w_bf) = refs[:3], refs[3:-2], refs[-2:]
    m_idx = pl.program_id(1)

    @pl.when(m_idx == 0)
    def _():
        w_bf[...] = w_ref[...].astype(BF16)

    y = jnp.dot(m_ref[...], w_bf[...], preferred_element_type=F32)
    o_ref[...] = _stream_tile(x_refs, m_idx, n_x_tiles) + gate_ref[...] * y


def out_proj_residual(mrg, w_o, stream, mod3, *, layer, gate_idx, n_rows, geom):
    batch, seq, _ = geom
    d = w_o.shape[2]
    tm = min(TILES["row"], seq)
    tn = min(TILES["col"], d)
    ncol = d // tn
    n_x = batch * seq // tm
    mrow = _mod_row_fn(n_x, seq // tm, batch, layer)
    x_specs, x_args = _stream_specs(stream, tm, n_x, tn=tn, grid_rank=2)
    return pl.pallas_call(
        functools.partial(_outproj_kernel, n_x_tiles=n_x),
        out_shape=jax.ShapeDtypeStruct((n_rows, d), F32),
        grid=(ncol, n_rows // tm),
        in_specs=[
            pl.BlockSpec((tm, mrg.shape[1]), lambda j, m: (m, 0)),
            _w_spec(w_o, layer, tn),
            pl.BlockSpec((None, 1, tn), lambda j, m: (mrow(m), 0, gate_idx * ncol + j)),
        ] + x_specs,
        out_specs=pl.BlockSpec((tm, tn), lambda j, m: (m, j)),
        scratch_shapes=[pltpu.VMEM((w_o.shape[1], tn), BF16)],
        compiler_params=_cparams(("parallel", "arbitrary")),
        name="out_proj_residual",
    )(mrg, w_o, mod3, *x_args)


def _pack_halves(v):
    n = v.shape[1] // 2
    lo = lax.bitcast_convert_type(v[:, :n].astype(BF16).astype(F32), jnp.int32)
    hi = lax.bitcast_convert_type(v[:, n:].astype(BF16).astype(F32), jnp.int32)
    return lax.shift_right_logical(lo, jnp.int32(16)) | (hi & jnp.int32(-65536))


def _unpack_halves(p):
    lo = lax.bitcast_convert_type(lax.shift_left(p, jnp.int32(16)), F32)
    hi = lax.bitcast_convert_type(p & jnp.int32(-65536), F32)
    return lo, hi


def _router_kernel(x_ref, g_ref, shift_ref, scale_ref, wr_ref, bias_ref, h_ref, ids_ref, wts_ref, cnt_ref,
                   *, per_group):
    h = _norm_mod(x_ref[...], g_ref[...], shift_ref[...], scale_ref[...])
    h_ref[...] = _pack_halves(h)
    h_hi = h.astype(BF16)
    h_lo = (h - h_hi.astype(F32)).astype(BF16)
    w = wr_ref[...]
    w_hi = w.astype(BF16)
    w_lo = (w - w_hi.astype(F32)).astype(BF16)
    logits = (jnp.dot(h_hi, w_hi, preferred_element_type=F32)
              + jnp.dot(h_hi, w_lo, preferred_element_type=F32)
              + jnp.dot(h_lo, w_hi, preferred_element_type=F32))
    lane_i = lax.broadcasted_iota(jnp.int32, (h.shape[0], LANES), 1)
    lane = lane_i.astype(F32)
    valid = lane_i < N_GROUPS
    neg = jnp.float32(-jnp.inf)
    score, sel = [], []
    for j in range(per_group):
        s = _sigmoid(logits[:, j * LANES:(j + 1) * LANES])
        score.append(s)
        sel.append(jnp.where(valid, s + bias_ref[j:j + 1, :], neg))
    assert per_group == 4
    hi1, lo1 = jnp.maximum(sel[0], sel[1]), jnp.minimum(sel[0], sel[1])
    hi2, lo2 = jnp.maximum(sel[2], sel[3]), jnp.minimum(sel[2], sel[3])
    top1 = jnp.maximum(hi1, hi2)
    top2 = jnp.maximum(jnp.minimum(hi1, hi2), jnp.maximum(lo1, lo2))
    gscore = jnp.where(valid, top1 + top2, neg)
    gmax = jnp.max(gscore, axis=-1, keepdims=True)
    grp = jnp.min(jnp.where(gscore == gmax, lane, float(LANES)), axis=-1, keepdims=True)
    pick = lane == grp
    zero = jnp.float32(0.0)
    sel_g = [jnp.sum(jnp.where(pick, s, zero), axis=-1, keepdims=True) for s in sel]
    sc_g = [jnp.sum(jnp.where(pick, s, zero), axis=-1, keepdims=True) for s in score]
    best, best_i, best_s = sel_g[0], jnp.zeros_like(grp), sc_g[0]
    for j in range(1, per_group):
        better = sel_g[j] > best
        best = jnp.where(better, sel_g[j], best)
        best_i = jnp.where(better, float(j), best_i)
        best_s = jnp.where(better, sc_g[j], best_s)
    second, second_i, second_s = jnp.full_like(best, neg), jnp.zeros_like(grp), jnp.zeros_like(best)
    for j in range(per_group):
        better = (sel_g[j] > second) & (best_i != float(j))
        second = jnp.where(better, sel_g[j], second)
        second_i = jnp.where(better, float(j), second_i)
        second_s = jnp.where(better, sc_g[j], second_s)
    tot = best_s + second_s
    e0 = grp * per_group + best_i
    e1 = grp * per_group + second_i
    @pl.when(pl.program_id(0) == 0)
    def _():
        cnt_ref[...] = jnp.zeros_like(cnt_ref)

    oh0 = lane == e0
    oh1 = lane == e1
    both = jnp.where(oh0 | oh1, jnp.float32(1.0), zero)
    tm = h.shape[0]
    earlier = (lax.broadcasted_iota(jnp.int32, (tm, tm), 1)
               < lax.broadcasted_iota(jnp.int32, (tm, tm), 0)).astype(BF16)
    before = jnp.dot(earlier, both.astype(BF16), preferred_element_type=F32) + cnt_ref[0:1, :]
    r0 = jnp.sum(jnp.where(oh0, before, zero), axis=-1, keepdims=True)
    r1 = jnp.sum(jnp.where(oh1, before, zero), axis=-1, keepdims=True)
    cnt_ref[...] = cnt_ref[...] + jnp.sum(both, axis=0, keepdims=True)
    ids = jnp.where(lane_i == 0, e0, jnp.where(lane_i == 1, e1, jnp.where(lane_i == 2, r0,
                    jnp.where(lane_i == 3, r1, zero))))
    ids_ref[...] = ids.astype(jnp.int32)
    wts_ref[...] = jnp.where(lane_i == 0, best_s / tot, jnp.where(lane_i == 1, second_s / tot, zero))


def router(xall, g, mod3, wr_pad, bias_pad, *, layer, shift_idx, n_rows, geom, per_group):
    batch, seq, _ = geom
    d = xall.shape[1]
    tm = min(TILES["router"], seq)
    mrow = _mod_row_fn(batch * seq // tm, seq // tm, batch, layer)
    kern = functools.partial(_router_kernel, per_group=per_group)
    return pl.pallas_call(
        kern,
        out_shape=(jax.ShapeDtypeStruct((n_rows, d // 2), jnp.int32),
                   jax.ShapeDtypeStruct((n_rows, LANES), jnp.int32),
                   jax.ShapeDtypeStruct((n_rows, LANES), F32),
                   jax.ShapeDtypeStruct((8, LANES), F32)),
        grid=(n_rows // tm,),
        in_specs=[
            pl.BlockSpec((tm, d), lambda m: (m, 0)),
            pl.BlockSpec((1, d), lambda m: (0, 0)),
            pl.BlockSpec((None, 1, d), lambda m: (mrow(m), 0, shift_idx)),
            pl.BlockSpec((None, 1, d), lambda m: (mrow(m), 0, shift_idx + 1)),
            pl.BlockSpec(wr_pad.shape, lambda m: (0, 0)),
            pl.BlockSpec(bias_pad.shape, lambda m: (0, 0)),
        ],
        out_specs=(pl.BlockSpec((tm, d // 2), lambda m: (m, 0)),
                   pl.BlockSpec((tm, LANES), lambda m: (m, 0)),
                   pl.BlockSpec((tm, LANES), lambda m: (m, 0)),
                   pl.BlockSpec((8, LANES), lambda m: (0, 0))),
        compiler_params=_cparams(("arbitrary",)),
        name="router",
    )(xall, g.reshape(1, d), mod3, mod3, wr_pad, bias_pad)


def _expert_kernel(be_ref, nxt_ref, nlive_ref,tok_cur, tok_next, dst_prev, dst_cur, h_hbm, win_hbm, wout_hbm, y_hbm,
                   xbuf0, xbuf1, ybuf0, ybuf1, win_f32, wout_f32, win_bf, wout_bf, gsem, ssem, wsem,
                   *, ff, n_blocks, layer):
    i = pl.program_id(0)
    blk = xbuf0.shape[0]
    xbufs, ybufs = (xbuf0, xbuf1), (ybuf0, ybuf1)

    def weight_copies(e):
        return (pltpu.make_async_copy(win_hbm.at[layer, e], win_f32, wsem.at[0]),
                pltpu.make_async_copy(wout_hbm.at[layer, e], wout_f32, wsem.at[1]))

    @pl.when(i == 0)
    def _():
        for cp in weight_copies(be_ref[0]):
            cp.start(priority=1)

    @pl.when((i == 0) | (be_ref[i] != be_ref[jnp.maximum(i - 1, 0)]))
    def _():
        for cp in weight_copies(be_ref[i]):
            cp.wait()
        win_bf[...] = win_f32[...].astype(BF16)
        wout_bf[...] = wout_f32[...].astype(BF16)
        for cp in weight_copies(nxt_ref[i]):
            cp.start(priority=1)

    def gather(tok_ref, p):
        for r in range(blk):
            pltpu.make_async_copy(h_hbm.at[pl.ds(tok_ref[0, r], 1)], xbufs[p].at[pl.ds(r, 1)],
                                  gsem.at[p]).start()

    def scatter(dst_ref, p):
        for r in range(blk):
            pltpu.make_async_copy(ybufs[p].at[pl.ds(r, 1)], y_hbm.at[pl.ds(dst_ref[0, r], 1)],
                                  ssem.at[p]).start()

    def wait_gather(p):
        pltpu.make_async_copy(h_hbm.at[pl.ds(0, blk)], xbufs[p], gsem.at[p]).wait()

    def wait_scatter(p):
        pltpu.make_async_copy(ybufs[p], y_hbm.at[pl.ds(0, blk)], ssem.at[p]).wait()

    @pl.when(i == 0)
    def _():
        ybuf1[...] = jnp.zeros_like(ybuf1)
        gather(tok_cur, 0)
        wait_gather(0)

    def step(p, live):
        if live:
            x_lo, x_hi = _unpack_halves(xbufs[p][...])
            x = jnp.concatenate([x_lo, x_hi], axis=1).astype(BF16)
            gu = jnp.dot(x, win_bf[...], preferred_element_type=F32)
            g = gu[:, :ff]
            act = (g * _sigmoid(g) * gu[:, ff:]).astype(BF16)
            y = jnp.dot(act, wout_bf[...], preferred_element_type=F32)
            gather(tok_next, 1 - p)
        scatter(dst_prev, 1 - p)
        pl.when(i >= 1)(lambda: wait_scatter(p))
        ybufs[p][...] = _pack_halves(y) if live else jnp.zeros_like(ybufs[p])
        if live:
            wait_gather(1 - p)

        @pl.when(i == n_blocks - 1)
        def _():
            scatter(dst_cur, p)
            wait_scatter(p)
            wait_scatter(1 - p)
            for cp in weight_copies(be_ref[i]):
                cp.wait()

    live = i < nlive_ref[0]
    for p in (0, 1):
        pl.when(live & (i % 2 == p))(functools.partial(step, p, True))
        pl.when(jnp.logical_not(live) & (i % 2 == p))(functools.partial(step, p, False))


def expert_ffn(h2, row_tok, row_dst, block_e, next_e, n_live, w_in, w_out, *, layer):
    dh = h2.shape[1]
    d = 2 * dh
    rows = row_tok.shape[0]
    blk = TILES["moe_block"]
    n_blocks = rows // blk
    ff = w_out.shape[2]
    tok3 = row_tok.reshape(n_blocks, 1, blk)
    spare = rows + jnp.arange(blk, dtype=jnp.int32)
    dst3 = jnp.concatenate([spare, row_dst]).reshape(n_blocks + 1, 1, blk)
    smem_blk = lambda f: pl.BlockSpec((None, 1, blk), f, memory_space=pltpu.SMEM)
    any_spec = pl.BlockSpec(memory_space=pl.ANY)
    kern = functools.partial(_expert_kernel, ff=ff, n_blocks=n_blocks, layer=layer)
    grid_spec = pltpu.PrefetchScalarGridSpec(
        num_scalar_prefetch=3,
        grid=(n_blocks,),
        in_specs=[
            smem_blk(lambda i, be, nx, nl: (i, 0, 0)),
            smem_blk(lambda i, be, nx, nl: (jnp.minimum(i + 1, n_blocks - 1), 0, 0)),
            smem_blk(lambda i, be, nx, nl: (i, 0, 0)),
            smem_blk(lambda i, be, nx, nl: (i + 1, 0, 0)),
            any_spec, any_spec, any_spec,
        ],
        out_specs=any_spec,
        scratch_shapes=[pltpu.VMEM((blk, dh), jnp.int32), pltpu.VMEM((blk, dh), jnp.int32),
                        pltpu.VMEM((blk, dh), jnp.int32), pltpu.VMEM((blk, dh), jnp.int32),
                        pltpu.VMEM((d, 2 * ff), F32), pltpu.VMEM((ff, d), F32),
                        pltpu.VMEM((d, 2 * ff), BF16), pltpu.VMEM((ff, d), BF16),
                        pltpu.SemaphoreType.DMA((2,)), pltpu.SemaphoreType.DMA((2,)),
                        pltpu.SemaphoreType.DMA((2,))],
    )
    return pl.pallas_call(
        kern,
        out_shape=jax.ShapeDtypeStruct((rows + blk, dh), jnp.int32),
        grid_spec=grid_spec,
        compiler_params=pltpu.CompilerParams(dimension_semantics=("arbitrary",),
                                             vmem_limit_bytes=TILES["vmem_expert"]),
        name="expert_ffn",
    )(block_e, next_e, n_live, tok3, tok3, dst3, dst3, h2, w_in, w_out)


def _combine_kernel(x_ref, y0_ref, y1_ref, w_ref, gate_ref, *rest, next_norm):
    w = w_ref[...]
    y0_lo, y0_hi = _unpack_halves(y0_ref[...])
    y1_lo, y1_hi = _unpack_halves(y1_ref[...])
    y = jnp.concatenate([w[:, 0:1] * y0_lo + w[:, 1:2] * y1_lo, w[:, 0:1] * y0_hi + w[:, 1:2] * y1_hi], axis=1)
    out = x_ref[...] + gate_ref[...] * y
    if next_norm:
        g_ref, shift_ref, scale_ref, o_ref, h_ref = rest
        h_ref[...] = _norm_mod(out, g_ref[...], shift_ref[...], scale_ref[...]).astype(h_ref.dtype)
    else:
        (o_ref,) = rest
    o_ref[...] = out


def moe_combine(xall, y_both, wts, mod3, *, layer, gate_idx, n_rows, geom, next_g=None):
    batch, seq, _ = geom
    d = xall.shape[1]
    tm = min(TILES["row"], seq)
    n_x = batch * seq // tm
    mrow = _mod_row_fn(n_x, seq // tm, batch, layer)
    in_specs = [
        pl.BlockSpec((tm, d), lambda m: (m, 0)),
        pl.BlockSpec((tm, d // 2), lambda m: (m, 0)),
        pl.BlockSpec((tm, d // 2), lambda m: (n_rows // tm + m, 0)),
        pl.BlockSpec((tm, LANES), lambda m: (m, 0)),
        pl.BlockSpec((None, 1, d), lambda m: (mrow(m), 0, gate_idx)),
    ]
    args = [xall, y_both, y_both, wts, mod3]
    out_shape = [jax.ShapeDtypeStruct((n_rows, d), F32)]
    out_specs = [pl.BlockSpec((tm, d), lambda m: (m, 0))]
    if next_g is not None:
        nrow = _mod_row_fn(n_x, seq // tm, batch, layer + 1)
        in_specs += [pl.BlockSpec((1, d), lambda m: (0, 0)),
                     pl.BlockSpec((None, 1, d), lambda m: (nrow(m), 0, 0)),
                     pl.BlockSpec((None, 1, d), lambda m: (nrow(m), 0, 1))]
        args += [next_g.reshape(1, d), mod3, mod3]
        out_shape.append(jax.ShapeDtypeStruct((n_rows, d), BF16))
        out_specs.append(pl.BlockSpec((tm, d), lambda m: (m, 0)))
    return pl.pallas_call(
        functools.partial(_combine_kernel, next_norm=next_g is not None),
        out_shape=tuple(out_shape),
        grid=(n_rows // tm,),
        in_specs=in_specs,
        out_specs=tuple(out_specs),
        compiler_params=_cparams(("parallel",)),
        name="moe_combine",
    )(*args)


def moe_dispatch(ids, counts, n_experts):
    blk = TILES["moe_block"]
    n = ids.shape[0]
    experts, rank = ids[:, :TOP_K], ids[:, TOP_K:2 * TOP_K]
    counts = counts[0, :n_experts].astype(jnp.int32)
    padded = (counts + blk - 1) // blk * blk
    pend = jnp.cumsum(padded)
    pstart = pend - padded
    eye = experts[:, :, None] == jnp.arange(n_experts, dtype=jnp.int32)[None, None, :]
    dest = jnp.sum(jnp.where(eye, pstart[None, None, :], 0), axis=-1) + rank
    n_blocks = -(-n * TOP_K // blk) + n_experts
    rows = n_blocks * blk
    slot = jnp.full((rows,), -1, jnp.int32).at[dest.T.reshape(-1)].set(jnp.arange(n * TOP_K, dtype=jnp.int32))
    block_start = jnp.arange(n_blocks, dtype=jnp.int32) * blk
    block_e = jnp.minimum(jnp.sum(block_start[:, None] >= pend[None, :], axis=1), n_experts - 1).astype(jnp.int32)
    pad_before = jnp.cumsum(padded - counts) - (padded - counts)
    row_e = jnp.repeat(block_e, blk)
    over = jnp.arange(rows, dtype=jnp.int32) - (pstart + counts)[row_e]
    pad = over >= 0
    row_dst = jnp.where(pad, n * TOP_K + pad_before[row_e] + over, slot)
    row_tok = jnp.where(pad, jnp.arange(rows, dtype=jnp.int32) % n, slot % n)
    e_ids = jnp.arange(n_experts, dtype=jnp.int32)
    present = jnp.any(block_e[:, None] == e_ids[None, :], axis=0)
    later = present[None, :] & (e_ids[None, :] > e_ids[:, None])
    next_present = jnp.where(jnp.any(later, axis=1), jnp.argmax(later, axis=1).astype(jnp.int32), e_ids)
    next_e = next_present[block_e]
    n_live = (pend[-1:] // blk).astype(jnp.int32)
    return row_tok, row_dst, block_e, next_e, n_live


def _rope_tables(seq, tm):
    inv = ROPE_THETA ** (-jnp.arange(0, AXIS_DIM, 2, dtype=F32) / AXIS_DIM)
    t = jnp.arange(seq)
    ang_r = (t // GRID_W).astype(F32)[:, None] * inv
    ang_c = (t % GRID_W).astype(F32)[:, None] * inv
    cos = jnp.concatenate([jnp.cos(ang_r), jnp.cos(ang_r), jnp.cos(ang_c), jnp.cos(ang_c)], axis=-1)
    sin = jnp.concatenate([-jnp.sin(ang_r), jnp.sin(ang_r), -jnp.sin(ang_c), jnp.sin(ang_c)], axis=-1)
    pad = LANES - ROPE_DIM
    cos = jnp.pad(cos, ((0, 0), (0, pad)))
    sin = jnp.pad(sin, ((0, 0), (0, pad)))
    ident_c = jnp.pad(jnp.ones((tm, ROPE_DIM), F32), ((0, 0), (0, pad)))
    return (jnp.concatenate([ident_c, cos], axis=0),
            jnp.concatenate([jnp.zeros((tm, LANES), F32), sin], axis=0))


def _partner():
    j = jnp.arange(ROPE_DIM)
    half = AXIS_DIM // 2
    return jnp.where((j % AXIS_DIM) < half, j + half, j - half)


def _partner_matrix():
    rows = jnp.arange(LANES)[:, None]
    cols = jnp.arange(LANES)[None, :]
    partner = jnp.pad(_partner(), (0, LANES - ROPE_DIM), constant_values=-1)
    return (rows == partner[None, :]).astype(BF16) * (cols < ROPE_DIM).astype(BF16)


def _pad_lanes(a):
    return jnp.pad(a, [(0, 0)] * (a.ndim - 1) + [(0, LANES - a.shape[-1])])


def _norm_gain_rows(g):
    rope = g[NOPE_DIM:]
    rows = jnp.stack([g[:NOPE_DIM], _pad_lanes(rope), _pad_lanes(rope[_partner()])])
    return jnp.pad(rows, ((0, 8 - rows.shape[0]), (0, 0))).astype(F32)


def _layer_weights(l, w_uq, w_ukv, q_lora, kv_lora):
    heads = MLA_HEADS
    wq = w_uq[l].reshape(q_lora, heads, QK_DIM)
    wq_n = wq[:, :, :NOPE_DIM].reshape(q_lora, heads * LANES)
    wq_r = _pad_lanes(wq[:, :, NOPE_DIM:]).reshape(q_lora, heads * LANES)
    wq_s = _pad_lanes(wq[:, :, NOPE_DIM:][:, :, _partner()]).reshape(q_lora, heads * LANES)
    wq_all = jnp.concatenate([wq_n, wq_r, wq_s], axis=1).astype(BF16)
    wkv = w_ukv[l].reshape(kv_lora, heads, NOPE_DIM + V_DIM)
    wkv_all = jnp.concatenate([wkv[:, :, :NOPE_DIM].reshape(kv_lora, heads * NOPE_DIM),
                               wkv[:, :, NOPE_DIM:].reshape(kv_lora, heads * V_DIM)], axis=1).astype(BF16)
    return wq_all, wkv_all


def _router_layout(w_router, router_bias):
    d, n_experts = w_router.shape
    per_group = n_experts // N_GROUPS
    wr = w_router.reshape(d, N_GROUPS, per_group).transpose(0, 2, 1)
    wr_pad = _pad_lanes(wr).reshape(d, per_group * LANES).astype(F32)
    bias = _pad_lanes(router_bias.reshape(N_GROUPS, per_group).T).astype(F32)
    bias = jnp.pad(bias, ((0, 8 - per_group), (0, 0)))
    return wr_pad, bias, per_group


def kernel(x, c, ctx, c_ctx, w_mod, b_mod, g_norm1, g_norm2, w_in, conv_a, ln_a_g, ln_a_b, w_a_out, g_cq, g_ckv, w_uq, w_ukv, g_q, g_k, w_b_out, conv_c, w_c_out, w_o, w_router, router_bias, w_e_in, w_e_out):
    batch, seq, d = x.shape
    n_ctx = ctx.shape[1]
    depth = w_mod.shape[0]
    geom = (batch, seq, n_ctx)
    t_x = batch * seq
    t_all = t_x + batch * n_ctx
    q_lora, kv_lora = g_cq.shape[1], g_ckv.shape[1]
    ch_a, ch_c = conv_a.shape[2], conv_c.shape[2]
    n_experts = w_router.shape[1]
    assert batch + 1 <= MOD_ROWS and t_x % n_ctx == 0 and ch_a == ch_c and (3 * d) % ch_a == 0

    stream = (x.reshape(t_x, d), ctx.reshape(batch * n_ctx, d))
    cvec = jnp.concatenate([c, c_ctx[None, :], jnp.zeros((MOD_ROWS - batch - 1, d), F32)], axis=0)
    mod3 = modulation(cvec, w_mod, b_mod).reshape(depth * MOD_ROWS, 1, N_MOD * d)

    cos_tab, sin_tab = _rope_tables(seq, _mla_tile(geom))
    wr_pad, bias_pad, per_group = _router_layout(w_router, router_bias)
    col0 = 3 * d // ch_a
    off_b = 3 * d + 2 * ch_a
    off_c = off_b + q_lora + kv_lora + ROPE_DIM
    w_in2 = w_in.reshape(depth * d, w_in.shape[2])

    for l in range(depth):
        last = l == depth - 1
        rows_main = t_x if last else t_all
        wq_all, wkv_all = _layer_weights(l, w_uq, w_ukv, q_lora, kv_lora)

        if l == 0:
            h = norm_modulate(stream, g_norm1[l], mod3, layer=l, shift_idx=0, n_rows=t_all, geom=geom)
        u = project(h, w_in2, n_cols=off_b, n_rows=rows_main, row_block=l)
        uc = project_window(h, w_in2, col0=off_c, n_cols=3 * ch_c, n_rows=rows_main, row_block=l)
        ub = project_window(h, w_in2, col0=off_b, n_cols=q_lora + kv_lora, n_rows=t_all, row_block=l,
                            keep_tail=True)
        qt, k, vt = mla_project(ub, g_cq[l], g_ckv[l], wq_all, wkv_all, _norm_gain_rows(g_q[l]),
                                _norm_gain_rows(g_k[l]), cos_tab, sin_tab, geom=geom)
        ob_x = attention_latent(qt, k, vt, geom=geom)
        ob_c = ob_x if last else attention_context(qt, k, vt, geom=geom)
        ha = depthwise_branch(u, conv_a[l], col_a=col0, col_b=col0 + 1, col_gate=None,
                              ln=(ln_a_g[l], ln_a_b[l]), n_rows=rows_main, geom=geom)
        hc = depthwise_branch(uc, conv_c[l], col_a=1, col_b=2, col_gate=0,
                              ln=None, n_rows=rows_main, geom=geom)
        mrg = branch_merge(ha, ob_x, ob_c, hc, w_a_out, w_b_out, w_c_out, u, layer=l,
                           n_rows=rows_main, geom=geom)
        x1 = out_proj_residual(mrg, w_o, stream, mod3, layer=l, gate_idx=2, n_rows=rows_main, geom=geom)

        h2, ids, wts, counts = router(x1, g_norm2[l], mod3, wr_pad, bias_pad, layer=l, shift_idx=3,
                                      n_rows=rows_main, geom=geom, per_group=per_group)
        row_tok, row_dst, block_e, next_e, n_live = moe_dispatch(ids, counts, n_experts)
        y_both = expert_ffn(h2, row_tok, row_dst, block_e, next_e, n_live, w_e_in, w_e_out, layer=l)
        if last:
            (stream,) = moe_combine(x1, y_both, wts, mod3, layer=l, gate_idx=5, n_rows=rows_main, geom=geom)
        else:
            stream, h = moe_combine(x1, y_both, wts, mod3, layer=l, gate_idx=5, n_rows=rows_main,
                                    geom=geom, next_g=g_norm1[l + 1])

    return stream.reshape(batch, seq, d)
```

```python
import functools
import math

import jax
import jax.numpy as jnp
from jax import lax
from jax.experimental import pallas as pl
from jax.experimental.pallas import tpu as pltpu

F32 = jnp.float32
BF16 = jnp.bfloat16

GRID_W = 64
MLA_HEADS = 8
NOPE_DIM = 128
ROPE_DIM = 64
V_DIM = 128
QK_DIM = NOPE_DIM + ROPE_DIM
AXIS_DIM = ROPE_DIM // 2
ROPE_THETA = 10000.0
N_GROUPS = 8
TOP_K = 2
N_MOD = 6
EPS = 1e-6

LANES = 128
HEAD_SLOT = 2 * LANES
HALO_ROWS = 16
MOD_ROWS = 8

TILES = dict(
    row=512,
    proj_row=1024,
    col=1024,
    mod_col=1024,
    mla=512,
    tq=512,
    tq_sub=512,
    attn_keys=512,
    conv=256,
    conv_rows=64,
    router=512,
    moe_block=256,
    vmem=56 * 1024 * 1024,
    vmem_expert=60 * 1024 * 1024,
)


def _cparams(sem):
    return pltpu.CompilerParams(dimension_semantics=sem, vmem_limit_bytes=TILES["vmem"])


def _sigmoid(x):
    return 1.0 / (1.0 + jnp.exp(-x))


def _mod_kernel(c_ref, w_ref, b_ref, o_ref):
    c = c_ref[...]
    s = (c * _sigmoid(c)).astype(BF16)
    o_ref[...] = jnp.dot(s, w_ref[...].astype(BF16), preferred_element_type=F32) + b_ref[...]


def modulation(cvec, w_mod, b_mod):
    n_layers, d, n = w_mod.shape
    tn = min(TILES["mod_col"], n)
    return pl.pallas_call(
        _mod_kernel,
        out_shape=jax.ShapeDtypeStruct((n_layers, MOD_ROWS, n), F32),
        grid=(n_layers, n // tn),
        in_specs=[
            pl.BlockSpec((MOD_ROWS, d), lambda l, j: (0, 0)),
            pl.BlockSpec((None, d, tn), lambda l, j: (l, 0, j)),
            pl.BlockSpec((None, 1, tn), lambda l, j: (l, 0, j)),
        ],
        out_specs=pl.BlockSpec((None, MOD_ROWS, tn), lambda l, j: (l, 0, j)),
        compiler_params=_cparams(("parallel", "parallel")),
        name="modulation",
    )(cvec, w_mod, b_mod.reshape(n_layers, 1, n))


def _norm_mod(x, g, shift, scale):
    ms = jnp.mean(x * x, axis=-1, keepdims=True)
    return x * lax.rsqrt(ms + EPS) * g * (1.0 + scale) + shift


def _stream_specs(stream, tm, n_x_tiles, tn=None, grid_rank=1):
    arrays = stream if isinstance(stream, tuple) else (stream,)
    width = arrays[0].shape[1] if tn is None else tn

    def spec(row_fn):
        if grid_rank == 1:
            return pl.BlockSpec((tm, width), lambda m: (row_fn(m), 0))
        return pl.BlockSpec((tm, width), lambda j, m: (row_fn(m), j))

    if len(arrays) == 1:
        return [spec(lambda m: m)], list(arrays)
    return [spec(lambda m: jnp.minimum(m, n_x_tiles - 1)),
            spec(lambda m: jnp.maximum(m - n_x_tiles, 0))], list(arrays)


def _stream_tile(refs, m, n_x_tiles):
    if len(refs) == 1:
        return refs[0][...]
    return jnp.where(m < n_x_tiles, refs[0][...], refs[1][...])


def _norm_kernel(*refs, n_x_tiles):
    x_refs, (g_ref, shift_ref, scale_ref, h_ref) = refs[:-4], refs[-4:]
    x = _stream_tile(x_refs, pl.program_id(0), n_x_tiles)
    h_ref[...] = _norm_mod(x, g_ref[...], shift_ref[...], scale_ref[...]).astype(h_ref.dtype)


def _mod_row_fn(n_x_tiles, tiles_per_seq, batch, layer):
    def f(m):
        return layer * MOD_ROWS + jnp.where(m < n_x_tiles, m // tiles_per_seq, batch)
    return f


def norm_modulate(stream, g, mod3, *, layer, shift_idx, n_rows, geom):
    batch, seq, _ = geom
    d = g.shape[0]
    tm = min(TILES["row"], seq)
    n_x = batch * seq // tm
    mrow = _mod_row_fn(n_x, seq // tm, batch, layer)
    x_specs, x_args = _stream_specs(stream, tm, n_x)
    return pl.pallas_call(
        functools.partial(_norm_kernel, n_x_tiles=n_x),
        out_shape=jax.ShapeDtypeStruct((n_rows, d), BF16),
        grid=(n_rows // tm,),
        in_specs=x_specs + [
            pl.BlockSpec((1, d), lambda m: (0, 0)),
            pl.BlockSpec((None, 1, d), lambda m: (mrow(m), 0, shift_idx)),
            pl.BlockSpec((None, 1, d), lambda m: (mrow(m), 0, shift_idx + 1)),
        ],
        out_specs=pl.BlockSpec((tm, d), lambda m: (m, 0)),
        compiler_params=_cparams(("parallel",)),
        name="norm_modulate",
    )(*x_args, g.reshape(1, d), mod3, mod3)


def _proj_kernel(a_ref, w_ref, o_ref, w_bf):
    @pl.when(pl.program_id(1) == 0)
    def _():
        w_bf[...] = w_ref[...].astype(BF16)

    o_ref[...] = jnp.dot(a_ref[...], w_bf[...], preferred_element_type=F32).astype(o_ref.dtype)


def _w_spec(w, layer, tn, col0=0):
    if w.ndim == 3:
        return pl.BlockSpec((None, w.shape[1], tn), lambda j, m: (layer, 0, col0 + j))
    return pl.BlockSpec((w.shape[0], tn), lambda j, m: (0, col0 + j))


def project(a, w, *, n_cols, n_rows, row_block=0):
    k = a.shape[1]
    tm = min(TILES["proj_row"], n_rows)
    tn = min(TILES["col"], n_cols)
    if n_cols % tn:
        tn = n_cols
    return pl.pallas_call(
        _proj_kernel,
        out_shape=jax.ShapeDtypeStruct((n_rows, n_cols), BF16),
        grid=(n_cols // tn, n_rows // tm),
        in_specs=[pl.BlockSpec((tm, k), lambda j, m: (m, 0)),
                  pl.BlockSpec((k, tn), lambda j, m: (row_block, j))],
        out_specs=pl.BlockSpec((tm, tn), lambda j, m: (m, j)),
        scratch_shapes=[pltpu.VMEM((k, tn), BF16)],
        compiler_params=_cparams(("parallel", "arbitrary")),
        name="project",
    )(a, w)


def _proj_window_kernel(a_ref, w1_ref, w2_ref, o_ref, w_bf, *, shift):
    @pl.when(pl.program_id(1) == 0)
    def _():
        n1 = w1_ref.shape[1]
        rows = 256
        for r0 in range(0, w1_ref.shape[0], rows):
            cat = jnp.concatenate([w1_ref[r0:r0 + rows, :], w2_ref[r0:r0 + rows, :]], axis=1)
            if shift:
                cat = pltpu.roll(cat, cat.shape[1] - shift, axis=1)
            w_bf[r0:r0 + rows, :] = cat[:, :w_bf.shape[1]].astype(BF16)

    o_ref[...] = jnp.dot(a_ref[...], w_bf[...], preferred_element_type=F32).astype(o_ref.dtype)


def project_window(a, w, *, col0, n_cols, n_rows, row_block, keep_tail=False):
    k = a.shape[1]
    shift = col0 % LANES
    base = col0 - shift
    tm = min(TILES["proj_row"], n_rows)
    assert n_cols % LANES == 0 and not (keep_tail and shift)
    tn = n_cols if keep_tail else min(TILES["col"], n_cols)
    while n_cols % tn or base % tn:
        tn -= LANES
    assert tn == n_cols or not keep_tail
    out_w = tn + LANES if keep_tail else tn
    return pl.pallas_call(
        functools.partial(_proj_window_kernel, shift=shift),
        out_shape=jax.ShapeDtypeStruct((n_rows, n_cols // tn * out_w), BF16),
        grid=(n_cols // tn, n_rows // tm),
        in_specs=[pl.BlockSpec((tm, k), lambda j, m: (m, 0)),
                  pl.BlockSpec((k, tn), lambda j, m: (row_block, base // tn + j)),
                  pl.BlockSpec((k, LANES), lambda j, m: (row_block, (base + (j + 1) * tn) // LANES))],
        out_specs=pl.BlockSpec((tm, out_w), lambda j, m: (m, j)),
        scratch_shapes=[pltpu.VMEM((k, out_w), BF16)],
        compiler_params=_cparams(("parallel", "arbitrary")),
        name="project_window",
    )(a, w, w)


def _mla_kernel(ub_ref, gcq_ref, gckv_ref, wq_ref, wkv_ref, gq_ref, gk_ref, cos_ref, sin_ref, perm_ref,
                qt_ref, k_ref, vt_ref, *, q_lora, kv_lora, heads):
    hn = heads * LANES

    def rms(xv, g):
        return (xv * lax.rsqrt(jnp.mean(xv * xv, axis=-1, keepdims=True) + EPS) * g).astype(BF16)

    cqn = rms(ub_ref[:, 0:q_lora].astype(F32), gcq_ref[...])
    ckvn = rms(ub_ref[:, q_lora:q_lora + kv_lora].astype(F32), gckv_ref[...])
    off = q_lora + kv_lora
    rope_lane = lax.broadcasted_iota(jnp.int32, (ub_ref.shape[0], LANES), 1) < ROPE_DIM
    kpe_bf = jnp.where(rope_lane, ub_ref[:, off:off + LANES], jnp.zeros((), BF16))
    kpe = kpe_bf.astype(F32)
    kpes = jnp.dot(kpe_bf, perm_ref[...], preferred_element_type=F32)

    qall = jnp.dot(cqn, wq_ref[...], preferred_element_type=F32)
    kvall = jnp.dot(ckvn, wkv_ref[...], preferred_element_type=F32)

    cos = cos_ref[...]
    sin = sin_ref[...]
    gq_n, gq_r, gq_s = gq_ref[0:1, :], gq_ref[1:2, :], gq_ref[2:3, :]
    gk_n, gk_r, gk_s = gk_ref[0:1, :], gk_ref[1:2, :], gk_ref[2:3, :]
    q_cos, q_sin = gq_r * cos, gq_s * sin
    k_rope = kpe * (gk_r * cos) + kpes * (gk_s * sin)
    ssq_pe = jnp.sum(kpe * kpe, axis=-1, keepdims=True)
    q_scale = QK_DIM ** -0.5 * math.log2(math.e)

    for h in range(heads):
        lo = h * LANES
        qn = qall[:, lo:lo + LANES]
        qr = qall[:, hn + lo:hn + lo + LANES]
        qs = qall[:, 2 * hn + lo:2 * hn + lo + LANES]
        ssq = jnp.sum(qn * qn, axis=-1, keepdims=True) + jnp.sum(qr * qr, axis=-1, keepdims=True)
        inv = lax.rsqrt(ssq * (1.0 / QK_DIM) + EPS) * q_scale
        qt_ref[h * HEAD_SLOT:h * HEAD_SLOT + LANES, :] = (qn * gq_n * inv).T.astype(BF16)
        qt_ref[h * HEAD_SLOT + LANES:(h + 1) * HEAD_SLOT, :] = ((qr * q_cos + qs * q_sin) * inv).T.astype(BF16)

        kn = kvall[:, lo:lo + LANES]
        invk = lax.rsqrt((jnp.sum(kn * kn, axis=-1, keepdims=True) + ssq_pe) * (1.0 / QK_DIM) + EPS)
        k_ref[:, h * HEAD_SLOT:h * HEAD_SLOT + LANES] = (kn * gk_n * invk).astype(BF16)
        k_ref[:, h * HEAD_SLOT + LANES:(h + 1) * HEAD_SLOT] = (k_rope * invk).astype(BF16)

        vt_ref[h * V_DIM:(h + 1) * V_DIM, :] = kvall[:, hn + h * V_DIM:hn + (h + 1) * V_DIM].T.astype(BF16)


def _mla_tile(geom):
    batch, seq, ctx = geom
    tm = min(TILES["mla"], seq, batch * ctx)
    assert seq % tm == 0 and (batch * ctx) % tm == 0
    return tm


def mla_project(ub, g_cq, g_ckv, wq, wkv, gq3, gk3, cos_tab, sin_tab, *, geom):
    batch, seq, ctx = geom
    t_all = ub.shape[0]
    q_lora, kv_lora = g_cq.shape[0], g_ckv.shape[0]
    heads = MLA_HEADS
    tm = _mla_tile(geom)
    n_x = batch * seq // tm
    per_seq = seq // tm

    def pos(m):
        return jnp.where(m < n_x, 1 + m % per_seq, 0)

    kern = functools.partial(_mla_kernel, q_lora=q_lora, kv_lora=kv_lora, heads=heads)
    return pl.pallas_call(
        kern,
        out_shape=(
            jax.ShapeDtypeStruct((heads * HEAD_SLOT, t_all), BF16),
            jax.ShapeDtypeStruct((t_all, heads * HEAD_SLOT), BF16),
            jax.ShapeDtypeStruct((heads * V_DIM, t_all), BF16),
        ),
        grid=(t_all // tm,),
        in_specs=[
            pl.BlockSpec((tm, ub.shape[1]), lambda m: (m, 0)),
            pl.BlockSpec((1, q_lora), lambda m: (0, 0)),
            pl.BlockSpec((1, kv_lora), lambda m: (0, 0)),
            pl.BlockSpec(wq.shape, lambda m: (0, 0)),
            pl.BlockSpec(wkv.shape, lambda m: (0, 0)),
            pl.BlockSpec((8, LANES), lambda m: (0, 0)),
            pl.BlockSpec((8, LANES), lambda m: (0, 0)),
            pl.BlockSpec((tm, LANES), lambda m: (pos(m), 0)),
            pl.BlockSpec((tm, LANES), lambda m: (pos(m), 0)),
            pl.BlockSpec((LANES, LANES), lambda m: (0, 0)),
        ],
        out_specs=(
            pl.BlockSpec((heads * HEAD_SLOT, tm), lambda m: (0, m)),
            pl.BlockSpec((tm, heads * HEAD_SLOT), lambda m: (m, 0)),
            pl.BlockSpec((heads * V_DIM, tm), lambda m: (0, m)),
        ),
        compiler_params=_cparams(("parallel",)),
        name="mla_project",
    )(ub, g_cq.reshape(1, -1), g_ckv.reshape(1, -1), wq, wkv, gq3, gk3, cos_tab, sin_tab, _partner_matrix())


def _attn_kernel(*refs, sub, latent_keys):
    if latent_keys:
        qt_ref, kx_ref, kc_ref, vxt_ref, vct_ref, o_ref = refs
    else:
        qt_ref, kc_ref, vct_ref, o_ref = refs
    for j in range(qt_ref.shape[1] // sub):
        qt = qt_ref[:, j * sub:(j + 1) * sub]
        sc = jnp.dot(kc_ref[...], qt, preferred_element_type=F32)
        m = jnp.max(sc, axis=0, keepdims=True)
        if latent_keys:
            sx = jnp.dot(kx_ref[...], qt, preferred_element_type=F32)
            m = jnp.maximum(m, jnp.max(sx, axis=0, keepdims=True))
        pc = jnp.exp2(sc - m)
        denom = jnp.sum(pc, axis=0, keepdims=True)
        ot = jnp.dot(vct_ref[...], pc.astype(BF16), preferred_element_type=F32)
        if latent_keys:
            px = jnp.exp2(sx - m)
            denom = denom + jnp.sum(px, axis=0, keepdims=True)
            ot = ot + jnp.dot(vxt_ref[...], px.astype(BF16), preferred_element_type=F32)
        o_ref[j * sub:(j + 1) * sub, :] = (ot * (1.0 / denom)).T.astype(o_ref.dtype)


def _attn_pipelined_kernel(qt_ref, kx_ref, kc_ref, vxt_ref, vct_ref, o_ref, s0_ref, s1_ref, m0_ref, m1_ref,
                           *, n_q, chunk):
    i = pl.program_id(2)
    seq, ctx = kx_ref.shape[0], kc_ref.shape[0]
    tq = qt_ref.shape[1]
    chunks = [(kc_ref, vct_ref, 0, ctx, 0)]
    chunks += [(kx_ref, vxt_ref, c0, chunk, ctx + c0) for c0 in range(0, seq, chunk)]
    slots = ((s0_ref, m0_ref), (s1_ref, m1_ref))

    def run(new, old):
        if new is not None:
            s_new, m_new = slots[new]
            qt = qt_ref[...]
            m = jnp.full((1, tq), -jnp.inf, F32)
        if old is not None:
            s_old, m_old_ref = slots[old]
            m_old = m_old_ref[...]
            acc = jnp.zeros((V_DIM, tq), F32)
            den = jnp.zeros((1, tq), F32)
        for kref, vtref, c0, size, row0 in chunks:
            if new is not None:
                s = jnp.dot(kref[c0:c0 + size, :], qt, preferred_element_type=F32)
                s_new[row0:row0 + size, :] = s
                m = jnp.maximum(m, jnp.max(s, axis=0, keepdims=True))
            if old is not None:
                p = jnp.exp2(s_old[row0:row0 + size, :] - m_old)
                den = den + jnp.sum(p, axis=0, keepdims=True)
                acc = acc + jnp.dot(vtref[:, c0:c0 + size], p.astype(BF16), preferred_element_type=F32)
        if new is not None:
            m_new[...] = m
        if old is not None:
            o_ref[...] = (acc * (1.0 / den)).T.astype(o_ref.dtype)

    pl.when(i == 0)(lambda: run(0, None))
    pl.when((i > 0) & (i < n_q) & (i % 2 == 1))(lambda: run(1, 0))
    pl.when((i > 0) & (i < n_q) & (i % 2 == 0))(lambda: run(0, 1))
    pl.when(i == n_q)(lambda: run(None, (n_q - 1) % 2))


def attention_latent(qt, k, vt, *, geom):
    batch, seq, ctx = geom
    heads = MLA_HEADS
    tq = min(TILES["tq"], seq)
    n_q = seq // tq
    ctx_blk0 = batch * seq // ctx
    kern = functools.partial(_attn_pipelined_kernel, n_q=n_q, chunk=min(TILES["attn_keys"], seq))
    return pl.pallas_call(
        kern,
        out_shape=jax.ShapeDtypeStruct((batch * seq, heads * V_DIM), BF16),
        grid=(batch, heads, n_q + 1),
        in_specs=[
            pl.BlockSpec((HEAD_SLOT, tq), lambda b, h, i: (h, b * n_q + jnp.minimum(i, n_q - 1))),
            pl.BlockSpec((seq, HEAD_SLOT), lambda b, h, i: (b, h)),
            pl.BlockSpec((ctx, HEAD_SLOT), lambda b, h, i: (ctx_blk0 + b, h)),
            pl.BlockSpec((V_DIM, seq), lambda b, h, i: (h, b)),
            pl.BlockSpec((V_DIM, ctx), lambda b, h, i: (h, ctx_blk0 + b)),
        ],
        out_specs=pl.BlockSpec((tq, V_DIM), lambda b, h, i: (b * n_q + jnp.maximum(i - 1, 0), h)),
        scratch_shapes=[pltpu.VMEM((ctx + seq, tq), F32), pltpu.VMEM((ctx + seq, tq), F32),
                        pltpu.VMEM((1, tq), F32), pltpu.VMEM((1, tq), F32)],
        compiler_params=_cparams(("parallel", "parallel", "arbitrary")),
        name="attention_latent",
    )(qt, k, k, vt, vt)


def attention_context(qt, k, vt, *, geom):
    batch, seq, ctx = geom
    heads = MLA_HEADS
    ctx_blk0 = batch * seq // ctx
    kern = functools.partial(_attn_kernel, sub=min(TILES["tq_sub"], ctx), latent_keys=False)
    return pl.pallas_call(
        kern,
        out_shape=jax.ShapeDtypeStruct((batch * ctx, heads * V_DIM), BF16),
        grid=(batch, heads),
        in_specs=[
            pl.BlockSpec((HEAD_SLOT, ctx), lambda b, h: (h, ctx_blk0 + b)),
            pl.BlockSpec((ctx, HEAD_SLOT), lambda b, h: (ctx_blk0 + b, h)),
            pl.BlockSpec((V_DIM, ctx), lambda b, h: (h, ctx_blk0 + b)),
        ],
        out_specs=pl.BlockSpec((ctx, V_DIM), lambda b, h: (b, h)),
        compiler_params=_cparams(("parallel", "parallel")),
        name="attention_context",
    )(qt, k, vt)


def _conv_kernel(*refs, width, conformer, tp, n_x_tiles, x_per_seq, c_per_seq):
    if conformer:
        (ap, bp, am, bm, an, bn, w_ref, lng_ref, lnb_ref, o_ref, buf, ybuf) = refs
        gate_ref = None
    else:
        (gate_ref, ap, bp, am, bm, an, bn, w_ref, o_ref, buf, ybuf) = refs
    nchunk = buf.shape[0]
    i = pl.program_id(0)
    in_x = i < n_x_tiles
    pos = jnp.where(in_x, i % x_per_seq, (i - n_x_tiles) % c_per_seq)
    per_seq = jnp.where(in_x, x_per_seq, c_per_seq)
    keep_prev = (pos > 0).astype(F32)
    keep_next = (pos < per_seq - 1).astype(F32)

    def pre(a, b):
        a = a.astype(F32)
        b = b.astype(F32)
        return a * _sigmoid(b) if conformer else a * b

    for c in range(nchunk):
        sl = slice(c * LANES, (c + 1) * LANES)
        buf[c, 0:HALO_ROWS, :] = pre(ap[:, sl], bp[:, sl]) * keep_prev
        buf[c, HALO_ROWS:HALO_ROWS + tp, :] = pre(am[:, sl], bm[:, sl])
        buf[c, HALO_ROWS + tp:2 * HALO_ROWS + tp, :] = pre(an[:, sl], bn[:, sl]) * keep_next

    rc = min(TILES["conv_rows"], tp)
    base = HALO_ROWS - width // 2

    def chunk(c, carry):
        wts = w_ref[c]
        for r0 in range(0, tp, rc):
            acc = jnp.zeros((rc, LANES), F32)
            for k in range(width):
                acc = acc + wts[k:k + 1, :] * buf[c, pl.ds(base + r0 + k, rc), :]
            ybuf[c, pl.ds(r0, rc), :] = acc
        return carry

    lax.fori_loop(0, nchunk, chunk, 0)

    if conformer:
        n_ch = nchunk * LANES
        tot = jnp.zeros((tp, 1), F32)
        for c in range(nchunk):
            tot = tot + jnp.sum(ybuf[c], axis=-1, keepdims=True)
        mu = tot * (1.0 / n_ch)
        tot2 = jnp.zeros((tp, 1), F32)
        for c in range(nchunk):
            dlt = ybuf[c] - mu
            tot2 = tot2 + jnp.sum(dlt * dlt, axis=-1, keepdims=True)
        inv = lax.rsqrt(tot2 * (1.0 / n_ch) + EPS)
        for c in range(nchunk):
            sl = slice(c * LANES, (c + 1) * LANES)
            y = (ybuf[c] - mu) * inv * lng_ref[:, sl] + lnb_ref[:, sl]
            o_ref[:, sl] = (y * _sigmoid(y)).astype(o_ref.dtype)
    else:
        for c in range(nchunk):
            sl = slice(c * LANES, (c + 1) * LANES)
            o_ref[:, sl] = (gate_ref[:, sl].astype(F32) * ybuf[c]).astype(o_ref.dtype)


def depthwise_branch(u, conv_w, *, col_a, col_b, col_gate, ln, n_rows, geom):
    batch, seq, ctx = geom
    width, ch = conv_w.shape
    conformer = ln is not None
    tp = min(TILES["conv"], seq, ctx)
    nchunk = ch // LANES
    hb = tp // HALO_ROWS
    last_halo = u.shape[0] // HALO_ROWS - 1
    wpad = -(-width // 8) * 8
    w3 = jnp.pad(conv_w.astype(F32), ((0, wpad - width), (0, 0))).reshape(wpad, nchunk, LANES).transpose(1, 0, 2)

    def prev_map(col):
        return lambda i: (jnp.maximum(i * hb - 1, 0), col)

    def main_map(col):
        return lambda i: (i, col)

    def next_map(col):
        return lambda i: (jnp.minimum((i + 1) * hb, last_halo), col)

    in_specs, args = [], []
    if not conformer:
        in_specs.append(pl.BlockSpec((tp, ch), main_map(col_gate)))
        args.append(u)
    for rows, mk in ((HALO_ROWS, prev_map), (tp, main_map), (HALO_ROWS, next_map)):
        for col in (col_a, col_b):
            in_specs.append(pl.BlockSpec((rows, ch), mk(col)))
            args.append(u)
    in_specs.append(pl.BlockSpec((nchunk, wpad, LANES), lambda i: (0, 0, 0)))
    args.append(w3)
    if conformer:
        for p in ln:
            in_specs.append(pl.BlockSpec((1, ch), lambda i: (0, 0)))
            args.append(p.reshape(1, ch).astype(F32))

    kern = functools.partial(
        _conv_kernel, width=width, conformer=conformer, tp=tp, n_x_tiles=batch * seq // tp,
        x_per_seq=seq // tp, c_per_seq=ctx // tp)
    return pl.pallas_call(
        kern,
        out_shape=jax.ShapeDtypeStruct((n_rows, ch), BF16),
        grid=(n_rows // tp,),
        in_specs=in_specs,
        out_specs=pl.BlockSpec((tp, ch), lambda i: (i, 0)),
        scratch_shapes=[pltpu.VMEM((nchunk, tp + 2 * HALO_ROWS, LANES), F32),
                        pltpu.VMEM((nchunk, tp, LANES), F32)],
        compiler_params=_cparams(("parallel",)),
        name="conformer_conv" if conformer else "short_conv",
    )(*args)


def _merge_kernel(ha_ref, obx_ref, obc_ref, hc_ref, wa_ref, wb_ref, wc_ref, ga_ref, gb_ref, gc_ref, o_ref,
                  wa_bf, wb_bf, wc_bf, *, n_x_tiles):
    m_idx = pl.program_id(1)

    @pl.when(m_idx == 0)
    def _():
        wa_bf[...] = wa_ref[...].astype(BF16)
        wb_bf[...] = wb_ref[...].astype(BF16)
        wc_bf[...] = wc_ref[...].astype(BF16)

    ob = jnp.where(m_idx < n_x_tiles, obx_ref[...], obc_ref[...])
    ya = jnp.dot(ha_ref[...], wa_bf[...], preferred_element_type=F32)
    yb = jnp.dot(ob, wb_bf[...], preferred_element_type=F32)
    yc = jnp.dot(hc_ref[...], wc_bf[...], preferred_element_type=F32)
    m = (_sigmoid(ga_ref[...].astype(F32)) * ya + _sigmoid(gb_ref[...].astype(F32)) * yb
         + _sigmoid(gc_ref[...].astype(F32)) * yc)
    o_ref[...] = m.astype(o_ref.dtype)


def branch_merge(ha, ob_x, ob_c, hc, wa, wb, wc, u, *, layer, n_rows, geom):
    batch, seq, _ = geom
    d = wa.shape[2]
    tm = min(TILES["row"], seq)
    tn = min(TILES["col"], d)
    ncol = d // tn
    n_x = batch * seq // tm
    row_spec = lambda a: pl.BlockSpec((tm, a.shape[1]), lambda j, m: (m, 0))
    gate_spec = lambda k: pl.BlockSpec((tm, tn), lambda j, m: (m, k * ncol + j))
    kern = functools.partial(_merge_kernel, n_x_tiles=n_x)
    return pl.pallas_call(
        kern,
        out_shape=jax.ShapeDtypeStruct((n_rows, d), BF16),
        grid=(ncol, n_rows // tm),
        in_specs=[row_spec(ha),
                  pl.BlockSpec((tm, ob_x.shape[1]), lambda j, m: (jnp.minimum(m, n_x - 1), 0)),
                  pl.BlockSpec((tm, ob_c.shape[1]), lambda j, m: (jnp.maximum(m - n_x, 0), 0)),
                  row_spec(hc), _w_spec(wa, layer, tn), _w_spec(wb, layer, tn), _w_spec(wc, layer, tn),
                  gate_spec(0), gate_spec(1), gate_spec(2)],
        out_specs=pl.BlockSpec((tm, tn), lambda j, m: (m, j)),
        scratch_shapes=[pltpu.VMEM((w.shape[1], tn), BF16) for w in (wa, wb, wc)],
        compiler_params=_cparams(("parallel", "arbitrary")),
        name="branch_merge",
    )(ha, ob_x, ob_c, hc, wa, wb, wc, u, u, u)


def _outproj_kernel(*refs, n_x_tiles):
    (m_ref, w_ref, gate_ref), x_refs, (o_ref, w_bf) = refs[:3], refs[3:-2], refs[-2:]
    m_idx = pl.program_id(1)

    @pl.when(m_idx == 0)
    def _():
        w_bf[...] = w_ref[...].astype(BF16)

    y = jnp.dot(m_ref[...], w_bf[...], preferred_element_type=F32)
    o_ref[...] = _stream_tile(x_refs, m_idx, n_x_tiles) + gate_ref[...] * y


def out_proj_residual(mrg, w_o, stream, mod3, *, layer, gate_idx, n_rows, geom):
    batch, seq, _ = geom
    d = w_o.shape[2]
    tm = min(TILES["row"], seq)
    tn = min(TILES["col"], d)
    ncol = d // tn
    n_x = batch * seq // tm
    mrow = _mod_row_fn(n_x, seq // tm, batch, layer)
    x_specs, x_args = _stream_specs(stream, tm, n_x, tn=tn, grid_rank=2)
    return pl.pallas_call(
        functools.partial(_outproj_kernel, n_x_tiles=n_x),
        out_shape=jax.ShapeDtypeStruct((n_rows, d), F32),
        grid=(ncol, n_rows // tm),
        in_specs=[
            pl.BlockSpec((tm, mrg.shape[1]), lambda j, m: (m, 0)),
            _w_spec(w_o, layer, tn),
            pl.BlockSpec((None, 1, tn), lambda j, m: (mrow(m), 0, gate_idx * ncol + j)),
        ] + x_specs,
        out_specs=pl.BlockSpec((tm, tn), lambda j, m: (m, j)),
        scratch_shapes=[pltpu.VMEM((w_o.shape[1], tn), BF16)],
        compiler_params=_cparams(("parallel", "arbitrary")),
        name="out_proj_residual",
    )(mrg, w_o, mod3, *x_args)


def _pack_halves(v):
    n = v.shape[1] // 2
    lo = lax.bitcast_convert_type(v[:, :n].astype(BF16).astype(F32), jnp.int32)
    hi = lax.bitcast_convert_type(v[:, n:].astype(BF16).astype(F32), jnp.int32)
    return lax.shift_right_logical(lo, jnp.int32(16)) | (hi & jnp.int32(-65536))


def _unpack_halves(p):
    lo = lax.bitcast_convert_type(lax.shift_left(p, jnp.int32(16)), F32)
    hi = lax.bitcast_convert_type(p & jnp.int32(-65536), F32)
    return lo, hi


def _router_kernel(x_ref, g_ref, shift_ref, scale_ref, wr_ref, bias_ref, h_ref, ids_ref, wts_ref, cnt_ref,
                   *, per_group):
    h = _norm_mod(x_ref[...], g_ref[...], shift_ref[...], scale_ref[...])
    h_ref[...] = _pack_halves(h)
    h_hi = h.astype(BF16)
    h_lo = (h - h_hi.astype(F32)).astype(BF16)
    w = wr_ref[...]
    w_hi = w.astype(BF16)
    w_lo = (w - w_hi.astype(F32)).astype(BF16)
    logits = (jnp.dot(h_hi, w_hi, preferred_element_type=F32)
              + jnp.dot(h_hi, w_lo, preferred_element_type=F32)
              + jnp.dot(h_lo, w_hi, preferred_element_type=F32))
    lane_i = lax.broadcasted_iota(jnp.int32, (h.shape[0], LANES), 1)
    lane = lane_i.astype(F32)
    valid = lane_i < N_GROUPS
    neg = jnp.float32(-jnp.inf)
    score, sel = [], []
    for j in range(per_group):
        s = _sigmoid(logits[:, j * LANES:(j + 1) * LANES])
        score.append(s)
        sel.append(jnp.where(valid, s + bias_ref[j:j + 1, :], neg))
    assert per_group == 4
    hi1, lo1 = jnp.maximum(sel[0], sel[1]), jnp.minimum(sel[0], sel[1])
    hi2, lo2 = jnp.maximum(sel[2], sel[3]), jnp.minimum(sel[2], sel[3])
    top1 = jnp.maximum(hi1, hi2)
    top2 = jnp.maximum(jnp.minimum(hi1, hi2), jnp.maximum(lo1, lo2))
    gscore = jnp.where(valid, top1 + top2, neg)
    gmax = jnp.max(gscore, axis=-1, keepdims=True)
    grp = jnp.min(jnp.where(gscore == gmax, lane, float(LANES)), axis=-1, keepdims=True)
    pick = lane == grp
    zero = jnp.float32(0.0)
    sel_g = [jnp.sum(jnp.where(pick, s, zero), axis=-1, keepdims=True) for s in sel]
    sc_g = [jnp.sum(jnp.where(pick, s, zero), axis=-1, keepdims=True) for s in score]
    best, best_i, best_s = sel_g[0], jnp.zeros_like(grp), sc_g[0]
    for j in range(1, per_group):
        better = sel_g[j] > best
        best = jnp.where(better, sel_g[j], best)
        best_i = jnp.where(better, float(j), best_i)
        best_s = jnp.where(better, sc_g[j], best_s)
    second, second_i, second_s = jnp.full_like(best, neg), jnp.zeros_like(grp), jnp.zeros_like(best)
    for j in range(per_group):
        better = (sel_g[j] > second) & (best_i != float(j))
        second = jnp.where(better, sel_g[j], second)
        second_i = jnp.where(better, float(j), second_i)
        second_s = jnp.where(better, sc_g[j], second_s)
    tot = best_s + second_s
    e0 = grp * per_group + best_i
    e1 = grp * per_group + second_i
    @pl.when(pl.program_id(0) == 0)
    def _():
        cnt_ref[...] = jnp.zeros_like(cnt_ref)

    oh0 = lane == e0
    oh1 = lane == e1
    both = jnp.where(oh0 | oh1, jnp.float32(1.0), zero)
    tm = h.shape[0]
    earlier = (lax.broadcasted_iota(jnp.int32, (tm, tm), 1)
               < lax.broadcasted_iota(jnp.int32, (tm, tm), 0)).astype(BF16)
    before = jnp.dot(earlier, both.astype(BF16), preferred_element_type=F32) + cnt_ref[0:1, :]
    r0 = jnp.sum(jnp.where(oh0, before, zero), axis=-1, keepdims=True)
    r1 = jnp.sum(jnp.where(oh1, before, zero), axis=-1, keepdims=True)
    cnt_ref[...] = cnt_ref[...] + jnp.sum(both, axis=0, keepdims=True)
    ids = jnp.where(lane_i == 0, e0, jnp.where(lane_i == 1, e1, jnp.where(lane_i == 2, r0,
                    jnp.where(lane_i == 3, r1, zero))))
    ids_ref[...] = ids.astype(jnp.int32)
    wts_ref[...] = jnp.where(lane_i == 0, best_s / tot, jnp.where(lane_i == 1, second_s / tot, zero))


def router(xall, g, mod3, wr_pad, bias_pad, *, layer, shift_idx, n_rows, geom, per_group):
    batch, seq, _ = geom
    d = xall.shape[1]
    tm = min(TILES["router"], seq)
    mrow = _mod_row_fn(batch * seq // tm, seq // tm, batch, layer)
    kern = functools.partial(_router_kernel, per_group=per_group)
    return pl.pallas_call(
        kern,
        out_shape=(jax.ShapeDtypeStruct((n_rows, d // 2), jnp.int32),
                   jax.ShapeDtypeStruct((n_rows, LANES), jnp.int32),
                   jax.ShapeDtypeStruct((n_rows, LANES), F32),
                   jax.ShapeDtypeStruct((8, LANES), F32)),
        grid=(n_rows // tm,),
        in_specs=[
            pl.BlockSpec((tm, d), lambda m: (m, 0)),
            pl.BlockSpec((1, d), lambda m: (0, 0)),
            pl.BlockSpec((None, 1, d), lambda m: (mrow(m), 0, shift_idx)),
            pl.BlockSpec((None, 1, d), lambda m: (mrow(m), 0, shift_idx + 1)),
            pl.BlockSpec(wr_pad.shape, lambda m: (0, 0)),
            pl.BlockSpec(bias_pad.shape, lambda m: (0, 0)),
        ],
        out_specs=(pl.BlockSpec((tm, d // 2), lambda m: (m, 0)),
                   pl.BlockSpec((tm, LANES), lambda m: (m, 0)),
                   pl.BlockSpec((tm, LANES), lambda m: (m, 0)),
                   pl.BlockSpec((8, LANES), lambda m: (0, 0))),
        compiler_params=_cparams(("arbitrary",)),
        name="router",
    )(xall, g.reshape(1, d), mod3, mod3, wr_pad, bias_pad)


def _expert_kernel(be_ref, nxt_ref, nlive_ref,tok_cur, tok_next, dst_prev, dst_cur, h_hbm, win_hbm, wout_hbm, y_hbm,
                   xbuf0, xbuf1, ybuf0, ybuf1, win_f32, wout_f32, win_bf, wout_bf, gsem, ssem, wsem,
                   *, ff, n_blocks, layer):
    i = pl.program_id(0)
    blk = xbuf0.shape[0]
    xbufs, ybufs = (xbuf0, xbuf1), (ybuf0, ybuf1)

    def weight_copies(e):
        return (pltpu.make_async_copy(win_hbm.at[layer, e], win_f32, wsem.at[0]),
                pltpu.make_async_copy(wout_hbm.at[layer, e], wout_f32, wsem.at[1]))

    @pl.when(i == 0)
    def _():
        for cp in weight_copies(be_ref[0]):
            cp.start(priority=1)

    @pl.when((i == 0) | (be_ref[i] != be_ref[jnp.maximum(i - 1, 0)]))
    def _():
        for cp in weight_copies(be_ref[i]):
            cp.wait()
        win_bf[...] = win_f32[...].astype(BF16)
        wout_bf[...] = wout_f32[...].astype(BF16)
        for cp in weight_copies(nxt_ref[i]):
            cp.start(priority=1)

    def gather(tok_ref, p):
        for r in range(blk):
            pltpu.make_async_copy(h_hbm.at[pl.ds(tok_ref[0, r], 1)], xbufs[p].at[pl.ds(r, 1)],
                                  gsem.at[p]).start()

    def scatter(dst_ref, p):
        for r in range(blk):
            pltpu.make_async_copy(ybufs[p].at[pl.ds(r, 1)], y_hbm.at[pl.ds(dst_ref[0, r], 1)],
                                  ssem.at[p]).start()

    def wait_gather(p):
        pltpu.make_async_copy(h_hbm.at[pl.ds(0, blk)], xbufs[p], gsem.at[p]).wait()

    def wait_scatter(p):
        pltpu.make_async_copy(ybufs[p], y_hbm.at[pl.ds(0, blk)], ssem.at[p]).wait()

    @pl.when(i == 0)
    def _():
        ybuf1[...] = jnp.zeros_like(ybuf1)
        gather(tok_cur, 0)
        wait_gather(0)

    def step(p, live):
        if live:
            x_lo, x_hi = _unpack_halves(xbufs[p][...])
            x = jnp.concatenate([x_lo, x_hi], axis=1).astype(BF16)
            gu = jnp.dot(x, win_bf[...], preferred_element_type=F32)
            g = gu[:, :ff]
            act = (g * _sigmoid(g) * gu[:, ff:]).astype(BF16)
            y = jnp.dot(act, wout_bf[...], preferred_element_type=F32)
            gather(tok_next, 1 - p)
        scatter(dst_prev, 1 - p)
        pl.when(i >= 1)(lambda: wait_scatter(p))
        ybufs[p][...] = _pack_halves(y) if live else jnp.zeros_like(ybufs[p])
        if live:
            wait_gather(1 - p)

        @pl.when(i == n_blocks - 1)
        def _():
            scatter(dst_cur, p)
            wait_scatter(p)
            wait_scatter(1 - p)
            for cp in weight_copies(be_ref[i]):
                cp.wait()

    live = i < nlive_ref[0]
    for p in (0, 1):
        pl.when(live & (i % 2 == p))(functools.partial(step, p, True))
        pl.when(jnp.logical_not(live) & (i % 2 == p))(functools.partial(step, p, False))


def expert_ffn(h2, row_tok, row_dst, block_e, next_e, n_live, w_in, w_out, *, layer):
    dh = h2.shape[1]
    d = 2 * dh
    rows = row_tok.shape[0]
    blk = TILES["moe_block"]
    n_blocks = rows // blk
    ff = w_out.shape[2]
    tok3 = row_tok.reshape(n_blocks, 1, blk)
    spare = rows + jnp.arange(blk, dtype=jnp.int32)
    dst3 = jnp.concatenate([spare, row_dst]).reshape(n_blocks + 1, 1, blk)
    smem_blk = lambda f: pl.BlockSpec((None, 1, blk), f, memory_space=pltpu.SMEM)
    any_spec = pl.BlockSpec(memory_space=pl.ANY)
    kern = functools.partial(_expert_kernel, ff=ff, n_blocks=n_blocks, layer=layer)
    grid_spec = pltpu.PrefetchScalarGridSpec(
        num_scalar_prefetch=3,
        grid=(n_blocks,),
        in_specs=[
            smem_blk(lambda i, be, nx, nl: (i, 0, 0)),
            smem_blk(lambda i, be, nx, nl: (jnp.minimum(i + 1, n_blocks - 1), 0, 0)),
            smem_blk(lambda i, be, nx, nl: (i, 0, 0)),
            smem_blk(lambda i, be, nx, nl: (i + 1, 0, 0)),
            any_spec, any_spec, any_spec,
        ],
        out_specs=any_spec,
        scratch_shapes=[pltpu.VMEM((blk, dh), jnp.int32), pltpu.VMEM((blk, dh), jnp.int32),
                        pltpu.VMEM((blk, dh), jnp.int32), pltpu.VMEM((blk, dh), jnp.int32),
                        pltpu.VMEM((d, 2 * ff), F32), pltpu.VMEM((ff, d), F32),
                        pltpu.VMEM((d, 2 * ff), BF16), pltpu.VMEM((ff, d), BF16),
                        pltpu.SemaphoreType.DMA((2,)), pltpu.SemaphoreType.DMA((2,)),
                        pltpu.SemaphoreType.DMA((2,))],
    )
    return pl.pallas_call(
        kern,
        out_shape=jax.ShapeDtypeStruct((rows + blk, dh), jnp.int32),
        grid_spec=grid_spec,
        compiler_params=pltpu.CompilerParams(dimension_semantics=("arbitrary",),
                                             vmem_limit_bytes=TILES["vmem_expert"]),
        name="expert_ffn",
    )(block_e, next_e, n_live, tok3, tok3, dst3, dst3, h2, w_in, w_out)


def _combine_kernel(x_ref, y0_ref, y1_ref, w_ref, gate_ref, *rest, next_norm):
    w = w_ref[...]
    y0_lo, y0_hi = _unpack_halves(y0_ref[...])
    y1_lo, y1_hi = _unpack_halves(y1_ref[...])
    y = jnp.concatenate([w[:, 0:1] * y0_lo + w[:, 1:2] * y1_lo, w[:, 0:1] * y0_hi + w[:, 1:2] * y1_hi], axis=1)
    out = x_ref[...] + gate_ref[...] * y
    if next_norm:
        g_ref, shift_ref, scale_ref, o_ref, h_ref = rest
        h_ref[...] = _norm_mod(out, g_ref[...], shift_ref[...], scale_ref[...]).astype(h_ref.dtype)
    else:
        (o_ref,) = rest
    o_ref[...] = out


def moe_combine(xall, y_both, wts, mod3, *, layer, gate_idx, n_rows, geom, next_g=None):
    batch, seq, _ = geom
    d = xall.shape[1]
    tm = min(TILES["row"], seq)
    n_x = batch * seq // tm
    mrow = _mod_row_fn(n_x, seq // tm, batch, layer)
    in_specs = [
        pl.BlockSpec((tm, d), lambda m: (m, 0)),
        pl.BlockSpec((tm, d // 2), lambda m: (m, 0)),
        pl.BlockSpec((tm, d // 2), lambda m: (n_rows // tm + m, 0)),
        pl.BlockSpec((tm, LANES), lambda m: (m, 0)),
        pl.BlockSpec((None, 1, d), lambda m: (mrow(m), 0, gate_idx)),
    ]
    args = [xall, y_both, y_both, wts, mod3]
    out_shape = [jax.ShapeDtypeStruct((n_rows, d), F32)]
    out_specs = [pl.BlockSpec((tm, d), lambda m: (m, 0))]
    if next_g is not None:
        nrow = _mod_row_fn(n_x, seq // tm, batch, layer + 1)
        in_specs += [pl.BlockSpec((1, d), lambda m: (0, 0)),
                     pl.BlockSpec((None, 1, d), lambda m: (nrow(m), 0, 0)),
                     pl.BlockSpec((None, 1, d), lambda m: (nrow(m), 0, 1))]
        args += [next_g.reshape(1, d), mod3, mod3]
        out_shape.append(jax.ShapeDtypeStruct((n_rows, d), BF16))
        out_specs.append(pl.BlockSpec((tm, d), lambda m: (m, 0)))
    return pl.pallas_call(
        functools.partial(_combine_kernel, next_norm=next_g is not None),
        out_shape=tuple(out_shape),
        grid=(n_rows // tm,),
        in_specs=in_specs,
        out_specs=tuple(out_specs),
        compiler_params=_cparams(("parallel",)),
        name="moe_combine",
    )(*args)


def moe_dispatch(ids, counts, n_experts):
    blk = TILES["moe_block"]
    n = ids.shape[0]
    experts, rank = ids[:, :TOP_K], ids[:, TOP_K:2 * TOP_K]
    counts = counts[0, :n_experts].astype(jnp.int32)
    padded = (counts + blk - 1) // blk * blk
    pend = jnp.cumsum(padded)
    pstart = pend - padded
    eye = experts[:, :, None] == jnp.arange(n_experts, dtype=jnp.int32)[None, None, :]
    dest = jnp.sum(jnp.where(eye, pstart[None, None, :], 0), axis=-1) + rank
    n_blocks = -(-n * TOP_K // blk) + n_experts
    rows = n_blocks * blk
    slot = jnp.full((rows,), -1, jnp.int32).at[dest.T.reshape(-1)].set(jnp.arange(n * TOP_K, dtype=jnp.int32))
    pad = slot < 0
    pad_rank = jnp.cumsum(pad.astype(jnp.int32)) - 1
    row_dst = jnp.where(pad, n * TOP_K + pad_rank, slot)
    row_tok = jnp.where(pad, jnp.arange(rows, dtype=jnp.int32) % n, slot % n)
    block_start = jnp.arange(n_blocks, dtype=jnp.int32) * blk
    block_e = jnp.minimum(jnp.sum(block_start[:, None] >= pend[None, :], axis=1), n_experts - 1).astype(jnp.int32)
    e_ids = jnp.arange(n_experts, dtype=jnp.int32)
    present = jnp.any(block_e[:, None] == e_ids[None, :], axis=0)
    later = present[None, :] & (e_ids[None, :] > e_ids[:, None])
    next_present = jnp.where(jnp.any(later, axis=1), jnp.argmax(later, axis=1).astype(jnp.int32), e_ids)
    next_e = next_present[block_e]
    n_live = (pend[-1:] // blk).astype(jnp.int32)
    return row_tok, row_dst, block_e, next_e, n_live


def _rope_tables(seq, tm):
    inv = ROPE_THETA ** (-jnp.arange(0, AXIS_DIM, 2, dtype=F32) / AXIS_DIM)
    t = jnp.arange(seq)
    ang_r = (t // GRID_W).astype(F32)[:, None] * inv
    ang_c = (t % GRID_W).astype(F32)[:, None] * inv
    cos = jnp.concatenate([jnp.cos(ang_r), jnp.cos(ang_r), jnp.cos(ang_c), jnp.cos(ang_c)], axis=-1)
    sin = jnp.concatenate([-jnp.sin(ang_r), jnp.sin(ang_r), -jnp.sin(ang_c), jnp.sin(ang_c)], axis=-1)
    pad = LANES - ROPE_DIM
    cos = jnp.pad(cos, ((0, 0), (0, pad)))
    sin = jnp.pad(sin, ((0, 0), (0, pad)))
    ident_c = jnp.pad(jnp.ones((tm, ROPE_DIM), F32), ((0, 0), (0, pad)))
    return (jnp.concatenate([ident_c, cos], axis=0),
            jnp.concatenate([jnp.zeros((tm, LANES), F32), sin], axis=0))


def _partner():
    j = jnp.arange(ROPE_DIM)
    half = AXIS_DIM // 2
    return jnp.where((j % AXIS_DIM) < half, j + half, j - half)


def _partner_matrix():
    rows = jnp.arange(LANES)[:, None]
    cols = jnp.arange(LANES)[None, :]
    partner = jnp.pad(_partner(), (0, LANES - ROPE_DIM), constant_values=-1)
    return (rows == partner[None, :]).astype(BF16) * (cols < ROPE_DIM).astype(BF16)


def _pad_lanes(a):
    return jnp.pad(a, [(0, 0)] * (a.ndim - 1) + [(0, LANES - a.shape[-1])])


def _norm_gain_rows(g):
    rope = g[NOPE_DIM:]
    rows = jnp.stack([g[:NOPE_DIM], _pad_lanes(rope), _pad_lanes(rope[_partner()])])
    return jnp.pad(rows, ((0, 8 - rows.shape[0]), (0, 0))).astype(F32)


def _layer_weights(l, w_uq, w_ukv, q_lora, kv_lora):
    heads = MLA_HEADS
    wq = w_uq[l].reshape(q_lora, heads, QK_DIM)
    wq_n = wq[:, :, :NOPE_DIM].reshape(q_lora, heads * LANES)
    wq_r = _pad_lanes(wq[:, :, NOPE_DIM:]).reshape(q_lora, heads * LANES)
    wq_s = _pad_lanes(wq[:, :, NOPE_DIM:][:, :, _partner()]).reshape(q_lora, heads * LANES)
    wq_all = jnp.concatenate([wq_n, wq_r, wq_s], axis=1).astype(BF16)
    wkv = w_ukv[l].reshape(kv_lora, heads, NOPE_DIM + V_DIM)
    wkv_all = jnp.concatenate([wkv[:, :, :NOPE_DIM].reshape(kv_lora, heads * NOPE_DIM),
                               wkv[:, :, NOPE_DIM:].reshape(kv_lora, heads * V_DIM)], axis=1).astype(BF16)
    return wq_all, wkv_all


def _router_layout(w_router, router_bias):
    d, n_experts = w_router.shape
    per_group = n_experts // N_GROUPS
    wr = w_router.reshape(d, N_GROUPS, per_group).transpose(0, 2, 1)
    wr_pad = _pad_lanes(wr).reshape(d, per_group * LANES).astype(F32)
    bias = _pad_lanes(router_bias.reshape(N_GROUPS, per_group).T).astype(F32)
    bias = jnp.pad(bias, ((0, 8 - per_group), (0, 0)))
    return wr_pad, bias, per_group


def kernel(x, c, ctx, c_ctx, w_mod, b_mod, g_norm1, g_norm2, w_in, conv_a, ln_a_g, ln_a_b, w_a_out, g_cq, g_ckv, w_uq, w_ukv, g_q, g_k, w_b_out, conv_c, w_c_out, w_o, w_router, router_bias, w_e_in, w_e_out):
    batch, seq, d = x.shape
    n_ctx = ctx.shape[1]
    depth = w_mod.shape[0]
    geom = (batch, seq, n_ctx)
    t_x = batch * seq
    t_all = t_x + batch * n_ctx
    q_lora, kv_lora = g_cq.shape[1], g_ckv.shape[1]
    ch_a, ch_c = conv_a.shape[2], conv_c.shape[2]
    n_experts = w_router.shape[1]
    assert batch + 1 <= MOD_ROWS and t_x % n_ctx == 0 and ch_a == ch_c and (3 * d) % ch_a == 0

    stream = (x.reshape(t_x, d), ctx.reshape(batch * n_ctx, d))
    cvec = jnp.concatenate([c, c_ctx[None, :], jnp.zeros((MOD_ROWS - batch - 1, d), F32)], axis=0)
    mod3 = modulation(cvec, w_mod, b_mod).reshape(depth * MOD_ROWS, 1, N_MOD * d)

    cos_tab, sin_tab = _rope_tables(seq, _mla_tile(geom))
    wr_pad, bias_pad, per_group = _router_layout(w_router, router_bias)
    col0 = 3 * d // ch_a
    off_b = 3 * d + 2 * ch_a
    off_c = off_b + q_lora + kv_lora + ROPE_DIM
    w_in2 = w_in.reshape(depth * d, w_in.shape[2])

    for l in range(depth):
        last = l == depth - 1
        rows_main = t_x if last else t_all
        wq_all, wkv_all = _layer_weights(l, w_uq, w_ukv, q_lora, kv_lora)

        if l == 0:
            h = norm_modulate(stream, g_norm1[l], mod3, layer=l, shift_idx=0, n_rows=t_all, geom=geom)
        u = project(h, w_in2, n_cols=off_b, n_rows=rows_main, row_block=l)
        uc = project_window(h, w_in2, col0=off_c, n_cols=3 * ch_c, n_rows=rows_main, row_block=l)
        ub = project_window(h, w_in2, col0=off_b, n_cols=q_lora + kv_lora, n_rows=t_all, row_block=l,
                            keep_tail=True)
        qt, k, vt = mla_project(ub, g_cq[l], g_ckv[l], wq_all, wkv_all, _norm_gain_rows(g_q[l]),
                                _norm_gain_rows(g_k[l]), cos_tab, sin_tab, geom=geom)
        ob_x = attention_latent(qt, k, vt, geom=geom)
        ob_c = ob_x if last else attention_context(qt, k, vt, geom=geom)
        ha = depthwise_branch(u, conv_a[l], col_a=col0, col_b=col0 + 1, col_gate=None,
                              ln=(ln_a_g[l], ln_a_b[l]), n_rows=rows_main, geom=geom)
        hc = depthwise_branch(uc, conv_c[l], col_a=1, col_b=2, col_gate=0,
                              ln=None, n_rows=rows_main, geom=geom)
        mrg = branch_merge(ha, ob_x, ob_c, hc, w_a_out, w_b_out, w_c_out, u, layer=l,
                           n_rows=rows_main, geom=geom)
        x1 = out_proj_residual(mrg, w_o, stream, mod3, layer=l, gate_idx=2, n_rows=rows_main, geom=geom)

        h2, ids, wts, counts = router(x1, g_norm2[l], mod3, wr_pad, bias_pad, layer=l, shift_idx=3,
                                      n_rows=rows_main, geom=geom, per_group=per_group)
        row_tok, row_dst, block_e, next_e, n_live = moe_dispatch(ids, counts, n_experts)
        y_both = expert_ffn(h2, row_tok, row_dst, block_e, next_e, n_live, w_e_in, w_e_out, layer=l)
        if last:
            (stream,) = moe_combine(x1, y_both, wts, mod3, layer=l, gate_idx=5, n_rows=rows_main, geom=geom)
        else:
            stream, h = moe_combine(x1, y_both, wts, mod3, layer=l, gate_idx=5, n_rows=rows_main,
                                    geom=geom, next_g=g_norm1[l + 1])

    return stream.reshape(batch, seq, d)
```

```python
import functools
import math

import jax
import jax.numpy as jnp
from jax import lax
from jax.experimental import pallas as pl
from jax.experimental.pallas import tpu as pltpu

F32 = jnp.float32
BF16 = jnp.bfloat16

GRID_W = 64
MLA_HEADS = 8
NOPE_DIM = 128
ROPE_DIM = 64
V_DIM = 128
QK_DIM = NOPE_DIM + ROPE_DIM
AXIS_DIM = ROPE_DIM // 2
ROPE_THETA = 10000.0
N_GROUPS = 8
TOP_K = 2
N_MOD = 6
EPS = 1e-6

LANES = 128
HEAD_SLOT = 2 * LANES
HALO_ROWS = 16
MOD_ROWS = 8

TILES = dict(
    row=512,
    proj_row=1024,
    col=1024,
    mod_col=1024,
    mla=512,
    tq=512,
    tq_sub=512,
    attn_keys=512,
    conv=256,
    conv_rows=64,
    router=512,
    moe_block=256,
    vmem=56 * 1024 * 1024,
    vmem_expert=60 * 1024 * 1024,
)


def _cparams(sem):
    return pltpu.CompilerParams(dimension_semantics=sem, vmem_limit_bytes=TILES["vmem"])


def _sigmoid(x):
    return 1.0 / (1.0 + jnp.exp(-x))


def _mod_kernel(c_ref, w_ref, b_ref, o_ref):
    c = c_ref[...]
    s = (c * _sigmoid(c)).astype(BF16)
    o_ref[...] = jnp.dot(s, w_ref[...].astype(BF16), preferred_element_type=F32) + b_ref[...]


def modulation(cvec, w_mod, b_mod):
    n_layers, d, n = w_mod.shape
    tn = min(TILES["mod_col"], n)
    return pl.pallas_call(
        _mod_kernel,
        out_shape=jax.ShapeDtypeStruct((n_layers, MOD_ROWS, n), F32),
        grid=(n_layers, n // tn),
        in_specs=[
            pl.BlockSpec((MOD_ROWS, d), lambda l, j: (0, 0)),
            pl.BlockSpec((None, d, tn), lambda l, j: (l, 0, j)),
            pl.BlockSpec((None, 1, tn), lambda l, j: (l, 0, j)),
        ],
        out_specs=pl.BlockSpec((None, MOD_ROWS, tn), lambda l, j: (l, 0, j)),
        compiler_params=_cparams(("parallel", "parallel")),
        name="modulation",
    )(cvec, w_mod, b_mod.reshape(n_layers, 1, n))


def _norm_mod(x, g, shift, scale):
    ms = jnp.mean(x * x, axis=-1, keepdims=True)
    return x * lax.rsqrt(ms + EPS) * g * (1.0 + scale) + shift


def _stream_specs(stream, tm, n_x_tiles, tn=None, grid_rank=1):
    arrays = stream if isinstance(stream, tuple) else (stream,)
    width = arrays[0].shape[1] if tn is None else tn

    def spec(row_fn):
        if grid_rank == 1:
            return pl.BlockSpec((tm, width), lambda m: (row_fn(m), 0))
        return pl.BlockSpec((tm, width), lambda j, m: (row_fn(m), j))

    if len(arrays) == 1:
        return [spec(lambda m: m)], list(arrays)
    return [spec(lambda m: jnp.minimum(m, n_x_tiles - 1)),
            spec(lambda m: jnp.maximum(m - n_x_tiles, 0))], list(arrays)


def _stream_tile(refs, m, n_x_tiles):
    if len(refs) == 1:
        return refs[0][...]
    return jnp.where(m < n_x_tiles, refs[0][...], refs[1][...])


def _norm_kernel(*refs, n_x_tiles):
    x_refs, (g_ref, shift_ref, scale_ref, h_ref) = refs[:-4], refs[-4:]
    x = _stream_tile(x_refs, pl.program_id(0), n_x_tiles)
    h_ref[...] = _norm_mod(x, g_ref[...], shift_ref[...], scale_ref[...]).astype(h_ref.dtype)


def _mod_row_fn(n_x_tiles, tiles_per_seq, batch, layer):
    def f(m):
        return layer * MOD_ROWS + jnp.where(m < n_x_tiles, m // tiles_per_seq, batch)
    return f


def norm_modulate(stream, g, mod3, *, layer, shift_idx, n_rows, geom):
    batch, seq, _ = geom
    d = g.shape[0]
    tm = min(TILES["row"], seq)
    n_x = batch * seq // tm
    mrow = _mod_row_fn(n_x, seq // tm, batch, layer)
    x_specs, x_args = _stream_specs(stream, tm, n_x)
    return pl.pallas_call(
        functools.partial(_norm_kernel, n_x_tiles=n_x),
        out_shape=jax.ShapeDtypeStruct((n_rows, d), BF16),
        grid=(n_rows // tm,),
        in_specs=x_specs + [
            pl.BlockSpec((1, d), lambda m: (0, 0)),
            pl.BlockSpec((None, 1, d), lambda m: (mrow(m), 0, shift_idx)),
            pl.BlockSpec((None, 1, d), lambda m: (mrow(m), 0, shift_idx + 1)),
        ],
        out_specs=pl.BlockSpec((tm, d), lambda m: (m, 0)),
        compiler_params=_cparams(("parallel",)),
        name="norm_modulate",
    )(*x_args, g.reshape(1, d), mod3, mod3)


def _proj_kernel(a_ref, w_ref, o_ref, w_bf):
    @pl.when(pl.program_id(1) == 0)
    def _():
        w_bf[...] = w_ref[...].astype(BF16)

    o_ref[...] = jnp.dot(a_ref[...], w_bf[...], preferred_element_type=F32).astype(o_ref.dtype)


def _w_spec(w, layer, tn, col0=0):
    if w.ndim == 3:
        return pl.BlockSpec((None, w.shape[1], tn), lambda j, m: (layer, 0, col0 + j))
    return pl.BlockSpec((w.shape[0], tn), lambda j, m: (0, col0 + j))


def project(a, w, *, n_cols, n_rows, row_block=0):
    k = a.shape[1]
    tm = min(TILES["proj_row"], n_rows)
    tn = min(TILES["col"], n_cols)
    if n_cols % tn:
        tn = n_cols
    return pl.pallas_call(
        _proj_kernel,
        out_shape=jax.ShapeDtypeStruct((n_rows, n_cols), BF16),
        grid=(n_cols // tn, n_rows // tm),
        in_specs=[pl.BlockSpec((tm, k), lambda j, m: (m, 0)),
                  pl.BlockSpec((k, tn), lambda j, m: (row_block, j))],
        out_specs=pl.BlockSpec((tm, tn), lambda j, m: (m, j)),
        scratch_shapes=[pltpu.VMEM((k, tn), BF16)],
        compiler_params=_cparams(("parallel", "arbitrary")),
        name="project",
    )(a, w)


def _proj_window_kernel(a_ref, w1_ref, w2_ref, o_ref, w_bf, *, shift):
    @pl.when(pl.program_id(1) == 0)
    def _():
        n1 = w1_ref.shape[1]
        rows = 256
        for r0 in range(0, w1_ref.shape[0], rows):
            cat = jnp.concatenate([w1_ref[r0:r0 + rows, :], w2_ref[r0:r0 + rows, :]], axis=1)
            if shift:
                cat = pltpu.roll(cat, cat.shape[1] - shift, axis=1)
            w_bf[r0:r0 + rows, :] = cat[:, :w_bf.shape[1]].astype(BF16)

    o_ref[...] = jnp.dot(a_ref[...], w_bf[...], preferred_element_type=F32).astype(o_ref.dtype)


def project_window(a, w, *, col0, n_cols, n_rows, row_block, keep_tail=False):
    k = a.shape[1]
    shift = col0 % LANES
    base = col0 - shift
    tm = min(TILES["proj_row"], n_rows)
    assert n_cols % LANES == 0 and not (keep_tail and shift)
    tn = n_cols if keep_tail else min(TILES["col"], n_cols)
    while n_cols % tn or base % tn:
        tn -= LANES
    assert tn == n_cols or not keep_tail
    out_w = tn + LANES if keep_tail else tn
    return pl.pallas_call(
        functools.partial(_proj_window_kernel, shift=shift),
        out_shape=jax.ShapeDtypeStruct((n_rows, n_cols // tn * out_w), BF16),
        grid=(n_cols // tn, n_rows // tm),
        in_specs=[pl.BlockSpec((tm, k), lambda j, m: (m, 0)),
                  pl.BlockSpec((k, tn), lambda j, m: (row_block, base // tn + j)),
                  pl.BlockSpec((k, LANES), lambda j, m: (row_block, (base + (j + 1) * tn) // LANES))],
        out_specs=pl.BlockSpec((tm, out_w), lambda j, m: (m, j)),
        scratch_shapes=[pltpu.VMEM((k, out_w), BF16)],
        compiler_params=_cparams(("parallel", "arbitrary")),
        name="project_window",
    )(a, w, w)


def _mla_kernel(ub_ref, gcq_ref, gckv_ref, wq_ref, wkv_ref, gq_ref, gk_ref, cos_ref, sin_ref, perm_ref,
                qt_ref, k_ref, vt_ref, *, q_lora, kv_lora, heads):
    hn = heads * LANES

    def rms(xv, g):
        return (xv * lax.rsqrt(jnp.mean(xv * xv, axis=-1, keepdims=True) + EPS) * g).astype(BF16)

    cqn = rms(ub_ref[:, 0:q_lora].astype(F32), gcq_ref[...])
    ckvn = rms(ub_ref[:, q_lora:q_lora + kv_lora].astype(F32), gckv_ref[...])
    off = q_lora + kv_lora
    rope_lane = lax.broadcasted_iota(jnp.int32, (ub_ref.shape[0], LANES), 1) < ROPE_DIM
    kpe_bf = jnp.where(rope_lane, ub_ref[:, off:off + LANES], jnp.zeros((), BF16))
    kpe = kpe_bf.astype(F32)
    kpes = jnp.dot(kpe_bf, perm_ref[...], preferred_element_type=F32)

    qall = jnp.dot(cqn, wq_ref[...], preferred_element_type=F32)
    kvall = jnp.dot(ckvn, wkv_ref[...], preferred_element_type=F32)

    cos = cos_ref[...]
    sin = sin_ref[...]
    gq_n, gq_r, gq_s = gq_ref[0:1, :], gq_ref[1:2, :], gq_ref[2:3, :]
    gk_n, gk_r, gk_s = gk_ref[0:1, :], gk_ref[1:2, :], gk_ref[2:3, :]
    q_cos, q_sin = gq_r * cos, gq_s * sin
    k_rope = kpe * (gk_r * cos) + kpes * (gk_s * sin)
    ssq_pe = jnp.sum(kpe * kpe, axis=-1, keepdims=True)
    q_scale = QK_DIM ** -0.5 * math.log2(math.e)

    for h in range(heads):
        lo = h * LANES
        qn = qall[:, lo:lo + LANES]
        qr = qall[:, hn + lo:hn + lo + LANES]
        qs = qall[:, 2 * hn + lo:2 * hn + lo + LANES]
        ssq = jnp.sum(qn * qn, axis=-1, keepdims=True) + jnp.sum(qr * qr, axis=-1, keepdims=True)
        inv = lax.rsqrt(ssq * (1.0 / QK_DIM) + EPS) * q_scale
        qt_ref[h * HEAD_SLOT:h * HEAD_SLOT + LANES, :] = (qn * gq_n * inv).T.astype(BF16)
        qt_ref[h * HEAD_SLOT + LANES:(h + 1) * HEAD_SLOT, :] = ((qr * q_cos + qs * q_sin) * inv).T.astype(BF16)

        kn = kvall[:, lo:lo + LANES]
        invk = lax.rsqrt((jnp.sum(kn * kn, axis=-1, keepdims=True) + ssq_pe) * (1.0 / QK_DIM) + EPS)
        k_ref[:, h * HEAD_SLOT:h * HEAD_SLOT + LANES] = (kn * gk_n * invk).astype(BF16)
        k_ref[:, h * HEAD_SLOT + LANES:(h + 1) * HEAD_SLOT] = (k_rope * invk).astype(BF16)

        vt_ref[h * V_DIM:(h + 1) * V_DIM, :] = kvall[:, hn + h * V_DIM:hn + (h + 1) * V_DIM].T.astype(BF16)


def _mla_tile(geom):
    batch, seq, ctx = geom
    tm = min(TILES["mla"], seq, batch * ctx)
    assert seq % tm == 0 and (batch * ctx) % tm == 0
    return tm


def mla_project(ub, g_cq, g_ckv, wq, wkv, gq3, gk3, cos_tab, sin_tab, *, geom):
    batch, seq, ctx = geom
    t_all = ub.shape[0]
    q_lora, kv_lora = g_cq.shape[0], g_ckv.shape[0]
    heads = MLA_HEADS
    tm = _mla_tile(geom)
    n_x = batch * seq // tm
    per_seq = seq // tm

    def pos(m):
        return jnp.where(m < n_x, 1 + m % per_seq, 0)

    kern = functools.partial(_mla_kernel, q_lora=q_lora, kv_lora=kv_lora, heads=heads)
    return pl.pallas_call(
        kern,
        out_shape=(
            jax.ShapeDtypeStruct((heads * HEAD_SLOT, t_all), BF16),
            jax.ShapeDtypeStruct((t_all, heads * HEAD_SLOT), BF16),
            jax.ShapeDtypeStruct((heads * V_DIM, t_all), BF16),
        ),
        grid=(t_all // tm,),
        in_specs=[
            pl.BlockSpec((tm, ub.shape[1]), lambda m: (m, 0)),
            pl.BlockSpec((1, q_lora), lambda m: (0, 0)),
            pl.BlockSpec((1, kv_lora), lambda m: (0, 0)),
            pl.BlockSpec(wq.shape, lambda m: (0, 0)),
            pl.BlockSpec(wkv.shape, lambda m: (0, 0)),
            pl.BlockSpec((8, LANES), lambda m: (0, 0)),
            pl.BlockSpec((8, LANES), lambda m: (0, 0)),
            pl.BlockSpec((tm, LANES), lambda m: (pos(m), 0)),
            pl.BlockSpec((tm, LANES), lambda m: (pos(m), 0)),
            pl.BlockSpec((LANES, LANES), lambda m: (0, 0)),
        ],
        out_specs=(
            pl.BlockSpec((heads * HEAD_SLOT, tm), lambda m: (0, m)),
            pl.BlockSpec((tm, heads * HEAD_SLOT), lambda m: (m, 0)),
            pl.BlockSpec((heads * V_DIM, tm), lambda m: (0, m)),
        ),
        compiler_params=_cparams(("parallel",)),
        name="mla_project",
    )(ub, g_cq.reshape(1, -1), g_ckv.reshape(1, -1), wq, wkv, gq3, gk3, cos_tab, sin_tab, _partner_matrix())


def _attn_kernel(*refs, sub, latent_keys):
    if latent_keys:
        qt_ref, kx_ref, kc_ref, vxt_ref, vct_ref, o_ref = refs
    else:
        qt_ref, kc_ref, vct_ref, o_ref = refs
    for j in range(qt_ref.shape[1] // sub):
        qt = qt_ref[:, j * sub:(j + 1) * sub]
        sc = jnp.dot(kc_ref[...], qt, preferred_element_type=F32)
        m = jnp.max(sc, axis=0, keepdims=True)
        if latent_keys:
            sx = jnp.dot(kx_ref[...], qt, preferred_element_type=F32)
            m = jnp.maximum(m, jnp.max(sx, axis=0, keepdims=True))
        pc = jnp.exp2(sc - m)
        denom = jnp.sum(pc, axis=0, keepdims=True)
        ot = jnp.dot(vct_ref[...], pc.astype(BF16), preferred_element_type=F32)
        if latent_keys:
            px = jnp.exp2(sx - m)
            denom = denom + jnp.sum(px, axis=0, keepdims=True)
            ot = ot + jnp.dot(vxt_ref[...], px.astype(BF16), preferred_element_type=F32)
        o_ref[j * sub:(j + 1) * sub, :] = (ot * (1.0 / denom)).T.astype(o_ref.dtype)


def _attn_pipelined_kernel(qt_ref, kx_ref, kc_ref, vxt_ref, vct_ref, o_ref, s0_ref, s1_ref, m0_ref, m1_ref,
                           *, n_q, chunk):
    i = pl.program_id(2)
    seq, ctx = kx_ref.shape[0], kc_ref.shape[0]
    tq = qt_ref.shape[1]
    chunks = [(kc_ref, vct_ref, 0, ctx, 0)]
    chunks += [(kx_ref, vxt_ref, c0, chunk, ctx + c0) for c0 in range(0, seq, chunk)]
    slots = ((s0_ref, m0_ref), (s1_ref, m1_ref))

    def run(new, old):
        if new is not None:
            s_new, m_new = slots[new]
            qt = qt_ref[...]
            m = jnp.full((1, tq), -jnp.inf, F32)
        if old is not None:
            s_old, m_old_ref = slots[old]
            m_old = m_old_ref[...]
            acc = jnp.zeros((V_DIM, tq), F32)
            den = jnp.zeros((1, tq), F32)
        for kref, vtref, c0, size, row0 in chunks:
            if new is not None:
                s = jnp.dot(kref[c0:c0 + size, :], qt, preferred_element_type=F32)
                s_new[row0:row0 + size, :] = s
                m = jnp.maximum(m, jnp.max(s, axis=0, keepdims=True))
            if old is not None:
                p = jnp.exp2(s_old[row0:row0 + size, :] - m_old)
                den = den + jnp.sum(p, axis=0, keepdims=True)
                acc = acc + jnp.dot(vtref[:, c0:c0 + size], p.astype(BF16), preferred_element_type=F32)
        if new is not None:
            m_new[...] = m
        if old is not None:
            o_ref[...] = (acc * (1.0 / den)).T.astype(o_ref.dtype)

    pl.when(i == 0)(lambda: run(0, None))
    pl.when((i > 0) & (i < n_q) & (i % 2 == 1))(lambda: run(1, 0))
    pl.when((i > 0) & (i < n_q) & (i % 2 == 0))(lambda: run(0, 1))
    pl.when(i == n_q)(lambda: run(None, (n_q - 1) % 2))


def attention_latent(qt, k, vt, *, geom):
    batch, seq, ctx = geom
    heads = MLA_HEADS
    tq = min(TILES["tq"], seq)
    n_q = seq // tq
    ctx_blk0 = batch * seq // ctx
    kern = functools.partial(_attn_pipelined_kernel, n_q=n_q, chunk=min(TILES["attn_keys"], seq))
    return pl.pallas_call(
        kern,
        out_shape=jax.ShapeDtypeStruct((batch * seq, heads * V_DIM), BF16),
        grid=(batch, heads, n_q + 1),
        in_specs=[
            pl.BlockSpec((HEAD_SLOT, tq), lambda b, h, i: (h, b * n_q + jnp.minimum(i, n_q - 1))),
            pl.BlockSpec((seq, HEAD_SLOT), lambda b, h, i: (b, h)),
            pl.BlockSpec((ctx, HEAD_SLOT), lambda b, h, i: (ctx_blk0 + b, h)),
            pl.BlockSpec((V_DIM, seq), lambda b, h, i: (h, b)),
            pl.BlockSpec((V_DIM, ctx), lambda b, h, i: (h, ctx_blk0 + b)),
        ],
        out_specs=pl.BlockSpec((tq, V_DIM), lambda b, h, i: (b * n_q + jnp.maximum(i - 1, 0), h)),
        scratch_shapes=[pltpu.VMEM((ctx + seq, tq), F32), pltpu.VMEM((ctx + seq, tq), F32),
                        pltpu.VMEM((1, tq), F32), pltpu.VMEM((1, tq), F32)],
        compiler_params=_cparams(("parallel", "parallel", "arbitrary")),
        name="attention_latent",
    )(qt, k, k, vt, vt)


def attention_context(qt, k, vt, *, geom):
    batch, seq, ctx = geom
    heads = MLA_HEADS
    ctx_blk0 = batch * seq // ctx
    kern = functools.partial(_attn_kernel, sub=min(TILES["tq_sub"], ctx), latent_keys=False)
    return pl.pallas_call(
        kern,
        out_shape=jax.ShapeDtypeStruct((batch * ctx, heads * V_DIM), BF16),
        grid=(batch, heads),
        in_specs=[
            pl.BlockSpec((HEAD_SLOT, ctx), lambda b, h: (h, ctx_blk0 + b)),
            pl.BlockSpec((ctx, HEAD_SLOT), lambda b, h: (ctx_blk0 + b, h)),
            pl.BlockSpec((V_DIM, ctx), lambda b, h: (h, ctx_blk0 + b)),
        ],
        out_specs=pl.BlockSpec((ctx, V_DIM), lambda b, h: (b, h)),
        compiler_params=_cparams(("parallel", "parallel")),
        name="attention_context",
    )(qt, k, vt)


def _conv_kernel(*refs, width, conformer, tp, n_x_tiles, x_per_seq, c_per_seq):
    if conformer:
        (ap, bp, am, bm, an, bn, w_ref, lng_ref, lnb_ref, o_ref, buf, ybuf) = refs
        gate_ref = None
    else:
        (gate_ref, ap, bp, am, bm, an, bn, w_ref, o_ref, buf, ybuf) = refs
    nchunk = buf.shape[0]
    i = pl.program_id(0)
    in_x = i < n_x_tiles
    pos = jnp.where(in_x, i % x_per_seq, (i - n_x_tiles) % c_per_seq)
    per_seq = jnp.where(in_x, x_per_seq, c_per_seq)
    keep_prev = (pos > 0).astype(F32)
    keep_next = (pos < per_seq - 1).astype(F32)

    def pre(a, b):
        a = a.astype(F32)
        b = b.astype(F32)
        return a * _sigmoid(b) if conformer else a * b

    for c in range(nchunk):
        sl = slice(c * LANES, (c + 1) * LANES)
        buf[c, 0:HALO_ROWS, :] = pre(ap[:, sl], bp[:, sl]) * keep_prev
        buf[c, HALO_ROWS:HALO_ROWS + tp, :] = pre(am[:, sl], bm[:, sl])
        buf[c, HALO_ROWS + tp:2 * HALO_ROWS + tp, :] = pre(an[:, sl], bn[:, sl]) * keep_next

    rc = min(TILES["conv_rows"], tp)
    base = HALO_ROWS - width // 2

    def chunk(c, carry):
        wts = w_ref[c]
        for r0 in range(0, tp, rc):
            acc = jnp.zeros((rc, LANES), F32)
            for k in range(width):
                acc = acc + wts[k:k + 1, :] * buf[c, pl.ds(base + r0 + k, rc), :]
            ybuf[c, pl.ds(r0, rc), :] = acc
        return carry

    lax.fori_loop(0, nchunk, chunk, 0)

    if conformer:
        n_ch = nchunk * LANES
        tot = jnp.zeros((tp, 1), F32)
        for c in range(nchunk):
            tot = tot + jnp.sum(ybuf[c], axis=-1, keepdims=True)
        mu = tot * (1.0 / n_ch)
        tot2 = jnp.zeros((tp, 1), F32)
        for c in range(nchunk):
            dlt = ybuf[c] - mu
            tot2 = tot2 + jnp.sum(dlt * dlt, axis=-1, keepdims=True)
        inv = lax.rsqrt(tot2 * (1.0 / n_ch) + EPS)
        for c in range(nchunk):
            sl = slice(c * LANES, (c + 1) * LANES)
            y = (ybuf[c] - mu) * inv * lng_ref[:, sl] + lnb_ref[:, sl]
            o_ref[:, sl] = (y * _sigmoid(y)).astype(o_ref.dtype)
    else:
        for c in range(nchunk):
            sl = slice(c * LANES, (c + 1) * LANES)
            o_ref[:, sl] = (gate_ref[:, sl].astype(F32) * ybuf[c]).astype(o_ref.dtype)


def depthwise_branch(u, conv_w, *, col_a, col_b, col_gate, ln, n_rows, geom):
    batch, seq, ctx = geom
    width, ch = conv_w.shape
    conformer = ln is not None
    tp = min(TILES["conv"], seq, ctx)
    nchunk = ch // LANES
    hb = tp // HALO_ROWS
    last_halo = u.shape[0] // HALO_ROWS - 1
    wpad = -(-width // 8) * 8
    w3 = jnp.pad(conv_w.astype(F32), ((0, wpad - width), (0, 0))).reshape(wpad, nchunk, LANES).transpose(1, 0, 2)

    def prev_map(col):
        return lambda i: (jnp.maximum(i * hb - 1, 0), col)

    def main_map(col):
        return lambda i: (i, col)

    def next_map(col):
        return lambda i: (jnp.minimum((i + 1) * hb, last_halo), col)

    in_specs, args = [], []
    if not conformer:
        in_specs.append(pl.BlockSpec((tp, ch), main_map(col_gate)))
        args.append(u)
    for rows, mk in ((HALO_ROWS, prev_map), (tp, main_map), (HALO_ROWS, next_map)):
        for col in (col_a, col_b):
            in_specs.append(pl.BlockSpec((rows, ch), mk(col)))
            args.append(u)
    in_specs.append(pl.BlockSpec((nchunk, wpad, LANES), lambda i: (0, 0, 0)))
    args.append(w3)
    if conformer:
        for p in ln:
            in_specs.append(pl.BlockSpec((1, ch), lambda i: (0, 0)))
            args.append(p.reshape(1, ch).astype(F32))

    kern = functools.partial(
        _conv_kernel, width=width, conformer=conformer, tp=tp, n_x_tiles=batch * seq // tp,
        x_per_seq=seq // tp, c_per_seq=ctx // tp)
    return pl.pallas_call(
        kern,
        out_shape=jax.ShapeDtypeStruct((n_rows, ch), BF16),
        grid=(n_rows // tp,),
        in_specs=in_specs,
        out_specs=pl.BlockSpec((tp, ch), lambda i: (i, 0)),
        scratch_shapes=[pltpu.VMEM((nchunk, tp + 2 * HALO_ROWS, LANES), F32),
                        pltpu.VMEM((nchunk, tp, LANES), F32)],
        compiler_params=_cparams(("parallel",)),
        name="conformer_conv" if conformer else "short_conv",
    )(*args)


def _merge_kernel(ha_ref, obx_ref, obc_ref, hc_ref, wa_ref, wb_ref, wc_ref, ga_ref, gb_ref, gc_ref, o_ref,
                  wa_bf, wb_bf, wc_bf, *, n_x_tiles):
    m_idx = pl.program_id(1)

    @pl.when(m_idx == 0)
    def _():
        wa_bf[...] = wa_ref[...].astype(BF16)
        wb_bf[...] = wb_ref[...].astype(BF16)
        wc_bf[...] = wc_ref[...].astype(BF16)

    ob = jnp.where(m_idx < n_x_tiles, obx_ref[...], obc_ref[...])
    ya = jnp.dot(ha_ref[...], wa_bf[...], preferred_element_type=F32)
    yb = jnp.dot(ob, wb_bf[...], preferred_element_type=F32)
    yc = jnp.dot(hc_ref[...], wc_bf[...], preferred_element_type=F32)
    m = (_sigmoid(ga_ref[...].astype(F32)) * ya + _sigmoid(gb_ref[...].astype(F32)) * yb
         + _sigmoid(gc_ref[...].astype(F32)) * yc)
    o_ref[...] = m.astype(o_ref.dtype)


def branch_merge(ha, ob_x, ob_c, hc, wa, wb, wc, u, *, layer, n_rows, geom):
    batch, seq, _ = geom
    d = wa.shape[2]
    tm = min(TILES["row"], seq)
    tn = min(TILES["col"], d)
    ncol = d // tn
    n_x = batch * seq // tm
    row_spec = lambda a: pl.BlockSpec((tm, a.shape[1]), lambda j, m: (m, 0))
    gate_spec = lambda k: pl.BlockSpec((tm, tn), lambda j, m: (m, k * ncol + j))
    kern = functools.partial(_merge_kernel, n_x_tiles=n_x)
    return pl.pallas_call(
        kern,
        out_shape=jax.ShapeDtypeStruct((n_rows, d), BF16),
        grid=(ncol, n_rows // tm),
        in_specs=[row_spec(ha),
                  pl.BlockSpec((tm, ob_x.shape[1]), lambda j, m: (jnp.minimum(m, n_x - 1), 0)),
                  pl.BlockSpec((tm, ob_c.shape[1]), lambda j, m: (jnp.maximum(m - n_x, 0), 0)),
                  row_spec(hc), _w_spec(wa, layer, tn), _w_spec(wb, layer, tn), _w_spec(wc, layer, tn),
                  gate_spec(0), gate_spec(1), gate_spec(2)],
        out_specs=pl.BlockSpec((tm, tn), lambda j, m: (m, j)),
        scratch_shapes=[pltpu.VMEM((w.shape[1], tn), BF16) for w in (wa, wb, wc)],
        compiler_params=_cparams(("parallel", "arbitrary")),
        name="branch_merge",
    )(ha, ob_x, ob_c, hc, wa, wb, wc, u, u, u)


def _outproj_kernel(*refs, n_x_tiles):
    (m_ref, w_ref, gate_ref), x_refs, (o_ref, w_bf) = refs[:3], refs[3:-2], refs[-2:]
    m_idx = pl.program_id(1)

    @pl.when(m_idx == 0)
    def _():
        w_bf[...] = w_ref[...].astype(BF16)

    y = jnp.dot(m_ref[...], w_bf[...], preferred_element_type=F32)
    o_ref[...] = _stream_tile(x_refs, m_idx, n_x_tiles) + gate_ref[...] * y


def out_proj_residual(mrg, w_o, stream, mod3, *, layer, gate_idx, n_rows, geom):
    batch, seq, _ = geom
    d = w_o.shape[2]
    tm = min(TILES["row"], seq)
    tn = min(TILES["col"], d)
    ncol = d // tn
    n_x = batch * seq // tm
    mrow = _mod_row_fn(n_x, seq // tm, batch, layer)
    x_specs, x_args = _stream_specs(stream, tm, n_x, tn=tn, grid_rank=2)
    return pl.pallas_call(
        functools.partial(_outproj_kernel, n_x_tiles=n_x),
        out_shape=jax.ShapeDtypeStruct((n_rows, d), F32),
        grid=(ncol, n_rows // tm),
        in_specs=[
            pl.BlockSpec((tm, mrg.shape[1]), lambda j, m: (m, 0)),
            _w_spec(w_o, layer, tn),
            pl.BlockSpec((None, 1, tn), lambda j, m: (mrow(m), 0, gate_idx * ncol + j)),
        ] + x_specs,
        out_specs=pl.BlockSpec((tm, tn), lambda j, m: (m, j)),
        scratch_shapes=[pltpu.VMEM((w_o.shape[1], tn), BF16)],
        compiler_params=_cparams(("parallel", "arbitrary")),
        name="out_proj_residual",
    )(mrg, w_o, mod3, *x_args)


def _pack_halves(v):
    n = v.shape[1] // 2
    lo = lax.bitcast_convert_type(v[:, :n].astype(BF16).astype(F32), jnp.int32)
    hi = lax.bitcast_convert_type(v[:, n:].astype(BF16).astype(F32), jnp.int32)
    return lax.shift_right_logical(lo, jnp.int32(16)) | (hi & jnp.int32(-65536))


def _unpack_halves(p):
    lo = lax.bitcast_convert_type(lax.shift_left(p, jnp.int32(16)), F32)
    hi = lax.bitcast_convert_type(p & jnp.int32(-65536), F32)
    return lo, hi


def _router_kernel(x_ref, g_ref, shift_ref, scale_ref, wr_ref, bias_ref, h_ref, ids_ref, wts_ref, cnt_ref,
                   *, per_group):
    h = _norm_mod(x_ref[...], g_ref[...], shift_ref[...], scale_ref[...])
    h_ref[...] = _pack_halves(h)
    h_hi = h.astype(BF16)
    h_lo = (h - h_hi.astype(F32)).astype(BF16)
    w = wr_ref[...]
    w_hi = w.astype(BF16)
    w_lo = (w - w_hi.astype(F32)).astype(BF16)
    logits = (jnp.dot(h_hi, w_hi, preferred_element_type=F32)
              + jnp.dot(h_hi, w_lo, preferred_element_type=F32)
              + jnp.dot(h_lo, w_hi, preferred_element_type=F32))
    lane_i = lax.broadcasted_iota(jnp.int32, (h.shape[0], LANES), 1)
    lane = lane_i.astype(F32)
    valid = lane_i < N_GROUPS
    neg = jnp.float32(-jnp.inf)
    score, sel = [], []
    for j in range(per_group):
        member = logits[:, (j // 2) * LANES:(j // 2 + 1) * LANES]
        if j % 2:
            member = pltpu.roll(member, LANES // 2, axis=1)
        s = _sigmoid(member)
        score.append(s)
        sel.append(jnp.where(valid, s + bias_ref[j:j + 1, :], neg))
    assert per_group == 4
    hi1, lo1 = jnp.maximum(sel[0], sel[1]), jnp.minimum(sel[0], sel[1])
    hi2, lo2 = jnp.maximum(sel[2], sel[3]), jnp.minimum(sel[2], sel[3])
    top1 = jnp.maximum(hi1, hi2)
    top2 = jnp.maximum(jnp.minimum(hi1, hi2), jnp.maximum(lo1, lo2))
    gscore = jnp.where(valid, top1 + top2, neg)
    gmax = jnp.max(gscore, axis=-1, keepdims=True)
    grp = jnp.min(jnp.where(gscore == gmax, lane, float(LANES)), axis=-1, keepdims=True)
    pick = lane == grp
    zero = jnp.float32(0.0)
    sel_g = [jnp.sum(jnp.where(pick, s, zero), axis=-1, keepdims=True) for s in sel]
    sc_g = [jnp.sum(jnp.where(pick, s, zero), axis=-1, keepdims=True) for s in score]
    best, best_i, best_s = sel_g[0], jnp.zeros_like(grp), sc_g[0]
    for j in range(1, per_group):
        better = sel_g[j] > best
        best = jnp.where(better, sel_g[j], best)
        best_i = jnp.where(better, float(j), best_i)
        best_s = jnp.where(better, sc_g[j], best_s)
    second, second_i, second_s = jnp.full_like(best, neg), jnp.zeros_like(grp), jnp.zeros_like(best)
    for j in range(per_group):
        better = (sel_g[j] > second) & (best_i != float(j))
        second = jnp.where(better, sel_g[j], second)
        second_i = jnp.where(better, float(j), second_i)
        second_s = jnp.where(better, sc_g[j], second_s)
    tot = best_s + second_s
    e0 = grp * per_group + best_i
    e1 = grp * per_group + second_i
    @pl.when(pl.program_id(0) == 0)
    def _():
        cnt_ref[...] = jnp.zeros_like(cnt_ref)

    oh0 = lane == e0
    oh1 = lane == e1
    both = jnp.where(oh0 | oh1, jnp.float32(1.0), zero)
    tm = h.shape[0]
    earlier = (lax.broadcasted_iota(jnp.int32, (tm, tm), 1)
               < lax.broadcasted_iota(jnp.int32, (tm, tm), 0)).astype(BF16)
    before = jnp.dot(earlier, both.astype(BF16), preferred_element_type=F32) + cnt_ref[0:1, :]
    r0 = jnp.sum(jnp.where(oh0, before, zero), axis=-1, keepdims=True)
    r1 = jnp.sum(jnp.where(oh1, before, zero), axis=-1, keepdims=True)
    cnt_ref[...] = cnt_ref[...] + jnp.sum(both, axis=0, keepdims=True)
    ids = jnp.where(lane_i == 0, e0, jnp.where(lane_i == 1, e1, jnp.where(lane_i == 2, r0,
                    jnp.where(lane_i == 3, r1, zero))))
    ids_ref[...] = ids.astype(jnp.int32)
    wts_ref[...] = jnp.where(lane_i == 0, best_s / tot, jnp.where(lane_i == 1, second_s / tot, zero))


def router(xall, g, mod3, wr_pad, bias_pad, *, layer, shift_idx, n_rows, geom, per_group):
    batch, seq, _ = geom
    d = xall.shape[1]
    tm = min(TILES["router"], seq)
    mrow = _mod_row_fn(batch * seq // tm, seq // tm, batch, layer)
    kern = functools.partial(_router_kernel, per_group=per_group)
    return pl.pallas_call(
        kern,
        out_shape=(jax.ShapeDtypeStruct((n_rows, d // 2), jnp.int32),
                   jax.ShapeDtypeStruct((n_rows, LANES), jnp.int32),
                   jax.ShapeDtypeStruct((n_rows, LANES), F32),
                   jax.ShapeDtypeStruct((8, LANES), F32)),
        grid=(n_rows // tm,),
        in_specs=[
            pl.BlockSpec((tm, d), lambda m: (m, 0)),
            pl.BlockSpec((1, d), lambda m: (0, 0)),
            pl.BlockSpec((None, 1, d), lambda m: (mrow(m), 0, shift_idx)),
            pl.BlockSpec((None, 1, d), lambda m: (mrow(m), 0, shift_idx + 1)),
            pl.BlockSpec(wr_pad.shape, lambda m: (0, 0)),
            pl.BlockSpec(bias_pad.shape, lambda m: (0, 0)),
        ],
        out_specs=(pl.BlockSpec((tm, d // 2), lambda m: (m, 0)),
                   pl.BlockSpec((tm, LANES), lambda m: (m, 0)),
                   pl.BlockSpec((tm, LANES), lambda m: (m, 0)),
                   pl.BlockSpec((8, LANES), lambda m: (0, 0))),
        compiler_params=_cparams(("arbitrary",)),
        name="router",
    )(xall, g.reshape(1, d), mod3, mod3, wr_pad, bias_pad)


def _expert_kernel(be_ref, nxt_ref, nlive_ref,tok_cur, tok_next, dst_prev, dst_cur, h_hbm, win_hbm, wout_hbm, y_hbm,
                   xbuf0, xbuf1, ybuf0, ybuf1, win_f32, wout_f32, win_bf, wout_bf, gsem, ssem, wsem,
                   *, ff, n_blocks, layer):
    i = pl.program_id(0)
    blk = xbuf0.shape[0]
    xbufs, ybufs = (xbuf0, xbuf1), (ybuf0, ybuf1)

    def weight_copies(e):
        return (pltpu.make_async_copy(win_hbm.at[layer, e], win_f32, wsem.at[0]),
                pltpu.make_async_copy(wout_hbm.at[layer, e], wout_f32, wsem.at[1]))

    @pl.when(i == 0)
    def _():
        for cp in weight_copies(be_ref[0]):
            cp.start(priority=1)

    @pl.when((i == 0) | (be_ref[i] != be_ref[jnp.maximum(i - 1, 0)]))
    def _():
        for cp in weight_copies(be_ref[i]):
            cp.wait()
        win_bf[...] = win_f32[...].astype(BF16)
        wout_bf[...] = wout_f32[...].astype(BF16)
        for cp in weight_copies(nxt_ref[i]):
            cp.start(priority=1)

    def gather(tok_ref, p):
        for r in range(blk):
            pltpu.make_async_copy(h_hbm.at[pl.ds(tok_ref[0, r], 1)], xbufs[p].at[pl.ds(r, 1)],
                                  gsem.at[p]).start()

    def scatter(dst_ref, p):
        for r in range(blk):
            pltpu.make_async_copy(ybufs[p].at[pl.ds(r, 1)], y_hbm.at[pl.ds(dst_ref[0, r], 1)],
                                  ssem.at[p]).start()

    def wait_gather(p):
        pltpu.make_async_copy(h_hbm.at[pl.ds(0, blk)], xbufs[p], gsem.at[p]).wait()

    def wait_scatter(p):
        pltpu.make_async_copy(ybufs[p], y_hbm.at[pl.ds(0, blk)], ssem.at[p]).wait()

    @pl.when(i == 0)
    def _():
        ybuf1[...] = jnp.zeros_like(ybuf1)
        gather(tok_cur, 0)
        wait_gather(0)

    def step(p, live):
        if live:
            x_lo, x_hi = _unpack_halves(xbufs[p][...])
            x = jnp.concatenate([x_lo, x_hi], axis=1).astype(BF16)
            gu = jnp.dot(x, win_bf[...], preferred_element_type=F32)
            g = gu[:, :ff]
            act = (g * _sigmoid(g) * gu[:, ff:]).astype(BF16)
            y = jnp.dot(act, wout_bf[...], preferred_element_type=F32)
            gather(tok_next, 1 - p)
        scatter(dst_prev, 1 - p)
        pl.when(i >= 1)(lambda: wait_scatter(p))
        ybufs[p][...] = _pack_halves(y) if live else jnp.zeros_like(ybufs[p])
        if live:
            wait_gather(1 - p)

        @pl.when(i == n_blocks - 1)
        def _():
            scatter(dst_cur, p)
            wait_scatter(p)
            wait_scatter(1 - p)
            for cp in weight_copies(be_ref[i]):
                cp.wait()

    live = i < nlive_ref[0]
    for p in (0, 1):
        pl.when(live & (i % 2 == p))(functools.partial(step, p, True))
        pl.when(jnp.logical_not(live) & (i % 2 == p))(functools.partial(step, p, False))


def expert_ffn(h2, row_tok, row_dst, block_e, next_e, n_live, w_in, w_out, *, layer):
    dh = h2.shape[1]
    d = 2 * dh
    rows = row_tok.shape[0]
    blk = TILES["moe_block"]
    n_blocks = rows // blk
    ff = w_out.shape[2]
    tok3 = row_tok.reshape(n_blocks, 1, blk)
    spare = rows + jnp.arange(blk, dtype=jnp.int32)
    dst3 = jnp.concatenate([spare, row_dst]).reshape(n_blocks + 1, 1, blk)
    smem_blk = lambda f: pl.BlockSpec((None, 1, blk), f, memory_space=pltpu.SMEM)
    any_spec = pl.BlockSpec(memory_space=pl.ANY)
    kern = functools.partial(_expert_kernel, ff=ff, n_blocks=n_blocks, layer=layer)
    grid_spec = pltpu.PrefetchScalarGridSpec(
        num_scalar_prefetch=3,
        grid=(n_blocks,),
        in_specs=[
            smem_blk(lambda i, be, nx, nl: (i, 0, 0)),
            smem_blk(lambda i, be, nx, nl: (jnp.minimum(i + 1, n_blocks - 1), 0, 0)),
            smem_blk(lambda i, be, nx, nl: (i, 0, 0)),
            smem_blk(lambda i, be, nx, nl: (i + 1, 0, 0)),
            any_spec, any_spec, any_spec,
        ],
        out_specs=any_spec,
        scratch_shapes=[pltpu.VMEM((blk, dh), jnp.int32), pltpu.VMEM((blk, dh), jnp.int32),
                        pltpu.VMEM((blk, dh), jnp.int32), pltpu.VMEM((blk, dh), jnp.int32),
                        pltpu.VMEM((d, 2 * ff), F32), pltpu.VMEM((ff, d), F32),
                        pltpu.VMEM((d, 2 * ff), BF16), pltpu.VMEM((ff, d), BF16),
                        pltpu.SemaphoreType.DMA((2,)), pltpu.SemaphoreType.DMA((2,)),
                        pltpu.SemaphoreType.DMA((2,))],
    )
    return pl.pallas_call(
        kern,
        out_shape=jax.ShapeDtypeStruct((rows + blk, dh), jnp.int32),
        grid_spec=grid_spec,
        compiler_params=pltpu.CompilerParams(dimension_semantics=("arbitrary",),
                                             vmem_limit_bytes=TILES["vmem_expert"]),
        name="expert_ffn",
    )(block_e, next_e, n_live, tok3, tok3, dst3, dst3, h2, w_in, w_out)


def _combine_kernel(x_ref, y0_ref, y1_ref, w_ref, gate_ref, *rest, next_norm):
    w = w_ref[...]
    y0_lo, y0_hi = _unpack_halves(y0_ref[...])
    y1_lo, y1_hi = _unpack_halves(y1_ref[...])
    y = jnp.concatenate([w[:, 0:1] * y0_lo + w[:, 1:2] * y1_lo, w[:, 0:1] * y0_hi + w[:, 1:2] * y1_hi], axis=1)
    out = x_ref[...] + gate_ref[...] * y
    if next_norm:
        g_ref, shift_ref, scale_ref, o_ref, h_ref = rest
        h_ref[...] = _norm_mod(out, g_ref[...], shift_ref[...], scale_ref[...]).astype(h_ref.dtype)
    else:
        (o_ref,) = rest
    o_ref[...] = out


def moe_combine(xall, y_both, wts, mod3, *, layer, gate_idx, n_rows, geom, next_g=None):
    batch, seq, _ = geom
    d = xall.shape[1]
    tm = min(TILES["row"], seq)
    n_x = batch * seq // tm
    mrow = _mod_row_fn(n_x, seq // tm, batch, layer)
    in_specs = [
        pl.BlockSpec((tm, d), lambda m: (m, 0)),
        pl.BlockSpec((tm, d // 2), lambda m: (m, 0)),
        pl.BlockSpec((tm, d // 2), lambda m: (n_rows // tm + m, 0)),
        pl.BlockSpec((tm, LANES), lambda m: (m, 0)),
        pl.BlockSpec((None, 1, d), lambda m: (mrow(m), 0, gate_idx)),
    ]
    args = [xall, y_both, y_both, wts, mod3]
    out_shape = [jax.ShapeDtypeStruct((n_rows, d), F32)]
    out_specs = [pl.BlockSpec((tm, d), lambda m: (m, 0))]
    if next_g is not None:
        nrow = _mod_row_fn(n_x, seq // tm, batch, layer + 1)
        in_specs += [pl.BlockSpec((1, d), lambda m: (0, 0)),
                     pl.BlockSpec((None, 1, d), lambda m: (nrow(m), 0, 0)),
                     pl.BlockSpec((None, 1, d), lambda m: (nrow(m), 0, 1))]
        args += [next_g.reshape(1, d), mod3, mod3]
        out_shape.append(jax.ShapeDtypeStruct((n_rows, d), BF16))
        out_specs.append(pl.BlockSpec((tm, d), lambda m: (m, 0)))
    return pl.pallas_call(
        functools.partial(_combine_kernel, next_norm=next_g is not None),
        out_shape=tuple(out_shape),
        grid=(n_rows // tm,),
        in_specs=in_specs,
        out_specs=tuple(out_specs),
        compiler_params=_cparams(("parallel",)),
        name="moe_combine",
    )(*args)


def moe_dispatch(ids, counts, n_experts):
    blk = TILES["moe_block"]
    n = ids.shape[0]
    experts, rank = ids[:, :TOP_K], ids[:, TOP_K:2 * TOP_K]
    counts = counts[0, :n_experts].astype(jnp.int32)
    padded = (counts + blk - 1) // blk * blk
    pend = jnp.cumsum(padded)
    pstart = pend - padded
    eye = experts[:, :, None] == jnp.arange(n_experts, dtype=jnp.int32)[None, None, :]
    dest = jnp.sum(jnp.where(eye, pstart[None, None, :], 0), axis=-1) + rank
    n_blocks = -(-n * TOP_K // blk) + n_experts
    rows = n_blocks * blk
    slot = jnp.full((rows,), -1, jnp.int32).at[dest.T.reshape(-1)].set(jnp.arange(n * TOP_K, dtype=jnp.int32))
    pad = slot < 0
    pad_rank = jnp.cumsum(pad.astype(jnp.int32)) - 1
    row_dst = jnp.where(pad, n * TOP_K + pad_rank, slot)
    row_tok = jnp.where(pad, jnp.arange(rows, dtype=jnp.int32) % n, slot % n)
    block_start = jnp.arange(n_blocks, dtype=jnp.int32) * blk
    block_e = jnp.minimum(jnp.sum(block_start[:, None] >= pend[None, :], axis=1), n_experts - 1).astype(jnp.int32)
    e_ids = jnp.arange(n_experts, dtype=jnp.int32)
    present = jnp.any(block_e[:, None] == e_ids[None, :], axis=0)
    later = present[None, :] & (e_ids[None, :] > e_ids[:, None])
    next_present = jnp.where(jnp.any(later, axis=1), jnp.argmax(later, axis=1).astype(jnp.int32), e_ids)
    next_e = next_present[block_e]
    n_live = (pend[-1:] // blk).astype(jnp.int32)
    return row_tok, row_dst, block_e, next_e, n_live


def _rope_tables(seq, tm):
    inv = ROPE_THETA ** (-jnp.arange(0, AXIS_DIM, 2, dtype=F32) / AXIS_DIM)
    t = jnp.arange(seq)
    ang_r = (t // GRID_W).astype(F32)[:, None] * inv
    ang_c = (t % GRID_W).astype(F32)[:, None] * inv
    cos = jnp.concatenate([jnp.cos(ang_r), jnp.cos(ang_r), jnp.cos(ang_c), jnp.cos(ang_c)], axis=-1)
    sin = jnp.concatenate([-jnp.sin(ang_r), jnp.sin(ang_r), -jnp.sin(ang_c), jnp.sin(ang_c)], axis=-1)
    pad = LANES - ROPE_DIM
    cos = jnp.pad(cos, ((0, 0), (0, pad)))
    sin = jnp.pad(sin, ((0, 0), (0, pad)))
    ident_c = jnp.pad(jnp.ones((tm, ROPE_DIM), F32), ((0, 0), (0, pad)))
    return (jnp.concatenate([ident_c, cos], axis=0),
            jnp.concatenate([jnp.zeros((tm, LANES), F32), sin], axis=0))


def _partner():
    j = jnp.arange(ROPE_DIM)
    half = AXIS_DIM // 2
    return jnp.where((j % AXIS_DIM) < half, j + half, j - half)


def _partner_matrix():
    rows = jnp.arange(LANES)[:, None]
    cols = jnp.arange(LANES)[None, :]
    partner = jnp.pad(_partner(), (0, LANES - ROPE_DIM), constant_values=-1)
    return (rows == partner[None, :]).astype(BF16) * (cols < ROPE_DIM).astype(BF16)


def _pad_lanes(a):
    return jnp.pad(a, [(0, 0)] * (a.ndim - 1) + [(0, LANES - a.shape[-1])])


def _norm_gain_rows(g):
    rope = g[NOPE_DIM:]
    rows = jnp.stack([g[:NOPE_DIM], _pad_lanes(rope), _pad_lanes(rope[_partner()])])
    return jnp.pad(rows, ((0, 8 - rows.shape[0]), (0, 0))).astype(F32)


def _layer_weights(l, w_uq, w_ukv, q_lora, kv_lora):
    heads = MLA_HEADS
    wq = w_uq[l].reshape(q_lora, heads, QK_DIM)
    wq_n = wq[:, :, :NOPE_DIM].reshape(q_lora, heads * LANES)
    wq_r = _pad_lanes(wq[:, :, NOPE_DIM:]).reshape(q_lora, heads * LANES)
    wq_s = _pad_lanes(wq[:, :, NOPE_DIM:][:, :, _partner()]).reshape(q_lora, heads * LANES)
    wq_all = jnp.concatenate([wq_n, wq_r, wq_s], axis=1).astype(BF16)
    wkv = w_ukv[l].reshape(kv_lora, heads, NOPE_DIM + V_DIM)
    wkv_all = jnp.concatenate([wkv[:, :, :NOPE_DIM].reshape(kv_lora, heads * NOPE_DIM),
                               wkv[:, :, NOPE_DIM:].reshape(kv_lora, heads * V_DIM)], axis=1).astype(BF16)
    return wq_all, wkv_all


def _router_layout(w_router, router_bias):
    d, n_experts = w_router.shape
    per_group = n_experts // N_GROUPS
    assert per_group % 2 == 0 and N_GROUPS <= LANES // 2
    wr = w_router.reshape(d, N_GROUPS, per_group).transpose(0, 2, 1)
    half = jnp.pad(wr, ((0, 0), (0, 0), (0, LANES // 2 - N_GROUPS)))
    wr_pad = half.reshape(d, per_group * LANES // 2).astype(F32)
    bias = _pad_lanes(router_bias.reshape(N_GROUPS, per_group).T).astype(F32)
    bias = jnp.pad(bias, ((0, 8 - per_group), (0, 0)))
    return wr_pad, bias, per_group


def kernel(x, c, ctx, c_ctx, w_mod, b_mod, g_norm1, g_norm2, w_in, conv_a, ln_a_g, ln_a_b, w_a_out, g_cq, g_ckv, w_uq, w_ukv, g_q, g_k, w_b_out, conv_c, w_c_out, w_o, w_router, router_bias, w_e_in, w_e_out):
    batch, seq, d = x.shape
    n_ctx = ctx.shape[1]
    depth = w_mod.shape[0]
    geom = (batch, seq, n_ctx)
    t_x = batch * seq
    t_all = t_x + batch * n_ctx
    q_lora, kv_lora = g_cq.shape[1], g_ckv.shape[1]
    ch_a, ch_c = conv_a.shape[2], conv_c.shape[2]
    n_experts = w_router.shape[1]
    assert batch + 1 <= MOD_ROWS and t_x % n_ctx == 0 and ch_a == ch_c and (3 * d) % ch_a == 0

    stream = (x.reshape(t_x, d), ctx.reshape(batch * n_ctx, d))
    cvec = jnp.concatenate([c, c_ctx[None, :], jnp.zeros((MOD_ROWS - batch - 1, d), F32)], axis=0)
    mod3 = modulation(cvec, w_mod, b_mod).reshape(depth * MOD_ROWS, 1, N_MOD * d)

    cos_tab, sin_tab = _rope_tables(seq, _mla_tile(geom))
    wr_pad, bias_pad, per_group = _router_layout(w_router, router_bias)
    col0 = 3 * d // ch_a
    off_b = 3 * d + 2 * ch_a
    off_c = off_b + q_lora + kv_lora + ROPE_DIM
    w_in2 = w_in.reshape(depth * d, w_in.shape[2])

    for l in range(depth):
        last = l == depth - 1
        rows_main = t_x if last else t_all
        wq_all, wkv_all = _layer_weights(l, w_uq, w_ukv, q_lora, kv_lora)

        if l == 0:
            h = norm_modulate(stream, g_norm1[l], mod3, layer=l, shift_idx=0, n_rows=t_all, geom=geom)
        u = project(h, w_in2, n_cols=off_b, n_rows=rows_main, row_block=l)
        uc = project_window(h, w_in2, col0=off_c, n_cols=3 * ch_c, n_rows=rows_main, row_block=l)
        ub = project_window(h, w_in2, col0=off_b, n_cols=q_lora + kv_lora, n_rows=t_all, row_block=l,
                            keep_tail=True)
        qt, k, vt = mla_project(ub, g_cq[l], g_ckv[l], wq_all, wkv_all, _norm_gain_rows(g_q[l]),
                                _norm_gain_rows(g_k[l]), cos_tab, sin_tab, geom=geom)
        ob_x = attention_latent(qt, k, vt, geom=geom)
        ob_c = ob_x if last else attention_context(qt, k, vt, geom=geom)
        ha = depthwise_branch(u, conv_a[l], col_a=col0, col_b=col0 + 1, col_gate=None,
                              ln=(ln_a_g[l], ln_a_b[l]), n_rows=rows_main, geom=geom)
        hc = depthwise_branch(uc, conv_c[l], col_a=1, col_b=2, col_gate=0,
                              ln=None, n_rows=rows_main, geom=geom)
        mrg = branch_merge(ha, ob_x, ob_c, hc, w_a_out, w_b_out, w_c_out, u, layer=l,
                           n_rows=rows_main, geom=geom)
        x1 = out_proj_residual(mrg, w_o, stream, mod3, layer=l, gate_idx=2, n_rows=rows_main, geom=geom)

        h2, ids, wts, counts = router(x1, g_norm2[l], mod3, wr_pad, bias_pad, layer=l, shift_idx=3,
                                      n_rows=rows_main, geom=geom, per_group=per_group)
        row_tok, row_dst, block_e, next_e, n_live = moe_dispatch(ids, counts, n_experts)
        y_both = expert_ffn(h2, row_tok, row_dst, block_e, next_e, n_live, w_e_in, w_e_out, layer=l)
        if last:
            (stream,) = moe_combine(x1, y_both, wts, mod3, layer=l, gate_idx=5, n_rows=rows_main, geom=geom)
        else:
            stream, h = moe_combine(x1, y_both, wts, mod3, layer=l, gate_idx=5, n_rows=rows_main,
                                    geom=geom, next_g=g_norm1[l + 1])

    return stream.reshape(batch, seq, d)
```

```python
import functools
import math

import jax
import jax.numpy as jnp
from jax import lax
from jax.experimental import pallas as pl
from jax.experimental.pallas import tpu as pltpu

F32 = jnp.float32
BF16 = jnp.bfloat16

GRID_W = 64
MLA_HEADS = 8
NOPE_DIM = 128
ROPE_DIM = 64
V_DIM = 128
QK_DIM = NOPE_DIM + ROPE_DIM
AXIS_DIM = ROPE_DIM // 2
ROPE_THETA = 10000.0
N_GROUPS = 8
TOP_K = 2
N_MOD = 6
EPS = 1e-6

LANES = 128
HEAD_SLOT = 2 * LANES
HALO_ROWS = 16
MOD_ROWS = 8

TILES = dict(
    row=512,
    proj_row=1024,
    col=1024,
    mod_col=1024,
    mla=512,
    tq=512,
    tq_sub=512,
    attn_keys=512,
    conv=256,
    conv_rows=64,
    router=512,
    moe_block=256,
    vmem=56 * 1024 * 1024,
    vmem_expert=60 * 1024 * 1024,
)


def _cparams(sem):
    return pltpu.CompilerParams(dimension_semantics=sem, vmem_limit_bytes=TILES["vmem"])


def _sigmoid(x):
    return 0.5 * jnp.tanh(0.5 * x) + 0.5


def _mod_kernel(c_ref, w_ref, b_ref, o_ref):
    c = c_ref[...]
    s = (c * _sigmoid(c)).astype(BF16)
    o_ref[...] = jnp.dot(s, w_ref[...].astype(BF16), preferred_element_type=F32) + b_ref[...]


def modulation(cvec, w_mod, b_mod):
    n_layers, d, n = w_mod.shape
    tn = min(TILES["mod_col"], n)
    return pl.pallas_call(
        _mod_kernel,
        out_shape=jax.ShapeDtypeStruct((n_layers, MOD_ROWS, n), F32),
        grid=(n_layers, n // tn),
        in_specs=[
            pl.BlockSpec((MOD_ROWS, d), lambda l, j: (0, 0)),
            pl.BlockSpec((None, d, tn), lambda l, j: (l, 0, j)),
            pl.BlockSpec((None, 1, tn), lambda l, j: (l, 0, j)),
        ],
        out_specs=pl.BlockSpec((None, MOD_ROWS, tn), lambda l, j: (l, 0, j)),
        compiler_params=_cparams(("parallel", "parallel")),
        name="modulation",
    )(cvec, w_mod, b_mod.reshape(n_layers, 1, n))


def _norm_mod(x, g, shift, scale):
    ms = jnp.mean(x * x, axis=-1, keepdims=True)
    return x * lax.rsqrt(ms + EPS) * g * (1.0 + scale) + shift


def _stream_specs(stream, tm, n_x_tiles, tn=None, grid_rank=1):
    arrays = stream if isinstance(stream, tuple) else (stream,)
    width = arrays[0].shape[1] if tn is None else tn

    def spec(row_fn):
        if grid_rank == 1:
            return pl.BlockSpec((tm, width), lambda m: (row_fn(m), 0))
        return pl.BlockSpec((tm, width), lambda j, m: (row_fn(m), j))

    if len(arrays) == 1:
        return [spec(lambda m: m)], list(arrays)
    return [spec(lambda m: jnp.minimum(m, n_x_tiles - 1)),
            spec(lambda m: jnp.maximum(m - n_x_tiles, 0))], list(arrays)


def _stream_tile(refs, m, n_x_tiles):
    if len(refs) == 1:
        return refs[0][...]
    return jnp.where(m < n_x_tiles, refs[0][...], refs[1][...])


def _norm_kernel(*refs, n_x_tiles):
    x_refs, (g_ref, shift_ref, scale_ref, h_ref) = refs[:-4], refs[-4:]
    x = _stream_tile(x_refs, pl.program_id(0), n_x_tiles)
    h_ref[...] = _norm_mod(x, g_ref[...], shift_ref[...], scale_ref[...]).astype(h_ref.dtype)


def _mod_row_fn(n_x_tiles, tiles_per_seq, batch, layer):
    def f(m):
        return layer * MOD_ROWS + jnp.where(m < n_x_tiles, m // tiles_per_seq, batch)
    return f


def norm_modulate(stream, g, mod3, *, layer, shift_idx, n_rows, geom):
    batch, seq, _ = geom
    d = g.shape[0]
    tm = min(TILES["row"], seq)
    n_x = batch * seq // tm
    mrow = _mod_row_fn(n_x, seq // tm, batch, layer)
    x_specs, x_args = _stream_specs(stream, tm, n_x)
    return pl.pallas_call(
        functools.partial(_norm_kernel, n_x_tiles=n_x),
        out_shape=jax.ShapeDtypeStruct((n_rows, d), BF16),
        grid=(n_rows // tm,),
        in_specs=x_specs + [
            pl.BlockSpec((1, d), lambda m: (0, 0)),
            pl.BlockSpec((None, 1, d), lambda m: (mrow(m), 0, shift_idx)),
            pl.BlockSpec((None, 1, d), lambda m: (mrow(m), 0, shift_idx + 1)),
        ],
        out_specs=pl.BlockSpec((tm, d), lambda m: (m, 0)),
        compiler_params=_cparams(("parallel",)),
        name="norm_modulate",
    )(*x_args, g.reshape(1, d), mod3, mod3)


def _proj_kernel(a_ref, w_ref, o_ref, w_bf):
    @pl.when(pl.program_id(1) == 0)
    def _():
        w_bf[...] = w_ref[...].astype(BF16)

    o_ref[...] = jnp.dot(a_ref[...], w_bf[...], preferred_element_type=F32).astype(o_ref.dtype)


def _w_spec(w, layer, tn, col0=0):
    if w.ndim == 3:
        return pl.BlockSpec((None, w.shape[1], tn), lambda j, m: (layer, 0, col0 + j))
    return pl.BlockSpec((w.shape[0], tn), lambda j, m: (0, col0 + j))


def project(a, w, *, n_cols, n_rows, row_block=0):
    k = a.shape[1]
    tm = min(TILES["proj_row"], n_rows)
    tn = min(TILES["col"], n_cols)
    if n_cols % tn:
        tn = n_cols
    return pl.pallas_call(
        _proj_kernel,
        out_shape=jax.ShapeDtypeStruct((n_rows, n_cols), BF16),
        grid=(n_cols // tn, n_rows // tm),
        in_specs=[pl.BlockSpec((tm, k), lambda j, m: (m, 0)),
                  pl.BlockSpec((k, tn), lambda j, m: (row_block, j))],
        out_specs=pl.BlockSpec((tm, tn), lambda j, m: (m, j)),
        scratch_shapes=[pltpu.VMEM((k, tn), BF16)],
        compiler_params=_cparams(("parallel", "arbitrary")),
        name="project",
    )(a, w)


def _proj_window_kernel(a_ref, w1_ref, w2_ref, o_ref, w_bf, *, shift):
    @pl.when(pl.program_id(1) == 0)
    def _():
        n1 = w1_ref.shape[1]
        rows = 256
        for r0 in range(0, w1_ref.shape[0], rows):
            cat = jnp.concatenate([w1_ref[r0:r0 + rows, :], w2_ref[r0:r0 + rows, :]], axis=1)
            if shift:
                cat = pltpu.roll(cat, cat.shape[1] - shift, axis=1)
            w_bf[r0:r0 + rows, :] = cat[:, :w_bf.shape[1]].astype(BF16)

    o_ref[...] = jnp.dot(a_ref[...], w_bf[...], preferred_element_type=F32).astype(o_ref.dtype)


def project_window(a, w, *, col0, n_cols, n_rows, row_block, keep_tail=False):
    k = a.shape[1]
    shift = col0 % LANES
    base = col0 - shift
    tm = min(TILES["proj_row"], n_rows)
    assert n_cols % LANES == 0 and not (keep_tail and shift)
    tn = n_cols if keep_tail else min(TILES["col"], n_cols)
    while n_cols % tn or base % tn:
        tn -= LANES
    assert tn == n_cols or not keep_tail
    out_w = tn + LANES if keep_tail else tn
    return pl.pallas_call(
        functools.partial(_proj_window_kernel, shift=shift),
        out_shape=jax.ShapeDtypeStruct((n_rows, n_cols // tn * out_w), BF16),
        grid=(n_cols // tn, n_rows // tm),
        in_specs=[pl.BlockSpec((tm, k), lambda j, m: (m, 0)),
                  pl.BlockSpec((k, tn), lambda j, m: (row_block, base // tn + j)),
                  pl.BlockSpec((k, LANES), lambda j, m: (row_block, (base + (j + 1) * tn) // LANES))],
        out_specs=pl.BlockSpec((tm, out_w), lambda j, m: (m, j)),
        scratch_shapes=[pltpu.VMEM((k, out_w), BF16)],
        compiler_params=_cparams(("parallel", "arbitrary")),
        name="project_window",
    )(a, w, w)


def _mla_kernel(ub_ref, gcq_ref, gckv_ref, wq_ref, wkv_ref, gq_ref, gk_ref, cos_ref, sin_ref, perm_ref,
                qt_ref, k_ref, vt_ref, *, q_lora, kv_lora, heads):
    hn = heads * LANES

    def rms(xv, g):
        return (xv * lax.rsqrt(jnp.mean(xv * xv, axis=-1, keepdims=True) + EPS) * g).astype(BF16)

    cqn = rms(ub_ref[:, 0:q_lora].astype(F32), gcq_ref[...])
    ckvn = rms(ub_ref[:, q_lora:q_lora + kv_lora].astype(F32), gckv_ref[...])
    off = q_lora + kv_lora
    rope_lane = lax.broadcasted_iota(jnp.int32, (ub_ref.shape[0], LANES), 1) < ROPE_DIM
    kpe_bf = jnp.where(rope_lane, ub_ref[:, off:off + LANES], jnp.zeros((), BF16))
    kpe = kpe_bf.astype(F32)
    kpes = jnp.dot(kpe_bf, perm_ref[...], preferred_element_type=F32)

    qall = jnp.dot(cqn, wq_ref[...], preferred_element_type=F32)
    kvall = jnp.dot(ckvn, wkv_ref[...], preferred_element_type=F32)

    cos = cos_ref[...]
    sin = sin_ref[...]
    gq_n, gq_r, gq_s = gq_ref[0:1, :], gq_ref[1:2, :], gq_ref[2:3, :]
    gk_n, gk_r, gk_s = gk_ref[0:1, :], gk_ref[1:2, :], gk_ref[2:3, :]
    q_cos, q_sin = gq_r * cos, gq_s * sin
    k_rope = kpe * (gk_r * cos) + kpes * (gk_s * sin)
    ssq_pe = jnp.sum(kpe * kpe, axis=-1, keepdims=True)
    q_scale = QK_DIM ** -0.5 * math.log2(math.e)

    for h in range(heads):
        lo = h * LANES
        qn = qall[:, lo:lo + LANES]
        qr = qall[:, hn + lo:hn + lo + LANES]
        qs = qall[:, 2 * hn + lo:2 * hn + lo + LANES]
        ssq = jnp.sum(qn * qn, axis=-1, keepdims=True) + jnp.sum(qr * qr, axis=-1, keepdims=True)
        inv = lax.rsqrt(ssq * (1.0 / QK_DIM) + EPS) * q_scale
        qt_ref[h * HEAD_SLOT:h * HEAD_SLOT + LANES, :] = (qn * gq_n * inv).T.astype(BF16)
        qt_ref[h * HEAD_SLOT + LANES:(h + 1) * HEAD_SLOT, :] = ((qr * q_cos + qs * q_sin) * inv).T.astype(BF16)

        kn = kvall[:, lo:lo + LANES]
        invk = lax.rsqrt((jnp.sum(kn * kn, axis=-1, keepdims=True) + ssq_pe) * (1.0 / QK_DIM) + EPS)
        k_ref[:, h * HEAD_SLOT:h * HEAD_SLOT + LANES] = (kn * gk_n * invk).astype(BF16)
        k_ref[:, h * HEAD_SLOT + LANES:(h + 1) * HEAD_SLOT] = (k_rope * invk).astype(BF16)

        vt_ref[h * V_DIM:(h + 1) * V_DIM, :] = kvall[:, hn + h * V_DIM:hn + (h + 1) * V_DIM].T.astype(BF16)


def _mla_tile(geom):
    batch, seq, ctx = geom
    tm = min(TILES["mla"], seq, batch * ctx)
    assert seq % tm == 0 and (batch * ctx) % tm == 0
    return tm


def mla_project(ub, g_cq, g_ckv, wq, wkv, gq3, gk3, cos_tab, sin_tab, *, geom):
    batch, seq, ctx = geom
    t_all = ub.shape[0]
    q_lora, kv_lora = g_cq.shape[0], g_ckv.shape[0]
    heads = MLA_HEADS
    tm = _mla_tile(geom)
    n_x = batch * seq // tm
    per_seq = seq // tm

    def pos(m):
        return jnp.where(m < n_x, 1 + m % per_seq, 0)

    kern = functools.partial(_mla_kernel, q_lora=q_lora, kv_lora=kv_lora, heads=heads)
    return pl.pallas_call(
        kern,
        out_shape=(
            jax.ShapeDtypeStruct((heads * HEAD_SLOT, t_all), BF16),
            jax.ShapeDtypeStruct((t_all, heads * HEAD_SLOT), BF16),
            jax.ShapeDtypeStruct((heads * V_DIM, t_all), BF16),
        ),
        grid=(t_all // tm,),
        in_specs=[
            pl.BlockSpec((tm, ub.shape[1]), lambda m: (m, 0)),
            pl.BlockSpec((1, q_lora), lambda m: (0, 0)),
            pl.BlockSpec((1, kv_lora), lambda m: (0, 0)),
            pl.BlockSpec(wq.shape, lambda m: (0, 0)),
            pl.BlockSpec(wkv.shape, lambda m: (0, 0)),
            pl.BlockSpec((8, LANES), lambda m: (0, 0)),
            pl.BlockSpec((8, LANES), lambda m: (0, 0)),
            pl.BlockSpec((tm, LANES), lambda m: (pos(m), 0)),
            pl.BlockSpec((tm, LANES), lambda m: (pos(m), 0)),
            pl.BlockSpec((LANES, LANES), lambda m: (0, 0)),
        ],
        out_specs=(
            pl.BlockSpec((heads * HEAD_SLOT, tm), lambda m: (0, m)),
            pl.BlockSpec((tm, heads * HEAD_SLOT), lambda m: (m, 0)),
            pl.BlockSpec((heads * V_DIM, tm), lambda m: (0, m)),
        ),
        compiler_params=_cparams(("parallel",)),
        name="mla_project",
    )(ub, g_cq.reshape(1, -1), g_ckv.reshape(1, -1), wq, wkv, gq3, gk3, cos_tab, sin_tab, _partner_matrix())


def _attn_kernel(*refs, sub, latent_keys):
    if latent_keys:
        qt_ref, kx_ref, kc_ref, vxt_ref, vct_ref, o_ref = refs
    else:
        qt_ref, kc_ref, vct_ref, o_ref = refs
    for j in range(qt_ref.shape[1] // sub):
        qt = qt_ref[:, j * sub:(j + 1) * sub]
        sc = jnp.dot(kc_ref[...], qt, preferred_element_type=F32)
        m = jnp.max(sc, axis=0, keepdims=True)
        if latent_keys:
            sx = jnp.dot(kx_ref[...], qt, preferred_element_type=F32)
            m = jnp.maximum(m, jnp.max(sx, axis=0, keepdims=True))
        pc = jnp.exp2(sc - m)
        denom = jnp.sum(pc, axis=0, keepdims=True)
        ot = jnp.dot(vct_ref[...], pc.astype(BF16), preferred_element_type=F32)
        if latent_keys:
            px = jnp.exp2(sx - m)
            denom = denom + jnp.sum(px, axis=0, keepdims=True)
            ot = ot + jnp.dot(vxt_ref[...], px.astype(BF16), preferred_element_type=F32)
        o_ref[j * sub:(j + 1) * sub, :] = (ot * (1.0 / denom)).T.astype(o_ref.dtype)


def _attn_pipelined_kernel(qt_ref, kx_ref, kc_ref, vxt_ref, vct_ref, o_ref, s0_ref, s1_ref, m0_ref, m1_ref,
                           *, n_q, chunk):
    i = pl.program_id(2)
    seq, ctx = kx_ref.shape[0], kc_ref.shape[0]
    tq = qt_ref.shape[1]
    chunks = [(kc_ref, vct_ref, 0, ctx, 0)]
    chunks += [(kx_ref, vxt_ref, c0, chunk, ctx + c0) for c0 in range(0, seq, chunk)]
    slots = ((s0_ref, m0_ref), (s1_ref, m1_ref))

    def run(new, old):
        if new is not None:
            s_new, m_new = slots[new]
            qt = qt_ref[...]
            m = jnp.full((1, tq), -jnp.inf, F32)
        if old is not None:
            s_old, m_old_ref = slots[old]
            m_old = m_old_ref[...]
            acc = jnp.zeros((V_DIM, tq), F32)
            den = jnp.zeros((1, tq), F32)
        for kref, vtref, c0, size, row0 in chunks:
            if new is not None:
                s = jnp.dot(kref[c0:c0 + size, :], qt, preferred_element_type=F32)
                s_new[row0:row0 + size, :] = s
                m = jnp.maximum(m, jnp.max(s, axis=0, keepdims=True))
            if old is not None:
                p = jnp.exp2(s_old[row0:row0 + size, :] - m_old)
                den = den + jnp.sum(p, axis=0, keepdims=True)
                acc = acc + jnp.dot(vtref[:, c0:c0 + size], p.astype(BF16), preferred_element_type=F32)
        if new is not None:
            m_new[...] = m
        if old is not None:
            o_ref[...] = (acc * (1.0 / den)).T.astype(o_ref.dtype)

    pl.when(i == 0)(lambda: run(0, None))
    pl.when((i > 0) & (i < n_q) & (i % 2 == 1))(lambda: run(1, 0))
    pl.when((i > 0) & (i < n_q) & (i % 2 == 0))(lambda: run(0, 1))
    pl.when(i == n_q)(lambda: run(None, (n_q - 1) % 2))


def attention_latent(qt, k, vt, *, geom):
    batch, seq, ctx = geom
    heads = MLA_HEADS
    tq = min(TILES["tq"], seq)
    n_q = seq // tq
    ctx_blk0 = batch * seq // ctx
    kern = functools.partial(_attn_pipelined_kernel, n_q=n_q, chunk=min(TILES["attn_keys"], seq))
    return pl.pallas_call(
        kern,
        out_shape=jax.ShapeDtypeStruct((batch * seq, heads * V_DIM), BF16),
        grid=(batch, heads, n_q + 1),
        in_specs=[
            pl.BlockSpec((HEAD_SLOT, tq), lambda b, h, i: (h, b * n_q + jnp.minimum(i, n_q - 1))),
            pl.BlockSpec((seq, HEAD_SLOT), lambda b, h, i: (b, h)),
            pl.BlockSpec((ctx, HEAD_SLOT), lambda b, h, i: (ctx_blk0 + b, h)),
            pl.BlockSpec((V_DIM, seq), lambda b, h, i: (h, b)),
            pl.BlockSpec((V_DIM, ctx), lambda b, h, i: (h, ctx_blk0 + b)),
        ],
        out_specs=pl.BlockSpec((tq, V_DIM), lambda b, h, i: (b * n_q + jnp.maximum(i - 1, 0), h)),
        scratch_shapes=[pltpu.VMEM((ctx + seq, tq), F32), pltpu.VMEM((ctx + seq, tq), F32),
                        pltpu.VMEM((1, tq), F32), pltpu.VMEM((1, tq), F32)],
        compiler_params=_cparams(("parallel", "parallel", "arbitrary")),
        name="attention_latent",
    )(qt, k, k, vt, vt)


def attention_context(qt, k, vt, *, geom):
    batch, seq, ctx = geom
    heads = MLA_HEADS
    ctx_blk0 = batch * seq // ctx
    kern = functools.partial(_attn_kernel, sub=min(TILES["tq_sub"], ctx), latent_keys=False)
    return pl.pallas_call(
        kern,
        out_shape=jax.ShapeDtypeStruct((batch * ctx, heads * V_DIM), BF16),
        grid=(batch, heads),
        in_specs=[
            pl.BlockSpec((HEAD_SLOT, ctx), lambda b, h: (h, ctx_blk0 + b)),
            pl.BlockSpec((ctx, HEAD_SLOT), lambda b, h: (ctx_blk0 + b, h)),
            pl.BlockSpec((V_DIM, ctx), lambda b, h: (h, ctx_blk0 + b)),
        ],
        out_specs=pl.BlockSpec((ctx, V_DIM), lambda b, h: (b, h)),
        compiler_params=_cparams(("parallel", "parallel")),
        name="attention_context",
    )(qt, k, vt)


def _conv_kernel(*refs, width, conformer, tp, n_x_tiles, x_per_seq, c_per_seq):
    if conformer:
        (ap, bp, am, bm, an, bn, w_ref, lng_ref, lnb_ref, o_ref, buf, ybuf) = refs
        gate_ref = None
    else:
        (gate_ref, ap, bp, am, bm, an, bn, w_ref, o_ref, buf, ybuf) = refs
    nchunk = buf.shape[0]
    i = pl.program_id(0)
    in_x = i < n_x_tiles
    pos = jnp.where(in_x, i % x_per_seq, (i - n_x_tiles) % c_per_seq)
    per_seq = jnp.where(in_x, x_per_seq, c_per_seq)
    keep_prev = (pos > 0).astype(F32)
    keep_next = (pos < per_seq - 1).astype(F32)

    def pre(a, b):
        a = a.astype(F32)
        b = b.astype(F32)
        return a * _sigmoid(b) if conformer else a * b

    for c in range(nchunk):
        sl = slice(c * LANES, (c + 1) * LANES)
        buf[c, 0:HALO_ROWS, :] = pre(ap[:, sl], bp[:, sl]) * keep_prev
        buf[c, HALO_ROWS:HALO_ROWS + tp, :] = pre(am[:, sl], bm[:, sl])
        buf[c, HALO_ROWS + tp:2 * HALO_ROWS + tp, :] = pre(an[:, sl], bn[:, sl]) * keep_next

    rc = min(TILES["conv_rows"], tp)
    base = HALO_ROWS - width // 2

    def chunk(c, carry):
        wts = w_ref[c]
        for r0 in range(0, tp, rc):
            acc = jnp.zeros((rc, LANES), F32)
            for k in range(width):
                acc = acc + wts[k:k + 1, :] * buf[c, pl.ds(base + r0 + k, rc), :]
            ybuf[c, pl.ds(r0, rc), :] = acc
        return carry

    lax.fori_loop(0, nchunk, chunk, 0)

    if conformer:
        n_ch = nchunk * LANES
        tot = jnp.zeros((tp, 1), F32)
        for c in range(nchunk):
            tot = tot + jnp.sum(ybuf[c], axis=-1, keepdims=True)
        mu = tot * (1.0 / n_ch)
        tot2 = jnp.zeros((tp, 1), F32)
        for c in range(nchunk):
            dlt = ybuf[c] - mu
            tot2 = tot2 + jnp.sum(dlt * dlt, axis=-1, keepdims=True)
        inv = lax.rsqrt(tot2 * (1.0 / n_ch) + EPS)
        for c in range(nchunk):
            sl = slice(c * LANES, (c + 1) * LANES)
            y = (ybuf[c] - mu) * inv * lng_ref[:, sl] + lnb_ref[:, sl]
            o_ref[:, sl] = (y * _sigmoid(y)).astype(o_ref.dtype)
    else:
        for c in range(nchunk):
            sl = slice(c * LANES, (c + 1) * LANES)
            o_ref[:, sl] = (gate_ref[:, sl].astype(F32) * ybuf[c]).astype(o_ref.dtype)


def depthwise_branch(u, conv_w, *, col_a, col_b, col_gate, ln, n_rows, geom):
    batch, seq, ctx = geom
    width, ch = conv_w.shape
    conformer = ln is not None
    tp = min(TILES["conv"], seq, ctx)
    nchunk = ch // LANES
    hb = tp // HALO_ROWS
    last_halo = u.shape[0] // HALO_ROWS - 1
    wpad = -(-width // 8) * 8
    w3 = jnp.pad(conv_w.astype(F32), ((0, wpad - width), (0, 0))).reshape(wpad, nchunk, LANES).transpose(1, 0, 2)

    def prev_map(col):
        return lambda i: (jnp.maximum(i * hb - 1, 0), col)

    def main_map(col):
        return lambda i: (i, col)

    def next_map(col):
        return lambda i: (jnp.minimum((i + 1) * hb, last_halo), col)

    in_specs, args = [], []
    if not conformer:
        in_specs.append(pl.BlockSpec((tp, ch), main_map(col_gate)))
        args.append(u)
    for rows, mk in ((HALO_ROWS, prev_map), (tp, main_map), (HALO_ROWS, next_map)):
        for col in (col_a, col_b):
            in_specs.append(pl.BlockSpec((rows, ch), mk(col)))
            args.append(u)
    in_specs.append(pl.BlockSpec((nchunk, wpad, LANES), lambda i: (0, 0, 0)))
    args.append(w3)
    if conformer:
        for p in ln:
            in_specs.append(pl.BlockSpec((1, ch), lambda i: (0, 0)))
            args.append(p.reshape(1, ch).astype(F32))

    kern = functools.partial(
        _conv_kernel, width=width, conformer=conformer, tp=tp, n_x_tiles=batch * seq // tp,
        x_per_seq=seq // tp, c_per_seq=ctx // tp)
    return pl.pallas_call(
        kern,
        out_shape=jax.ShapeDtypeStruct((n_rows, ch), BF16),
        grid=(n_rows // tp,),
        in_specs=in_specs,
        out_specs=pl.BlockSpec((tp, ch), lambda i: (i, 0)),
        scratch_shapes=[pltpu.VMEM((nchunk, tp + 2 * HALO_ROWS, LANES), F32),
                        pltpu.VMEM((nchunk, tp, LANES), F32)],
        compiler_params=_cparams(("parallel",)),
        name="conformer_conv" if conformer else "short_conv",
    )(*args)


def _merge_kernel(ha_ref, obx_ref, obc_ref, hc_ref, wa_ref, wb_ref, wc_ref, ga_ref, gb_ref, gc_ref, o_ref,
                  wa_bf, wb_bf, wc_bf, *, n_x_tiles):
    m_idx = pl.program_id(1)

    @pl.when(m_idx == 0)
    def _():
        wa_bf[...] = wa_ref[...].astype(BF16)
        wb_bf[...] = wb_ref[...].astype(BF16)
        wc_bf[...] = wc_ref[...].astype(BF16)

    ob = jnp.where(m_idx < n_x_tiles, obx_ref[...], obc_ref[...])
    ya = jnp.dot(ha_ref[...], wa_bf[...], preferred_element_type=F32)
    yb = jnp.dot(ob, wb_bf[...], preferred_element_type=F32)
    yc = jnp.dot(hc_ref[...], wc_bf[...], preferred_element_type=F32)
    m = (_sigmoid(ga_ref[...].astype(F32)) * ya + _sigmoid(gb_ref[...].astype(F32)) * yb
         + _sigmoid(gc_ref[...].astype(F32)) * yc)
    o_ref[...] = m.astype(o_ref.dtype)


def branch_merge(ha, ob_x, ob_c, hc, wa, wb, wc, u, *, layer, n_rows, geom):
    batch, seq, _ = geom
    d = wa.shape[2]
    tm = min(TILES["row"], seq)
    tn = min(TILES["col"], d)
    ncol = d // tn
    n_x = batch * seq // tm
    row_spec = lambda a: pl.BlockSpec((tm, a.shape[1]), lambda j, m: (m, 0))
    gate_spec = lambda k: pl.BlockSpec((tm, tn), lambda j, m: (m, k * ncol + j))
    kern = functools.partial(_merge_kernel, n_x_tiles=n_x)
    return pl.pallas_call(
        kern,
        out_shape=jax.ShapeDtypeStruct((n_rows, d), BF16),
        grid=(ncol, n_rows // tm),
        in_specs=[row_spec(ha),
                  pl.BlockSpec((tm, ob_x.shape[1]), lambda j, m: (jnp.minimum(m, n_x - 1), 0)),
                  pl.BlockSpec((tm, ob_c.shape[1]), lambda j, m: (jnp.maximum(m - n_x, 0), 0)),
                  row_spec(hc), _w_spec(wa, layer, tn), _w_spec(wb, layer, tn), _w_spec(wc, layer, tn),
                  gate_spec(0), gate_spec(1), gate_spec(2)],
        out_specs=pl.BlockSpec((tm, tn), lambda j, m: (m, j)),
        scratch_shapes=[pltpu.VMEM((w.shape[1], tn), BF16) for w in (wa, wb, wc)],
        compiler_params=_cparams(("parallel", "arbitrary")),
        name="branch_merge",
    )(ha, ob_x, ob_c, hc, wa, wb, wc, u, u, u)


def _outproj_kernel(*refs, n_x_tiles):
    (m_ref, w_ref, gate_ref), x_refs, (o_ref, w_bf) = refs[:3], refs[3:-2], refs[-2:]
    m_idx = pl.program_id(1)

    @pl.when(m_idx == 0)
    def _():
        w_bf[...] = w_ref[...].astype(BF16)

    y = jnp.dot(m_ref[...], w_bf[...], preferred_element_type=F32)
    o_ref[...] = _stream_tile(x_refs, m_idx, n_x_tiles) + gate_ref[...] * y


def out_proj_residual(mrg, w_o, stream, mod3, *, layer, gate_idx, n_rows, geom):
    batch, seq, _ = geom
    d = w_o.shape[2]
    tm = min(TILES["row"], seq)
    tn = min(TILES["col"], d)
    ncol = d // tn
    n_x = batch * seq // tm
    mrow = _mod_row_fn(n_x, seq // tm, batch, layer)
    x_specs, x_args = _stream_specs(stream, tm, n_x, tn=tn, grid_rank=2)
    return pl.pallas_call(
        functools.partial(_outproj_kernel, n_x_tiles=n_x),
        out_shape=jax.ShapeDtypeStruct((n_rows, d), F32),
        grid=(ncol, n_rows // tm),
        in_specs=[
            pl.BlockSpec((tm, mrg.shape[1]), lambda j, m: (m, 0)),
            _w_spec(w_o, layer, tn),
            pl.BlockSpec((None, 1, tn), lambda j, m: (mrow(m), 0, gate_idx * ncol + j)),
        ] + x_specs,
        out_specs=pl.BlockSpec((tm, tn), lambda j, m: (m, j)),
        scratch_shapes=[pltpu.VMEM((w_o.shape[1], tn), BF16)],
        compiler_params=_cparams(("parallel", "arbitrary")),
        name="out_proj_residual",
    )(mrg, w_o, mod3, *x_args)


def _pack_halves(v):
    n = v.shape[1] // 2
    lo = lax.bitcast_convert_type(v[:, :n].astype(BF16).astype(F32), jnp.int32)
    hi = lax.bitcast_convert_type(v[:, n:].astype(BF16).astype(F32), jnp.int32)
    return lax.shift_right_logical(lo, jnp.int32(16)) | (hi & jnp.int32(-65536))


def _unpack_halves(p):
    lo = lax.bitcast_convert_type(lax.shift_left(p, jnp.int32(16)), F32)
    hi = lax.bitcast_convert_type(p & jnp.int32(-65536), F32)
    return lo, hi


def _router_kernel(x_ref, g_ref, shift_ref, scale_ref, wr_ref, bias_ref, h_ref, ids_ref, wts_ref, cnt_ref,
                   *, per_group):
    h = _norm_mod(x_ref[...], g_ref[...], shift_ref[...], scale_ref[...])
    h_ref[...] = _pack_halves(h)
    h_hi = h.astype(BF16)
    h_lo = (h - h_hi.astype(F32)).astype(BF16)
    w = wr_ref[...]
    w_hi = w.astype(BF16)
    w_lo = (w - w_hi.astype(F32)).astype(BF16)
    logits = (jnp.dot(h_hi, w_hi, preferred_element_type=F32)
              + jnp.dot(h_hi, w_lo, preferred_element_type=F32)
              + jnp.dot(h_lo, w_hi, preferred_element_type=F32))
    lane_i = lax.broadcasted_iota(jnp.int32, (h.shape[0], LANES), 1)
    lane = lane_i.astype(F32)
    valid = lane_i < N_GROUPS
    neg = jnp.float32(-jnp.inf)
    score, sel = [], []
    for j in range(per_group):
        member = logits[:, (j // 2) * LANES:(j // 2 + 1) * LANES]
        if j % 2:
            member = pltpu.roll(member, LANES // 2, axis=1)
        s = _sigmoid(member)
        score.append(s)
        sel.append(jnp.where(valid, s + bias_ref[j:j + 1, :], neg))
    assert per_group == 4
    hi1, lo1 = jnp.maximum(sel[0], sel[1]), jnp.minimum(sel[0], sel[1])
    hi2, lo2 = jnp.maximum(sel[2], sel[3]), jnp.minimum(sel[2], sel[3])
    top1 = jnp.maximum(hi1, hi2)
    top2 = jnp.maximum(jnp.minimum(hi1, hi2), jnp.maximum(lo1, lo2))
    gscore = jnp.where(valid, top1 + top2, neg)
    gmax = jnp.max(gscore, axis=-1, keepdims=True)
    grp = jnp.min(jnp.where(gscore == gmax, lane, float(LANES)), axis=-1, keepdims=True)
    pick = lane == grp
    zero = jnp.float32(0.0)
    sel_g = [jnp.sum(jnp.where(pick, s, zero), axis=-1, keepdims=True) for s in sel]
    sc_g = [jnp.sum(jnp.where(pick, s, zero), axis=-1, keepdims=True) for s in score]
    best, best_i, best_s = sel_g[0], jnp.zeros_like(grp), sc_g[0]
    for j in range(1, per_group):
        better = sel_g[j] > best
        best = jnp.where(better, sel_g[j], best)
        best_i = jnp.where(better, float(j), best_i)
        best_s = jnp.where(better, sc_g[j], best_s)
    second, second_i, second_s = jnp.full_like(best, neg), jnp.zeros_like(grp), jnp.zeros_like(best)
    for j in range(per_group):
        better = (sel_g[j] > second) & (best_i != float(j))
        second = jnp.where(better, sel_g[j], second)
        second_i = jnp.where(better, float(j), second_i)
        second_s = jnp.where(better, sc_g[j], second_s)
    tot = best_s + second_s
    e0 = grp * per_group + best_i
    e1 = grp * per_group + second_i
    @pl.when(pl.program_id(0) == 0)
    def _():
        cnt_ref[...] = jnp.zeros_like(cnt_ref)

    oh0 = lane == e0
    oh1 = lane == e1
    both = jnp.where(oh0 | oh1, jnp.float32(1.0), zero)
    tm = h.shape[0]
    earlier = (lax.broadcasted_iota(jnp.int32, (tm, tm), 1)
               < lax.broadcasted_iota(jnp.int32, (tm, tm), 0)).astype(BF16)
    before = jnp.dot(earlier, both.astype(BF16), preferred_element_type=F32) + cnt_ref[0:1, :]
    r0 = jnp.sum(jnp.where(oh0, before, zero), axis=-1, keepdims=True)
    r1 = jnp.sum(jnp.where(oh1, before, zero), axis=-1, keepdims=True)
    cnt_ref[...] = cnt_ref[...] + jnp.sum(both, axis=0, keepdims=True)
    ids = jnp.where(lane_i == 0, e0, jnp.where(lane_i == 1, e1, jnp.where(lane_i == 2, r0,
                    jnp.where(lane_i == 3, r1, zero))))
    ids_ref[...] = ids.astype(jnp.int32)
    wts_ref[...] = jnp.where(lane_i == 0, best_s / tot, jnp.where(lane_i == 1, second_s / tot, zero))


def router(xall, g, mod3, wr_pad, bias_pad, *, layer, shift_idx, n_rows, geom, per_group):
    batch, seq, _ = geom
    d = xall.shape[1]
    tm = min(TILES["router"], seq)
    mrow = _mod_row_fn(batch * seq // tm, seq // tm, batch, layer)
    kern = functools.partial(_router_kernel, per_group=per_group)
    return pl.pallas_call(
        kern,
        out_shape=(jax.ShapeDtypeStruct((n_rows, d // 2), jnp.int32),
                   jax.ShapeDtypeStruct((n_rows, LANES), jnp.int32),
                   jax.ShapeDtypeStruct((n_rows, LANES), F32),
                   jax.ShapeDtypeStruct((8, LANES), F32)),
        grid=(n_rows // tm,),
        in_specs=[
            pl.BlockSpec((tm, d), lambda m: (m, 0)),
            pl.BlockSpec((1, d), lambda m: (0, 0)),
            pl.BlockSpec((None, 1, d), lambda m: (mrow(m), 0, shift_idx)),
            pl.BlockSpec((None, 1, d), lambda m: (mrow(m), 0, shift_idx + 1)),
            pl.BlockSpec(wr_pad.shape, lambda m: (0, 0)),
            pl.BlockSpec(bias_pad.shape, lambda m: (0, 0)),
        ],
        out_specs=(pl.BlockSpec((tm, d // 2), lambda m: (m, 0)),
                   pl.BlockSpec((tm, LANES), lambda m: (m, 0)),
                   pl.BlockSpec((tm, LANES), lambda m: (m, 0)),
                   pl.BlockSpec((8, LANES), lambda m: (0, 0))),
        compiler_params=_cparams(("arbitrary",)),
        name="router",
    )(xall, g.reshape(1, d), mod3, mod3, wr_pad, bias_pad)


def _expert_kernel(be_ref, nxt_ref, nlive_ref,tok_cur, tok_next, dst_prev, dst_cur, h_hbm, win_hbm, wout_hbm, y_hbm,
                   xbuf0, xbuf1, ybuf0, ybuf1, win_f32, wout_f32, win_bf, wout_bf, gsem, ssem, wsem,
                   *, ff, n_blocks, layer):
    i = pl.program_id(0)
    blk = xbuf0.shape[0]
    xbufs, ybufs = (xbuf0, xbuf1), (ybuf0, ybuf1)

    def weight_copies(e):
        return (pltpu.make_async_copy(win_hbm.at[layer, e], win_f32, wsem.at[0]),
                pltpu.make_async_copy(wout_hbm.at[layer, e], wout_f32, wsem.at[1]))

    @pl.when(i == 0)
    def _():
        for cp in weight_copies(be_ref[0]):
            cp.start(priority=1)

    @pl.when((i == 0) | (be_ref[i] != be_ref[jnp.maximum(i - 1, 0)]))
    def _():
        for cp in weight_copies(be_ref[i]):
            cp.wait()
        win_bf[...] = win_f32[...].astype(BF16)
        wout_bf[...] = wout_f32[...].astype(BF16)
        for cp in weight_copies(nxt_ref[i]):
            cp.start(priority=1)

    def gather(tok_ref, p):
        for r in range(blk):
            pltpu.make_async_copy(h_hbm.at[pl.ds(tok_ref[0, r], 1)], xbufs[p].at[pl.ds(r, 1)],
                                  gsem.at[p]).start()

    def scatter(dst_ref, p):
        for r in range(blk):
            pltpu.make_async_copy(ybufs[p].at[pl.ds(r, 1)], y_hbm.at[pl.ds(dst_ref[0, r], 1)],
                                  ssem.at[p]).start()

    def wait_gather(p):
        pltpu.make_async_copy(h_hbm.at[pl.ds(0, blk)], xbufs[p], gsem.at[p]).wait()

    def wait_scatter(p):
        pltpu.make_async_copy(ybufs[p], y_hbm.at[pl.ds(0, blk)], ssem.at[p]).wait()

    @pl.when(i == 0)
    def _():
        ybuf1[...] = jnp.zeros_like(ybuf1)
        gather(tok_cur, 0)
        wait_gather(0)

    def step(p, live):
        if live:
            x_lo, x_hi = _unpack_halves(xbufs[p][...])
            x = jnp.concatenate([x_lo, x_hi], axis=1).astype(BF16)
            gu = jnp.dot(x, win_bf[...], preferred_element_type=F32)
            g = gu[:, :ff]
            act = (g * _sigmoid(g) * gu[:, ff:]).astype(BF16)
            y = jnp.dot(act, wout_bf[...], preferred_element_type=F32)
            gather(tok_next, 1 - p)
        scatter(dst_prev, 1 - p)
        pl.when(i >= 1)(lambda: wait_scatter(p))
        ybufs[p][...] = _pack_halves(y) if live else jnp.zeros_like(ybufs[p])
        if live:
            wait_gather(1 - p)

        @pl.when(i == n_blocks - 1)
        def _():
            scatter(dst_cur, p)
            wait_scatter(p)
            wait_scatter(1 - p)
            for cp in weight_copies(be_ref[i]):
                cp.wait()

    live = i < nlive_ref[0]
    for p in (0, 1):
        pl.when(live & (i % 2 == p))(functools.partial(step, p, True))
        pl.when(jnp.logical_not(live) & (i % 2 == p))(functools.partial(step, p, False))


def expert_ffn(h2, row_tok, row_dst, block_e, next_e, n_live, w_in, w_out, *, layer):
    dh = h2.shape[1]
    d = 2 * dh
    rows = row_tok.shape[0]
    blk = TILES["moe_block"]
    n_blocks = rows // blk
    ff = w_out.shape[2]
    tok3 = row_tok.reshape(n_blocks, 1, blk)
    spare = rows + jnp.arange(blk, dtype=jnp.int32)
    dst3 = jnp.concatenate([spare, row_dst]).reshape(n_blocks + 1, 1, blk)
    smem_blk = lambda f: pl.BlockSpec((None, 1, blk), f, memory_space=pltpu.SMEM)
    any_spec = pl.BlockSpec(memory_space=pl.ANY)
    kern = functools.partial(_expert_kernel, ff=ff, n_blocks=n_blocks, layer=layer)
    grid_spec = pltpu.PrefetchScalarGridSpec(
        num_scalar_prefetch=3,
        grid=(n_blocks,),
        in_specs=[
            smem_blk(lambda i, be, nx, nl: (i, 0, 0)),
            smem_blk(lambda i, be, nx, nl: (jnp.minimum(i + 1, n_blocks - 1), 0, 0)),
            smem_blk(lambda i, be, nx, nl: (i, 0, 0)),
            smem_blk(lambda i, be, nx, nl: (i + 1, 0, 0)),
            any_spec, any_spec, any_spec,
        ],
        out_specs=any_spec,
        scratch_shapes=[pltpu.VMEM((blk, dh), jnp.int32), pltpu.VMEM((blk, dh), jnp.int32),
                        pltpu.VMEM((blk, dh), jnp.int32), pltpu.VMEM((blk, dh), jnp.int32),
                        pltpu.VMEM((d, 2 * ff), F32), pltpu.VMEM((ff, d), F32),
                        pltpu.VMEM((d, 2 * ff), BF16), pltpu.VMEM((ff, d), BF16),
                        pltpu.SemaphoreType.DMA((2,)), pltpu.SemaphoreType.DMA((2,)),
                        pltpu.SemaphoreType.DMA((2,))],
    )
    return pl.pallas_call(
        kern,
        out_shape=jax.ShapeDtypeStruct((rows + blk, dh), jnp.int32),
        grid_spec=grid_spec,
        compiler_params=pltpu.CompilerParams(dimension_semantics=("arbitrary",),
                                             vmem_limit_bytes=TILES["vmem_expert"]),
        name="expert_ffn",
    )(block_e, next_e, n_live, tok3, tok3, dst3, dst3, h2, w_in, w_out)


def _combine_kernel(x_ref, y0_ref, y1_ref, w_ref, gate_ref, *rest, next_norm):
    w = w_ref[...]
    y0_lo, y0_hi = _unpack_halves(y0_ref[...])
    y1_lo, y1_hi = _unpack_halves(y1_ref[...])
    y = jnp.concatenate([w[:, 0:1] * y0_lo + w[:, 1:2] * y1_lo, w[:, 0:1] * y0_hi + w[:, 1:2] * y1_hi], axis=1)
    out = x_ref[...] + gate_ref[...] * y
    if next_norm:
        g_ref, shift_ref, scale_ref, o_ref, h_ref = rest
        h_ref[...] = _norm_mod(out, g_ref[...], shift_ref[...], scale_ref[...]).astype(h_ref.dtype)
    else:
        (o_ref,) = rest
    o_ref[...] = out


def moe_combine(xall, y_both, wts, mod3, *, layer, gate_idx, n_rows, geom, next_g=None):
    batch, seq, _ = geom
    d = xall.shape[1]
    tm = min(TILES["row"], seq)
    n_x = batch * seq // tm
    mrow = _mod_row_fn(n_x, seq // tm, batch, layer)
    in_specs = [
        pl.BlockSpec((tm, d), lambda m: (m, 0)),
        pl.BlockSpec((tm, d // 2), lambda m: (m, 0)),
        pl.BlockSpec((tm, d // 2), lambda m: (n_rows // tm + m, 0)),
        pl.BlockSpec((tm, LANES), lambda m: (m, 0)),
        pl.BlockSpec((None, 1, d), lambda m: (mrow(m), 0, gate_idx)),
    ]
    args = [xall, y_both, y_both, wts, mod3]
    out_shape = [jax.ShapeDtypeStruct((n_rows, d), F32)]
    out_specs = [pl.BlockSpec((tm, d), lambda m: (m, 0))]
    if next_g is not None:
        nrow = _mod_row_fn(n_x, seq // tm, batch, layer + 1)
        in_specs += [pl.BlockSpec((1, d), lambda m: (0, 0)),
                     pl.BlockSpec((None, 1, d), lambda m: (nrow(m), 0, 0)),
                     pl.BlockSpec((None, 1, d), lambda m: (nrow(m), 0, 1))]
        args += [next_g.reshape(1, d), mod3, mod3]
        out_shape.append(jax.ShapeDtypeStruct((n_rows, d), BF16))
        out_specs.append(pl.BlockSpec((tm, d), lambda m: (m, 0)))
    return pl.pallas_call(
        functools.partial(_combine_kernel, next_norm=next_g is not None),
        out_shape=tuple(out_shape),
        grid=(n_rows // tm,),
        in_specs=in_specs,
        out_specs=tuple(out_specs),
        compiler_params=_cparams(("parallel",)),
        name="moe_combine",
    )(*args)


def moe_dispatch(ids, counts, n_experts):
    blk = TILES["moe_block"]
    n = ids.shape[0]
    experts, rank = ids[:, :TOP_K], ids[:, TOP_K:2 * TOP_K]
    counts = counts[0, :n_experts].astype(jnp.int32)
    padded = (counts + blk - 1) // blk * blk
    pend = jnp.cumsum(padded)
    pstart = pend - padded
    eye = experts[:, :, None] == jnp.arange(n_experts, dtype=jnp.int32)[None, None, :]
    dest = jnp.sum(jnp.where(eye, pstart[None, None, :], 0), axis=-1) + rank
    n_blocks = -(-n * TOP_K // blk) + n_experts
    rows = n_blocks * blk
    slot = jnp.full((rows,), -1, jnp.int32).at[dest.T.reshape(-1)].set(jnp.arange(n * TOP_K, dtype=jnp.int32))
    pad = slot < 0
    pad_rank = jnp.cumsum(pad.astype(jnp.int32)) - 1
    row_dst = jnp.where(pad, n * TOP_K + pad_rank, slot)
    row_tok = jnp.where(pad, jnp.arange(rows, dtype=jnp.int32) % n, slot % n)
    block_start = jnp.arange(n_blocks, dtype=jnp.int32) * blk
    block_e = jnp.minimum(jnp.sum(block_start[:, None] >= pend[None, :], axis=1), n_experts - 1).astype(jnp.int32)
    e_ids = jnp.arange(n_experts, dtype=jnp.int32)
    present = jnp.any(block_e[:, None] == e_ids[None, :], axis=0)
    later = present[None, :] & (e_ids[None, :] > e_ids[:, None])
    next_present = jnp.where(jnp.any(later, axis=1), jnp.argmax(later, axis=1).astype(jnp.int32), e_ids)
    next_e = next_present[block_e]
    n_live = (pend[-1:] // blk).astype(jnp.int32)
    return row_tok, row_dst, block_e, next_e, n_live


def _rope_tables(seq, tm):
    inv = ROPE_THETA ** (-jnp.arange(0, AXIS_DIM, 2, dtype=F32) / AXIS_DIM)
    t = jnp.arange(seq)
    ang_r = (t // GRID_W).astype(F32)[:, None] * inv
    ang_c = (t % GRID_W).astype(F32)[:, None] * inv
    cos = jnp.concatenate([jnp.cos(ang_r), jnp.cos(ang_r), jnp.cos(ang_c), jnp.cos(ang_c)], axis=-1)
    sin = jnp.concatenate([-jnp.sin(ang_r), jnp.sin(ang_r), -jnp.sin(ang_c), jnp.sin(ang_c)], axis=-1)
    pad = LANES - ROPE_DIM
    cos = jnp.pad(cos, ((0, 0), (0, pad)))
    sin = jnp.pad(sin, ((0, 0), (0, pad)))
    ident_c = jnp.pad(jnp.ones((tm, ROPE_DIM), F32), ((0, 0), (0, pad)))
    return (jnp.concatenate([ident_c, cos], axis=0),
            jnp.concatenate([jnp.zeros((tm, LANES), F32), sin], axis=0))


def _partner():
    j = jnp.arange(ROPE_DIM)
    half = AXIS_DIM // 2
    return jnp.where((j % AXIS_DIM) < half, j + half, j - half)


def _partner_matrix():
    rows = jnp.arange(LANES)[:, None]
    cols = jnp.arange(LANES)[None, :]
    partner = jnp.pad(_partner(), (0, LANES - ROPE_DIM), constant_values=-1)
    return (rows == partner[None, :]).astype(BF16) * (cols < ROPE_DIM).astype(BF16)


def _pad_lanes(a):
    return jnp.pad(a, [(0, 0)] * (a.ndim - 1) + [(0, LANES - a.shape[-1])])


def _norm_gain_rows(g):
    rope = g[NOPE_DIM:]
    rows = jnp.stack([g[:NOPE_DIM], _pad_lanes(rope), _pad_lanes(rope[_partner()])])
    return jnp.pad(rows, ((0, 8 - rows.shape[0]), (0, 0))).astype(F32)


def _layer_weights(l, w_uq, w_ukv, q_lora, kv_lora):
    heads = MLA_HEADS
    wq = w_uq[l].reshape(q_lora, heads, QK_DIM)
    wq_n = wq[:, :, :NOPE_DIM].reshape(q_lora, heads * LANES)
    wq_r = _pad_lanes(wq[:, :, NOPE_DIM:]).reshape(q_lora, heads * LANES)
    wq_s = _pad_lanes(wq[:, :, NOPE_DIM:][:, :, _partner()]).reshape(q_lora, heads * LANES)
    wq_all = jnp.concatenate([wq_n, wq_r, wq_s], axis=1).astype(BF16)
    wkv = w_ukv[l].reshape(kv_lora, heads, NOPE_DIM + V_DIM)
    wkv_all = jnp.concatenate([wkv[:, :, :NOPE_DIM].reshape(kv_lora, heads * NOPE_DIM),
                               wkv[:, :, NOPE_DIM:].reshape(kv_lora, heads * V_DIM)], axis=1).astype(BF16)
    return wq_all, wkv_all


def _router_layout(w_router, router_bias):
    d, n_experts = w_router.shape
    per_group = n_experts // N_GROUPS
    assert per_group % 2 == 0 and N_GROUPS <= LANES // 2
    wr = w_router.reshape(d, N_GROUPS, per_group).transpose(0, 2, 1)
    half = jnp.pad(wr, ((0, 0), (0, 0), (0, LANES // 2 - N_GROUPS)))
    wr_pad = half.reshape(d, per_group * LANES // 2).astype(F32)
    bias = _pad_lanes(router_bias.reshape(N_GROUPS, per_group).T).astype(F32)
    bias = jnp.pad(bias, ((0, 8 - per_group), (0, 0)))
    return wr_pad, bias, per_group


def kernel(x, c, ctx, c_ctx, w_mod, b_mod, g_norm1, g_norm2, w_in, conv_a, ln_a_g, ln_a_b, w_a_out, g_cq, g_ckv, w_uq, w_ukv, g_q, g_k, w_b_out, conv_c, w_c_out, w_o, w_router, router_bias, w_e_in, w_e_out):
    batch, seq, d = x.shape
    n_ctx = ctx.shape[1]
    depth = w_mod.shape[0]
    geom = (batch, seq, n_ctx)
    t_x = batch * seq
    t_all = t_x + batch * n_ctx
    q_lora, kv_lora = g_cq.shape[1], g_ckv.shape[1]
    ch_a, ch_c = conv_a.shape[2], conv_c.shape[2]
    n_experts = w_router.shape[1]
    assert batch + 1 <= MOD_ROWS and t_x % n_ctx == 0 and ch_a == ch_c and (3 * d) % ch_a == 0

    stream = (x.reshape(t_x, d), ctx.reshape(batch * n_ctx, d))
    cvec = jnp.concatenate([c, c_ctx[None, :], jnp.zeros((MOD_ROWS - batch - 1, d), F32)], axis=0)
    mod3 = modulation(cvec, w_mod, b_mod).reshape(depth * MOD_ROWS, 1, N_MOD * d)

    cos_tab, sin_tab = _rope_tables(seq, _mla_tile(geom))
    wr_pad, bias_pad, per_group = _router_layout(w_router, router_bias)
    col0 = 3 * d // ch_a
    off_b = 3 * d + 2 * ch_a
    off_c = off_b + q_lora + kv_lora + ROPE_DIM
    w_in2 = w_in.reshape(depth * d, w_in.shape[2])

    for l in range(depth):
        last = l == depth - 1
        rows_main = t_x if last else t_all
        wq_all, wkv_all = _layer_weights(l, w_uq, w_ukv, q_lora, kv_lora)

        if l == 0:
            h = norm_modulate(stream, g_norm1[l], mod3, layer=l, shift_idx=0, n_rows=t_all, geom=geom)
        u = project(h, w_in2, n_cols=off_b, n_rows=rows_main, row_block=l)
        uc = project_window(h, w_in2, col0=off_c, n_cols=3 * ch_c, n_rows=rows_main, row_block=l)
        ub = project_window(h, w_in2, col0=off_b, n_cols=q_lora + kv_lora, n_rows=t_all, row_block=l,
                            keep_tail=True)
        qt, k, vt = mla_project(ub, g_cq[l], g_ckv[l], wq_all, wkv_all, _norm_gain_rows(g_q[l]),
                                _norm_gain_rows(g_k[l]), cos_tab, sin_tab, geom=geom)
        ob_x = attention_latent(qt, k, vt, geom=geom)
        ob_c = ob_x if last else attention_context(qt, k, vt, geom=geom)
        ha = depthwise_branch(u, conv_a[l], col_a=col0, col_b=col0 + 1, col_gate=None,
                              ln=(ln_a_g[l], ln_a_b[l]), n_rows=rows_main, geom=geom)
        hc = depthwise_branch(uc, conv_c[l], col_a=1, col_b=2, col_gate=0,
                              ln=None, n_rows=rows_main, geom=geom)
        mrg = branch_merge(ha, ob_x, ob_c, hc, w_a_out, w_b_out, w_c_out, u, layer=l,
                           n_rows=rows_main, geom=geom)
        x1 = out_proj_residual(mrg, w_o, stream, mod3, layer=l, gate_idx=2, n_rows=rows_main, geom=geom)

        h2, ids, wts, counts = router(x1, g_norm2[l], mod3, wr_pad, bias_pad, layer=l, shift_idx=3,
                                      n_rows=rows_main, geom=geom, per_group=per_group)
        row_tok, row_dst, block_e, next_e, n_live = moe_dispatch(ids, counts, n_experts)
        y_both = expert_ffn(h2, row_tok, row_dst, block_e, next_e, n_live, w_e_in, w_e_out, layer=l)
        if last:
            (stream,) = moe_combine(x1, y_both, wts, mod3, layer=l, gate_idx=5, n_rows=rows_main, geom=geom)
        else:
            stream, h = moe_combine(x1, y_both, wts, mod3, layer=l, gate_idx=5, n_rows=rows_main,
                                    geom=geom, next_g=g_norm1[l + 1])

    return stream.reshape(batch, seq, d)
```
